```python
import jax, jax.numpy as jnp
from jax import lax
import numpy as np

D_MODEL = 1024
BATCH = 4
SEQ = 4096
DEPTH = 1

N_META = 16
HEAD_DIM = 64
ATTN_HEADS = 8
ATTN_WIDTH = ATTN_HEADS * HEAD_DIM
CONV_WIDTH = D_MODEL - ATTN_WIDTH
CONV_GROUPS = CONV_WIDTH // HEAD_DIM
CONV_K = 3
BLOCK_Q = 128
N_EXPERT_GROUPS = 4
EXPERTS_PER_GROUP = 8
N_EXPERTS = N_EXPERT_GROUPS * EXPERTS_PER_GROUP
TOP_K = 2
D_EXPERT = 512
EXPERT_BLOCK = 128
EPS = 1e-6
MASK_VALUE = -1e30
SPLIT_SIZES = (ATTN_WIDTH, ATTN_WIDTH, ATTN_WIDTH, ATTN_HEADS, CONV_WIDTH, CONV_WIDTH, CONV_WIDTH)
PROJ_COLS = sum(SPLIT_SIZES)
SPLIT_POINTS = tuple(int(p) for p in np.cumsum(SPLIT_SIZES)[:-1])

kernel_name = "hymba_fox_shortconv_hiermoe_block"


def rms_norm(x, g):
    xf = x.astype(jnp.float32)
    y = xf * lax.rsqrt(jnp.mean(xf * xf, axis=-1, keepdims=True) + EPS)
    return (y * g.astype(jnp.float32)).astype(x.dtype)


def forgetting_attention(q, k, v, log_f):
    Bsz, T, H, Dh = q.shape
    pad = (-T) % BLOCK_Q
    pw = ((0, 0), (pad, 0), (0, 0), (0, 0))
    q = jnp.pad(q, pw)
    k = jnp.pad(k, pw)
    v = jnp.pad(v, pw)
    log_f = jnp.pad(log_f, ((0, 0), (pad, 0), (0, 0)))
    Tp = T + pad
    nb = Tp // BLOCK_Q
    cum = jnp.cumsum(log_f, axis=1)
    cum_k = cum.transpose(0, 2, 1)
    k_pos = jnp.arange(Tp)
    k_valid = k_pos >= pad
    qb = q.reshape(Bsz, nb, BLOCK_Q, H, Dh).transpose(1, 0, 3, 2, 4)
    cq = cum.reshape(Bsz, nb, BLOCK_Q, H).transpose(1, 0, 3, 2)
    scale = Dh ** -0.5

    def block(args):
        i, qi, ci = args
        s = jnp.einsum('bhqd,bkhd->bhqk', qi, k).astype(jnp.float32) * scale
        s = s + ci[..., None] - cum_k[:, :, None, :]
        q_pos = i * BLOCK_Q + jnp.arange(BLOCK_Q)
        allowed = (k_pos[None, :] <= q_pos[:, None]) & k_valid[None, :]
        s = jnp.where(allowed, s, MASK_VALUE)
        p = jax.nn.softmax(s, axis=-1)
        return jnp.einsum('bhqk,bkhd->bqhd', p.astype(v.dtype), v)

    out = lax.map(block, (jnp.arange(nb), qb, cq))
    out = out.transpose(1, 0, 2, 3, 4).reshape(Bsz, Tp, H, Dh)
    return out[:, pad:]


def short_conv_mixer(b_gate, c_gate, hc, conv_w):
    u = c_gate * hc
    ch = u.shape[-1]
    y = lax.conv_general_dilated(
        u, conv_w[:, None, :].astype(u.dtype), window_strides=(1,),
        padding=[(CONV_K - 1, 0)], dimension_numbers=('NWC', 'WIO', 'NWC'),
        feature_group_count=ch)
    return b_gate * y


def hierarchical_moe(u, w_rg, b_rg, w_re, b_re, w_gate, w_up, w_down):
    Bsz, T, D = u.shape
    N = Bsz * T
    xt = u.reshape(N, D)
    g_prob = jax.nn.softmax((xt @ w_rg).astype(jnp.float32) + b_rg.astype(jnp.float32), axis=-1)
    g_p, g_idx = lax.top_k(g_prob, 1)
    e_logits = ((xt @ w_re).astype(jnp.float32) + b_re.astype(jnp.float32)).reshape(
        N, N_EXPERT_GROUPS, EXPERTS_PER_GROUP)
    e_logits = jnp.take_along_axis(e_logits, g_idx[:, :, None], axis=1)[:, 0]
    e_prob = jax.nn.softmax(e_logits, axis=-1)
    e_p, e_idx = lax.top_k(e_prob, TOP_K)
    gates = g_p * e_p / jnp.sum(e_p, axis=-1, keepdims=True)
    expert_ids = g_idx * EXPERTS_PER_GROUP + e_idx

    A = N * TOP_K
    flat_e = expert_ids.reshape(A)
    flat_tok = jnp.repeat(jnp.arange(N, dtype=jnp.int32), TOP_K)
    flat_w = gates.reshape(A)
    order = jnp.argsort(flat_e)
    se, stok, sw = flat_e[order], flat_tok[order], flat_w[order]
    counts = jnp.bincount(flat_e, length=N_EXPERTS)
    padded = (counts + EXPERT_BLOCK - 1) // EXPERT_BLOCK * EXPERT_BLOCK
    pad_end = jnp.cumsum(padded)
    pad_start = pad_end - padded
    start = jnp.cumsum(counts) - counts
    dest = pad_start[se] + jnp.arange(A) - start[se]
    n_blocks = -(-A // EXPERT_BLOCK) + N_EXPERTS
    R = n_blocks * EXPERT_BLOCK
    buf_tok = jnp.zeros((R,), jnp.int32).at[dest].set(stok)
    buf_w = jnp.zeros((R,), jnp.float32).at[dest].set(sw)
    block_e = jnp.minimum(
        jnp.searchsorted(pad_end, jnp.arange(n_blocks) * EXPERT_BLOCK, side='right'),
        N_EXPERTS - 1)
    xs = xt[buf_tok].reshape(n_blocks, EXPERT_BLOCK, D)

    def expert_block(args):
        xb, e = args
        hdn = jax.nn.silu(xb @ w_gate[e]) * (xb @ w_up[e])
        return hdn @ w_down[e]

    ys = lax.map(expert_block, (xs, block_e)).reshape(R, D)
    out = jax.ops.segment_sum(ys * buf_w[:, None].astype(ys.dtype), buf_tok, num_segments=N)
    return out.reshape(Bsz, T, D)


def hybrid_layer(h, g_mix, w_in, b_forget, q_norm_g, k_norm_g, conv_w, attn_out_g,
                 conv_out_g, w_out, g_ffn, w_rg, b_rg, w_re, b_re, w_gate, w_up, w_down):
    Bsz, T, _ = h.shape
    u = rms_norm(h, g_mix)
    proj = u @ w_in
    q, k, v, f_logit, b_gate, c_gate, hc = jnp.split(proj, SPLIT_POINTS, axis=-1)
    q = rms_norm(q.reshape(Bsz, T, ATTN_HEADS, HEAD_DIM), q_norm_g)
    k = rms_norm(k.reshape(Bsz, T, ATTN_HEADS, HEAD_DIM), k_norm_g)
    v = v.reshape(Bsz, T, ATTN_HEADS, HEAD_DIM)
    log_f = jax.nn.log_sigmoid(f_logit.astype(jnp.float32) + b_forget.astype(jnp.float32))
    attn = forgetting_attention(q, k, v, log_f)
    attn = rms_norm(attn, attn_out_g.reshape(ATTN_HEADS, HEAD_DIM)).reshape(Bsz, T, ATTN_WIDTH)
    conv = short_conv_mixer(b_gate, c_gate, hc, conv_w)
    conv = rms_norm(conv.reshape(Bsz, T, CONV_GROUPS, HEAD_DIM),
                    conv_out_g.reshape(CONV_GROUPS, HEAD_DIM)).reshape(Bsz, T, CONV_WIDTH)
    h = h + jnp.concatenate([attn, conv], axis=-1) @ w_out
    h = h + hierarchical_moe(rms_norm(h, g_ffn), w_rg, b_rg, w_re, b_re, w_gate, w_up, w_down)
    return h


def setup_inputs(seed: int = 0) -> dict:
    key = jax.random.key(seed)
    ks = jax.random.split(key, 20)

    def nrm(k, shape, scale):
        return jax.random.normal(k, shape, jnp.float32) * scale

    return {
        "x": nrm(ks[0], (BATCH, SEQ, D_MODEL), 1.0),
        "meta_tokens": nrm(ks[1], (N_META, D_MODEL), 1.0),
        "norm_mix_g": 1.0 + nrm(ks[2], (DEPTH, D_MODEL), 0.02),
        "w_in": nrm(ks[3], (DEPTH, D_MODEL, PROJ_COLS), D_MODEL ** -0.5),
        "b_forget": jax.random.uniform(ks[4], (DEPTH, ATTN_HEADS), jnp.float32, 1.0, 5.0),
        "q_norm_g": 1.0 + nrm(ks[5], (DEPTH, HEAD_DIM), 0.02),
        "k_norm_g": 1.0 + nrm(ks[6], (DEPTH, HEAD_DIM), 0.02),
        "conv_w": nrm(ks[7], (DEPTH, CONV_K, CONV_WIDTH), CONV_K ** -0.5),
        "attn_out_g": 1.0 + nrm(ks[8], (DEPTH, ATTN_WIDTH), 0.02),
        "conv_out_g": 1.0 + nrm(ks[9], (DEPTH, CONV_WIDTH), 0.02),
        "w_out": nrm(ks[10], (DEPTH, D_MODEL, D_MODEL), D_MODEL ** -0.5),
        "norm_ffn_g": 1.0 + nrm(ks[11], (DEPTH, D_MODEL), 0.02),
        "w_router_group": nrm(ks[12], (DEPTH, D_MODEL, N_EXPERT_GROUPS), D_MODEL ** -0.5),
        "b_router_group": nrm(ks[13], (DEPTH, N_EXPERT_GROUPS), 0.01),
        "w_router_expert": nrm(ks[14], (DEPTH, D_MODEL, N_EXPERTS), D_MODEL ** -0.5),
        "b_router_expert": nrm(ks[15], (DEPTH, N_EXPERTS), 0.01),
        "w_gate": nrm(ks[16], (DEPTH, N_EXPERTS, D_MODEL, D_EXPERT), D_MODEL ** -0.5),
        "w_up": nrm(ks[17], (DEPTH, N_EXPERTS, D_MODEL, D_EXPERT), D_MODEL ** -0.5),
        "w_down": nrm(ks[18], (DEPTH, N_EXPERTS, D_EXPERT, D_MODEL), D_EXPERT ** -0.5),
    }


def reference(x, meta_tokens, norm_mix_g, w_in, b_forget, q_norm_g, k_norm_g, conv_w,
              attn_out_g, conv_out_g, w_out, norm_ffn_g, w_router_group, b_router_group,
              w_router_expert, b_router_expert, w_gate, w_up, w_down):
    Bsz = x.shape[0]
    meta = jnp.broadcast_to(meta_tokens[None].astype(x.dtype), (Bsz, N_META, D_MODEL))
    h = jnp.concatenate([meta, x], axis=1)
    for layer in range(DEPTH):
        h = hybrid_layer(
            h, norm_mix_g[layer], w_in[layer], b_forget[layer], q_norm_g[layer],
            k_norm_g[layer], conv_w[layer], attn_out_g[layer], conv_out_g[layer],
            w_out[layer], norm_ffn_g[layer], w_router_group[layer], b_router_group[layer],
            w_router_expert[layer], b_router_expert[layer], w_gate[layer], w_up[layer],
            w_down[layer])
    return h[:, N_META:]
```

```python
import functools

import jax
import jax.numpy as jnp
from jax import lax
from jax.experimental import pallas as pl
from jax.experimental.pallas import tpu as pltpu

D_MODEL = 1024
N_META = 16
HEAD_DIM = 64
ATTN_HEADS = 8
ATTN_WIDTH = ATTN_HEADS * HEAD_DIM
CONV_WIDTH = D_MODEL - ATTN_WIDTH
CONV_K = 3
N_EXPERT_GROUPS = 4
EXPERTS_PER_GROUP = 8
N_EXPERTS = N_EXPERT_GROUPS * EXPERTS_PER_GROUP
TOP_K = 2
D_EXPERT = 512
EPS = 1e-6
MASK_VALUE = -1e30

LANES = 128
SUBLANES = 8
HEAD_PAIRS = ATTN_HEADS * HEAD_DIM // LANES
PROJ_PAD_COLS = 6 * ATTN_WIDTH + LANES
VMEM_LIMIT = 56 * 1024 * 1024

IN_TILE = 512
META_TILE = 128
ATTN_TILE = 512
OUT_TILE = 512
EXPERT_ROWS = 256
COMBINE_TILE = 256

F32 = jnp.float32
BF16 = jnp.bfloat16


def _lane_iota(shape):
    return lax.broadcasted_iota(jnp.int32, shape, len(shape) - 1)


def _inproj_kernel(is_meta, tm, x_ref, gmix_ref, w1_ref, bf_ref, gq_ref, gk_ref, cw_ref, gco_ref,
                   gmat_ref, tri_ref, halo_ref, cum0_ref, *rest):
    if is_meta:
        k_ref, v_ref, cum_ref, uc_ref, ucbuf, carry = rest
    else:
        q_ref, k_ref, v_ref, conv_ref, cump_ref, ckt_ref, ucbuf, carry = rest
    t = pl.program_id(1)

    @pl.when(t == 0)
    def _():
        ucbuf[0:SUBLANES, :] = halo_ref[...]
        carry[0:1, :] = cum0_ref[...]

    x = x_ref[0]
    ms = jnp.mean(x * x, axis=-1, keepdims=True)
    u = (x * lax.rsqrt(ms + EPS)) * gmix_ref[...]
    proj = jnp.dot(u.astype(BF16), w1_ref[...], preferred_element_type=F32)

    def head_norm(z, g):
        ssq = jnp.dot((z * z).astype(BF16), gmat_ref[...], preferred_element_type=F32)
        return z * lax.rsqrt(ssq * (1.0 / HEAD_DIM) + EPS) * g

    w = ATTN_WIDTH
    kn = head_norm(proj[:, w:2 * w], gk_ref[...])
    k_ref[0] = kn.astype(BF16)
    v_ref[0] = proj[:, 2 * w:3 * w].astype(BF16)

    z = proj[:, 6 * w:6 * w + LANES] + bf_ref[...]
    ls = jnp.minimum(z, 0.0) - jnp.log1p(jnp.exp(-jnp.abs(z)))
    ls = jnp.where(_lane_iota(ls.shape) < ATTN_HEADS, ls, 0.0)
    hi = ls.astype(BF16)
    r1 = ls - hi.astype(F32)
    mid = r1.astype(BF16)
    lo = (r1 - mid.astype(F32)).astype(BF16)
    tri = tri_ref[...]
    cs = (jnp.dot(tri, hi, preferred_element_type=F32)
          + jnp.dot(tri, mid, preferred_element_type=F32)
          + jnp.dot(tri, lo, preferred_element_type=F32))
    cum = cs + carry[0:1, :]
    carry[0:1, :] = cum[tm - 1:tm, :]

    uc = proj[:, 4 * w:5 * w] * proj[:, 5 * w:6 * w]
    ucbuf[SUBLANES:SUBLANES + tm, :] = uc
    uc1 = ucbuf[SUBLANES - 1:SUBLANES - 1 + tm, :]
    uc2 = ucbuf[SUBLANES - 2:SUBLANES - 2 + tm, :]
    ucbuf[0:SUBLANES, :] = uc[tm - SUBLANES:tm, :]

    if is_meta:
        cum_ref[0] = cum
        uc_ref[0] = uc
        return

    qn = head_norm(proj[:, 0:w], gq_ref[...])
    q_ref[0] = qn.astype(BF16)
    y = cw_ref[0:1, :] * uc2 + cw_ref[1:2, :] * uc1 + cw_ref[2:3, :] * uc
    conv = proj[:, 3 * w:4 * w] * y
    conv_ref[0] = head_norm(conv, gco_ref[...]).astype(BF16)
    for p in range(HEAD_PAIRS):
        cump_ref[0, p] = cum if p == 0 else pltpu.roll(cum, LANES - 2 * p, axis=1)
    ckt_ref[0, 0] = cum.T[0:ATTN_HEADS, :]


def _inproj(x, is_meta, tm, consts, halo, cum0):
    bsz, tlen, _ = x.shape
    nt = tlen // tm
    gmix, w1, bfp, gq, gk, cw, gco, gmat = consts
    tri = jnp.tril(jnp.ones((tm, tm), F32)).astype(BF16)
    w = ATTN_WIDTH

    def full(a):
        return pl.BlockSpec(a.shape, lambda b, t: (0,) * a.ndim)

    in_specs = [pl.BlockSpec((1, tm, D_MODEL), lambda b, t: (b, t, 0))] + [
        full(a) for a in (gmix, w1, bfp, gq, gk, cw, gco, gmat, tri, halo, cum0)]
    tok = lambda width: pl.BlockSpec((1, tm, width), lambda b, t: (b, t, 0))
    if is_meta:
        out_shape = [jax.ShapeDtypeStruct((bsz, tlen, w), BF16),
                     jax.ShapeDtypeStruct((bsz, tlen, w), BF16),
                     jax.ShapeDtypeStruct((bsz, tlen, LANES), F32),
                     jax.ShapeDtypeStruct((bsz, tlen, w), F32)]
        out_specs = [tok(w), tok(w), tok(LANES), tok(w)]
    else:
        out_shape = [jax.ShapeDtypeStruct((bsz, tlen, w), BF16)] * 4 + [
            jax.ShapeDtypeStruct((bsz, HEAD_PAIRS, tlen, LANES), F32),
            jax.ShapeDtypeStruct((bsz, nt, ATTN_HEADS, tm), F32)]
        out_specs = [tok(w)] * 4 + [
            pl.BlockSpec((1, HEAD_PAIRS, tm, LANES), lambda b, t: (b, 0, t, 0)),
            pl.BlockSpec((1, 1, ATTN_HEADS, tm), lambda b, t: (b, t, 0, 0))]
    return pl.pallas_call(
        functools.partial(_inproj_kernel, is_meta, tm),
        grid=(bsz, nt),
        in_specs=in_specs,
        out_specs=out_specs,
        out_shape=out_shape,
        scratch_shapes=[pltpu.VMEM((tm + SUBLANES, w), F32), pltpu.VMEM((SUBLANES, LANES), F32)],
        compiler_params=pltpu.CompilerParams(
            dimension_semantics=("arbitrary", "arbitrary"), vmem_limit_bytes=VMEM_LIMIT),
        name="inproj_meta" if is_meta else "inproj",
    )(x, gmix, w1, bfp, gq, gk, cw, gco, gmat, tri, halo, cum0)


def _attn_kernel(tq, q_ref, k_ref, v_ref, cq_ref, ckt_ref, km_ref, vm_ref, ckm_ref, gao_ref,
                 o_ref, m_sc, l_sc, acc_sc):
    hp = pl.program_id(1)
    qi = pl.program_id(2)
    q = q_ref[0]
    lane = _lane_iota(q.shape)
    first = lane < HEAD_DIM
    zero = jnp.zeros_like(q)
    qh = (jnp.where(first, q, zero), jnp.where(first, zero, q))
    cqb = cq_ref[0, 0]
    cq = (cqb[:, 0:1], cqb[:, 1:2])

    m_sc[...] = jnp.full(m_sc.shape, MASK_VALUE, F32)
    l_sc[...] = jnp.zeros(l_sc.shape, F32)
    acc_sc[...] = jnp.zeros(acc_sc.shape, F32)

    def scores(hh, kblk):
        return lax.dot_general(qh[hh], kblk, (((1,), (1,)), ((), ())), preferred_element_type=F32)

    def update(hh, s, vblk):
        m_old = m_sc[hh]
        m_new = jnp.maximum(m_old, jnp.max(s, axis=1, keepdims=True))
        alpha = jnp.exp(m_old - m_new)
        p = jnp.exp(s - m_new)
        l_sc[hh] = alpha * l_sc[hh] + jnp.sum(p, axis=1, keepdims=True)
        acc_sc[hh] = alpha * acc_sc[hh] + jnp.dot(p.astype(BF16), vblk, preferred_element_type=F32)
        m_sc[hh] = m_new

    km = km_ref[...]
    vm = vm_ref[...]
    for hh in range(2):
        s = scores(hh, km) + cq[hh] - ckm_ref[0, hh:hh + 1, 0:N_META]
        update(hh, s, vm)

    def kv_block(j, masked):
        start = pl.multiple_of(j * tq, tq)
        kb = k_ref[0, pl.ds(start, tq), :]
        vb = v_ref[0, pl.ds(start, tq), :]
        for hh in range(2):
            ck = ckt_ref[0, j, pl.ds(2 * hp + hh, 1), :]
            s = scores(hh, kb) + cq[hh] - ck
            if masked:
                row = lax.broadcasted_iota(jnp.int32, s.shape, 0)
                col = lax.broadcasted_iota(jnp.int32, s.shape, 1)
                s = jnp.where(col <= row, s, MASK_VALUE)
            update(hh, s, vb)

    def body(j, carry):
        kv_block(j, False)
        return carry

    lax.fori_loop(0, qi, body, 0)
    kv_block(qi, True)

    o = jnp.where(first, acc_sc[0] / l_sc[0], acc_sc[1] / l_sc[1])
    o2 = o * o
    ss0 = jnp.sum(jnp.where(first, o2, 0.0), axis=1, keepdims=True)
    ss1 = jnp.sum(jnp.where(first, 0.0, o2), axis=1, keepdims=True)
    msq = jnp.where(first, ss0, ss1) * (1.0 / HEAD_DIM)
    o_ref[0] = (o * lax.rsqrt(msq + EPS) * gao_ref[...]).astype(BF16)


def _attention(q, k, v, cump, ckt, km, vm, ckm, gao):
    bsz, slen, w = q.shape
    tq = ATTN_TILE
    nq = slen // tq
    return pl.pallas_call(
        functools.partial(_attn_kernel, tq),
        grid=(bsz, HEAD_PAIRS, nq),
        in_specs=[
            pl.BlockSpec((1, tq, LANES), lambda b, p, i: (b, i, p)),
            pl.BlockSpec((1, slen, LANES), lambda b, p, i: (b, 0, p)),
            pl.BlockSpec((1, slen, LANES), lambda b, p, i: (b, 0, p)),
            pl.BlockSpec((1, 1, tq, LANES), lambda b, p, i: (b, p, i, 0)),
            pl.BlockSpec((1, nq, ATTN_HEADS, tq), lambda b, p, i: (b, 0, 0, 0)),
            pl.BlockSpec((N_META, LANES), lambda b, p, i: (0, p)),
            pl.BlockSpec((N_META, LANES), lambda b, p, i: (0, p)),
            pl.BlockSpec((1, SUBLANES, LANES), lambda b, p, i: (p, 0, 0)),
            pl.BlockSpec((1, LANES), lambda b, p, i: (0, p)),
        ],
        out_specs=pl.BlockSpec((1, tq, LANES), lambda b, p, i: (b, i, p)),
        out_shape=jax.ShapeDtypeStruct((bsz, slen, w), BF16),
        scratch_shapes=[pltpu.VMEM((2, tq, 1), F32), pltpu.VMEM((2, tq, 1), F32),
                        pltpu.VMEM((2, tq, LANES), F32)],
        compiler_params=pltpu.CompilerParams(
            dimension_semantics=("arbitrary", "arbitrary", "arbitrary"), vmem_limit_bytes=VMEM_LIMIT),
        name="fox_attention",
    )(q, k, v, cump, ckt, km, vm, ckm, gao)


def _outproj_kernel(tm, attn_ref, conv_ref, x_ref, woa_ref, woc_ref, gffn_ref, wr_ref, br_ref, tri_ref,
                    h_ref, xt_ref, ri_ref, rf_ref, cnt_ref, carry):
    i = pl.program_id(0)

    @pl.when(i == 0)
    def _():
        carry[...] = jnp.zeros(carry.shape, F32)

    h = (x_ref[...]
         + jnp.dot(attn_ref[...], woa_ref[...], preferred_element_type=F32)
         + jnp.dot(conv_ref[...], woc_ref[...], preferred_element_type=F32))
    h_ref[...] = h
    ms = jnp.mean(h * h, axis=-1, keepdims=True)
    xt = (h * lax.rsqrt(ms + EPS)) * gffn_ref[...]
    xt_ref[...] = xt

    logits = jnp.dot(xt, wr_ref[...], preferred_element_type=F32,
                     precision=lax.Precision.HIGHEST) + br_ref[...]
    lane = _lane_iota(logits.shape)
    lanef = lane.astype(F32)
    big = float(LANES)

    def first_argmax(vals, vmax):
        return jnp.min(jnp.where(vals == vmax, lanef, big), axis=1, keepdims=True)

    is_g = lane < N_EXPERT_GROUPS
    gl = jnp.where(is_g, logits, MASK_VALUE)
    gmax = jnp.max(gl, axis=1, keepdims=True)
    gidx = first_argmax(gl, gmax)
    g_p = 1.0 / jnp.sum(jnp.where(is_g, jnp.exp(gl - gmax), 0.0), axis=1, keepdims=True)

    base = N_EXPERT_GROUPS + EXPERTS_PER_GROUP * gidx
    in_grp = (lanef >= base) & (lanef < base + EXPERTS_PER_GROUP)
    el = jnp.where(in_grp, logits, MASK_VALUE)
    l1 = jnp.max(el, axis=1, keepdims=True)
    e1 = first_argmax(el, l1)
    el2 = jnp.where(lanef == e1, MASK_VALUE, el)
    l2 = jnp.max(el2, axis=1, keepdims=True)
    e2 = first_argmax(el2, l2)
    zsum = jnp.sum(jnp.where(in_grp, jnp.exp(el - l1), 0.0), axis=1, keepdims=True)
    p1 = 1.0 / zsum
    p2 = jnp.exp(l2 - l1) / zsum
    den = p1 + p2
    w1 = g_p * p1 / den
    w2 = g_p * p2 / den
    id1 = e1 - N_EXPERT_GROUPS
    id2 = e2 - N_EXPERT_GROUPS

    oh1 = lanef == id1
    oh2 = lanef == id2
    oh = jnp.where(oh1 | oh2, 1.0, 0.0)
    before = jnp.dot(tri_ref[...], oh.astype(BF16), preferred_element_type=F32) + carry[0:1, :]
    rank1 = jnp.sum(jnp.where(oh1, before, 0.0), axis=1, keepdims=True)
    rank2 = jnp.sum(jnp.where(oh2, before, 0.0), axis=1, keepdims=True)
    carry[0:1, :] = carry[0:1, :] + jnp.sum(oh, axis=0, keepdims=True)
    cnt_ref[...] = jnp.broadcast_to(carry[0:1, :], cnt_ref.shape)

    ri = jnp.where(lane == 0, id1, jnp.where(lane == 1, id2, jnp.where(lane == 2, rank1, rank2)))
    ri_ref[...] = ri[:, 0:SUBLANES].astype(jnp.int32)
    rf = jnp.where(lane == 0, w1, w2)
    rf_ref[...] = rf[:, 0:SUBLANES]


def _outproj(attn, conv, x, woa, woc, gffn, wr, br):
    n = x.shape[0]
    tm = OUT_TILE
    tri = jnp.tril(jnp.ones((tm, tm), F32), k=-1).astype(BF16)

    def full(a):
        return pl.BlockSpec(a.shape, lambda i: (0,) * a.ndim)

    rows = lambda width: pl.BlockSpec((tm, width), lambda i: (i, 0))
    return pl.pallas_call(
        functools.partial(_outproj_kernel, tm),
        grid=(n // tm,),
        in_specs=[rows(ATTN_WIDTH), rows(CONV_WIDTH), rows(D_MODEL)] + [
            full(a) for a in (woa, woc, gffn, wr, br, tri)],
        out_specs=[rows(D_MODEL), rows(D_MODEL), rows(SUBLANES), rows(SUBLANES),
                   pl.BlockSpec((SUBLANES, LANES), lambda i: (0, 0))],
        out_shape=[jax.ShapeDtypeStruct((n, D_MODEL), F32), jax.ShapeDtypeStruct((n, D_MODEL), F32),
                   jax.ShapeDtypeStruct((n, SUBLANES), jnp.int32), jax.ShapeDtypeStruct((n, SUBLANES), F32),
                   jax.ShapeDtypeStruct((SUBLANES, LANES), F32)],
        scratch_shapes=[pltpu.VMEM((SUBLANES, LANES), F32)],
        compiler_params=pltpu.CompilerParams(
            dimension_semantics=("arbitrary",), vmem_limit_bytes=VMEM_LIMIT),
        name="outproj_router",
    )(attn, conv, x, woa, woc, gffn, wr, br, tri)


def _row_copy(src_hbm, row, dst_rows, r, sem):
    return pltpu.make_async_copy(src_hbm.at[pl.ds(row, 1), :], dst_rows.at[pl.ds(r, 1), :], sem)


def _expert_kernel(bm, be_ref, nu_ref, rowtok_ref, xt_hbm, wg_ref, wu_ref, wd_ref, ys_ref, xbuf, sem):
    j = pl.program_id(0)
    n_used = nu_ref[0]

    @pl.when(j < n_used)
    def _():
        slot = j % 2

        def issue(r, carry):
            _row_copy(xt_hbm, rowtok_ref[0, 0, r], xbuf.at[slot], r, sem.at[slot]).start()
            return carry

        lax.fori_loop(0, bm, issue, 0, unroll=8)

    @pl.when((j >= 1) & (j - 1 < n_used))
    def _():
        slot = (j - 1) % 2

        def wait(r, carry):
            _row_copy(xt_hbm, 0, xbuf.at[slot], r, sem.at[slot]).wait()
            return carry

        lax.fori_loop(0, bm, wait, 0, unroll=8)
        x = xbuf[slot]
        hg = jnp.dot(x, wg_ref[0], preferred_element_type=F32)
        hu = jnp.dot(x, wu_ref[0], preferred_element_type=F32)
        hdn = hg * (1.0 / (1.0 + jnp.exp(-hg))) * hu
        ys_ref[...] = jnp.dot(hdn, wd_ref[0], preferred_element_type=F32)

    @pl.when((j >= 1) & (j - 1 >= n_used))
    def _():
        ys_ref[...] = jnp.zeros(ys_ref.shape, F32)


def _experts(xt, row_tok, block_expert, n_used, w_gate, w_up, w_down):
    bm = EXPERT_ROWS
    nb = row_tok.shape[0]
    prev = lambda j: jnp.maximum(j - 1, 0)
    grid_spec = pltpu.PrefetchScalarGridSpec(
        num_scalar_prefetch=2,
        grid=(nb + 1,),
        in_specs=[
            pl.BlockSpec((1, 1, bm), lambda j, be, nu: (jnp.minimum(j, nb - 1), 0, 0),
                         memory_space=pltpu.SMEM),
            pl.BlockSpec(memory_space=pl.ANY),
            pl.BlockSpec((1, D_MODEL, D_EXPERT), lambda j, be, nu: (be[prev(j)], 0, 0)),
            pl.BlockSpec((1, D_MODEL, D_EXPERT), lambda j, be, nu: (be[prev(j)], 0, 0)),
            pl.BlockSpec((1, D_EXPERT, D_MODEL), lambda j, be, nu: (be[prev(j)], 0, 0)),
        ],
        out_specs=pl.BlockSpec((bm, D_MODEL), lambda j, be, nu: (prev(j), 0)),
        scratch_shapes=[pltpu.VMEM((2, bm, D_MODEL), F32), pltpu.SemaphoreType.DMA((2,))],
    )
    return pl.pallas_call(
        functools.partial(_expert_kernel, bm),
        grid_spec=grid_spec,
        out_shape=jax.ShapeDtypeStruct((nb * bm, D_MODEL), F32),
        compiler_params=pltpu.CompilerParams(
            dimension_semantics=("arbitrary",), vmem_limit_bytes=VMEM_LIMIT),
        name="moe_experts",
    )(block_expert, n_used, row_tok, xt, w_gate, w_up, w_down)


def _combine_kernel(tm, d1_ref, d2_ref, ys_hbm, h_ref, rf_ref, o_ref, ybuf, sem):
    i = pl.program_id(0)
    nt = pl.num_programs(0) - 1

    @pl.when(i < nt)
    def _():
        slot = i % 2

        def issue(r, carry):
            _row_copy(ys_hbm, d1_ref[0, 0, r], ybuf.at[0, slot], r, sem.at[0, slot]).start()
            _row_copy(ys_hbm, d2_ref[0, 0, r], ybuf.at[1, slot], r, sem.at[1, slot]).start()
            return carry

        lax.fori_loop(0, tm, issue, 0, unroll=8)

    @pl.when(i >= 1)
    def _():
        slot = (i - 1) % 2

        def wait(r, carry):
            _row_copy(ys_hbm, 0, ybuf.at[0, slot], r, sem.at[0, slot]).wait()
            _row_copy(ys_hbm, 0, ybuf.at[1, slot], r, sem.at[1, slot]).wait()
            return carry

        lax.fori_loop(0, tm, wait, 0, unroll=8)
        rf = rf_ref[...]
        o_ref[...] = h_ref[...] + rf[:, 0:1] * ybuf[0, slot] + rf[:, 1:2] * ybuf[1, slot]


def _combine(ys, dest1, dest2, h, rf):
    n = h.shape[0]
    tm = COMBINE_TILE
    nt = n // tm
    cur = lambda i: jnp.minimum(i, nt - 1)
    prev = lambda i: jnp.maximum(i - 1, 0)
    return pl.pallas_call(
        functools.partial(_combine_kernel, tm),
        grid=(nt + 1,),
        in_specs=[
            pl.BlockSpec((1, 1, tm), lambda i: (cur(i), 0, 0), memory_space=pltpu.SMEM),
            pl.BlockSpec((1, 1, tm), lambda i: (cur(i), 0, 0), memory_space=pltpu.SMEM),
            pl.BlockSpec(memory_space=pl.ANY),
            pl.BlockSpec((tm, D_MODEL), lambda i: (prev(i), 0)),
            pl.BlockSpec((tm, SUBLANES), lambda i: (prev(i), 0)),
        ],
        out_specs=pl.BlockSpec((tm, D_MODEL), lambda i: (prev(i), 0)),
        out_shape=jax.ShapeDtypeStruct((n, D_MODEL), F32),
        scratch_shapes=[pltpu.VMEM((2, 2, tm, D_MODEL), F32), pltpu.SemaphoreType.DMA((2, 2))],
        compiler_params=pltpu.CompilerParams(
            dimension_semantics=("arbitrary",), vmem_limit_bytes=VMEM_LIMIT),
        name="moe_combine",
    )(dest1.reshape(nt, 1, tm), dest2.reshape(nt, 1, tm), ys, h, rf)


def _pad_lanes(a, width=LANES):
    return jnp.pad(a, ((0, 0), (0, width - a.shape[-1])))


def _layer(x, meta_tokens, g_mix, w_in, b_forget, q_norm_g, k_norm_g, conv_w, attn_out_g, conv_out_g,
           w_out, g_ffn, w_rg, b_rg, w_re, b_re, w_gate, w_up, w_down):
    bsz, slen, _ = x.shape
    n = bsz * slen
    w = ATTN_WIDTH

    wq, wk, wv, wf, wb, wc, whc = jnp.split(
        w_in, [w, 2 * w, 3 * w, 3 * w + ATTN_HEADS, 4 * w + ATTN_HEADS, 5 * w + ATTN_HEADS], axis=1)
    w1 = jnp.concatenate([wq, wk, wv, wb, wc, whc, _pad_lanes(wf)], axis=1).astype(BF16)
    head_of = jnp.arange(w) // HEAD_DIM
    gmat = (head_of[:, None] == head_of[None, :]).astype(BF16)
    consts = (
        g_mix.reshape(1, D_MODEL), w1, _pad_lanes(b_forget.reshape(1, ATTN_HEADS)),
        jnp.tile(q_norm_g, ATTN_HEADS).reshape(1, w) * (HEAD_DIM ** -0.5),
        jnp.tile(k_norm_g, ATTN_HEADS).reshape(1, w),
        jnp.pad(conv_w, ((0, SUBLANES - CONV_K), (0, 0))), conv_out_g.reshape(1, w), gmat)

    xm = jnp.pad(meta_tokens, ((0, META_TILE - N_META), (0, 0)))[None]
    km, vm, cum_m, uc_m = _inproj(xm, True, META_TILE, consts,
                                  jnp.zeros((SUBLANES, w), F32), jnp.zeros((1, LANES), F32))
    km, vm = km[0, :N_META], vm[0, :N_META]
    cum_m = cum_m[0, :N_META]
    halo = jnp.zeros((SUBLANES, w), F32).at[SUBLANES - 2:].set(uc_m[0, N_META - 2:N_META])
    cum0 = cum_m[N_META - 1:N_META]
    ckm = jnp.zeros((HEAD_PAIRS, SUBLANES, LANES), F32).at[:, 0:2, 0:N_META].set(
        cum_m[:, :ATTN_HEADS].T.reshape(HEAD_PAIRS, 2, N_META))

    q, k, v, conv, cump, ckt = _inproj(x, False, IN_TILE, consts, halo, cum0)
    attn = _attention(q, k, v, cump, ckt, km, vm, ckm, attn_out_g.reshape(1, w))

    wr = _pad_lanes(jnp.concatenate([w_rg, w_re], axis=1))
    br = _pad_lanes(jnp.concatenate([b_rg, b_re]).reshape(1, -1))
    wo = w_out.astype(BF16)
    h, xt, ri, rf, cnt = _outproj(attn.reshape(n, w), conv.reshape(n, w), x.reshape(n, D_MODEL),
                                  wo[:w], wo[w:], g_ffn.reshape(1, D_MODEL), wr, br)

    bm = EXPERT_ROWS
    nb = (n * TOP_K) // bm + N_EXPERTS
    counts = cnt[0, :N_EXPERTS].astype(jnp.int32)
    blocks_e = (counts + bm - 1) // bm
    blk_end = jnp.cumsum(blocks_e)
    row_start = (blk_end - blocks_e) * bm
    dest1 = row_start[ri[:, 0]] + ri[:, 2]
    dest2 = row_start[ri[:, 1]] + ri[:, 3]
    tok = jnp.arange(n, dtype=jnp.int32)
    row_tok = jnp.zeros((nb * bm,), jnp.int32).at[dest1].set(tok).at[dest2].set(tok)
    block_expert = jnp.minimum(
        jnp.sum(blk_end[None, :] <= jnp.arange(nb, dtype=jnp.int32)[:, None], axis=1), N_EXPERTS - 1
    ).astype(jnp.int32)
    n_used = blk_end[-1:].astype(jnp.int32)

    ys = _experts(xt, row_tok.reshape(nb, 1, bm), block_expert, n_used, w_gate, w_up, w_down)
    out = _combine(ys, dest1, dest2, h, rf)
    return out.reshape(bsz, slen, D_MODEL)


def kernel(x, meta_tokens, norm_mix_g, w_in, b_forget, q_norm_g, k_norm_g, conv_w, attn_out_g, conv_out_g,
           w_out, norm_ffn_g, w_router_group, b_router_group, w_router_expert, b_router_expert,
           w_gate, w_up, w_down):
    assert norm_mix_g.shape[0] == 1, "single-layer block"
    return _layer(x, meta_tokens, norm_mix_g[0], w_in[0], b_forget[0], q_norm_g[0], k_norm_g[0], conv_w[0],
                  attn_out_g[0], conv_out_g[0], w_out[0], norm_ffn_g[0], w_router_group[0],
                  b_router_group[0], w_router_expert[0], b_router_expert[0], w_gate[0], w_up[0], w_down[0])
```

```python
import functools

import jax
import jax.numpy as jnp
from jax import lax
from jax.experimental import pallas as pl
from jax.experimental.pallas import tpu as pltpu

D_MODEL = 1024
N_META = 16
HEAD_DIM = 64
ATTN_HEADS = 8
ATTN_WIDTH = ATTN_HEADS * HEAD_DIM
CONV_WIDTH = D_MODEL - ATTN_WIDTH
CONV_K = 3
N_EXPERT_GROUPS = 4
EXPERTS_PER_GROUP = 8
N_EXPERTS = N_EXPERT_GROUPS * EXPERTS_PER_GROUP
TOP_K = 2
D_EXPERT = 512
EPS = 1e-6
MASK_VALUE = -1e30
LOG2E = 1.4426950408889634
AUG = 6

LANES = 128
SUBLANES = 8
HEAD_PAIRS = ATTN_HEADS * HEAD_DIM // LANES
PROJ_PAD_COLS = 5 * ATTN_WIDTH + LANES
VMEM_LIMIT = 56 * 1024 * 1024

IN_TILE = 512
META_TILE = 128
ATTN_TILE = 512
OUT_TILE = 512
EXPERT_ROWS = 256
COMBINE_TILE = 256

F32 = jnp.float32
BF16 = jnp.bfloat16


def _lane_iota(shape):
    return lax.broadcasted_iota(jnp.int32, shape, len(shape) - 1)


def _inproj_kernel(is_meta, tm, x_ref, gmix_ref, w1_ref, wvt_ref, bf_ref, gq_ref, gk_ref, cw_ref, gco_ref,
                   gmat_ref, tri_ref, eq_ref, ek_ref, halo_ref, cum0_ref, *rest):
    if is_meta:
        k_ref, vt_ref, kaug_ref, cum_ref, uc_ref, ucbuf, carry = rest
    else:
        q_ref, k_ref, vt_ref, conv_ref, qaug_ref, kaug_ref, ucbuf, carry = rest
    t = pl.program_id(1)

    @pl.when(t == 0)
    def _():
        ucbuf[0:SUBLANES, :] = halo_ref[...]
        carry[0:1, :] = cum0_ref[...]

    x = x_ref[0]
    ms = jnp.mean(x * x, axis=-1, keepdims=True)
    u = (x * lax.rsqrt(ms + EPS)) * gmix_ref[...]
    ub = u.astype(BF16)
    proj = jnp.dot(ub, w1_ref[...], preferred_element_type=F32)
    vt_ref[0, 0] = lax.dot_general(wvt_ref[...], ub, (((1,), (1,)), ((), ())),
                                   preferred_element_type=F32).astype(BF16)

    def head_norm(z, g):
        ssq = jnp.dot((z * z).astype(BF16), gmat_ref[...], preferred_element_type=F32)
        return z * lax.rsqrt(ssq * (1.0 / HEAD_DIM) + EPS) * g

    w = ATTN_WIDTH
    kn = head_norm(proj[:, w:2 * w], gk_ref[...])
    k_ref[0] = kn.astype(BF16)

    z = proj[:, 5 * w:5 * w + LANES] + bf_ref[...]
    ls = jnp.minimum(z, 0.0) - jnp.log1p(jnp.exp(-jnp.abs(z)))
    ls = jnp.where(_lane_iota(ls.shape) < ATTN_HEADS, ls, 0.0)
    hi = ls.astype(BF16)
    r1 = ls - hi.astype(F32)
    mid = r1.astype(BF16)
    lo = (r1 - mid.astype(F32)).astype(BF16)
    tri = tri_ref[...]
    cs = (jnp.dot(tri, hi, preferred_element_type=F32)
          + jnp.dot(tri, mid, preferred_element_type=F32)
          + jnp.dot(tri, lo, preferred_element_type=F32))
    cum = cs + carry[0:1, :]
    carry[0:1, :] = cum[tm - 1:tm, :]

    c2 = cum * LOG2E
    c_hi = c2.astype(BF16).astype(F32)
    r_1 = c2 - c_hi
    c_mid = r_1.astype(BF16).astype(F32)
    c_lo = (r_1 - c_mid).astype(BF16).astype(F32)
    lane_c = _lane_iota(c2.shape)
    packed = jnp.where(lane_c < ATTN_HEADS, c_hi,
                       jnp.where(lane_c < 2 * ATTN_HEADS, pltpu.roll(c_mid, ATTN_HEADS, axis=1),
                                 pltpu.roll(c_lo, 2 * ATTN_HEADS, axis=1))).astype(BF16)
    in_aug = lane_c < AUG * ATTN_HEADS
    ones_k = in_aug & ((lane_c % AUG) < AUG // 2)
    kaug = jnp.where(ones_k, 1.0, -jnp.dot(packed, ek_ref[...], preferred_element_type=F32))
    kaug_ref[0] = kaug.astype(BF16)

    uc = proj[:, 3 * w:4 * w] * proj[:, 4 * w:5 * w]
    ucbuf[SUBLANES:SUBLANES + tm, :] = uc
    uc1 = ucbuf[SUBLANES - 1:SUBLANES - 1 + tm, :]
    uc2 = ucbuf[SUBLANES - 2:SUBLANES - 2 + tm, :]
    ucbuf[0:SUBLANES, :] = uc[tm - SUBLANES:tm, :]

    if is_meta:
        cum_ref[0] = cum
        uc_ref[0] = uc
        return

    qn = head_norm(proj[:, 0:w], gq_ref[...])
    q_ref[0] = qn.astype(BF16)
    y = cw_ref[0:1, :] * uc2 + cw_ref[1:2, :] * uc1 + cw_ref[2:3, :] * uc
    conv = proj[:, 2 * w:3 * w] * y
    conv_ref[0] = head_norm(conv, gco_ref[...]).astype(BF16)
    ones_q = in_aug & ((lane_c % AUG) >= AUG // 2)
    qaug = jnp.where(ones_q, 1.0, jnp.dot(packed, eq_ref[...], preferred_element_type=F32))
    qaug_ref[0] = qaug.astype(BF16)


def _inproj(x, is_meta, tm, consts, halo, cum0):
    bsz, tlen, _ = x.shape
    nt = tlen // tm
    gmix, w1, wvt, bfp, gq, gk, cw, gco, gmat = consts
    tri = jnp.tril(jnp.ones((tm, tm), F32)).astype(BF16)
    w = ATTN_WIDTH
    src = jnp.arange(LANES)
    dst = jnp.arange(LANES)
    piece, head = src // ATTN_HEADS, src % ATTN_HEADS
    valid = src < 3 * ATTN_HEADS
    eq = (valid[:, None] & (dst[None, :] == (AUG * head + piece)[:, None])).astype(BF16)
    ek = (valid[:, None] & (dst[None, :] == (AUG * head + AUG // 2 + piece)[:, None])).astype(BF16)

    def full(a):
        return pl.BlockSpec(a.shape, lambda b, t: (0,) * a.ndim)

    in_specs = [pl.BlockSpec((1, tm, D_MODEL), lambda b, t: (b, t, 0))] + [
        full(a) for a in (gmix, w1, wvt, bfp, gq, gk, cw, gco, gmat, tri, eq, ek, halo, cum0)]
    tok = lambda width: pl.BlockSpec((1, tm, width), lambda b, t: (b, t, 0))
    vt_spec = pl.BlockSpec((1, 1, w, tm), lambda b, t: (b, t, 0, 0))
    if is_meta:
        out_shape = [jax.ShapeDtypeStruct((bsz, tlen, w), BF16),
                     jax.ShapeDtypeStruct((bsz, nt, w, tm), BF16),
                     jax.ShapeDtypeStruct((bsz, tlen, LANES), BF16),
                     jax.ShapeDtypeStruct((bsz, tlen, LANES), F32),
                     jax.ShapeDtypeStruct((bsz, tlen, w), F32)]
        out_specs = [tok(w), vt_spec, tok(LANES), tok(LANES), tok(w)]
    else:
        out_shape = [jax.ShapeDtypeStruct((bsz, tlen, w), BF16)] * 2 + [
            jax.ShapeDtypeStruct((bsz, nt, w, tm), BF16),
            jax.ShapeDtypeStruct((bsz, tlen, w), BF16),
            jax.ShapeDtypeStruct((bsz, tlen, LANES), BF16),
            jax.ShapeDtypeStruct((bsz, tlen, LANES), BF16)]
        out_specs = [tok(w), tok(w), vt_spec, tok(w), tok(LANES), tok(LANES)]
    return pl.pallas_call(
        functools.partial(_inproj_kernel, is_meta, tm),
        grid=(bsz, nt),
        in_specs=in_specs,
        out_specs=out_specs,
        out_shape=out_shape,
        scratch_shapes=[pltpu.VMEM((tm + SUBLANES, w), F32), pltpu.VMEM((SUBLANES, LANES), F32)],
        compiler_params=pltpu.CompilerParams(
            dimension_semantics=("arbitrary", "arbitrary"), vmem_limit_bytes=VMEM_LIMIT),
        name="inproj_meta" if is_meta else "inproj",
    )(x, gmix, w1, wvt, bfp, gq, gk, cw, gco, gmat, tri, eq, ek, halo, cum0)


def _attn_kernel(tq, q_ref, qaug_ref, k_ref, kaug_ref, vt_ref, km_ref, kaugm_ref, vmt_ref, gao_ref,
                 o_ref, m_sc, l_sc, acc_sc, sa_sc, sb_sc):
    hp = pl.program_id(1)
    qi = pl.program_id(2)
    q = q_ref[0]
    qaug = qaug_ref[0]
    lane = _lane_iota(q.shape)
    first = lane < HEAD_DIM
    zero = jnp.zeros_like(q)
    qcat = []
    for hh in range(2):
        qh = jnp.where(first, q, zero) if hh == 0 else jnp.where(first, zero, q)
        lo = AUG * (2 * hp + hh)
        qa = jnp.where((lane >= lo) & (lane < lo + AUG), qaug, zero)
        qcat.append(jnp.concatenate([qh, qa], axis=1))

    m_sc[...] = jnp.full(m_sc.shape, MASK_VALUE, F32)
    l_sc[...] = jnp.zeros(l_sc.shape, F32)
    acc_sc[...] = jnp.zeros(acc_sc.shape, F32)

    def scores_t(kcat):
        return tuple(lax.dot_general(kcat, qcat[hh], (((1,), (1,)), ((), ())), preferred_element_type=F32)
                     for hh in range(2))

    def update(hh, st, vtb):
        m_old = m_sc[hh]
        m_new = jnp.maximum(m_old, jnp.max(st, axis=0, keepdims=True))
        alpha = jnp.exp2(m_old - m_new)
        p = jnp.exp2(st - m_new)
        l_sc[hh] = alpha * l_sc[hh] + jnp.sum(p, axis=0, keepdims=True)
        vh = vtb[hh * HEAD_DIM:(hh + 1) * HEAD_DIM, :]
        acc_sc[hh] = alpha * acc_sc[hh] + jnp.dot(vh, p.astype(BF16), preferred_element_type=F32)
        m_sc[hh] = m_new

    st_m = scores_t(jnp.concatenate([km_ref[...], kaugm_ref[...]], axis=1))
    vmt = vmt_ref[...]
    for hh in range(2):
        update(hh, st_m[hh], vmt)

    def put_scores(j, buf):
        start = pl.multiple_of(j * tq, tq)
        sts = scores_t(jnp.concatenate([k_ref[0, pl.ds(start, tq), :], kaug_ref[0, pl.ds(start, tq), :]], axis=1))
        for hh in range(2):
            buf[hh] = sts[hh]

    def consume(buf, j, masked):
        vtb = vt_ref[0, j]
        for hh in range(2):
            st = buf[hh]
            if masked:
                key = lax.broadcasted_iota(jnp.int32, st.shape, 0)
                qry = lax.broadcasted_iota(jnp.int32, st.shape, 1)
                st = jnp.where(key <= qry, st, MASK_VALUE)
            update(hh, st, vtb)

    put_scores(0, sa_sc)

    def body(t, carry):
        j = 2 * t
        put_scores(j + 1, sb_sc)
        consume(sa_sc, j, False)
        put_scores(j + 2, sa_sc)
        consume(sb_sc, j + 1, False)
        return carry

    lax.fori_loop(0, lax.shift_right_logical(qi, 1), body, 0)
    odd = (qi & 1) == 1

    @pl.when(odd)
    def _():
        put_scores(qi, sb_sc)
        consume(sa_sc, qi - 1, False)
        consume(sb_sc, qi, True)

    @pl.when(jnp.logical_not(odd))
    def _():
        consume(sa_sc, qi, True)

    ot = jnp.concatenate([acc_sc[0] / l_sc[0], acc_sc[1] / l_sc[1]], axis=0)
    o2 = ot * ot
    ms0 = jnp.sum(o2[0:HEAD_DIM], axis=0, keepdims=True) * (1.0 / HEAD_DIM)
    ms1 = jnp.sum(o2[HEAD_DIM:], axis=0, keepdims=True) * (1.0 / HEAD_DIM)
    inv = jnp.concatenate([jnp.broadcast_to(lax.rsqrt(ms0 + EPS), (HEAD_DIM, tq)),
                           jnp.broadcast_to(lax.rsqrt(ms1 + EPS), (HEAD_DIM, tq))], axis=0)
    o_ref[0] = (ot * inv * gao_ref[...]).T.astype(BF16)


def _attention(q, qaug, k, kaug, vt, km, kaugm, vmt, gao):
    bsz, slen, w = q.shape
    tq = ATTN_TILE
    nq = slen // tq
    return pl.pallas_call(
        functools.partial(_attn_kernel, tq),
        grid=(bsz, HEAD_PAIRS, nq),
        in_specs=[
            pl.BlockSpec((1, tq, LANES), lambda b, p, i: (b, i, p)),
            pl.BlockSpec((1, tq, LANES), lambda b, p, i: (b, i, 0)),
            pl.BlockSpec((1, slen, LANES), lambda b, p, i: (b, 0, p)),
            pl.BlockSpec((1, slen, LANES), lambda b, p, i: (b, 0, 0)),
            pl.BlockSpec((1, nq, LANES, tq), lambda b, p, i: (b, 0, p, 0)),
            pl.BlockSpec((N_META, LANES), lambda b, p, i: (0, p)),
            pl.BlockSpec((N_META, LANES), lambda b, p, i: (0, 0)),
            pl.BlockSpec((LANES, N_META), lambda b, p, i: (p, 0)),
            pl.BlockSpec((LANES, 1), lambda b, p, i: (p, 0)),
        ],
        out_specs=pl.BlockSpec((1, tq, LANES), lambda b, p, i: (b, i, p)),
        out_shape=jax.ShapeDtypeStruct((bsz, slen, w), BF16),
        scratch_shapes=[pltpu.VMEM((2, 1, tq), F32), pltpu.VMEM((2, 1, tq), F32),
                        pltpu.VMEM((2, HEAD_DIM, tq), F32),
                        pltpu.VMEM((2, tq, tq), F32), pltpu.VMEM((2, tq, tq), F32)],
        compiler_params=pltpu.CompilerParams(
            dimension_semantics=("arbitrary", "arbitrary", "arbitrary"), vmem_limit_bytes=VMEM_LIMIT),
        name="fox_attention",
    )(q, qaug, k, kaug, vt, km, kaugm, vmt, gao)


def _outproj_kernel(tm, attn_ref, conv_ref, x_ref, woa_ref, woc_ref, gffn_ref, wr_ref, br_ref, tri_ref,
                    h_ref, xt_ref, ri_ref, rf_ref, cnt_ref, carry):
    i = pl.program_id(0)

    @pl.when(i == 0)
    def _():
        carry[...] = jnp.zeros(carry.shape, F32)

    h = (x_ref[...]
         + jnp.dot(attn_ref[...], woa_ref[...], preferred_element_type=F32)
         + jnp.dot(conv_ref[...], woc_ref[...], preferred_element_type=F32))
    h_ref[...] = h
    ms = jnp.mean(h * h, axis=-1, keepdims=True)
    xt = (h * lax.rsqrt(ms + EPS)) * gffn_ref[...]
    xt_ref[...] = xt

    logits = jnp.dot(xt, wr_ref[...], preferred_element_type=F32,
                     precision=lax.Precision.HIGHEST) + br_ref[...]
    lane = _lane_iota(logits.shape)
    lanef = lane.astype(F32)
    big = float(LANES)

    def first_argmax(vals, vmax):
        return jnp.min(jnp.where(vals == vmax, lanef, big), axis=1, keepdims=True)

    is_g = lane < N_EXPERT_GROUPS
    gl = jnp.where(is_g, logits, MASK_VALUE)
    gmax = jnp.max(gl, axis=1, keepdims=True)
    gidx = first_argmax(gl, gmax)
    g_p = 1.0 / jnp.sum(jnp.where(is_g, jnp.exp(gl - gmax), 0.0), axis=1, keepdims=True)

    base = N_EXPERT_GROUPS + EXPERTS_PER_GROUP * gidx
    in_grp = (lanef >= base) & (lanef < base + EXPERTS_PER_GROUP)
    el = jnp.where(in_grp, logits, MASK_VALUE)
    l1 = jnp.max(el, axis=1, keepdims=True)
    e1 = first_argmax(el, l1)
    el2 = jnp.where(lanef == e1, MASK_VALUE, el)
    l2 = jnp.max(el2, axis=1, keepdims=True)
    e2 = first_argmax(el2, l2)
    zsum = jnp.sum(jnp.where(in_grp, jnp.exp(el - l1), 0.0), axis=1, keepdims=True)
    p1 = 1.0 / zsum
    p2 = jnp.exp(l2 - l1) / zsum
    den = p1 + p2
    w1 = g_p * p1 / den
    w2 = g_p * p2 / den
    id1 = e1 - N_EXPERT_GROUPS
    id2 = e2 - N_EXPERT_GROUPS

    oh1 = lanef == id1
    oh2 = lanef == id2
    oh = jnp.where(oh1 | oh2, 1.0, 0.0)
    before = jnp.dot(tri_ref[...], oh.astype(BF16), preferred_element_type=F32) + carry[0:1, :]
    rank1 = jnp.sum(jnp.where(oh1, before, 0.0), axis=1, keepdims=True)
    rank2 = jnp.sum(jnp.where(oh2, before, 0.0), axis=1, keepdims=True)
    carry[0:1, :] = carry[0:1, :] + jnp.sum(oh, axis=0, keepdims=True)
    cnt_ref[...] = jnp.broadcast_to(carry[0:1, :], cnt_ref.shape)

    ri = jnp.where(lane == 0, id1, jnp.where(lane == 1, id2, jnp.where(lane == 2, rank1, rank2)))
    ri_ref[...] = ri[:, 0:SUBLANES].astype(jnp.int32)
    rf = jnp.where(lane == 0, w1, w2)
    rf_ref[...] = rf[:, 0:SUBLANES]


def _outproj(attn, conv, x, woa, woc, gffn, wr, br):
    n = x.shape[0]
    tm = OUT_TILE
    tri = jnp.tril(jnp.ones((tm, tm), F32), k=-1).astype(BF16)

    def full(a):
        return pl.BlockSpec(a.shape, lambda i: (0,) * a.ndim)

    rows = lambda width: pl.BlockSpec((tm, width), lambda i: (i, 0))
    return pl.pallas_call(
        functools.partial(_outproj_kernel, tm),
        grid=(n // tm,),
        in_specs=[rows(ATTN_WIDTH), rows(CONV_WIDTH), rows(D_MODEL)] + [
            full(a) for a in (woa, woc, gffn, wr, br, tri)],
        out_specs=[rows(D_MODEL), rows(D_MODEL), rows(SUBLANES), rows(SUBLANES),
                   pl.BlockSpec((SUBLANES, LANES), lambda i: (0, 0))],
        out_shape=[jax.ShapeDtypeStruct((n, D_MODEL), F32), jax.ShapeDtypeStruct((n, D_MODEL), F32),
                   jax.ShapeDtypeStruct((n, SUBLANES), jnp.int32), jax.ShapeDtypeStruct((n, SUBLANES), F32),
                   jax.ShapeDtypeStruct((SUBLANES, LANES), F32)],
        scratch_shapes=[pltpu.VMEM((SUBLANES, LANES), F32)],
        compiler_params=pltpu.CompilerParams(
            dimension_semantics=("arbitrary",), vmem_limit_bytes=VMEM_LIMIT),
        name="outproj_router",
    )(attn, conv, x, woa, woc, gffn, wr, br, tri)


def _row_copy(src_hbm, row, dst_rows, r, sem):
    return pltpu.make_async_copy(src_hbm.at[pl.ds(row, 1), :], dst_rows.at[pl.ds(r, 1), :], sem)


def _expert_kernel(bm, be_ref, nu_ref, rowtok_ref, xt_hbm, wg_ref, wu_ref, wd_ref, ys_ref, xbuf, sem):
    j = pl.program_id(0)
    n_used = nu_ref[0]

    @pl.when(j < n_used)
    def _():
        slot = j % 2

        def issue(r, carry):
            _row_copy(xt_hbm, rowtok_ref[0, 0, r], xbuf.at[slot], r, sem.at[slot]).start()
            return carry

        lax.fori_loop(0, bm, issue, 0, unroll=8)

    @pl.when((j >= 1) & (j - 1 < n_used))
    def _():
        slot = (j - 1) % 2

        def wait(r, carry):
            _row_copy(xt_hbm, 0, xbuf.at[slot], r, sem.at[slot]).wait()
            return carry

        lax.fori_loop(0, bm, wait, 0, unroll=8)
        x = xbuf[slot]
        hg = jnp.dot(x, wg_ref[0], preferred_element_type=F32)
        hu = jnp.dot(x, wu_ref[0], preferred_element_type=F32)
        hdn = hg * (1.0 / (1.0 + jnp.exp(-hg))) * hu
        ys_ref[...] = jnp.dot(hdn, wd_ref[0], preferred_element_type=F32)

    @pl.when((j >= 1) & (j - 1 >= n_used))
    def _():
        ys_ref[...] = jnp.zeros(ys_ref.shape, F32)


def _experts(xt, row_tok, block_expert, n_used, w_gate, w_up, w_down):
    bm = EXPERT_ROWS
    nb = row_tok.shape[0]
    prev = lambda j: jnp.maximum(j - 1, 0)
    grid_spec = pltpu.PrefetchScalarGridSpec(
        num_scalar_prefetch=2,
        grid=(nb + 1,),
        in_specs=[
            pl.BlockSpec((1, 1, bm), lambda j, be, nu: (jnp.minimum(j, nb - 1), 0, 0),
                         memory_space=pltpu.SMEM),
            pl.BlockSpec(memory_space=pl.ANY),
            pl.BlockSpec((1, D_MODEL, D_EXPERT), lambda j, be, nu: (be[prev(j)], 0, 0)),
            pl.BlockSpec((1, D_MODEL, D_EXPERT), lambda j, be, nu: (be[prev(j)], 0, 0)),
            pl.BlockSpec((1, D_EXPERT, D_MODEL), lambda j, be, nu: (be[prev(j)], 0, 0)),
        ],
        out_specs=pl.BlockSpec((bm, D_MODEL), lambda j, be, nu: (prev(j), 0)),
        scratch_shapes=[pltpu.VMEM((2, bm, D_MODEL), F32), pltpu.SemaphoreType.DMA((2,))],
    )
    return pl.pallas_call(
        functools.partial(_expert_kernel, bm),
        grid_spec=grid_spec,
        out_shape=jax.ShapeDtypeStruct((nb * bm, D_MODEL), F32),
        compiler_params=pltpu.CompilerParams(
            dimension_semantics=("arbitrary",), vmem_limit_bytes=VMEM_LIMIT),
        name="moe_experts",
    )(block_expert, n_used, row_tok, xt, w_gate, w_up, w_down)


def _combine_kernel(tm, d1_ref, d2_ref, ys_hbm, h_ref, rf_ref, o_ref, ybuf, sem):
    i = pl.program_id(0)
    nt = pl.num_programs(0) - 1

    @pl.when(i < nt)
    def _():
        slot = i % 2

        def issue(r, carry):
            _row_copy(ys_hbm, d1_ref[0, 0, r], ybuf.at[0, slot], r, sem.at[0, slot]).start()
            _row_copy(ys_hbm, d2_ref[0, 0, r], ybuf.at[1, slot], r, sem.at[1, slot]).start()
            return carry

        lax.fori_loop(0, tm, issue, 0, unroll=8)

    @pl.when(i >= 1)
    def _():
        slot = (i - 1) % 2

        def wait(r, carry):
            _row_copy(ys_hbm, 0, ybuf.at[0, slot], r, sem.at[0, slot]).wait()
            _row_copy(ys_hbm, 0, ybuf.at[1, slot], r, sem.at[1, slot]).wait()
            return carry

        lax.fori_loop(0, tm, wait, 0, unroll=8)
        rf = rf_ref[...]
        o_ref[...] = h_ref[...] + rf[:, 0:1] * ybuf[0, slot] + rf[:, 1:2] * ybuf[1, slot]


def _combine(ys, dest1, dest2, h, rf):
    n = h.shape[0]
    tm = COMBINE_TILE
    nt = n // tm
    cur = lambda i: jnp.minimum(i, nt - 1)
    prev = lambda i: jnp.maximum(i - 1, 0)
    return pl.pallas_call(
        functools.partial(_combine_kernel, tm),
        grid=(nt + 1,),
        in_specs=[
            pl.BlockSpec((1, 1, tm), lambda i: (cur(i), 0, 0), memory_space=pltpu.SMEM),
            pl.BlockSpec((1, 1, tm), lambda i: (cur(i), 0, 0), memory_space=pltpu.SMEM),
            pl.BlockSpec(memory_space=pl.ANY),
            pl.BlockSpec((tm, D_MODEL), lambda i: (prev(i), 0)),
            pl.BlockSpec((tm, SUBLANES), lambda i: (prev(i), 0)),
        ],
        out_specs=pl.BlockSpec((tm, D_MODEL), lambda i: (prev(i), 0)),
        out_shape=jax.ShapeDtypeStruct((n, D_MODEL), F32),
        scratch_shapes=[pltpu.VMEM((2, 2, tm, D_MODEL), F32), pltpu.SemaphoreType.DMA((2, 2))],
        compiler_params=pltpu.CompilerParams(
            dimension_semantics=("arbitrary",), vmem_limit_bytes=VMEM_LIMIT),
        name="moe_combine",
    )(dest1.reshape(nt, 1, tm), dest2.reshape(nt, 1, tm), ys, h, rf)


def _pad_lanes(a, width=LANES):
    return jnp.pad(a, ((0, 0), (0, width - a.shape[-1])))


def _layer(x, meta_tokens, g_mix, w_in, b_forget, q_norm_g, k_norm_g, conv_w, attn_out_g, conv_out_g,
           w_out, g_ffn, w_rg, b_rg, w_re, b_re, w_gate, w_up, w_down):
    bsz, slen, _ = x.shape
    n = bsz * slen
    w = ATTN_WIDTH

    wq, wk, wv, wf, wb, wc, whc = jnp.split(
        w_in, [w, 2 * w, 3 * w, 3 * w + ATTN_HEADS, 4 * w + ATTN_HEADS, 5 * w + ATTN_HEADS], axis=1)
    w1 = jnp.concatenate([wq, wk, wb, wc, whc, _pad_lanes(wf)], axis=1).astype(BF16)
    head_of = jnp.arange(w) // HEAD_DIM
    gmat = (head_of[:, None] == head_of[None, :]).astype(BF16)
    consts = (
        g_mix.reshape(1, D_MODEL), w1, wv.T.astype(BF16), _pad_lanes(b_forget.reshape(1, ATTN_HEADS)),
        jnp.tile(q_norm_g, ATTN_HEADS).reshape(1, w) * (HEAD_DIM ** -0.5 * LOG2E),
        jnp.tile(k_norm_g, ATTN_HEADS).reshape(1, w),
        jnp.pad(conv_w, ((0, SUBLANES - CONV_K), (0, 0))), conv_out_g.reshape(1, w), gmat)

    xm = jnp.pad(meta_tokens, ((0, META_TILE - N_META), (0, 0)))[None]
    km, vmt, kaugm, cum_m, uc_m = _inproj(xm, True, META_TILE, consts,
                                          jnp.zeros((SUBLANES, w), F32), jnp.zeros((1, LANES), F32))
    km, vmt, kaugm = km[0, :N_META], vmt[0, 0, :, :N_META], kaugm[0, :N_META]
    cum_m = cum_m[0, :N_META]
    halo = jnp.zeros((SUBLANES, w), F32).at[SUBLANES - 2:].set(uc_m[0, N_META - 2:N_META])
    cum0 = cum_m[N_META - 1:N_META]

    q, k, vt, conv, qaug, kaug = _inproj(x, False, IN_TILE, consts, halo, cum0)
    attn = _attention(q, qaug, k, kaug, vt, km, kaugm, vmt, attn_out_g.reshape(w, 1))

    wr = _pad_lanes(jnp.concatenate([w_rg, w_re], axis=1))
    br = _pad_lanes(jnp.concatenate([b_rg, b_re]).reshape(1, -1))
    wo = w_out.astype(BF16)
    h, xt, ri, rf, cnt = _outproj(attn.reshape(n, w), conv.reshape(n, w), x.reshape(n, D_MODEL),
                                  wo[:w], wo[w:], g_ffn.reshape(1, D_MODEL), wr, br)

    bm = EXPERT_ROWS
    nb = (n * TOP_K) // bm + N_EXPERTS
    counts = cnt[0, :N_EXPERTS].astype(jnp.int32)
    blocks_e = (counts + bm - 1) // bm
    blk_end = jnp.cumsum(blocks_e)
    row_start = (blk_end - blocks_e) * bm
    dest1 = row_start[ri[:, 0]] + ri[:, 2]
    dest2 = row_start[ri[:, 1]] + ri[:, 3]
    tok = jnp.arange(n, dtype=jnp.int32)
    row_tok = jnp.zeros((nb * bm,), jnp.int32).at[dest1].set(tok).at[dest2].set(tok)
    block_expert = jnp.minimum(
        jnp.sum(blk_end[None, :] <= jnp.arange(nb, dtype=jnp.int32)[:, None], axis=1), N_EXPERTS - 1
    ).astype(jnp.int32)
    n_used = blk_end[-1:].astype(jnp.int32)

    ys = _experts(xt, row_tok.reshape(nb, 1, bm), block_expert, n_used, w_gate, w_up, w_down)
    out = _combine(ys, dest1, dest2, h, rf)
    return out.reshape(bsz, slen, D_MODEL)


def kernel(x, meta_tokens, norm_mix_g, w_in, b_forget, q_norm_g, k_norm_g, conv_w, attn_out_g, conv_out_g,
           w_out, norm_ffn_g, w_router_group, b_router_group, w_router_expert, b_router_expert,
           w_gate, w_up, w_down):
    assert norm_mix_g.shape[0] == 1, "single-layer block"
    return _layer(x, meta_tokens, norm_mix_g[0], w_in[0], b_forget[0], q_norm_g[0], k_norm_g[0], conv_w[0],
                  attn_out_g[0], conv_out_g[0], w_out[0], norm_ffn_g[0], w_router_group[0],
                  b_router_group[0], w_router_expert[0], b_router_expert[0], w_gate[0], w_up[0], w_down[0])
```

```python
import functools

import jax
import jax.numpy as jnp
from jax import lax
from jax.experimental import pallas as pl
from jax.experimental.pallas import tpu as pltpu

D_MODEL = 1024
N_META = 16
HEAD_DIM = 64
ATTN_HEADS = 8
ATTN_WIDTH = ATTN_HEADS * HEAD_DIM
CONV_WIDTH = D_MODEL - ATTN_WIDTH
CONV_K = 3
N_EXPERT_GROUPS = 4
EXPERTS_PER_GROUP = 8
N_EXPERTS = N_EXPERT_GROUPS * EXPERTS_PER_GROUP
TOP_K = 2
D_EXPERT = 512
EPS = 1e-6
MASK_VALUE = -1e30
LOG2E = 1.4426950408889634
AUG = 6

LANES = 128
SUBLANES = 8
HEAD_PAIRS = ATTN_HEADS * HEAD_DIM // LANES
PROJ_PAD_COLS = 5 * ATTN_WIDTH + LANES
VMEM_LIMIT = 56 * 1024 * 1024

IN_TILE = 512
META_TILE = 128
ATTN_TILE = 512
OUT_TILE = 512
EXPERT_ROWS = 256
ROUTE_TILE = 256
DMA_GROUP = 8

F32 = jnp.float32
BF16 = jnp.bfloat16


def _lane_iota(shape):
    return lax.broadcasted_iota(jnp.int32, shape, len(shape) - 1)


def _inproj_kernel(is_meta, tm, x_ref, gmix_ref, w1_ref, wvt_ref, bf_ref, gq_ref, gk_ref, cw_ref, gco_ref,
                   gmat_ref, tri_ref, eq_ref, ek_ref, halo_ref, cum0_ref, *rest):
    if is_meta:
        k_ref, vt_ref, kaug_ref, cum_ref, uc_ref, ucbuf, carry = rest
    else:
        q_ref, k_ref, vt_ref, conv_ref, qaug_ref, kaug_ref, ucbuf, carry = rest
    t = pl.program_id(1)

    @pl.when(t == 0)
    def _():
        ucbuf[0:SUBLANES, :] = halo_ref[...]
        carry[0:1, :] = cum0_ref[...]

    x = x_ref[0]
    ms = jnp.mean(x * x, axis=-1, keepdims=True)
    u = (x * lax.rsqrt(ms + EPS)) * gmix_ref[...]
    ub = u.astype(BF16)
    proj = jnp.dot(ub, w1_ref[...], preferred_element_type=F32)
    vt_ref[0, 0] = lax.dot_general(wvt_ref[...], ub, (((1,), (1,)), ((), ())),
                                   preferred_element_type=F32).astype(BF16)

    def head_norm(z, g):
        ssq = jnp.dot((z * z).astype(BF16), gmat_ref[...], preferred_element_type=F32)
        return z * lax.rsqrt(ssq * (1.0 / HEAD_DIM) + EPS) * g

    w = ATTN_WIDTH
    kn = head_norm(proj[:, w:2 * w], gk_ref[...])
    k_ref[0] = kn.astype(BF16)

    z = proj[:, 5 * w:5 * w + LANES] + bf_ref[...]
    ls = jnp.minimum(z, 0.0) - jnp.log1p(jnp.exp(-jnp.abs(z)))
    ls = jnp.where(_lane_iota(ls.shape) < ATTN_HEADS, ls, 0.0)
    hi = ls.astype(BF16)
    r1 = ls - hi.astype(F32)
    mid = r1.astype(BF16)
    lo = (r1 - mid.astype(F32)).astype(BF16)
    tri = tri_ref[...]
    cs = (jnp.dot(tri, hi, preferred_element_type=F32)
          + jnp.dot(tri, mid, preferred_element_type=F32)
          + jnp.dot(tri, lo, preferred_element_type=F32))
    cum = cs + carry[0:1, :]
    carry[0:1, :] = cum[tm - 1:tm, :]

    c2 = cum * LOG2E
    c_hi = c2.astype(BF16).astype(F32)
    r_1 = c2 - c_hi
    c_mid = r_1.astype(BF16).astype(F32)
    c_lo = (r_1 - c_mid).astype(BF16).astype(F32)
    lane_c = _lane_iota(c2.shape)
    packed = jnp.where(lane_c < ATTN_HEADS, c_hi,
                       jnp.where(lane_c < 2 * ATTN_HEADS, pltpu.roll(c_mid, ATTN_HEADS, axis=1),
                                 pltpu.roll(c_lo, 2 * ATTN_HEADS, axis=1))).astype(BF16)
    in_aug = lane_c < AUG * ATTN_HEADS
    ones_k = in_aug & ((lane_c % AUG) < AUG // 2)
    kaug = jnp.where(ones_k, 1.0, -jnp.dot(packed, ek_ref[...], preferred_element_type=F32))
    kaug_ref[0] = kaug.astype(BF16)

    uc = proj[:, 3 * w:4 * w] * proj[:, 4 * w:5 * w]
    ucbuf[SUBLANES:SUBLANES + tm, :] = uc
    uc1 = ucbuf[SUBLANES - 1:SUBLANES - 1 + tm, :]
    uc2 = ucbuf[SUBLANES - 2:SUBLANES - 2 + tm, :]
    ucbuf[0:SUBLANES, :] = uc[tm - SUBLANES:tm, :]

    if is_meta:
        cum_ref[0] = cum
        uc_ref[0] = uc
        return

    qn = head_norm(proj[:, 0:w], gq_ref[...])
    q_ref[0] = qn.astype(BF16)
    y = cw_ref[0:1, :] * uc2 + cw_ref[1:2, :] * uc1 + cw_ref[2:3, :] * uc
    conv = proj[:, 2 * w:3 * w] * y
    conv_ref[0] = head_norm(conv, gco_ref[...]).astype(BF16)
    ones_q = in_aug & ((lane_c % AUG) >= AUG // 2)
    qaug = jnp.where(ones_q, 1.0, jnp.dot(packed, eq_ref[...], preferred_element_type=F32))
    qaug_ref[0] = qaug.astype(BF16)


def _inproj(x, is_meta, tm, consts, halo, cum0):
    bsz, tlen, _ = x.shape
    nt = tlen // tm
    gmix, w1, wvt, bfp, gq, gk, cw, gco, gmat = consts
    tri = jnp.tril(jnp.ones((tm, tm), F32)).astype(BF16)
    w = ATTN_WIDTH
    src = jnp.arange(LANES)
    dst = jnp.arange(LANES)
    piece, head = src // ATTN_HEADS, src % ATTN_HEADS
    valid = src < 3 * ATTN_HEADS
    eq = (valid[:, None] & (dst[None, :] == (AUG * head + piece)[:, None])).astype(BF16)
    ek = (valid[:, None] & (dst[None, :] == (AUG * head + AUG // 2 + piece)[:, None])).astype(BF16)

    def full(a):
        return pl.BlockSpec(a.shape, lambda b, t: (0,) * a.ndim)

    in_specs = [pl.BlockSpec((1, tm, D_MODEL), lambda b, t: (b, t, 0))] + [
        full(a) for a in (gmix, w1, wvt, bfp, gq, gk, cw, gco, gmat, tri, eq, ek, halo, cum0)]
    tok = lambda width: pl.BlockSpec((1, tm, width), lambda b, t: (b, t, 0))
    vt_spec = pl.BlockSpec((1, 1, w, tm), lambda b, t: (b, t, 0, 0))
    if is_meta:
        out_shape = [jax.ShapeDtypeStruct((bsz, tlen, w), BF16),
                     jax.ShapeDtypeStruct((bsz, nt, w, tm), BF16),
                     jax.ShapeDtypeStruct((bsz, tlen, LANES), BF16),
                     jax.ShapeDtypeStruct((bsz, tlen, LANES), F32),
                     jax.ShapeDtypeStruct((bsz, tlen, w), F32)]
        out_specs = [tok(w), vt_spec, tok(LANES), tok(LANES), tok(w)]
    else:
        out_shape = [jax.ShapeDtypeStruct((bsz, tlen, w), BF16)] * 2 + [
            jax.ShapeDtypeStruct((bsz, nt, w, tm), BF16),
            jax.ShapeDtypeStruct((bsz, tlen, w), BF16),
            jax.ShapeDtypeStruct((bsz, tlen, LANES), BF16),
            jax.ShapeDtypeStruct((bsz, tlen, LANES), BF16)]
        out_specs = [tok(w), tok(w), vt_spec, tok(w), tok(LANES), tok(LANES)]
    return pl.pallas_call(
        functools.partial(_inproj_kernel, is_meta, tm),
        grid=(bsz, nt),
        in_specs=in_specs,
        out_specs=out_specs,
        out_shape=out_shape,
        scratch_shapes=[pltpu.VMEM((tm + SUBLANES, w), F32), pltpu.VMEM((SUBLANES, LANES), F32)],
        compiler_params=pltpu.CompilerParams(
            dimension_semantics=("arbitrary", "arbitrary"), vmem_limit_bytes=VMEM_LIMIT),
        name="inproj_meta" if is_meta else "inproj",
    )(x, gmix, w1, wvt, bfp, gq, gk, cw, gco, gmat, tri, eq, ek, halo, cum0)


def _attn_kernel(tq, q_ref, qaug_ref, k_ref, kaug_ref, vt_ref, km_ref, kaugm_ref, vmt_ref, gao_ref,
                 o_ref, m_sc, l_sc, acc_sc, sa_sc, sb_sc):
    hp = pl.program_id(1)
    qi = pl.program_id(2)
    q = q_ref[0]
    qaug = qaug_ref[0]
    lane = _lane_iota(q.shape)
    first = lane < HEAD_DIM
    zero = jnp.zeros_like(q)
    qcat = []
    for hh in range(2):
        qh = jnp.where(first, q, zero) if hh == 0 else jnp.where(first, zero, q)
        lo = AUG * (2 * hp + hh)
        qa = jnp.where((lane >= lo) & (lane < lo + AUG), qaug, zero)
        qcat.append(jnp.concatenate([qh, qa], axis=1))

    m_sc[...] = jnp.full(m_sc.shape, MASK_VALUE, F32)
    l_sc[...] = jnp.zeros(l_sc.shape, F32)
    acc_sc[...] = jnp.zeros(acc_sc.shape, F32)

    def scores_t(kcat):
        return tuple(lax.dot_general(kcat, qcat[hh], (((1,), (1,)), ((), ())), preferred_element_type=F32)
                     for hh in range(2))

    def update(hh, st, vtb):
        m_old = m_sc[hh]
        m_new = jnp.maximum(m_old, jnp.max(st, axis=0, keepdims=True))
        alpha = jnp.exp2(m_old - m_new)
        p = jnp.exp2(st - m_new)
        l_sc[hh] = alpha * l_sc[hh] + jnp.sum(p, axis=0, keepdims=True)
        vh = vtb[hh * HEAD_DIM:(hh + 1) * HEAD_DIM, :]
        acc_sc[hh] = alpha * acc_sc[hh] + jnp.dot(vh, p.astype(BF16), preferred_element_type=F32)
        m_sc[hh] = m_new

    st_m = scores_t(jnp.concatenate([km_ref[...], kaugm_ref[...]], axis=1))
    vmt = vmt_ref[...]
    for hh in range(2):
        update(hh, st_m[hh], vmt)

    def put_scores(j, buf):
        start = pl.multiple_of(j * tq, tq)
        sts = scores_t(jnp.concatenate([k_ref[0, pl.ds(start, tq), :], kaug_ref[0, pl.ds(start, tq), :]], axis=1))
        for hh in range(2):
            buf[hh] = sts[hh]

    def consume(buf, j, masked):
        vtb = vt_ref[0, j]
        for hh in range(2):
            st = buf[hh]
            if masked:
                key = lax.broadcasted_iota(jnp.int32, st.shape, 0)
                qry = lax.broadcasted_iota(jnp.int32, st.shape, 1)
                st = jnp.where(key <= qry, st, MASK_VALUE)
            update(hh, st, vtb)

    put_scores(0, sa_sc)

    def body(t, carry):
        j = 2 * t
        put_scores(j + 1, sb_sc)
        consume(sa_sc, j, False)
        put_scores(j + 2, sa_sc)
        consume(sb_sc, j + 1, False)
        return carry

    lax.fori_loop(0, lax.shift_right_logical(qi, 1), body, 0)
    odd = (qi & 1) == 1

    @pl.when(odd)
    def _():
        put_scores(qi, sb_sc)
        consume(sa_sc, qi - 1, False)
        consume(sb_sc, qi, True)

    @pl.when(jnp.logical_not(odd))
    def _():
        consume(sa_sc, qi, True)

    ot = jnp.concatenate([acc_sc[0] / l_sc[0], acc_sc[1] / l_sc[1]], axis=0)
    o2 = ot * ot
    ms0 = jnp.sum(o2[0:HEAD_DIM], axis=0, keepdims=True) * (1.0 / HEAD_DIM)
    ms1 = jnp.sum(o2[HEAD_DIM:], axis=0, keepdims=True) * (1.0 / HEAD_DIM)
    inv = jnp.concatenate([jnp.broadcast_to(lax.rsqrt(ms0 + EPS), (HEAD_DIM, tq)),
                           jnp.broadcast_to(lax.rsqrt(ms1 + EPS), (HEAD_DIM, tq))], axis=0)
    o_ref[0] = (ot * inv * gao_ref[...]).T.astype(BF16)


def _attention(q, qaug, k, kaug, vt, km, kaugm, vmt, gao):
    bsz, slen, w = q.shape
    tq = ATTN_TILE
    nq = slen // tq
    return pl.pallas_call(
        functools.partial(_attn_kernel, tq),
        grid=(bsz, HEAD_PAIRS, nq),
        in_specs=[
            pl.BlockSpec((1, tq, LANES), lambda b, p, i: (b, i, p)),
            pl.BlockSpec((1, tq, LANES), lambda b, p, i: (b, i, 0)),
            pl.BlockSpec((1, slen, LANES), lambda b, p, i: (b, 0, p)),
            pl.BlockSpec((1, slen, LANES), lambda b, p, i: (b, 0, 0)),
            pl.BlockSpec((1, nq, LANES, tq), lambda b, p, i: (b, 0, p, 0)),
            pl.BlockSpec((N_META, LANES), lambda b, p, i: (0, p)),
            pl.BlockSpec((N_META, LANES), lambda b, p, i: (0, 0)),
            pl.BlockSpec((LANES, N_META), lambda b, p, i: (p, 0)),
            pl.BlockSpec((LANES, 1), lambda b, p, i: (p, 0)),
        ],
        out_specs=pl.BlockSpec((1, tq, LANES), lambda b, p, i: (b, i, p)),
        out_shape=jax.ShapeDtypeStruct((bsz, slen, w), BF16),
        scratch_shapes=[pltpu.VMEM((2, 1, tq), F32), pltpu.VMEM((2, 1, tq), F32),
                        pltpu.VMEM((2, HEAD_DIM, tq), F32),
                        pltpu.VMEM((2, tq, tq), F32), pltpu.VMEM((2, tq, tq), F32)],
        compiler_params=pltpu.CompilerParams(
            dimension_semantics=("arbitrary", "arbitrary", "arbitrary"), vmem_limit_bytes=VMEM_LIMIT),
        name="fox_attention",
    )(q, qaug, k, kaug, vt, km, kaugm, vmt, gao)


def _outproj_kernel(tm, attn_ref, conv_ref, x_ref, woa_ref, woc_ref, gffn_ref, wr_ref, br_ref, tri_ref,
                    h_ref, xt_ref, ri_ref, rf_ref, cnt_ref, carry):
    i = pl.program_id(0)

    @pl.when(i == 0)
    def _():
        carry[...] = jnp.zeros(carry.shape, F32)

    h = (x_ref[...]
         + jnp.dot(attn_ref[...], woa_ref[...], preferred_element_type=F32)
         + jnp.dot(conv_ref[...], woc_ref[...], preferred_element_type=F32))
    h_ref[...] = h
    ms = jnp.mean(h * h, axis=-1, keepdims=True)
    xt = (h * lax.rsqrt(ms + EPS)) * gffn_ref[...]
    for sl in range(SUBLANES):
        xt_ref[pl.ds(sl, tm, stride=SUBLANES), :] = xt[:, sl * LANES:(sl + 1) * LANES]

    logits = jnp.dot(xt, wr_ref[...], preferred_element_type=F32,
                     precision=lax.Precision.HIGHEST) + br_ref[...]
    lane = _lane_iota(logits.shape)
    lanef = lane.astype(F32)
    big = float(LANES)

    def first_argmax(vals, vmax):
        return jnp.min(jnp.where(vals == vmax, lanef, big), axis=1, keepdims=True)

    is_g = lane < N_EXPERT_GROUPS
    gl = jnp.where(is_g, logits, MASK_VALUE)
    gmax = jnp.max(gl, axis=1, keepdims=True)
    gidx = first_argmax(gl, gmax)
    g_p = 1.0 / jnp.sum(jnp.where(is_g, jnp.exp(gl - gmax), 0.0), axis=1, keepdims=True)

    base = N_EXPERT_GROUPS + EXPERTS_PER_GROUP * gidx
    in_grp = (lanef >= base) & (lanef < base + EXPERTS_PER_GROUP)
    el = jnp.where(in_grp, logits, MASK_VALUE)
    l1 = jnp.max(el, axis=1, keepdims=True)
    e1 = first_argmax(el, l1)
    el2 = jnp.where(lanef == e1, MASK_VALUE, el)
    l2 = jnp.max(el2, axis=1, keepdims=True)
    e2 = first_argmax(el2, l2)
    zsum = jnp.sum(jnp.where(in_grp, jnp.exp(el - l1), 0.0), axis=1, keepdims=True)
    p1 = 1.0 / zsum
    p2 = jnp.exp(l2 - l1) / zsum
    den = p1 + p2
    w1 = g_p * p1 / den
    w2 = g_p * p2 / den
    id1 = e1 - N_EXPERT_GROUPS
    id2 = e2 - N_EXPERT_GROUPS

    oh1 = lanef == id1
    oh2 = lanef == id2
    oh = jnp.where(oh1 | oh2, 1.0, 0.0)
    before = jnp.dot(tri_ref[...], oh.astype(BF16), preferred_element_type=F32) + carry[0:1, :]
    rank1 = jnp.sum(jnp.where(oh1, before, 0.0), axis=1, keepdims=True)
    rank2 = jnp.sum(jnp.where(oh2, before, 0.0), axis=1, keepdims=True)
    carry[0:1, :] = carry[0:1, :] + jnp.sum(oh, axis=0, keepdims=True)
    cnt_ref[...] = jnp.broadcast_to(carry[0:1, :], cnt_ref.shape)

    ri = jnp.where(lane == 0, id1, jnp.where(lane == 1, id2, jnp.where(lane == 2, rank1, rank2)))
    ri_ref[0] = ri.T[0:SUBLANES, :].astype(jnp.int32)
    rf = jnp.where(lane == 0, w1, w2)
    rf_ref[...] = rf[:, 0:SUBLANES]


def _outproj(attn, conv, x, woa, woc, gffn, wr, br):
    n = x.shape[0]
    tm = OUT_TILE
    tri = jnp.tril(jnp.ones((tm, tm), F32), k=-1).astype(BF16)

    def full(a):
        return pl.BlockSpec(a.shape, lambda i: (0,) * a.ndim)

    rows = lambda width: pl.BlockSpec((tm, width), lambda i: (i, 0))
    return pl.pallas_call(
        functools.partial(_outproj_kernel, tm),
        grid=(n // tm,),
        in_specs=[rows(ATTN_WIDTH), rows(CONV_WIDTH), rows(D_MODEL)] + [
            full(a) for a in (woa, woc, gffn, wr, br, tri)],
        out_specs=[rows(D_MODEL), pl.BlockSpec((tm * SUBLANES, LANES), lambda i: (i, 0)),
                   pl.BlockSpec((1, SUBLANES, tm), lambda i: (i, 0, 0)), rows(SUBLANES),
                   pl.BlockSpec((SUBLANES, LANES), lambda i: (0, 0))],
        out_shape=[jax.ShapeDtypeStruct((n, D_MODEL), F32), jax.ShapeDtypeStruct((n * SUBLANES, LANES), F32),
                   jax.ShapeDtypeStruct((n // tm, SUBLANES, tm), jnp.int32),
                   jax.ShapeDtypeStruct((n, SUBLANES), F32),
                   jax.ShapeDtypeStruct((SUBLANES, LANES), F32)],
        scratch_shapes=[pltpu.VMEM((SUBLANES, LANES), F32)],
        compiler_params=pltpu.CompilerParams(
            dimension_semantics=("arbitrary",), vmem_limit_bytes=VMEM_LIMIT),
        name="outproj_router",
    )(attn, conv, x, woa, woc, gffn, wr, br, tri)


def _dest_kernel(rs_ref, ri_ref, dest_ref):
    ri = ri_ref[0]
    experts = ri[0:TOP_K, :]
    start = jnp.zeros_like(experts)
    for e in range(N_EXPERTS):
        start = jnp.where(experts == e, rs_ref[e], start)
    dest_ref[0] = jnp.concatenate([start + ri[TOP_K:2 * TOP_K, :],
                                   jnp.zeros((SUBLANES - TOP_K, ri.shape[1]), jnp.int32)], axis=0)


def _dest_rows(ri_t, row_start):
    nt, _, tm = ri_t.shape
    grid_spec = pltpu.PrefetchScalarGridSpec(
        num_scalar_prefetch=1,
        grid=(nt,),
        in_specs=[pl.BlockSpec((1, SUBLANES, tm), lambda i, rs: (i, 0, 0))],
        out_specs=pl.BlockSpec((1, SUBLANES, tm), lambda i, rs: (i, 0, 0)),
    )
    return pl.pallas_call(
        _dest_kernel,
        grid_spec=grid_spec,
        out_shape=jax.ShapeDtypeStruct(ri_t.shape, jnp.int32),
        compiler_params=pltpu.CompilerParams(dimension_semantics=("arbitrary",)),
        name="moe_dest",
    )(row_start, ri_t)


def _tile_copy(src, src_row, dst, dst_row, sem):
    return pltpu.make_async_copy(src.at[pl.ds(pl.multiple_of(src_row * SUBLANES, SUBLANES), SUBLANES), :],
                                 dst.at[pl.ds(pl.multiple_of(dst_row * SUBLANES, SUBLANES), SUBLANES), :], sem)


def _dispatch_kernel(tm, bm, fz_ref, fv_ref, dest_ref, xt_hbm, xs_hbm, zeros, sem, zsem):
    i = pl.program_id(0)
    nt = pl.num_programs(0) - 1
    block_rows = bm * SUBLANES

    def zero_block(z):
        start = pl.multiple_of(fz_ref[z] * block_rows, block_rows)
        return pltpu.make_async_copy(zeros, xs_hbm.at[pl.ds(start, block_rows), :], zsem.at[0])

    @pl.when(i == 0)
    def _():
        zeros[...] = jnp.zeros(zeros.shape, F32)
        for z in range(fz_ref.shape[0]):
            @pl.when(fv_ref[z] == 1)
            def _():
                zero_block(z).start()
        for z in range(fz_ref.shape[0]):
            @pl.when(fv_ref[z] == 1)
            def _():
                zero_block(z).wait()

    @pl.when(i < nt)
    def _():
        slot = i % 2
        base = i * tm
        for g in range(0, tm, DMA_GROUP):
            dests = [(r, dest_ref[0, kk, r]) for r in range(g, g + DMA_GROUP) for kk in range(TOP_K)]
            for r, d in dests:
                _tile_copy(xt_hbm, base + r, xs_hbm, d, sem.at[slot]).start()

    @pl.when(i >= 1)
    def _():
        slot = (i - 1) % 2
        for r in range(TOP_K * tm):
            _tile_copy(xt_hbm, 0, xs_hbm, 0, sem.at[slot]).wait()


def _route_spec(dest, tm, extra_args):
    per = dest.shape[2] // tm
    nt = dest.shape[0] * per
    if extra_args == 0:
        index = lambda i: (jnp.minimum(i, nt - 1) // per, 0, jnp.minimum(i, nt - 1) % per)
    else:
        index = lambda i, *_: (jnp.minimum(i, nt - 1) // per, 0, jnp.minimum(i, nt - 1) % per)
    return nt, pl.BlockSpec((1, SUBLANES, tm), index, memory_space=pltpu.SMEM)


def _dispatch(xt_rows, dest, fill_blocks, fill_valid, nb):
    tm = ROUTE_TILE
    bm = EXPERT_ROWS
    nt, dest_spec = _route_spec(dest, tm, 2)
    grid_spec = pltpu.PrefetchScalarGridSpec(
        num_scalar_prefetch=2,
        grid=(nt + 1,),
        in_specs=[dest_spec, pl.BlockSpec(memory_space=pl.ANY)],
        out_specs=pl.BlockSpec(memory_space=pl.ANY),
        scratch_shapes=[pltpu.VMEM((bm * SUBLANES, LANES), F32), pltpu.SemaphoreType.DMA((2,)),
                        pltpu.SemaphoreType.DMA((1,))],
    )
    return pl.pallas_call(
        functools.partial(_dispatch_kernel, tm, bm),
        grid_spec=grid_spec,
        out_shape=jax.ShapeDtypeStruct((nb * bm * SUBLANES, LANES), F32),
        compiler_params=pltpu.CompilerParams(
            dimension_semantics=("arbitrary",), vmem_limit_bytes=VMEM_LIMIT),
        name="moe_dispatch",
    )(fill_blocks, fill_valid, dest, xt_rows)


def _expert_kernel(bm, be_ref, nu_ref, xs_ref, wg_ref, wu_ref, wd_ref, ys_ref):
    j = pl.program_id(0)

    @pl.when(j < nu_ref[0])
    def _():
        x = jnp.concatenate([xs_ref[pl.ds(sl, bm, stride=SUBLANES), :] for sl in range(SUBLANES)], axis=1)
        hg = jnp.dot(x, wg_ref[0], preferred_element_type=F32)
        hu = jnp.dot(x, wu_ref[0], preferred_element_type=F32)
        hdn = hg * (1.0 / (1.0 + jnp.exp(-hg))) * hu
        y = jnp.dot(hdn, wd_ref[0], preferred_element_type=F32)
        for sl in range(SUBLANES):
            ys_ref[pl.ds(sl, bm, stride=SUBLANES), :] = y[:, sl * LANES:(sl + 1) * LANES]

    @pl.when(j >= nu_ref[0])
    def _():
        ys_ref[...] = jnp.zeros(ys_ref.shape, F32)


def _experts(xs_rows, block_expert, n_used, w_gate, w_up, w_down):
    bm = EXPERT_ROWS
    nb = block_expert.shape[0]
    used = lambda j, nu: jnp.minimum(j, nu[0] - 1)
    grid_spec = pltpu.PrefetchScalarGridSpec(
        num_scalar_prefetch=2,
        grid=(nb,),
        in_specs=[
            pl.BlockSpec((bm * SUBLANES, LANES), lambda j, be, nu: (used(j, nu), 0)),
            pl.BlockSpec((1, D_MODEL, D_EXPERT), lambda j, be, nu: (be[used(j, nu)], 0, 0)),
            pl.BlockSpec((1, D_MODEL, D_EXPERT), lambda j, be, nu: (be[used(j, nu)], 0, 0)),
            pl.BlockSpec((1, D_EXPERT, D_MODEL), lambda j, be, nu: (be[used(j, nu)], 0, 0)),
        ],
        out_specs=pl.BlockSpec((bm * SUBLANES, LANES), lambda j, be, nu: (j, 0)),
    )
    return pl.pallas_call(
        functools.partial(_expert_kernel, bm),
        grid_spec=grid_spec,
        out_shape=jax.ShapeDtypeStruct((nb * bm * SUBLANES, LANES), F32),
        compiler_params=pltpu.CompilerParams(
            dimension_semantics=("arbitrary",), vmem_limit_bytes=VMEM_LIMIT),
        name="moe_experts",
    )(block_expert, n_used, xs_rows, w_gate, w_up, w_down)


def _combine_kernel(tm, dest_ref, ys_hbm, h_ref, rf_ref, o_ref, ybuf, sem):
    i = pl.program_id(0)
    nt = pl.num_programs(0) - 1

    @pl.when(i < nt)
    def _():
        slot = i % 2
        for g in range(0, tm, DMA_GROUP):
            dests = [(r, kk, dest_ref[0, kk, r]) for r in range(g, g + DMA_GROUP) for kk in range(TOP_K)]
            for r, kk, d in dests:
                _tile_copy(ys_hbm, d, ybuf.at[kk, slot], r, sem.at[kk, slot]).start(priority=kk)

    @pl.when(i >= 1)
    def _():
        slot = (i - 1) % 2
        for r in range(tm):
            for kk in range(TOP_K):
                _tile_copy(ys_hbm, 0, ybuf.at[kk, slot], r, sem.at[kk, slot]).wait()
        rf = rf_ref[...]
        for sl in range(SUBLANES):
            cols = slice(sl * LANES, (sl + 1) * LANES)
            o_ref[:, cols] = (h_ref[:, cols]
                              + rf[:, 0:1] * ybuf[0, slot, pl.ds(sl, tm, stride=SUBLANES), :]
                              + rf[:, 1:2] * ybuf[1, slot, pl.ds(sl, tm, stride=SUBLANES), :])


def _combine(ys_rows, dest, h, rf):
    n = h.shape[0]
    tm = ROUTE_TILE
    nt, dest_spec = _route_spec(dest, tm, 0)
    prev = lambda i: jnp.maximum(i - 1, 0)
    return pl.pallas_call(
        functools.partial(_combine_kernel, tm),
        grid=(nt + 1,),
        in_specs=[
            dest_spec,
            pl.BlockSpec(memory_space=pl.ANY),
            pl.BlockSpec((tm, D_MODEL), lambda i: (prev(i), 0)),
            pl.BlockSpec((tm, SUBLANES), lambda i: (prev(i), 0)),
        ],
        out_specs=pl.BlockSpec((tm, D_MODEL), lambda i: (prev(i), 0)),
        out_shape=jax.ShapeDtypeStruct((n, D_MODEL), F32),
        scratch_shapes=[pltpu.VMEM((TOP_K, 2, tm * SUBLANES, LANES), F32), pltpu.SemaphoreType.DMA((TOP_K, 2))],
        compiler_params=pltpu.CompilerParams(
            dimension_semantics=("arbitrary",), vmem_limit_bytes=VMEM_LIMIT),
        name="moe_combine",
    )(dest, ys_rows, h, rf)


def _pad_lanes(a, width=LANES):
    return jnp.pad(a, ((0, 0), (0, width - a.shape[-1])))


def _layer(x, meta_tokens, g_mix, w_in, b_forget, q_norm_g, k_norm_g, conv_w, attn_out_g, conv_out_g,
           w_out, g_ffn, w_rg, b_rg, w_re, b_re, w_gate, w_up, w_down):
    bsz, slen, _ = x.shape
    n = bsz * slen
    w = ATTN_WIDTH

    wq, wk, wv, wf, wb, wc, whc = jnp.split(
        w_in, [w, 2 * w, 3 * w, 3 * w + ATTN_HEADS, 4 * w + ATTN_HEADS, 5 * w + ATTN_HEADS], axis=1)
    w1 = jnp.concatenate([wq, wk, wb, wc, whc, _pad_lanes(wf)], axis=1).astype(BF16)
    head_of = jnp.arange(w) // HEAD_DIM
    gmat = (head_of[:, None] == head_of[None, :]).astype(BF16)
    consts = (
        g_mix.reshape(1, D_MODEL), w1, wv.T.astype(BF16), _pad_lanes(b_forget.reshape(1, ATTN_HEADS)),
        jnp.tile(q_norm_g, ATTN_HEADS).reshape(1, w) * (HEAD_DIM ** -0.5 * LOG2E),
        jnp.tile(k_norm_g, ATTN_HEADS).reshape(1, w),
        jnp.pad(conv_w, ((0, SUBLANES - CONV_K), (0, 0))), conv_out_g.reshape(1, w), gmat)

    xm = jnp.pad(meta_tokens, ((0, META_TILE - N_META), (0, 0)))[None]
    km, vmt, kaugm, cum_m, uc_m = _inproj(xm, True, META_TILE, consts,
                                          jnp.zeros((SUBLANES, w), F32), jnp.zeros((1, LANES), F32))
    km, vmt, kaugm = km[0, :N_META], vmt[0, 0, :, :N_META], kaugm[0, :N_META]
    cum_m = cum_m[0, :N_META]
    halo = jnp.zeros((SUBLANES, w), F32).at[SUBLANES - 2:].set(uc_m[0, N_META - 2:N_META])
    cum0 = cum_m[N_META - 1:N_META]

    q, k, vt, conv, qaug, kaug = _inproj(x, False, IN_TILE, consts, halo, cum0)
    attn = _attention(q, qaug, k, kaug, vt, km, kaugm, vmt, attn_out_g.reshape(w, 1))

    wr = _pad_lanes(jnp.concatenate([w_rg, w_re], axis=1))
    br = _pad_lanes(jnp.concatenate([b_rg, b_re]).reshape(1, -1))
    wo = w_out.astype(BF16)
    h, xt, ri, rf, cnt = _outproj(attn.reshape(n, w), conv.reshape(n, w), x.reshape(n, D_MODEL),
                                  wo[:w], wo[w:], g_ffn.reshape(1, D_MODEL), wr, br)

    bm = EXPERT_ROWS
    nb = (n * TOP_K) // bm + N_EXPERTS
    counts = cnt[0, :N_EXPERTS].astype(jnp.int32)
    blocks_e = (counts + bm - 1) // bm
    blk_end = jnp.cumsum(blocks_e)
    row_start = (blk_end - blocks_e) * bm
    blk = jnp.arange(nb, dtype=jnp.int32)
    block_expert = jnp.minimum(jnp.sum(blk_end[None, :] <= blk[:, None], axis=1), N_EXPERTS - 1).astype(jnp.int32)
    n_used = blk_end[-1:].astype(jnp.int32)
    tail = n_used[0] + jnp.arange(N_EXPERTS, dtype=jnp.int32)
    fill_blocks = jnp.clip(jnp.concatenate([blk_end - 1, tail]), 0, nb - 1).astype(jnp.int32)
    fill_valid = jnp.concatenate([counts % bm != 0, tail < nb]).astype(jnp.int32)

    dest = _dest_rows(ri, row_start.astype(jnp.int32))
    xs = _dispatch(xt, dest, fill_blocks, fill_valid, nb)
    ys = _experts(xs, block_expert, n_used, w_gate, w_up, w_down)
    out = _combine(ys, dest, h, rf)
    return out.reshape(bsz, slen, D_MODEL)


def kernel(x, meta_tokens, norm_mix_g, w_in, b_forget, q_norm_g, k_norm_g, conv_w, attn_out_g, conv_out_g,
           w_out, norm_ffn_g, w_router_group, b_router_group, w_router_expert, b_router_expert,
           w_gate, w_up, w_down):
    assert norm_mix_g.shape[0] == 1, "single-layer block"
    return _layer(x, meta_tokens, norm_mix_g[0], w_in[0], b_forget[0], q_norm_g[0], k_norm_g[0], conv_w[0],
                  attn_out_g[0], conv_out_g[0], w_out[0], norm_ffn_g[0], w_router_group[0],
                  b_router_group[0], w_router_expert[0], b_router_expert[0], w_gate[0], w_up[0], w_down[0])
```

```python
import functools

import jax
import jax.numpy as jnp
from jax import lax
from jax.experimental import pallas as pl
from jax.experimental.pallas import tpu as pltpu

D_MODEL = 1024
N_META = 16
HEAD_DIM = 64
ATTN_HEADS = 8
ATTN_WIDTH = ATTN_HEADS * HEAD_DIM
CONV_WIDTH = D_MODEL - ATTN_WIDTH
CONV_K = 3
N_EXPERT_GROUPS = 4
EXPERTS_PER_GROUP = 8
N_EXPERTS = N_EXPERT_GROUPS * EXPERTS_PER_GROUP
TOP_K = 2
D_EXPERT = 512
EPS = 1e-6
MASK_VALUE = -1e30
LOG2E = 1.4426950408889634
AUG = 6

LANES = 128
SUBLANES = 8
HEAD_PAIRS = ATTN_HEADS * HEAD_DIM // LANES
PROJ_PAD_COLS = 5 * ATTN_WIDTH + LANES
VMEM_LIMIT = 56 * 1024 * 1024

IN_TILE = 512
META_TILE = 128
ATTN_TILE = 512
OUT_TILE = 512
EXPERT_ROWS = 256
ROUTE_TILE = 256
DMA_GROUP = 8

F32 = jnp.float32
BF16 = jnp.bfloat16


def _lane_iota(shape):
    return lax.broadcasted_iota(jnp.int32, shape, len(shape) - 1)


def _inproj_kernel(is_meta, tm, x_ref, gmix_ref, w1_ref, wvt_ref, bf_ref, gq_ref, gk_ref, cw_ref, gco_ref,
                   gmat_ref, tri_ref, eq_ref, ek_ref, halo_ref, cum0_ref, *rest):
    if is_meta:
        k_ref, vt_ref, kaug_ref, cum_ref, uc_ref, ucbuf, carry = rest
    else:
        q_ref, k_ref, vt_ref, conv_ref, qaug_ref, kaug_ref, ucbuf, carry = rest
    t = pl.program_id(1)

    @pl.when(t == 0)
    def _():
        ucbuf[0:SUBLANES, :] = halo_ref[...]
        carry[0:1, :] = cum0_ref[...]

    x = x_ref[0]
    ms = jnp.mean(x * x, axis=-1, keepdims=True)
    u = (x * lax.rsqrt(ms + EPS)) * gmix_ref[...]
    ub = u.astype(BF16)
    proj = jnp.dot(ub, w1_ref[...], preferred_element_type=F32)
    vt_ref[0, 0] = lax.dot_general(wvt_ref[...], ub, (((1,), (1,)), ((), ())),
                                   preferred_element_type=F32).astype(BF16)

    def head_norm(z, g):
        ssq = jnp.dot((z * z).astype(BF16), gmat_ref[...], preferred_element_type=F32)
        return z * lax.rsqrt(ssq * (1.0 / HEAD_DIM) + EPS) * g

    w = ATTN_WIDTH
    kn = head_norm(proj[:, w:2 * w], gk_ref[...])
    k_ref[0] = kn.astype(BF16)

    z = proj[:, 5 * w:5 * w + LANES] + bf_ref[...]
    ls = jnp.minimum(z, 0.0) - jnp.log1p(jnp.exp(-jnp.abs(z)))
    ls = jnp.where(_lane_iota(ls.shape) < ATTN_HEADS, ls, 0.0)
    hi = ls.astype(BF16)
    r1 = ls - hi.astype(F32)
    mid = r1.astype(BF16)
    lo = (r1 - mid.astype(F32)).astype(BF16)
    tri = tri_ref[...]
    cs = (jnp.dot(tri, hi, preferred_element_type=F32)
          + jnp.dot(tri, mid, preferred_element_type=F32)
          + jnp.dot(tri, lo, preferred_element_type=F32))
    cum = cs + carry[0:1, :]
    carry[0:1, :] = cum[tm - 1:tm, :]

    c2 = cum * LOG2E
    c_hi = c2.astype(BF16).astype(F32)
    r_1 = c2 - c_hi
    c_mid = r_1.astype(BF16).astype(F32)
    c_lo = (r_1 - c_mid).astype(BF16).astype(F32)
    lane_c = _lane_iota(c2.shape)
    packed = jnp.where(lane_c < ATTN_HEADS, c_hi,
                       jnp.where(lane_c < 2 * ATTN_HEADS, pltpu.roll(c_mid, ATTN_HEADS, axis=1),
                                 pltpu.roll(c_lo, 2 * ATTN_HEADS, axis=1))).astype(BF16)
    in_aug = lane_c < AUG * ATTN_HEADS
    ones_k = in_aug & ((lane_c % AUG) < AUG // 2)
    kaug = jnp.where(ones_k, 1.0, -jnp.dot(packed, ek_ref[...], preferred_element_type=F32))
    kaug_ref[0] = kaug.astype(BF16)

    uc = proj[:, 3 * w:4 * w] * proj[:, 4 * w:5 * w]
    ucbuf[SUBLANES:SUBLANES + tm, :] = uc
    uc1 = ucbuf[SUBLANES - 1:SUBLANES - 1 + tm, :]
    uc2 = ucbuf[SUBLANES - 2:SUBLANES - 2 + tm, :]
    ucbuf[0:SUBLANES, :] = uc[tm - SUBLANES:tm, :]

    if is_meta:
        cum_ref[0] = cum
        uc_ref[0] = uc
        return

    qn = head_norm(proj[:, 0:w], gq_ref[...])
    q_ref[0] = qn.astype(BF16)
    y = cw_ref[0:1, :] * uc2 + cw_ref[1:2, :] * uc1 + cw_ref[2:3, :] * uc
    conv = proj[:, 2 * w:3 * w] * y
    conv_ref[0] = head_norm(conv, gco_ref[...]).astype(BF16)
    ones_q = in_aug & ((lane_c % AUG) >= AUG // 2)
    qaug = jnp.where(ones_q, 1.0, jnp.dot(packed, eq_ref[...], preferred_element_type=F32))
    qaug_ref[0] = qaug.astype(BF16)


def _inproj(x, is_meta, tm, consts, halo, cum0):
    bsz, tlen, _ = x.shape
    nt = tlen // tm
    gmix, w1, wvt, bfp, gq, gk, cw, gco, gmat = consts
    tri = jnp.tril(jnp.ones((tm, tm), F32)).astype(BF16)
    w = ATTN_WIDTH
    src = jnp.arange(LANES)
    dst = jnp.arange(LANES)
    piece, head = src // ATTN_HEADS, src % ATTN_HEADS
    valid = src < 3 * ATTN_HEADS
    eq = (valid[:, None] & (dst[None, :] == (AUG * head + piece)[:, None])).astype(BF16)
    ek = (valid[:, None] & (dst[None, :] == (AUG * head + AUG // 2 + piece)[:, None])).astype(BF16)

    def full(a):
        return pl.BlockSpec(a.shape, lambda b, t: (0,) * a.ndim)

    in_specs = [pl.BlockSpec((1, tm, D_MODEL), lambda b, t: (b, t, 0))] + [
        full(a) for a in (gmix, w1, wvt, bfp, gq, gk, cw, gco, gmat, tri, eq, ek, halo, cum0)]
    tok = lambda width: pl.BlockSpec((1, tm, width), lambda b, t: (b, t, 0))
    vt_spec = pl.BlockSpec((1, 1, w, tm), lambda b, t: (b, t, 0, 0))
    if is_meta:
        out_shape = [jax.ShapeDtypeStruct((bsz, tlen, w), BF16),
                     jax.ShapeDtypeStruct((bsz, nt, w, tm), BF16),
                     jax.ShapeDtypeStruct((bsz, tlen, LANES), BF16),
                     jax.ShapeDtypeStruct((bsz, tlen, LANES), F32),
                     jax.ShapeDtypeStruct((bsz, tlen, w), F32)]
        out_specs = [tok(w), vt_spec, tok(LANES), tok(LANES), tok(w)]
    else:
        out_shape = [jax.ShapeDtypeStruct((bsz, tlen, w), BF16)] * 2 + [
            jax.ShapeDtypeStruct((bsz, nt, w, tm), BF16),
            jax.ShapeDtypeStruct((bsz, tlen, w), BF16),
            jax.ShapeDtypeStruct((bsz, tlen, LANES), BF16),
            jax.ShapeDtypeStruct((bsz, tlen, LANES), BF16)]
        out_specs = [tok(w), tok(w), vt_spec, tok(w), tok(LANES), tok(LANES)]
    return pl.pallas_call(
        functools.partial(_inproj_kernel, is_meta, tm),
        grid=(bsz, nt),
        in_specs=in_specs,
        out_specs=out_specs,
        out_shape=out_shape,
        scratch_shapes=[pltpu.VMEM((tm + SUBLANES, w), F32), pltpu.VMEM((SUBLANES, LANES), F32)],
        compiler_params=pltpu.CompilerParams(
            dimension_semantics=("arbitrary", "arbitrary"), vmem_limit_bytes=VMEM_LIMIT),
        name="inproj_meta" if is_meta else "inproj",
    )(x, gmix, w1, wvt, bfp, gq, gk, cw, gco, gmat, tri, eq, ek, halo, cum0)


def _attn_kernel(tq, q_ref, qaug_ref, k_ref, kaug_ref, vt_ref, km_ref, kaugm_ref, vmt_ref, gao_ref,
                 o_ref, m_sc, l_sc, acc_sc, sa_sc, sb_sc):
    hp = pl.program_id(1)
    qi = pl.program_id(2)
    q = q_ref[0]
    qaug = qaug_ref[0]
    lane = _lane_iota(q.shape)
    first = lane < HEAD_DIM
    zero = jnp.zeros_like(q)
    qcat = []
    for hh in range(2):
        qh = jnp.where(first, q, zero) if hh == 0 else jnp.where(first, zero, q)
        lo = AUG * (2 * hp + hh)
        qa = jnp.where((lane >= lo) & (lane < lo + AUG), qaug, zero)
        qcat.append(jnp.concatenate([qh, qa], axis=1))

    m_sc[...] = jnp.full(m_sc.shape, MASK_VALUE, F32)
    l_sc[...] = jnp.zeros(l_sc.shape, F32)
    acc_sc[...] = jnp.zeros(acc_sc.shape, F32)

    def scores_t(kcat):
        return tuple(lax.dot_general(kcat, qcat[hh], (((1,), (1,)), ((), ())), preferred_element_type=F32)
                     for hh in range(2))

    def update(hh, st, vtb):
        m_old = m_sc[hh]
        m_new = jnp.maximum(m_old, jnp.max(st, axis=0, keepdims=True))
        alpha = jnp.exp2(m_old - m_new)
        p = jnp.exp2(st - m_new)
        l_sc[hh] = alpha * l_sc[hh] + jnp.sum(p, axis=0, keepdims=True)
        vh = vtb[hh * HEAD_DIM:(hh + 1) * HEAD_DIM, :]
        acc_sc[hh] = alpha * acc_sc[hh] + jnp.dot(vh, p.astype(BF16), preferred_element_type=F32)
        m_sc[hh] = m_new

    st_m = scores_t(jnp.concatenate([km_ref[...], kaugm_ref[...]], axis=1))
    vmt = vmt_ref[...]
    for hh in range(2):
        update(hh, st_m[hh], vmt)

    def put_scores(j, buf):
        start = pl.multiple_of(j * tq, tq)
        sts = scores_t(jnp.concatenate([k_ref[0, pl.ds(start, tq), :], kaug_ref[0, pl.ds(start, tq), :]], axis=1))
        for hh in range(2):
            buf[hh] = sts[hh]

    def consume(buf, j, masked):
        vtb = vt_ref[0, j]
        for hh in range(2):
            st = buf[hh]
            if masked:
                key = lax.broadcasted_iota(jnp.int32, st.shape, 0)
                qry = lax.broadcasted_iota(jnp.int32, st.shape, 1)
                st = jnp.where(key <= qry, st, MASK_VALUE)
            update(hh, st, vtb)

    put_scores(0, sa_sc)

    def body(t, carry):
        j = 2 * t
        put_scores(j + 1, sb_sc)
        consume(sa_sc, j, False)
        put_scores(j + 2, sa_sc)
        consume(sb_sc, j + 1, False)
        return carry

    lax.fori_loop(0, lax.shift_right_logical(qi, 1), body, 0)
    odd = (qi & 1) == 1

    @pl.when(odd)
    def _():
        put_scores(qi, sb_sc)
        consume(sa_sc, qi - 1, False)
        consume(sb_sc, qi, True)

    @pl.when(jnp.logical_not(odd))
    def _():
        consume(sa_sc, qi, True)

    ot = jnp.concatenate([acc_sc[0] / l_sc[0], acc_sc[1] / l_sc[1]], axis=0)
    o2 = ot * ot
    ms0 = jnp.sum(o2[0:HEAD_DIM], axis=0, keepdims=True) * (1.0 / HEAD_DIM)
    ms1 = jnp.sum(o2[HEAD_DIM:], axis=0, keepdims=True) * (1.0 / HEAD_DIM)
    inv = jnp.concatenate([jnp.broadcast_to(lax.rsqrt(ms0 + EPS), (HEAD_DIM, tq)),
                           jnp.broadcast_to(lax.rsqrt(ms1 + EPS), (HEAD_DIM, tq))], axis=0)
    o_ref[0] = (ot * inv * gao_ref[...]).T.astype(BF16)


def _attention(q, qaug, k, kaug, vt, km, kaugm, vmt, gao):
    bsz, slen, w = q.shape
    tq = ATTN_TILE
    nq = slen // tq
    return pl.pallas_call(
        functools.partial(_attn_kernel, tq),
        grid=(bsz, HEAD_PAIRS, nq),
        in_specs=[
            pl.BlockSpec((1, tq, LANES), lambda b, p, i: (b, i, p)),
            pl.BlockSpec((1, tq, LANES), lambda b, p, i: (b, i, 0)),
            pl.BlockSpec((1, slen, LANES), lambda b, p, i: (b, 0, p)),
            pl.BlockSpec((1, slen, LANES), lambda b, p, i: (b, 0, 0)),
            pl.BlockSpec((1, nq, LANES, tq), lambda b, p, i: (b, 0, p, 0)),
            pl.BlockSpec((N_META, LANES), lambda b, p, i: (0, p)),
            pl.BlockSpec((N_META, LANES), lambda b, p, i: (0, 0)),
            pl.BlockSpec((LANES, N_META), lambda b, p, i: (p, 0)),
            pl.BlockSpec((LANES, 1), lambda b, p, i: (p, 0)),
        ],
        out_specs=pl.BlockSpec((1, tq, LANES), lambda b, p, i: (b, i, p)),
        out_shape=jax.ShapeDtypeStruct((bsz, slen, w), BF16),
        scratch_shapes=[pltpu.VMEM((2, 1, tq), F32), pltpu.VMEM((2, 1, tq), F32),
                        pltpu.VMEM((2, HEAD_DIM, tq), F32),
                        pltpu.VMEM((2, tq, tq), F32), pltpu.VMEM((2, tq, tq), F32)],
        compiler_params=pltpu.CompilerParams(
            dimension_semantics=("arbitrary", "arbitrary", "arbitrary"), vmem_limit_bytes=VMEM_LIMIT),
        name="fox_attention",
    )(q, qaug, k, kaug, vt, km, kaugm, vmt, gao)


def _outproj_kernel(tm, attn_ref, conv_ref, x_ref, woa_ref, woc_ref, gffn_ref, wr_ref, br_ref, tri_ref,
                    h_ref, xt_ref, ri_ref, rf_ref, cnt_ref, carry):
    i = pl.program_id(0)

    @pl.when(i == 0)
    def _():
        carry[...] = jnp.zeros(carry.shape, F32)

    h = (x_ref[...]
         + jnp.dot(attn_ref[...], woa_ref[...], preferred_element_type=F32)
         + jnp.dot(conv_ref[...], woc_ref[...], preferred_element_type=F32))
    h_ref[...] = h
    ms = jnp.mean(h * h, axis=-1, keepdims=True)
    xt = (h * lax.rsqrt(ms + EPS)) * gffn_ref[...]
    for sl in range(SUBLANES):
        xt_ref[pl.ds(sl, tm, stride=SUBLANES), :] = xt[:, sl * LANES:(sl + 1) * LANES]

    logits = jnp.dot(xt, wr_ref[...], preferred_element_type=F32,
                     precision=lax.Precision.HIGHEST) + br_ref[...]
    lane = _lane_iota(logits.shape)
    lanef = lane.astype(F32)
    big = float(LANES)

    def first_argmax(vals, vmax):
        return jnp.min(jnp.where(vals == vmax, lanef, big), axis=1, keepdims=True)

    is_g = lane < N_EXPERT_GROUPS
    gl = jnp.where(is_g, logits, MASK_VALUE)
    gmax = jnp.max(gl, axis=1, keepdims=True)
    gidx = first_argmax(gl, gmax)
    g_p = 1.0 / jnp.sum(jnp.where(is_g, jnp.exp(gl - gmax), 0.0), axis=1, keepdims=True)

    base = N_EXPERT_GROUPS + EXPERTS_PER_GROUP * gidx
    in_grp = (lanef >= base) & (lanef < base + EXPERTS_PER_GROUP)
    el = jnp.where(in_grp, logits, MASK_VALUE)
    l1 = jnp.max(el, axis=1, keepdims=True)
    e1 = first_argmax(el, l1)
    el2 = jnp.where(lanef == e1, MASK_VALUE, el)
    l2 = jnp.max(el2, axis=1, keepdims=True)
    e2 = first_argmax(el2, l2)
    zsum = jnp.sum(jnp.where(in_grp, jnp.exp(el - l1), 0.0), axis=1, keepdims=True)
    p1 = 1.0 / zsum
    p2 = jnp.exp(l2 - l1) / zsum
    den = p1 + p2
    w1 = g_p * p1 / den
    w2 = g_p * p2 / den
    id1 = e1 - N_EXPERT_GROUPS
    id2 = e2 - N_EXPERT_GROUPS

    oh1 = lanef == id1
    oh2 = lanef == id2
    oh = jnp.where(oh1 | oh2, 1.0, 0.0)
    before = jnp.dot(tri_ref[...], oh.astype(BF16), preferred_element_type=F32) + carry[0:1, :]
    rank1 = jnp.sum(jnp.where(oh1, before, 0.0), axis=1, keepdims=True)
    rank2 = jnp.sum(jnp.where(oh2, before, 0.0), axis=1, keepdims=True)
    carry[0:1, :] = carry[0:1, :] + jnp.sum(oh, axis=0, keepdims=True)
    cnt_ref[...] = jnp.broadcast_to(carry[0:1, :], cnt_ref.shape)

    ri = jnp.where(lane == 0, id1, jnp.where(lane == 1, id2, jnp.where(lane == 2, rank1, rank2)))
    ri_ref[0] = ri.T[0:SUBLANES, :].astype(jnp.int32)
    rf = jnp.where(lane == 0, w1, w2)
    rf_ref[...] = rf[:, 0:SUBLANES]


def _outproj(attn, conv, x, woa, woc, gffn, wr, br):
    n = x.shape[0]
    tm = OUT_TILE
    tri = jnp.tril(jnp.ones((tm, tm), F32), k=-1).astype(BF16)

    def full(a):
        return pl.BlockSpec(a.shape, lambda i: (0,) * a.ndim)

    rows = lambda width: pl.BlockSpec((tm, width), lambda i: (i, 0))
    return pl.pallas_call(
        functools.partial(_outproj_kernel, tm),
        grid=(n // tm,),
        in_specs=[rows(ATTN_WIDTH), rows(CONV_WIDTH), rows(D_MODEL)] + [
            full(a) for a in (woa, woc, gffn, wr, br, tri)],
        out_specs=[rows(D_MODEL), pl.BlockSpec((tm * SUBLANES, LANES), lambda i: (i, 0)),
                   pl.BlockSpec((1, SUBLANES, tm), lambda i: (i, 0, 0)), rows(SUBLANES),
                   pl.BlockSpec((SUBLANES, LANES), lambda i: (0, 0))],
        out_shape=[jax.ShapeDtypeStruct((n, D_MODEL), F32), jax.ShapeDtypeStruct((n * SUBLANES, LANES), F32),
                   jax.ShapeDtypeStruct((n // tm, SUBLANES, tm), jnp.int32),
                   jax.ShapeDtypeStruct((n, SUBLANES), F32),
                   jax.ShapeDtypeStruct((SUBLANES, LANES), F32)],
        scratch_shapes=[pltpu.VMEM((SUBLANES, LANES), F32)],
        compiler_params=pltpu.CompilerParams(
            dimension_semantics=("arbitrary",), vmem_limit_bytes=VMEM_LIMIT),
        name="outproj_router",
    )(attn, conv, x, woa, woc, gffn, wr, br, tri)


def _dest_kernel(rs_ref, ri_ref, dest_ref):
    ri = ri_ref[0]
    experts = ri[0:TOP_K, :]
    start = jnp.zeros_like(experts)
    for e in range(N_EXPERTS):
        start = jnp.where(experts == e, rs_ref[e], start)
    dest_ref[0] = jnp.concatenate([start + ri[TOP_K:2 * TOP_K, :],
                                   jnp.zeros((SUBLANES - TOP_K, ri.shape[1]), jnp.int32)], axis=0)


def _dest_rows(ri_t, row_start):
    nt, _, tm = ri_t.shape
    grid_spec = pltpu.PrefetchScalarGridSpec(
        num_scalar_prefetch=1,
        grid=(nt,),
        in_specs=[pl.BlockSpec((1, SUBLANES, tm), lambda i, rs: (i, 0, 0))],
        out_specs=pl.BlockSpec((1, SUBLANES, tm), lambda i, rs: (i, 0, 0)),
    )
    return pl.pallas_call(
        _dest_kernel,
        grid_spec=grid_spec,
        out_shape=jax.ShapeDtypeStruct(ri_t.shape, jnp.int32),
        compiler_params=pltpu.CompilerParams(dimension_semantics=("arbitrary",)),
        name="moe_dest",
    )(row_start, ri_t)


def _tile_copy(src, src_row, dst, dst_row, sem):
    return pltpu.make_async_copy(src.at[pl.ds(pl.multiple_of(src_row * SUBLANES, SUBLANES), SUBLANES), :],
                                 dst.at[pl.ds(pl.multiple_of(dst_row * SUBLANES, SUBLANES), SUBLANES), :], sem)


def _dispatch_kernel(tm, bm, fz_ref, fv_ref, dest_ref, xt_ref, xs_hbm, zeros, sem, zsem):
    i = pl.program_id(0)
    block_rows = bm * SUBLANES

    def zero_block(z):
        start = pl.multiple_of(fz_ref[z] * block_rows, block_rows)
        return pltpu.make_async_copy(zeros, xs_hbm.at[pl.ds(start, block_rows), :], zsem.at[0])

    @pl.when(i == 0)
    def _():
        zeros[...] = jnp.zeros(zeros.shape, F32)
        for z in range(fz_ref.shape[0]):
            @pl.when(fv_ref[z] == 1)
            def _():
                zero_block(z).start()
        for z in range(fz_ref.shape[0]):
            @pl.when(fv_ref[z] == 1)
            def _():
                zero_block(z).wait()

    for g in range(0, tm, DMA_GROUP):
        dests = [(r, kk, dest_ref[0, kk, r]) for r in range(g, g + DMA_GROUP) for kk in range(TOP_K)]
        for r, kk, d in dests:
            _tile_copy(xt_ref, r, xs_hbm, d, sem.at[0]).start(priority=kk)
    for r in range(TOP_K * tm):
        _tile_copy(xt_ref, 0, xs_hbm, 0, sem.at[0]).wait()


def _route_spec(dest, tm, extra_args):
    per = dest.shape[2] // tm
    nt = dest.shape[0] * per
    if extra_args == 0:
        index = lambda i: (jnp.minimum(i, nt - 1) // per, 0, jnp.minimum(i, nt - 1) % per)
    else:
        index = lambda i, *_: (jnp.minimum(i, nt - 1) // per, 0, jnp.minimum(i, nt - 1) % per)
    return nt, pl.BlockSpec((1, SUBLANES, tm), index, memory_space=pltpu.SMEM)


def _dispatch(xt_rows, dest, fill_blocks, fill_valid, nb):
    tm = ROUTE_TILE
    bm = EXPERT_ROWS
    nt, dest_spec = _route_spec(dest, tm, 2)
    grid_spec = pltpu.PrefetchScalarGridSpec(
        num_scalar_prefetch=2,
        grid=(nt,),
        in_specs=[dest_spec, pl.BlockSpec((tm * SUBLANES, LANES), lambda i, fz, fv: (i, 0))],
        out_specs=pl.BlockSpec(memory_space=pl.ANY),
        scratch_shapes=[pltpu.VMEM((bm * SUBLANES, LANES), F32), pltpu.SemaphoreType.DMA((1,)),
                        pltpu.SemaphoreType.DMA((1,))],
    )
    return pl.pallas_call(
        functools.partial(_dispatch_kernel, tm, bm),
        grid_spec=grid_spec,
        out_shape=jax.ShapeDtypeStruct((nb * bm * SUBLANES, LANES), F32),
        compiler_params=pltpu.CompilerParams(
            dimension_semantics=("arbitrary",), vmem_limit_bytes=VMEM_LIMIT),
        name="moe_dispatch",
    )(fill_blocks, fill_valid, dest, xt_rows)


def _expert_kernel(bm, be_ref, nu_ref, first_ref, nxt_ref, slot_ref, xs_ref, wg_hbm, wu_hbm, wd_hbm, ys_ref,
                   wg_buf, wu_buf, wd_buf, wsem):
    j = pl.program_id(0)

    def weight_copies(e, slot):
        return [pltpu.make_async_copy(src.at[e], buf.at[slot], wsem.at[m, slot])
                for m, (src, buf) in enumerate(((wg_hbm, wg_buf), (wu_hbm, wu_buf), (wd_hbm, wd_buf)))]

    @pl.when(j == 0)
    def _():
        for c in weight_copies(be_ref[0], 0):
            c.start()

    @pl.when(j < nu_ref[0])
    def _():
        slot = slot_ref[j]

        @pl.when(first_ref[j] == 1)
        def _():
            for c in weight_copies(0, slot):
                c.wait()

            @pl.when(nxt_ref[j] >= 0)
            def _():
                for c in weight_copies(nxt_ref[j], 1 - slot):
                    c.start()

        x = jnp.concatenate([xs_ref[pl.ds(sl, bm, stride=SUBLANES), :] for sl in range(SUBLANES)], axis=1)
        hg = jnp.dot(x, wg_buf[slot], preferred_element_type=F32)
        hu = jnp.dot(x, wu_buf[slot], preferred_element_type=F32)
        hdn = hg * (1.0 / (1.0 + jnp.exp(-hg))) * hu
        y = jnp.dot(hdn, wd_buf[slot], preferred_element_type=F32)
        for sl in range(SUBLANES):
            ys_ref[pl.ds(sl, bm, stride=SUBLANES), :] = y[:, sl * LANES:(sl + 1) * LANES]

    @pl.when(j >= nu_ref[0])
    def _():
        ys_ref[...] = jnp.zeros(ys_ref.shape, F32)


def _experts(xs_rows, block_expert, n_used, w_gate, w_up, w_down):
    bm = EXPERT_ROWS
    nb = block_expert.shape[0]
    blk = jnp.arange(nb, dtype=jnp.int32)
    live = blk < n_used[0]
    first = live & ((blk == 0) | (block_expert != jnp.roll(block_expert, 1)))
    slot = (jnp.cumsum(first.astype(jnp.int32)) - 1) % 2
    later_first = first[None, :] & (blk[None, :] > blk[:, None])
    nxt_start = jnp.min(jnp.where(later_first, blk[None, :], nb), axis=1)
    nxt = jnp.sum(jnp.where(blk[None, :] == nxt_start[:, None], block_expert[None, :], 0), axis=1)
    nxt = jnp.where(nxt_start < nb, nxt, -1)
    grid_spec = pltpu.PrefetchScalarGridSpec(
        num_scalar_prefetch=5,
        grid=(nb,),
        in_specs=[
            pl.BlockSpec((bm * SUBLANES, LANES), lambda j, be, nu, *_: (jnp.minimum(j, nu[0] - 1), 0)),
            pl.BlockSpec(memory_space=pl.ANY),
            pl.BlockSpec(memory_space=pl.ANY),
            pl.BlockSpec(memory_space=pl.ANY),
        ],
        out_specs=pl.BlockSpec((bm * SUBLANES, LANES), lambda j, *_: (j, 0)),
        scratch_shapes=[pltpu.VMEM((2, D_MODEL, D_EXPERT), F32), pltpu.VMEM((2, D_MODEL, D_EXPERT), F32),
                        pltpu.VMEM((2, D_EXPERT, D_MODEL), F32), pltpu.SemaphoreType.DMA((3, 2))],
    )
    return pl.pallas_call(
        functools.partial(_expert_kernel, bm),
        grid_spec=grid_spec,
        out_shape=jax.ShapeDtypeStruct((nb * bm * SUBLANES, LANES), F32),
        compiler_params=pltpu.CompilerParams(
            dimension_semantics=("arbitrary",), vmem_limit_bytes=VMEM_LIMIT),
        name="moe_experts",
    )(block_expert, n_used, first.astype(jnp.int32), nxt.astype(jnp.int32), slot.astype(jnp.int32),
      xs_rows, w_gate, w_up, w_down)


def _combine_kernel(tm, dest_ref, ys_hbm, h_ref, rf_ref, o_ref, ybuf, sem):
    i = pl.program_id(0)
    nt = pl.num_programs(0) - 1

    @pl.when(i < nt)
    def _():
        slot = i % 2
        for g in range(0, tm, DMA_GROUP):
            dests = [(r, kk, dest_ref[0, kk, r]) for r in range(g, g + DMA_GROUP) for kk in range(TOP_K)]
            for r, kk, d in dests:
                _tile_copy(ys_hbm, d, ybuf.at[kk, slot], r, sem.at[kk, slot]).start(priority=kk)

    @pl.when(i >= 1)
    def _():
        slot = (i - 1) % 2
        for r in range(tm):
            for kk in range(TOP_K):
                _tile_copy(ys_hbm, 0, ybuf.at[kk, slot], r, sem.at[kk, slot]).wait()
        rf = rf_ref[...]
        for sl in range(SUBLANES):
            cols = slice(sl * LANES, (sl + 1) * LANES)
            o_ref[:, cols] = (h_ref[:, cols]
                              + rf[:, 0:1] * ybuf[0, slot, pl.ds(sl, tm, stride=SUBLANES), :]
                              + rf[:, 1:2] * ybuf[1, slot, pl.ds(sl, tm, stride=SUBLANES), :])


def _combine(ys_rows, dest, h, rf):
    n = h.shape[0]
    tm = ROUTE_TILE
    nt, dest_spec = _route_spec(dest, tm, 0)
    prev = lambda i: jnp.maximum(i - 1, 0)
    return pl.pallas_call(
        functools.partial(_combine_kernel, tm),
        grid=(nt + 1,),
        in_specs=[
            dest_spec,
            pl.BlockSpec(memory_space=pl.ANY),
            pl.BlockSpec((tm, D_MODEL), lambda i: (prev(i), 0)),
            pl.BlockSpec((tm, SUBLANES), lambda i: (prev(i), 0)),
        ],
        out_specs=pl.BlockSpec((tm, D_MODEL), lambda i: (prev(i), 0)),
        out_shape=jax.ShapeDtypeStruct((n, D_MODEL), F32),
        scratch_shapes=[pltpu.VMEM((TOP_K, 2, tm * SUBLANES, LANES), F32), pltpu.SemaphoreType.DMA((TOP_K, 2))],
        compiler_params=pltpu.CompilerParams(
            dimension_semantics=("arbitrary",), vmem_limit_bytes=VMEM_LIMIT),
        name="moe_combine",
    )(dest, ys_rows, h, rf)


def _pad_lanes(a, width=LANES):
    return jnp.pad(a, ((0, 0), (0, width - a.shape[-1])))


def _layer(x, meta_tokens, g_mix, w_in, b_forget, q_norm_g, k_norm_g, conv_w, attn_out_g, conv_out_g,
           w_out, g_ffn, w_rg, b_rg, w_re, b_re, w_gate, w_up, w_down):
    bsz, slen, _ = x.shape
    n = bsz * slen
    w = ATTN_WIDTH

    wq, wk, wv, wf, wb, wc, whc = jnp.split(
        w_in, [w, 2 * w, 3 * w, 3 * w + ATTN_HEADS, 4 * w + ATTN_HEADS, 5 * w + ATTN_HEADS], axis=1)
    w1 = jnp.concatenate([wq, wk, wb, wc, whc, _pad_lanes(wf)], axis=1).astype(BF16)
    head_of = jnp.arange(w) // HEAD_DIM
    gmat = (head_of[:, None] == head_of[None, :]).astype(BF16)
    consts = (
        g_mix.reshape(1, D_MODEL), w1, wv.T.astype(BF16), _pad_lanes(b_forget.reshape(1, ATTN_HEADS)),
        jnp.tile(q_norm_g, ATTN_HEADS).reshape(1, w) * (HEAD_DIM ** -0.5 * LOG2E),
        jnp.tile(k_norm_g, ATTN_HEADS).reshape(1, w),
        jnp.pad(conv_w, ((0, SUBLANES - CONV_K), (0, 0))), conv_out_g.reshape(1, w), gmat)

    xm = jnp.pad(meta_tokens, ((0, META_TILE - N_META), (0, 0)))[None]
    km, vmt, kaugm, cum_m, uc_m = _inproj(xm, True, META_TILE, consts,
                                          jnp.zeros((SUBLANES, w), F32), jnp.zeros((1, LANES), F32))
    km, vmt, kaugm = km[0, :N_META], vmt[0, 0, :, :N_META], kaugm[0, :N_META]
    cum_m = cum_m[0, :N_META]
    halo = jnp.zeros((SUBLANES, w), F32).at[SUBLANES - 2:].set(uc_m[0, N_META - 2:N_META])
    cum0 = cum_m[N_META - 1:N_META]

    q, k, vt, conv, qaug, kaug = _inproj(x, False, IN_TILE, consts, halo, cum0)
    attn = _attention(q, qaug, k, kaug, vt, km, kaugm, vmt, attn_out_g.reshape(w, 1))

    wr = _pad_lanes(jnp.concatenate([w_rg, w_re], axis=1))
    br = _pad_lanes(jnp.concatenate([b_rg, b_re]).reshape(1, -1))
    wo = w_out.astype(BF16)
    h, xt, ri, rf, cnt = _outproj(attn.reshape(n, w), conv.reshape(n, w), x.reshape(n, D_MODEL),
                                  wo[:w], wo[w:], g_ffn.reshape(1, D_MODEL), wr, br)

    bm = EXPERT_ROWS
    nb = (n * TOP_K) // bm + N_EXPERTS
    counts = cnt[0, :N_EXPERTS].astype(jnp.int32)
    blocks_e = (counts + bm - 1) // bm
    blk_end = jnp.cumsum(blocks_e)
    row_start = (blk_end - blocks_e) * bm
    blk = jnp.arange(nb, dtype=jnp.int32)
    block_expert = jnp.minimum(jnp.sum(blk_end[None, :] <= blk[:, None], axis=1), N_EXPERTS - 1).astype(jnp.int32)
    n_used = blk_end[-1:].astype(jnp.int32)
    tail = n_used[0] + jnp.arange(N_EXPERTS, dtype=jnp.int32)
    fill_blocks = jnp.clip(jnp.concatenate([blk_end - 1, tail]), 0, nb - 1).astype(jnp.int32)
    fill_valid = jnp.concatenate([counts % bm != 0, tail < nb]).astype(jnp.int32)

    dest = _dest_rows(ri, row_start.astype(jnp.int32))
    xs = _dispatch(xt, dest, fill_blocks, fill_valid, nb)
    ys = _experts(xs, block_expert, n_used, w_gate, w_up, w_down)
    out = _combine(ys, dest, h, rf)
    return out.reshape(bsz, slen, D_MODEL)


def kernel(x, meta_tokens, norm_mix_g, w_in, b_forget, q_norm_g, k_norm_g, conv_w, attn_out_g, conv_out_g,
           w_out, norm_ffn_g, w_router_group, b_router_group, w_router_expert, b_router_expert,
           w_gate, w_up, w_down):
    assert norm_mix_g.shape[0] == 1, "single-layer block"
    return _layer(x, meta_tokens, norm_mix_g[0], w_in[0], b_forget[0], q_norm_g[0], k_norm_g[0], conv_w[0],
                  attn_out_g[0], conv_out_g[0], w_out[0], norm_ffn_g[0], w_router_group[0],
                  b_router_group[0], w_router_expert[0], b_router_expert[0], w_gate[0], w_up[0], w_down[0])
```

```python
import functools

import jax
import jax.numpy as jnp
from jax import lax
from jax.experimental import pallas as pl
from jax.experimental.pallas import tpu as pltpu

D_MODEL = 1024
N_META = 16
HEAD_DIM = 64
ATTN_HEADS = 8
ATTN_WIDTH = ATTN_HEADS * HEAD_DIM
CONV_WIDTH = D_MODEL - ATTN_WIDTH
CONV_K = 3
N_EXPERT_GROUPS = 4
EXPERTS_PER_GROUP = 8
N_EXPERTS = N_EXPERT_GROUPS * EXPERTS_PER_GROUP
TOP_K = 2
D_EXPERT = 512
EPS = 1e-6
MASK_VALUE = -1e30
LOG2E = 1.4426950408889634
AUG = 6

LANES = 128
SUBLANES = 8
HEAD_PAIRS = ATTN_HEADS * HEAD_DIM // LANES
PROJ_PAD_COLS = 5 * ATTN_WIDTH + LANES
VMEM_LIMIT = 56 * 1024 * 1024

IN_TILE = 512
META_TILE = 128
ATTN_TILE = 512
OUT_TILE = 512
EXPERT_ROWS = 256
ROUTE_TILE = 256
DMA_GROUP = 8

F32 = jnp.float32
BF16 = jnp.bfloat16


def _lane_iota(shape):
    return lax.broadcasted_iota(jnp.int32, shape, len(shape) - 1)


def _inproj_kernel(is_meta, tm, x_ref, gmix_ref, w1_ref, wvt_ref, bf_ref, gq_ref, gk_ref, cw_ref, gco_ref,
                   gmat_ref, tri_ref, eq_ref, ek_ref, halo_ref, cum0_ref, *rest):
    if is_meta:
        k_ref, vt_ref, kaug_ref, cum_ref, uc_ref, ucbuf, carry = rest
    else:
        q_ref, k_ref, vt_ref, conv_ref, qaug_ref, kaug_ref, ucbuf, carry = rest
    t = pl.program_id(1)

    @pl.when(t == 0)
    def _():
        ucbuf[0:SUBLANES, :] = halo_ref[...]
        carry[0:1, :] = cum0_ref[...]

    x = x_ref[0]
    ms = jnp.mean(x * x, axis=-1, keepdims=True)
    u = (x * lax.rsqrt(ms + EPS)) * gmix_ref[...]
    ub = u.astype(BF16)
    proj = jnp.dot(ub, w1_ref[...], preferred_element_type=F32)
    vt_ref[0, 0] = lax.dot_general(wvt_ref[...], ub, (((1,), (1,)), ((), ())),
                                   preferred_element_type=F32).astype(BF16)

    def head_norm(z, g):
        ssq = jnp.dot((z * z).astype(BF16), gmat_ref[...], preferred_element_type=F32)
        return z * lax.rsqrt(ssq * (1.0 / HEAD_DIM) + EPS) * g

    w = ATTN_WIDTH
    kn = head_norm(proj[:, w:2 * w], gk_ref[...])
    k_ref[0] = kn.astype(BF16)

    z = proj[:, 5 * w:5 * w + LANES] + bf_ref[...]
    ls = jnp.minimum(z, 0.0) - jnp.log1p(jnp.exp(-jnp.abs(z)))
    ls = jnp.where(_lane_iota(ls.shape) < ATTN_HEADS, ls, 0.0)
    hi = ls.astype(BF16)
    r1 = ls - hi.astype(F32)
    mid = r1.astype(BF16)
    lo = (r1 - mid.astype(F32)).astype(BF16)
    tri = tri_ref[...]
    cs = (jnp.dot(tri, hi, preferred_element_type=F32)
          + jnp.dot(tri, mid, preferred_element_type=F32)
          + jnp.dot(tri, lo, preferred_element_type=F32))
    cum = cs + carry[0:1, :]
    carry[0:1, :] = cum[tm - 1:tm, :]

    c2 = cum * LOG2E
    c_hi = c2.astype(BF16).astype(F32)
    r_1 = c2 - c_hi
    c_mid = r_1.astype(BF16).astype(F32)
    c_lo = (r_1 - c_mid).astype(BF16).astype(F32)
    lane_c = _lane_iota(c2.shape)
    packed = jnp.where(lane_c < ATTN_HEADS, c_hi,
                       jnp.where(lane_c < 2 * ATTN_HEADS, pltpu.roll(c_mid, ATTN_HEADS, axis=1),
                                 pltpu.roll(c_lo, 2 * ATTN_HEADS, axis=1))).astype(BF16)
    in_aug = lane_c < AUG * ATTN_HEADS
    ones_k = in_aug & ((lane_c % AUG) < AUG // 2)
    kaug = jnp.where(ones_k, 1.0, -jnp.dot(packed, ek_ref[...], preferred_element_type=F32))
    kaug_ref[0] = kaug.astype(BF16)

    uc = proj[:, 3 * w:4 * w] * proj[:, 4 * w:5 * w]
    ucbuf[SUBLANES:SUBLANES + tm, :] = uc
    uc1 = ucbuf[SUBLANES - 1:SUBLANES - 1 + tm, :]
    uc2 = ucbuf[SUBLANES - 2:SUBLANES - 2 + tm, :]
    ucbuf[0:SUBLANES, :] = uc[tm - SUBLANES:tm, :]

    if is_meta:
        cum_ref[0] = cum
        uc_ref[0] = uc
        return

    qn = head_norm(proj[:, 0:w], gq_ref[...])
    q_ref[0] = qn.astype(BF16)
    y = cw_ref[0:1, :] * uc2 + cw_ref[1:2, :] * uc1 + cw_ref[2:3, :] * uc
    conv = proj[:, 2 * w:3 * w] * y
    conv_ref[0] = head_norm(conv, gco_ref[...]).astype(BF16)
    ones_q = in_aug & ((lane_c % AUG) >= AUG // 2)
    qaug = jnp.where(ones_q, 1.0, jnp.dot(packed, eq_ref[...], preferred_element_type=F32))
    qaug_ref[0] = qaug.astype(BF16)


def _inproj(x, is_meta, tm, consts, halo, cum0):
    bsz, tlen, _ = x.shape
    nt = tlen // tm
    gmix, w1, wvt, bfp, gq, gk, cw, gco, gmat = consts
    tri = jnp.tril(jnp.ones((tm, tm), F32)).astype(BF16)
    w = ATTN_WIDTH
    src = jnp.arange(LANES)
    dst = jnp.arange(LANES)
    piece, head = src // ATTN_HEADS, src % ATTN_HEADS
    valid = src < 3 * ATTN_HEADS
    eq = (valid[:, None] & (dst[None, :] == (AUG * head + piece)[:, None])).astype(BF16)
    ek = (valid[:, None] & (dst[None, :] == (AUG * head + AUG // 2 + piece)[:, None])).astype(BF16)

    def full(a):
        return pl.BlockSpec(a.shape, lambda b, t: (0,) * a.ndim)

    in_specs = [pl.BlockSpec((1, tm, D_MODEL), lambda b, t: (b, t, 0))] + [
        full(a) for a in (gmix, w1, wvt, bfp, gq, gk, cw, gco, gmat, tri, eq, ek, halo, cum0)]
    tok = lambda width: pl.BlockSpec((1, tm, width), lambda b, t: (b, t, 0))
    vt_spec = pl.BlockSpec((1, 1, w, tm), lambda b, t: (b, t, 0, 0))
    if is_meta:
        out_shape = [jax.ShapeDtypeStruct((bsz, tlen, w), BF16),
                     jax.ShapeDtypeStruct((bsz, nt, w, tm), BF16),
                     jax.ShapeDtypeStruct((bsz, tlen, LANES), BF16),
                     jax.ShapeDtypeStruct((bsz, tlen, LANES), F32),
                     jax.ShapeDtypeStruct((bsz, tlen, w), F32)]
        out_specs = [tok(w), vt_spec, tok(LANES), tok(LANES), tok(w)]
    else:
        out_shape = [jax.ShapeDtypeStruct((bsz, tlen, w), BF16)] * 2 + [
            jax.ShapeDtypeStruct((bsz, nt, w, tm), BF16),
            jax.ShapeDtypeStruct((bsz, tlen, w), BF16),
            jax.ShapeDtypeStruct((bsz, tlen, LANES), BF16),
            jax.ShapeDtypeStruct((bsz, tlen, LANES), BF16)]
        out_specs = [tok(w), tok(w), vt_spec, tok(w), tok(LANES), tok(LANES)]
    return pl.pallas_call(
        functools.partial(_inproj_kernel, is_meta, tm),
        grid=(bsz, nt),
        in_specs=in_specs,
        out_specs=out_specs,
        out_shape=out_shape,
        scratch_shapes=[pltpu.VMEM((tm + SUBLANES, w), F32), pltpu.VMEM((SUBLANES, LANES), F32)],
        compiler_params=pltpu.CompilerParams(
            dimension_semantics=("arbitrary", "arbitrary"), vmem_limit_bytes=VMEM_LIMIT),
        name="inproj_meta" if is_meta else "inproj",
    )(x, gmix, w1, wvt, bfp, gq, gk, cw, gco, gmat, tri, eq, ek, halo, cum0)


def _attn_kernel(tq, q_ref, qaug_ref, k_ref, kaug_ref, vt_ref, km_ref, kaugm_ref, vmt_ref, gao_ref,
                 o_ref, m_sc, l_sc, acc_sc, sa_sc, sb_sc):
    hp = pl.program_id(1)
    qi = pl.program_id(2)
    q = q_ref[0]
    qaug = qaug_ref[0]
    lane = _lane_iota(q.shape)
    first = lane < HEAD_DIM
    zero = jnp.zeros_like(q)
    qcat = []
    for hh in range(2):
        qh = jnp.where(first, q, zero) if hh == 0 else jnp.where(first, zero, q)
        lo = AUG * (2 * hp + hh)
        qa = jnp.where((lane >= lo) & (lane < lo + AUG), qaug, zero)
        qcat.append(jnp.concatenate([qh, qa], axis=1))

    m_sc[...] = jnp.full(m_sc.shape, MASK_VALUE, F32)
    l_sc[...] = jnp.zeros(l_sc.shape, F32)
    acc_sc[...] = jnp.zeros(acc_sc.shape, F32)

    def scores_t(kcat):
        return tuple(lax.dot_general(kcat, qcat[hh], (((1,), (1,)), ((), ())), preferred_element_type=F32)
                     for hh in range(2))

    def update(hh, st, vtb):
        m_old = m_sc[hh]
        m_new = jnp.maximum(m_old, jnp.max(st, axis=0, keepdims=True))
        alpha = jnp.exp2(m_old - m_new)
        p = jnp.exp2(st - m_new)
        l_sc[hh] = alpha * l_sc[hh] + jnp.sum(p, axis=0, keepdims=True)
        vh = vtb[hh * HEAD_DIM:(hh + 1) * HEAD_DIM, :]
        acc_sc[hh] = alpha * acc_sc[hh] + jnp.dot(vh, p.astype(BF16), preferred_element_type=F32)
        m_sc[hh] = m_new

    st_m = scores_t(jnp.concatenate([km_ref[...], kaugm_ref[...]], axis=1))
    vmt = vmt_ref[...]
    for hh in range(2):
        update(hh, st_m[hh], vmt)

    def put_scores(j, buf):
        start = pl.multiple_of(j * tq, tq)
        sts = scores_t(jnp.concatenate([k_ref[0, pl.ds(start, tq), :], kaug_ref[0, pl.ds(start, tq), :]], axis=1))
        for hh in range(2):
            buf[hh] = sts[hh]

    def consume(buf, j, masked):
        vtb = vt_ref[0, j]
        for hh in range(2):
            st = buf[hh]
            if masked:
                key = lax.broadcasted_iota(jnp.int32, st.shape, 0)
                qry = lax.broadcasted_iota(jnp.int32, st.shape, 1)
                st = jnp.where(key <= qry, st, MASK_VALUE)
            update(hh, st, vtb)

    put_scores(0, sa_sc)

    def body(t, carry):
        j = 2 * t
        put_scores(j + 1, sb_sc)
        consume(sa_sc, j, False)
        put_scores(j + 2, sa_sc)
        consume(sb_sc, j + 1, False)
        return carry

    lax.fori_loop(0, lax.shift_right_logical(qi, 1), body, 0)
    odd = (qi & 1) == 1

    @pl.when(odd)
    def _():
        put_scores(qi, sb_sc)
        consume(sa_sc, qi - 1, False)
        consume(sb_sc, qi, True)

    @pl.when(jnp.logical_not(odd))
    def _():
        consume(sa_sc, qi, True)

    ot = jnp.concatenate([acc_sc[0] / l_sc[0], acc_sc[1] / l_sc[1]], axis=0)
    o2 = ot * ot
    ms0 = jnp.sum(o2[0:HEAD_DIM], axis=0, keepdims=True) * (1.0 / HEAD_DIM)
    ms1 = jnp.sum(o2[HEAD_DIM:], axis=0, keepdims=True) * (1.0 / HEAD_DIM)
    inv = jnp.concatenate([jnp.broadcast_to(lax.rsqrt(ms0 + EPS), (HEAD_DIM, tq)),
                           jnp.broadcast_to(lax.rsqrt(ms1 + EPS), (HEAD_DIM, tq))], axis=0)
    o_ref[0] = (ot * inv * gao_ref[...]).T.astype(BF16)


def _attention(q, qaug, k, kaug, vt, km, kaugm, vmt, gao):
    bsz, slen, w = q.shape
    tq = ATTN_TILE
    nq = slen // tq
    return pl.pallas_call(
        functools.partial(_attn_kernel, tq),
        grid=(bsz, HEAD_PAIRS, nq),
        in_specs=[
            pl.BlockSpec((1, tq, LANES), lambda b, p, i: (b, i, p)),
            pl.BlockSpec((1, tq, LANES), lambda b, p, i: (b, i, 0)),
            pl.BlockSpec((1, slen, LANES), lambda b, p, i: (b, 0, p)),
            pl.BlockSpec((1, slen, LANES), lambda b, p, i: (b, 0, 0)),
            pl.BlockSpec((1, nq, LANES, tq), lambda b, p, i: (b, 0, p, 0)),
            pl.BlockSpec((N_META, LANES), lambda b, p, i: (0, p)),
            pl.BlockSpec((N_META, LANES), lambda b, p, i: (0, 0)),
            pl.BlockSpec((LANES, N_META), lambda b, p, i: (p, 0)),
            pl.BlockSpec((LANES, 1), lambda b, p, i: (p, 0)),
        ],
        out_specs=pl.BlockSpec((1, tq, LANES), lambda b, p, i: (b, i, p)),
        out_shape=jax.ShapeDtypeStruct((bsz, slen, w), BF16),
        scratch_shapes=[pltpu.VMEM((2, 1, tq), F32), pltpu.VMEM((2, 1, tq), F32),
                        pltpu.VMEM((2, HEAD_DIM, tq), F32),
                        pltpu.VMEM((2, tq, tq), F32), pltpu.VMEM((2, tq, tq), F32)],
        compiler_params=pltpu.CompilerParams(
            dimension_semantics=("arbitrary", "arbitrary", "arbitrary"), vmem_limit_bytes=VMEM_LIMIT),
        name="fox_attention",
    )(q, qaug, k, kaug, vt, km, kaugm, vmt, gao)


def _outproj_kernel(tm, attn_ref, conv_ref, x_ref, woa_ref, woc_ref, gffn_ref, wr_ref, br_ref, tri_ref,
                    h_ref, xt_ref, ri_ref, rf_ref, cnt_ref, carry):
    i = pl.program_id(0)

    @pl.when(i == 0)
    def _():
        carry[...] = jnp.zeros(carry.shape, F32)

    h = (x_ref[...]
         + jnp.dot(attn_ref[...], woa_ref[...], preferred_element_type=F32)
         + jnp.dot(conv_ref[...], woc_ref[...], preferred_element_type=F32))
    h_ref[...] = h
    ms = jnp.mean(h * h, axis=-1, keepdims=True)
    xt = (h * lax.rsqrt(ms + EPS)) * gffn_ref[...]
    for sl in range(SUBLANES):
        xt_ref[pl.ds(sl, tm, stride=SUBLANES), :] = xt[:, sl * LANES:(sl + 1) * LANES]

    x_hi = xt.astype(BF16)
    x_lo = (xt - x_hi.astype(F32)).astype(BF16)
    parts = jnp.dot(jnp.concatenate([x_hi, x_lo], axis=1), wr_ref[...], preferred_element_type=F32)
    logits = parts[:, 0:LANES] + parts[:, LANES:2 * LANES] + br_ref[...]
    lane = _lane_iota(logits.shape)
    lanef = lane.astype(F32)
    big = float(LANES)

    def first_argmax(vals, vmax):
        return jnp.min(jnp.where(vals == vmax, lanef, big), axis=1, keepdims=True)

    is_g = lane < N_EXPERT_GROUPS
    gl = jnp.where(is_g, logits, MASK_VALUE)
    gmax = jnp.max(gl, axis=1, keepdims=True)
    gidx = first_argmax(gl, gmax)
    g_p = 1.0 / jnp.sum(jnp.where(is_g, jnp.exp(gl - gmax), 0.0), axis=1, keepdims=True)

    base = N_EXPERT_GROUPS + EXPERTS_PER_GROUP * gidx
    in_grp = (lanef >= base) & (lanef < base + EXPERTS_PER_GROUP)
    el = jnp.where(in_grp, logits, MASK_VALUE)
    l1 = jnp.max(el, axis=1, keepdims=True)
    e1 = first_argmax(el, l1)
    el2 = jnp.where(lanef == e1, MASK_VALUE, el)
    l2 = jnp.max(el2, axis=1, keepdims=True)
    e2 = first_argmax(el2, l2)
    zsum = jnp.sum(jnp.where(in_grp, jnp.exp(el - l1), 0.0), axis=1, keepdims=True)
    p1 = 1.0 / zsum
    p2 = jnp.exp(l2 - l1) / zsum
    den = p1 + p2
    w1 = g_p * p1 / den
    w2 = g_p * p2 / den
    id1 = e1 - N_EXPERT_GROUPS
    id2 = e2 - N_EXPERT_GROUPS

    oh1 = lanef == id1
    oh2 = lanef == id2
    oh = jnp.where(oh1 | oh2, 1.0, 0.0)
    before = jnp.dot(tri_ref[...], oh.astype(BF16), preferred_element_type=F32) + carry[0:1, :]
    rank1 = jnp.sum(jnp.where(oh1, before, 0.0), axis=1, keepdims=True)
    rank2 = jnp.sum(jnp.where(oh2, before, 0.0), axis=1, keepdims=True)
    carry[0:1, :] = carry[0:1, :] + jnp.sum(oh, axis=0, keepdims=True)
    cnt_ref[...] = jnp.broadcast_to(carry[0:1, :], cnt_ref.shape)

    ri = jnp.where(lane == 0, id1, jnp.where(lane == 1, id2, jnp.where(lane == 2, rank1, rank2)))
    ri_ref[0] = ri.T[0:SUBLANES, :].astype(jnp.int32)
    rf = jnp.where(lane == 0, w1, w2)
    rf_ref[...] = rf[:, 0:SUBLANES]


def _outproj(attn, conv, x, woa, woc, gffn, wr, br):
    n = x.shape[0]
    tm = OUT_TILE
    tri = jnp.tril(jnp.ones((tm, tm), F32), k=-1).astype(BF16)

    def full(a):
        return pl.BlockSpec(a.shape, lambda i: (0,) * a.ndim)

    rows = lambda width: pl.BlockSpec((tm, width), lambda i: (i, 0))
    return pl.pallas_call(
        functools.partial(_outproj_kernel, tm),
        grid=(n // tm,),
        in_specs=[rows(ATTN_WIDTH), rows(CONV_WIDTH), rows(D_MODEL)] + [
            full(a) for a in (woa, woc, gffn, wr, br, tri)],
        out_specs=[rows(D_MODEL), pl.BlockSpec((tm * SUBLANES, LANES), lambda i: (i, 0)),
                   pl.BlockSpec((1, SUBLANES, tm), lambda i: (i, 0, 0)), rows(SUBLANES),
                   pl.BlockSpec((SUBLANES, LANES), lambda i: (0, 0))],
        out_shape=[jax.ShapeDtypeStruct((n, D_MODEL), F32), jax.ShapeDtypeStruct((n * SUBLANES, LANES), F32),
                   jax.ShapeDtypeStruct((n // tm, SUBLANES, tm), jnp.int32),
                   jax.ShapeDtypeStruct((n, SUBLANES), F32),
                   jax.ShapeDtypeStruct((SUBLANES, LANES), F32)],
        scratch_shapes=[pltpu.VMEM((SUBLANES, LANES), F32)],
        compiler_params=pltpu.CompilerParams(
            dimension_semantics=("arbitrary",), vmem_limit_bytes=VMEM_LIMIT),
        name="outproj_router",
    )(attn, conv, x, woa, woc, gffn, wr, br, tri)


def _dest_kernel(rs_ref, ri_ref, dest_ref):
    ri = ri_ref[0]
    experts = ri[0:TOP_K, :]
    start = jnp.zeros_like(experts)
    for e in range(N_EXPERTS):
        start = jnp.where(experts == e, rs_ref[e], start)
    dest_ref[0] = jnp.concatenate([start + ri[TOP_K:2 * TOP_K, :],
                                   jnp.zeros((SUBLANES - TOP_K, ri.shape[1]), jnp.int32)], axis=0)


def _dest_rows(ri_t, row_start):
    nt, _, tm = ri_t.shape
    grid_spec = pltpu.PrefetchScalarGridSpec(
        num_scalar_prefetch=1,
        grid=(nt,),
        in_specs=[pl.BlockSpec((1, SUBLANES, tm), lambda i, rs: (i, 0, 0))],
        out_specs=pl.BlockSpec((1, SUBLANES, tm), lambda i, rs: (i, 0, 0)),
    )
    return pl.pallas_call(
        _dest_kernel,
        grid_spec=grid_spec,
        out_shape=jax.ShapeDtypeStruct(ri_t.shape, jnp.int32),
        compiler_params=pltpu.CompilerParams(dimension_semantics=("arbitrary",)),
        name="moe_dest",
    )(row_start, ri_t)


def _tile_copy(src, src_row, dst, dst_row, sem):
    return pltpu.make_async_copy(src.at[pl.ds(pl.multiple_of(src_row * SUBLANES, SUBLANES), SUBLANES), :],
                                 dst.at[pl.ds(pl.multiple_of(dst_row * SUBLANES, SUBLANES), SUBLANES), :], sem)


def _dispatch_kernel(tm, bm, fz_ref, fv_ref, dest_ref, xt_ref, xs_hbm, zeros, sem, zsem):
    i = pl.program_id(0)
    block_rows = bm * SUBLANES

    def zero_block(z):
        start = pl.multiple_of(fz_ref[z] * block_rows, block_rows)
        return pltpu.make_async_copy(zeros, xs_hbm.at[pl.ds(start, block_rows), :], zsem.at[0])

    @pl.when(i == 0)
    def _():
        zeros[...] = jnp.zeros(zeros.shape, F32)
        for z in range(fz_ref.shape[0]):
            @pl.when(fv_ref[z] == 1)
            def _():
                zero_block(z).start()
        for z in range(fz_ref.shape[0]):
            @pl.when(fv_ref[z] == 1)
            def _():
                zero_block(z).wait()

    for g in range(0, tm, DMA_GROUP):
        dests = [(r, kk, dest_ref[0, kk, r]) for r in range(g, g + DMA_GROUP) for kk in range(TOP_K)]
        for r, kk, d in dests:
            _tile_copy(xt_ref, r, xs_hbm, d, sem.at[0]).start(priority=kk)
    for kk in range(TOP_K):
        pltpu.make_async_copy(xt_ref, xs_hbm.at[pl.ds(0, tm * SUBLANES), :], sem.at[0]).wait()


def _route_spec(dest, tm, extra_args):
    per = dest.shape[2] // tm
    nt = dest.shape[0] * per
    if extra_args == 0:
        index = lambda i: (jnp.minimum(i, nt - 1) // per, 0, jnp.minimum(i, nt - 1) % per)
    else:
        index = lambda i, *_: (jnp.minimum(i, nt - 1) // per, 0, jnp.minimum(i, nt - 1) % per)
    return nt, pl.BlockSpec((1, SUBLANES, tm), index, memory_space=pltpu.SMEM)


def _dispatch(xt_rows, dest, fill_blocks, fill_valid, nb):
    tm = ROUTE_TILE
    bm = EXPERT_ROWS
    nt, dest_spec = _route_spec(dest, tm, 2)
    grid_spec = pltpu.PrefetchScalarGridSpec(
        num_scalar_prefetch=2,
        grid=(nt,),
        in_specs=[dest_spec, pl.BlockSpec((tm * SUBLANES, LANES), lambda i, fz, fv: (i, 0))],
        out_specs=pl.BlockSpec(memory_space=pl.ANY),
        scratch_shapes=[pltpu.VMEM((bm * SUBLANES, LANES), F32), pltpu.SemaphoreType.DMA((1,)),
                        pltpu.SemaphoreType.DMA((1,))],
    )
    return pl.pallas_call(
        functools.partial(_dispatch_kernel, tm, bm),
        grid_spec=grid_spec,
        out_shape=jax.ShapeDtypeStruct((nb * bm * SUBLANES, LANES), F32),
        compiler_params=pltpu.CompilerParams(
            dimension_semantics=("arbitrary",), vmem_limit_bytes=VMEM_LIMIT),
        name="moe_dispatch",
    )(fill_blocks, fill_valid, dest, xt_rows)


def _expert_kernel(bm, be_ref, nu_ref, first_ref, nxt_ref, slot_ref, xs_ref, wg_hbm, wu_hbm, wd_hbm, ys_ref,
                   wg_buf, wu_buf, wd_buf, wsem):
    j = pl.program_id(0)

    def weight_copies(e, slot):
        return [pltpu.make_async_copy(src.at[e], buf.at[slot], wsem.at[m, slot])
                for m, (src, buf) in enumerate(((wg_hbm, wg_buf), (wu_hbm, wu_buf), (wd_hbm, wd_buf)))]

    @pl.when(j == 0)
    def _():
        for c in weight_copies(be_ref[0], 0):
            c.start()

    @pl.when(j < nu_ref[0])
    def _():
        slot = slot_ref[j]

        @pl.when(first_ref[j] == 1)
        def _():
            for c in weight_copies(0, slot):
                c.wait()

            @pl.when(nxt_ref[j] >= 0)
            def _():
                for c in weight_copies(nxt_ref[j], 1 - slot):
                    c.start()

        x = jnp.concatenate([xs_ref[pl.ds(sl, bm, stride=SUBLANES), :] for sl in range(SUBLANES)], axis=1)
        hg = jnp.dot(x, wg_buf[slot], preferred_element_type=F32)
        hu = jnp.dot(x, wu_buf[slot], preferred_element_type=F32)
        hdn = hg * (1.0 / (1.0 + jnp.exp(-hg))) * hu
        y = jnp.dot(hdn, wd_buf[slot], preferred_element_type=F32)
        for sl in range(SUBLANES):
            ys_ref[pl.ds(sl, bm, stride=SUBLANES), :] = y[:, sl * LANES:(sl + 1) * LANES]

    @pl.when(j >= nu_ref[0])
    def _():
        ys_ref[...] = jnp.zeros(ys_ref.shape, F32)


def _experts(xs_rows, block_expert, n_used, w_gate, w_up, w_down):
    bm = EXPERT_ROWS
    nb = block_expert.shape[0]
    blk = jnp.arange(nb, dtype=jnp.int32)
    live = blk < n_used[0]
    first = live & ((blk == 0) | (block_expert != jnp.roll(block_expert, 1)))
    slot = (jnp.cumsum(first.astype(jnp.int32)) - 1) % 2
    later_first = first[None, :] & (blk[None, :] > blk[:, None])
    nxt_start = jnp.min(jnp.where(later_first, blk[None, :], nb), axis=1)
    nxt = jnp.sum(jnp.where(blk[None, :] == nxt_start[:, None], block_expert[None, :], 0), axis=1)
    nxt = jnp.where(nxt_start < nb, nxt, -1)
    grid_spec = pltpu.PrefetchScalarGridSpec(
        num_scalar_prefetch=5,
        grid=(nb,),
        in_specs=[
            pl.BlockSpec((bm * SUBLANES, LANES), lambda j, be, nu, *_: (jnp.minimum(j, nu[0] - 1), 0)),
            pl.BlockSpec(memory_space=pl.ANY),
            pl.BlockSpec(memory_space=pl.ANY),
            pl.BlockSpec(memory_space=pl.ANY),
        ],
        out_specs=pl.BlockSpec((bm * SUBLANES, LANES), lambda j, *_: (j, 0)),
        scratch_shapes=[pltpu.VMEM((2, D_MODEL, D_EXPERT), F32), pltpu.VMEM((2, D_MODEL, D_EXPERT), F32),
                        pltpu.VMEM((2, D_EXPERT, D_MODEL), F32), pltpu.SemaphoreType.DMA((3, 2))],
    )
    return pl.pallas_call(
        functools.partial(_expert_kernel, bm),
        grid_spec=grid_spec,
        out_shape=jax.ShapeDtypeStruct((nb * bm * SUBLANES, LANES), F32),
        compiler_params=pltpu.CompilerParams(
            dimension_semantics=("arbitrary",), vmem_limit_bytes=VMEM_LIMIT),
        name="moe_experts",
    )(block_expert, n_used, first.astype(jnp.int32), nxt.astype(jnp.int32), slot.astype(jnp.int32),
      xs_rows, w_gate, w_up, w_down)


def _combine_kernel(tm, dest_ref, ys_hbm, h_ref, rf_ref, o_ref, ybuf, sem):
    i = pl.program_id(0)
    nt = pl.num_programs(0) - 1

    @pl.when(i < nt)
    def _():
        slot = i % 2
        for g in range(0, tm, DMA_GROUP):
            dests = [(r, kk, dest_ref[0, kk, r]) for r in range(g, g + DMA_GROUP) for kk in range(TOP_K)]
            for r, kk, d in dests:
                _tile_copy(ys_hbm, d, ybuf.at[kk, slot], r, sem.at[kk, slot]).start(priority=kk)

    @pl.when(i >= 1)
    def _():
        slot = (i - 1) % 2
        for kk in range(TOP_K):
            pltpu.make_async_copy(ys_hbm.at[pl.ds(0, tm * SUBLANES), :], ybuf.at[kk, slot], sem.at[kk, slot]).wait()
        rf = rf_ref[...]
        for sl in range(SUBLANES):
            cols = slice(sl * LANES, (sl + 1) * LANES)
            o_ref[:, cols] = (h_ref[:, cols]
                              + rf[:, 0:1] * ybuf[0, slot, pl.ds(sl, tm, stride=SUBLANES), :]
                              + rf[:, 1:2] * ybuf[1, slot, pl.ds(sl, tm, stride=SUBLANES), :])


def _combine(ys_rows, dest, h, rf):
    n = h.shape[0]
    tm = ROUTE_TILE
    nt, dest_spec = _route_spec(dest, tm, 0)
    prev = lambda i: jnp.maximum(i - 1, 0)
    return pl.pallas_call(
        functools.partial(_combine_kernel, tm),
        grid=(nt + 1,),
        in_specs=[
            dest_spec,
            pl.BlockSpec(memory_space=pl.ANY),
            pl.BlockSpec((tm, D_MODEL), lambda i: (prev(i), 0)),
            pl.BlockSpec((tm, SUBLANES), lambda i: (prev(i), 0)),
        ],
        out_specs=pl.BlockSpec((tm, D_MODEL), lambda i: (prev(i), 0)),
        out_shape=jax.ShapeDtypeStruct((n, D_MODEL), F32),
        scratch_shapes=[pltpu.VMEM((TOP_K, 2, tm * SUBLANES, LANES), F32), pltpu.SemaphoreType.DMA((TOP_K, 2))],
        compiler_params=pltpu.CompilerParams(
            dimension_semantics=("arbitrary",), vmem_limit_bytes=VMEM_LIMIT),
        name="moe_combine",
    )(dest, ys_rows, h, rf)


def _pad_lanes(a, width=LANES):
    return jnp.pad(a, ((0, 0), (0, width - a.shape[-1])))


def _layer(x, meta_tokens, g_mix, w_in, b_forget, q_norm_g, k_norm_g, conv_w, attn_out_g, conv_out_g,
           w_out, g_ffn, w_rg, b_rg, w_re, b_re, w_gate, w_up, w_down):
    bsz, slen, _ = x.shape
    n = bsz * slen
    w = ATTN_WIDTH

    wq, wk, wv, wf, wb, wc, whc = jnp.split(
        w_in, [w, 2 * w, 3 * w, 3 * w + ATTN_HEADS, 4 * w + ATTN_HEADS, 5 * w + ATTN_HEADS], axis=1)
    w1 = jnp.concatenate([wq, wk, wb, wc, whc, _pad_lanes(wf)], axis=1).astype(BF16)
    head_of = jnp.arange(w) // HEAD_DIM
    gmat = (head_of[:, None] == head_of[None, :]).astype(BF16)
    consts = (
        g_mix.reshape(1, D_MODEL), w1, wv.T.astype(BF16), _pad_lanes(b_forget.reshape(1, ATTN_HEADS)),
        jnp.tile(q_norm_g, ATTN_HEADS).reshape(1, w) * (HEAD_DIM ** -0.5 * LOG2E),
        jnp.tile(k_norm_g, ATTN_HEADS).reshape(1, w),
        jnp.pad(conv_w, ((0, SUBLANES - CONV_K), (0, 0))), conv_out_g.reshape(1, w), gmat)

    xm = jnp.pad(meta_tokens, ((0, META_TILE - N_META), (0, 0)))[None]
    km, vmt, kaugm, cum_m, uc_m = _inproj(xm, True, META_TILE, consts,
                                          jnp.zeros((SUBLANES, w), F32), jnp.zeros((1, LANES), F32))
    km, vmt, kaugm = km[0, :N_META], vmt[0, 0, :, :N_META], kaugm[0, :N_META]
    cum_m = cum_m[0, :N_META]
    halo = jnp.zeros((SUBLANES, w), F32).at[SUBLANES - 2:].set(uc_m[0, N_META - 2:N_META])
    cum0 = cum_m[N_META - 1:N_META]

    q, k, vt, conv, qaug, kaug = _inproj(x, False, IN_TILE, consts, halo, cum0)
    attn = _attention(q, qaug, k, kaug, vt, km, kaugm, vmt, attn_out_g.reshape(w, 1))

    wr = _pad_lanes(jnp.concatenate([w_rg, w_re], axis=1))
    wr_hi = wr.astype(BF16)
    wr_lo = (wr - wr_hi.astype(F32)).astype(BF16)
    wr = jnp.concatenate([jnp.concatenate([wr_hi, wr_lo], axis=1),
                          jnp.concatenate([wr_hi, jnp.zeros_like(wr_lo)], axis=1)], axis=0)
    br = _pad_lanes(jnp.concatenate([b_rg, b_re]).reshape(1, -1))
    wo = w_out.astype(BF16)
    h, xt, ri, rf, cnt = _outproj(attn.reshape(n, w), conv.reshape(n, w), x.reshape(n, D_MODEL),
                                  wo[:w], wo[w:], g_ffn.reshape(1, D_MODEL), wr, br)

    bm = EXPERT_ROWS
    nb = (n * TOP_K) // bm + N_EXPERTS
    counts = cnt[0, :N_EXPERTS].astype(jnp.int32)
    blocks_e = (counts + bm - 1) // bm
    blk_end = jnp.cumsum(blocks_e)
    row_start = (blk_end - blocks_e) * bm
    blk = jnp.arange(nb, dtype=jnp.int32)
    block_expert = jnp.minimum(jnp.sum(blk_end[None, :] <= blk[:, None], axis=1), N_EXPERTS - 1).astype(jnp.int32)
    n_used = blk_end[-1:].astype(jnp.int32)
    tail = n_used[0] + jnp.arange(N_EXPERTS, dtype=jnp.int32)
    fill_blocks = jnp.clip(jnp.concatenate([blk_end - 1, tail]), 0, nb - 1).astype(jnp.int32)
    fill_valid = jnp.concatenate([counts % bm != 0, tail < nb]).astype(jnp.int32)

    dest = _dest_rows(ri, row_start.astype(jnp.int32))
    xs = _dispatch(xt, dest, fill_blocks, fill_valid, nb)
    ys = _experts(xs, block_expert, n_used, w_gate, w_up, w_down)
    out = _combine(ys, dest, h, rf)
    return out.reshape(bsz, slen, D_MODEL)


def kernel(x, meta_tokens, norm_mix_g, w_in, b_forget, q_norm_g, k_norm_g, conv_w, attn_out_g, conv_out_g,
           w_out, norm_ffn_g, w_router_group, b_router_group, w_router_expert, b_router_expert,
           w_gate, w_up, w_down):
    assert norm_mix_g.shape[0] == 1, "single-layer block"
    return _layer(x, meta_tokens, norm_mix_g[0], w_in[0], b_forget[0], q_norm_g[0], k_norm_g[0], conv_w[0],
                  attn_out_g[0], conv_out_g[0], w_out[0], norm_ffn_g[0], w_router_group[0],
                  b_router_group[0], w_router_expert[0], b_router_expert[0], w_gate[0], w_up[0], w_down[0])
```

```python
import functools

import jax
import jax.numpy as jnp
from jax import lax
from jax.experimental import pallas as pl
from jax.experimental.pallas import tpu as pltpu

D_MODEL = 1024
N_META = 16
HEAD_DIM = 64
ATTN_HEADS = 8
ATTN_WIDTH = ATTN_HEADS * HEAD_DIM
CONV_WIDTH = D_MODEL - ATTN_WIDTH
CONV_K = 3
N_EXPERT_GROUPS = 4
EXPERTS_PER_GROUP = 8
N_EXPERTS = N_EXPERT_GROUPS * EXPERTS_PER_GROUP
TOP_K = 2
D_EXPERT = 512
EPS = 1e-6
MASK_VALUE = -1e30
LOG2E = 1.4426950408889634
AUG = 6

LANES = 128
SUBLANES = 8
HEAD_PAIRS = ATTN_HEADS * HEAD_DIM // LANES
PROJ_PAD_COLS = 5 * ATTN_WIDTH + LANES
VMEM_LIMIT = 56 * 1024 * 1024

IN_TILE = 512
META_TILE = 128
ATTN_TILE = 512
OUT_TILE = 512
EXPERT_ROWS = 256
ROUTE_TILE = 256
DMA_GROUP = 8

F32 = jnp.float32
BF16 = jnp.bfloat16


def _lane_iota(shape):
    return lax.broadcasted_iota(jnp.int32, shape, len(shape) - 1)


def _inproj_kernel(is_meta, tm, x_ref, gmix_ref, w1_ref, wvt_ref, bf_ref, gq_ref, gk_ref, cw_ref, gco_ref,
                   gmat_ref, tri_ref, eq_ref, ek_ref, halo_ref, cum0_ref, *rest):
    if is_meta:
        k_ref, vt_ref, kaug_ref, cum_ref, uc_ref, ucbuf, carry = rest
    else:
        q_ref, k_ref, vt_ref, conv_ref, qaug_ref, kaug_ref, ucbuf, carry = rest
    t = pl.program_id(1)

    @pl.when(t == 0)
    def _():
        ucbuf[0:SUBLANES, :] = halo_ref[...]
        carry[0:1, :] = cum0_ref[...]

    x = x_ref[0]
    ms = jnp.mean(x * x, axis=-1, keepdims=True)
    u = (x * lax.rsqrt(ms + EPS)) * gmix_ref[...]
    ub = u.astype(BF16)
    proj = jnp.dot(ub, w1_ref[...], preferred_element_type=F32)
    vt_ref[0, 0] = lax.dot_general(wvt_ref[...], ub, (((1,), (1,)), ((), ())),
                                   preferred_element_type=F32).astype(BF16)

    def head_norm(z, g):
        ssq = jnp.dot((z * z).astype(BF16), gmat_ref[...], preferred_element_type=F32)
        return z * lax.rsqrt(ssq * (1.0 / HEAD_DIM) + EPS) * g

    w = ATTN_WIDTH
    kn = head_norm(proj[:, w:2 * w], gk_ref[...])
    k_ref[0] = kn.astype(BF16)

    z = proj[:, 5 * w:5 * w + LANES] + bf_ref[...]
    ls = jnp.minimum(z, 0.0) - jnp.log1p(jnp.exp(-jnp.abs(z)))
    lane_c = _lane_iota(ls.shape)
    is_head = lane_c < ATTN_HEADS
    ls = jnp.where(is_head, ls, 0.0)

    def pieces(val):
        p_hi = val.astype(BF16).astype(F32)
        rem = val - p_hi
        p_mid = rem.astype(BF16).astype(F32)
        p_lo = (rem - p_mid).astype(BF16).astype(F32)
        return jnp.where(is_head, p_hi,
                         jnp.where(lane_c < 2 * ATTN_HEADS, pltpu.roll(p_mid, ATTN_HEADS, axis=1),
                                   pltpu.roll(p_lo, 2 * ATTN_HEADS, axis=1))).astype(BF16)

    cs3 = jnp.dot(tri_ref[...], pieces(ls), preferred_element_type=F32)
    cs = cs3 + pltpu.roll(cs3, LANES - ATTN_HEADS, axis=1) + pltpu.roll(cs3, LANES - 2 * ATTN_HEADS, axis=1)
    cum = jnp.where(is_head, cs, 0.0) + carry[0:1, :]
    carry[0:1, :] = cum[tm - 1:tm, :]

    packed = pieces(cum * LOG2E)
    in_aug = lane_c < AUG * ATTN_HEADS
    ones_k = in_aug & ((lane_c % AUG) < AUG // 2)
    kaug = jnp.where(ones_k, 1.0, -jnp.dot(packed, ek_ref[...], preferred_element_type=F32))
    kaug_ref[0] = kaug.astype(BF16)

    uc = proj[:, 3 * w:4 * w] * proj[:, 4 * w:5 * w]
    ucbuf[SUBLANES:SUBLANES + tm, :] = uc
    uc1 = ucbuf[SUBLANES - 1:SUBLANES - 1 + tm, :]
    uc2 = ucbuf[SUBLANES - 2:SUBLANES - 2 + tm, :]
    ucbuf[0:SUBLANES, :] = uc[tm - SUBLANES:tm, :]

    if is_meta:
        cum_ref[0] = cum
        uc_ref[0] = uc
        return

    qn = head_norm(proj[:, 0:w], gq_ref[...])
    q_ref[0] = qn.astype(BF16)
    y = cw_ref[0:1, :] * uc2 + cw_ref[1:2, :] * uc1 + cw_ref[2:3, :] * uc
    conv = proj[:, 2 * w:3 * w] * y
    conv_ref[0] = head_norm(conv, gco_ref[...]).astype(BF16)
    ones_q = in_aug & ((lane_c % AUG) >= AUG // 2)
    qaug = jnp.where(ones_q, 1.0, jnp.dot(packed, eq_ref[...], preferred_element_type=F32))
    qaug_ref[0] = qaug.astype(BF16)


def _inproj(x, is_meta, tm, consts, halo, cum0):
    bsz, tlen, _ = x.shape
    nt = tlen // tm
    gmix, w1, wvt, bfp, gq, gk, cw, gco, gmat = consts
    tri = jnp.tril(jnp.ones((tm, tm), F32)).astype(BF16)
    w = ATTN_WIDTH
    src = jnp.arange(LANES)
    dst = jnp.arange(LANES)
    piece, head = src // ATTN_HEADS, src % ATTN_HEADS
    valid = src < 3 * ATTN_HEADS
    eq = (valid[:, None] & (dst[None, :] == (AUG * head + piece)[:, None])).astype(BF16)
    ek = (valid[:, None] & (dst[None, :] == (AUG * head + AUG // 2 + piece)[:, None])).astype(BF16)

    def full(a):
        return pl.BlockSpec(a.shape, lambda b, t: (0,) * a.ndim)

    in_specs = [pl.BlockSpec((1, tm, D_MODEL), lambda b, t: (b, t, 0))] + [
        full(a) for a in (gmix, w1, wvt, bfp, gq, gk, cw, gco, gmat, tri, eq, ek, halo, cum0)]
    tok = lambda width: pl.BlockSpec((1, tm, width), lambda b, t: (b, t, 0))
    vt_spec = pl.BlockSpec((1, 1, w, tm), lambda b, t: (b, t, 0, 0))
    if is_meta:
        out_shape = [jax.ShapeDtypeStruct((bsz, tlen, w), BF16),
                     jax.ShapeDtypeStruct((bsz, nt, w, tm), BF16),
                     jax.ShapeDtypeStruct((bsz, tlen, LANES), BF16),
                     jax.ShapeDtypeStruct((bsz, tlen, LANES), F32),
                     jax.ShapeDtypeStruct((bsz, tlen, w), F32)]
        out_specs = [tok(w), vt_spec, tok(LANES), tok(LANES), tok(w)]
    else:
        out_shape = [jax.ShapeDtypeStruct((bsz, tlen, w), BF16)] * 2 + [
            jax.ShapeDtypeStruct((bsz, nt, w, tm), BF16),
            jax.ShapeDtypeStruct((bsz, tlen, w), BF16),
            jax.ShapeDtypeStruct((bsz, tlen, LANES), BF16),
            jax.ShapeDtypeStruct((bsz, tlen, LANES), BF16)]
        out_specs = [tok(w), tok(w), vt_spec, tok(w), tok(LANES), tok(LANES)]
    return pl.pallas_call(
        functools.partial(_inproj_kernel, is_meta, tm),
        grid=(bsz, nt),
        in_specs=in_specs,
        out_specs=out_specs,
        out_shape=out_shape,
        scratch_shapes=[pltpu.VMEM((tm + SUBLANES, w), F32), pltpu.VMEM((SUBLANES, LANES), F32)],
        compiler_params=pltpu.CompilerParams(
            dimension_semantics=("arbitrary", "arbitrary"), vmem_limit_bytes=VMEM_LIMIT),
        name="inproj_meta" if is_meta else "inproj",
    )(x, gmix, w1, wvt, bfp, gq, gk, cw, gco, gmat, tri, eq, ek, halo, cum0)


def _attn_kernel(tq, q_ref, qaug_ref, k_ref, kaug_ref, vt_ref, km_ref, kaugm_ref, vmt_ref, gao_ref,
                 o_ref, m_sc, l_sc, acc_sc, sa_sc, sb_sc):
    hp = pl.program_id(1)
    qi = pl.program_id(2)
    q = q_ref[0]
    qaug = qaug_ref[0]
    lane = _lane_iota(q.shape)
    first = lane < HEAD_DIM
    zero = jnp.zeros_like(q)
    qcat = []
    for hh in range(2):
        qh = jnp.where(first, q, zero) if hh == 0 else jnp.where(first, zero, q)
        lo = AUG * (2 * hp + hh)
        qa = jnp.where((lane >= lo) & (lane < lo + AUG), qaug, zero)
        qcat.append(jnp.concatenate([qh, qa], axis=1))

    m_sc[...] = jnp.full(m_sc.shape, MASK_VALUE, F32)
    l_sc[...] = jnp.zeros(l_sc.shape, F32)
    acc_sc[...] = jnp.zeros(acc_sc.shape, F32)

    def scores_t(kcat):
        return tuple(lax.dot_general(kcat, qcat[hh], (((1,), (1,)), ((), ())), preferred_element_type=F32)
                     for hh in range(2))

    def update(hh, st, vtb):
        m_old = m_sc[hh]
        m_new = jnp.maximum(m_old, jnp.max(st, axis=0, keepdims=True))
        alpha = jnp.exp2(m_old - m_new)
        p = jnp.exp2(st - m_new)
        l_sc[hh] = alpha * l_sc[hh] + jnp.sum(p, axis=0, keepdims=True)
        vh = vtb[hh * HEAD_DIM:(hh + 1) * HEAD_DIM, :]
        acc_sc[hh] = alpha * acc_sc[hh] + jnp.dot(vh, p.astype(BF16), preferred_element_type=F32)
        m_sc[hh] = m_new

    def put_scores(j, buf):
        start = pl.multiple_of(j * tq, tq)
        sts = scores_t(jnp.concatenate([k_ref[0, pl.ds(start, tq), :], kaug_ref[0, pl.ds(start, tq), :]], axis=1))
        for hh in range(2):
            buf[hh] = sts[hh]

    def consume(buf, j, masked):
        vtb = vt_ref[0, j]
        for hh in range(2):
            st = buf[hh]
            if masked:
                key = lax.broadcasted_iota(jnp.int32, st.shape, 0)
                qry = lax.broadcasted_iota(jnp.int32, st.shape, 1)
                st = jnp.where(key <= qry, st, MASK_VALUE)
            update(hh, st, vtb)

    st_m = scores_t(jnp.concatenate([km_ref[...], kaugm_ref[...]], axis=1))
    put_scores(0, sa_sc)
    vmt = vmt_ref[...]
    for hh in range(2):
        update(hh, st_m[hh], vmt)

    def body(t, carry):
        j = 2 * t
        put_scores(j + 1, sb_sc)
        consume(sa_sc, j, False)
        put_scores(j + 2, sa_sc)
        consume(sb_sc, j + 1, False)
        return carry

    lax.fori_loop(0, lax.shift_right_logical(qi, 1), body, 0)
    odd = (qi & 1) == 1

    @pl.when(odd)
    def _():
        put_scores(qi, sb_sc)
        consume(sa_sc, qi - 1, False)
        consume(sb_sc, qi, True)

    @pl.when(jnp.logical_not(odd))
    def _():
        consume(sa_sc, qi, True)

    ot = jnp.concatenate([acc_sc[0] / l_sc[0], acc_sc[1] / l_sc[1]], axis=0)
    o2 = ot * ot
    ms0 = jnp.sum(o2[0:HEAD_DIM], axis=0, keepdims=True) * (1.0 / HEAD_DIM)
    ms1 = jnp.sum(o2[HEAD_DIM:], axis=0, keepdims=True) * (1.0 / HEAD_DIM)
    inv = jnp.concatenate([jnp.broadcast_to(lax.rsqrt(ms0 + EPS), (HEAD_DIM, tq)),
                           jnp.broadcast_to(lax.rsqrt(ms1 + EPS), (HEAD_DIM, tq))], axis=0)
    o_ref[0] = (ot * inv * gao_ref[...]).T.astype(BF16)


def _attention(q, qaug, k, kaug, vt, km, kaugm, vmt, gao):
    bsz, slen, w = q.shape
    tq = ATTN_TILE
    nq = slen // tq
    return pl.pallas_call(
        functools.partial(_attn_kernel, tq),
        grid=(bsz, HEAD_PAIRS, nq),
        in_specs=[
            pl.BlockSpec((1, tq, LANES), lambda b, p, i: (b, i, p)),
            pl.BlockSpec((1, tq, LANES), lambda b, p, i: (b, i, 0)),
            pl.BlockSpec((1, slen, LANES), lambda b, p, i: (b, 0, p)),
            pl.BlockSpec((1, slen, LANES), lambda b, p, i: (b, 0, 0)),
            pl.BlockSpec((1, nq, LANES, tq), lambda b, p, i: (b, 0, p, 0)),
            pl.BlockSpec((N_META, LANES), lambda b, p, i: (0, p)),
            pl.BlockSpec((N_META, LANES), lambda b, p, i: (0, 0)),
            pl.BlockSpec((LANES, N_META), lambda b, p, i: (p, 0)),
            pl.BlockSpec((LANES, 1), lambda b, p, i: (p, 0)),
        ],
        out_specs=pl.BlockSpec((1, tq, LANES), lambda b, p, i: (b, i, p)),
        out_shape=jax.ShapeDtypeStruct((bsz, slen, w), BF16),
        scratch_shapes=[pltpu.VMEM((2, 1, tq), F32), pltpu.VMEM((2, 1, tq), F32),
                        pltpu.VMEM((2, HEAD_DIM, tq), F32),
                        pltpu.VMEM((2, tq, tq), F32), pltpu.VMEM((2, tq, tq), F32)],
        compiler_params=pltpu.CompilerParams(
            dimension_semantics=("arbitrary", "arbitrary", "arbitrary"), vmem_limit_bytes=VMEM_LIMIT),
        name="fox_attention",
    )(q, qaug, k, kaug, vt, km, kaugm, vmt, gao)


def _outproj_kernel(tm, attn_ref, conv_ref, x_ref, woa_ref, woc_ref, gffn_ref, wr_ref, br_ref, tri_ref,
                    h_ref, xt_ref, ri_ref, rf_ref, cnt_ref, carry):
    i = pl.program_id(0)

    @pl.when(i == 0)
    def _():
        carry[...] = jnp.zeros(carry.shape, F32)

    h = (x_ref[...]
         + jnp.dot(attn_ref[...], woa_ref[...], preferred_element_type=F32)
         + jnp.dot(conv_ref[...], woc_ref[...], preferred_element_type=F32))
    h_ref[...] = h
    ms = jnp.mean(h * h, axis=-1, keepdims=True)
    xt = (h * lax.rsqrt(ms + EPS)) * gffn_ref[...]
    for sl in range(SUBLANES):
        xt_ref[pl.ds(sl, tm, stride=SUBLANES), :] = xt[:, sl * LANES:(sl + 1) * LANES]

    x_hi = xt.astype(BF16)
    x_lo = (xt - x_hi.astype(F32)).astype(BF16)
    parts = jnp.dot(jnp.concatenate([x_hi, x_lo], axis=1), wr_ref[...], preferred_element_type=F32)
    logits = parts[:, 0:LANES] + parts[:, LANES:2 * LANES] + br_ref[...]
    lane = _lane_iota(logits.shape)
    lanef = lane.astype(F32)
    big = float(LANES)

    def first_argmax(vals, vmax):
        return jnp.min(jnp.where(vals == vmax, lanef, big), axis=1, keepdims=True)

    is_g = lane < N_EXPERT_GROUPS
    gl = jnp.where(is_g, logits, MASK_VALUE)
    gmax = jnp.max(gl, axis=1, keepdims=True)
    gidx = first_argmax(gl, gmax)
    g_p = 1.0 / jnp.sum(jnp.where(is_g, jnp.exp(gl - gmax), 0.0), axis=1, keepdims=True)

    base = N_EXPERT_GROUPS + EXPERTS_PER_GROUP * gidx
    in_grp = (lanef >= base) & (lanef < base + EXPERTS_PER_GROUP)
    el = jnp.where(in_grp, logits, MASK_VALUE)
    l1 = jnp.max(el, axis=1, keepdims=True)
    e1 = first_argmax(el, l1)
    el2 = jnp.where(lanef == e1, MASK_VALUE, el)
    l2 = jnp.max(el2, axis=1, keepdims=True)
    e2 = first_argmax(el2, l2)
    zsum = jnp.sum(jnp.where(in_grp, jnp.exp(el - l1), 0.0), axis=1, keepdims=True)
    p1 = 1.0 / zsum
    p2 = jnp.exp(l2 - l1) / zsum
    den = p1 + p2
    w1 = g_p * p1 / den
    w2 = g_p * p2 / den
    id1 = e1 - N_EXPERT_GROUPS
    id2 = e2 - N_EXPERT_GROUPS

    oh1 = lanef == id1
    oh2 = lanef == id2
    oh = jnp.where(oh1 | oh2, 1.0, 0.0)
    before = jnp.dot(tri_ref[...], oh.astype(BF16), preferred_element_type=F32) + carry[0:1, :]
    rank1 = jnp.sum(jnp.where(oh1, before, 0.0), axis=1, keepdims=True)
    rank2 = jnp.sum(jnp.where(oh2, before, 0.0), axis=1, keepdims=True)
    carry[0:1, :] = carry[0:1, :] + jnp.sum(oh, axis=0, keepdims=True)
    cnt_ref[...] = jnp.broadcast_to(carry[0:1, :], cnt_ref.shape)

    ri = jnp.where(lane == 0, id1, jnp.where(lane == 1, id2, jnp.where(lane == 2, rank1, rank2)))
    ri_ref[0] = ri.T[0:SUBLANES, :].astype(jnp.int32)
    rf = jnp.where(lane == 0, w1, w2)
    rf_ref[...] = rf[:, 0:SUBLANES]


def _outproj(attn, conv, x, woa, woc, gffn, wr, br):
    n = x.shape[0]
    tm = OUT_TILE
    tri = jnp.tril(jnp.ones((tm, tm), F32), k=-1).astype(BF16)

    def full(a):
        return pl.BlockSpec(a.shape, lambda i: (0,) * a.ndim)

    rows = lambda width: pl.BlockSpec((tm, width), lambda i: (i, 0))
    return pl.pallas_call(
        functools.partial(_outproj_kernel, tm),
        grid=(n // tm,),
        in_specs=[rows(ATTN_WIDTH), rows(CONV_WIDTH), rows(D_MODEL)] + [
            full(a) for a in (woa, woc, gffn, wr, br, tri)],
        out_specs=[rows(D_MODEL), pl.BlockSpec((tm * SUBLANES, LANES), lambda i: (i, 0)),
                   pl.BlockSpec((1, SUBLANES, tm), lambda i: (i, 0, 0)), rows(SUBLANES),
                   pl.BlockSpec((SUBLANES, LANES), lambda i: (0, 0))],
        out_shape=[jax.ShapeDtypeStruct((n, D_MODEL), F32), jax.ShapeDtypeStruct((n * SUBLANES, LANES), F32),
                   jax.ShapeDtypeStruct((n // tm, SUBLANES, tm), jnp.int32),
                   jax.ShapeDtypeStruct((n, SUBLANES), F32),
                   jax.ShapeDtypeStruct((SUBLANES, LANES), F32)],
        scratch_shapes=[pltpu.VMEM((SUBLANES, LANES), F32)],
        compiler_params=pltpu.CompilerParams(
            dimension_semantics=("arbitrary",), vmem_limit_bytes=VMEM_LIMIT),
        name="outproj_router",
    )(attn, conv, x, woa, woc, gffn, wr, br, tri)


def _dest_kernel(rs_ref, ri_ref, dest_ref):
    ri = ri_ref[0]
    experts = ri[0:TOP_K, :]
    start = jnp.zeros_like(experts)
    for e in range(N_EXPERTS):
        start = jnp.where(experts == e, rs_ref[e], start)
    dest_ref[0] = jnp.concatenate([start + ri[TOP_K:2 * TOP_K, :],
                                   jnp.zeros((SUBLANES - TOP_K, ri.shape[1]), jnp.int32)], axis=0)


def _dest_rows(ri_t, row_start):
    nt, _, tm = ri_t.shape
    grid_spec = pltpu.PrefetchScalarGridSpec(
        num_scalar_prefetch=1,
        grid=(nt,),
        in_specs=[pl.BlockSpec((1, SUBLANES, tm), lambda i, rs: (i, 0, 0))],
        out_specs=pl.BlockSpec((1, SUBLANES, tm), lambda i, rs: (i, 0, 0)),
    )
    return pl.pallas_call(
        _dest_kernel,
        grid_spec=grid_spec,
        out_shape=jax.ShapeDtypeStruct(ri_t.shape, jnp.int32),
        compiler_params=pltpu.CompilerParams(dimension_semantics=("arbitrary",)),
        name="moe_dest",
    )(row_start, ri_t)


def _tile_copy(src, src_row, dst, dst_row, sem):
    return pltpu.make_async_copy(src.at[pl.ds(pl.multiple_of(src_row * SUBLANES, SUBLANES), SUBLANES), :],
                                 dst.at[pl.ds(pl.multiple_of(dst_row * SUBLANES, SUBLANES), SUBLANES), :], sem)


def _dispatch_kernel(tm, bm, fz_ref, fv_ref, dest_ref, xt_ref, xs_hbm, zeros, sem, zsem):
    i = pl.program_id(0)
    block_rows = bm * SUBLANES

    def zero_block(z):
        start = pl.multiple_of(fz_ref[z] * block_rows, block_rows)
        return pltpu.make_async_copy(zeros, xs_hbm.at[pl.ds(start, block_rows), :], zsem.at[0])

    @pl.when(i == 0)
    def _():
        zeros[...] = jnp.zeros(zeros.shape, F32)
        for z in range(fz_ref.shape[0]):
            @pl.when(fv_ref[z] == 1)
            def _():
                zero_block(z).start()
        for z in range(fz_ref.shape[0]):
            @pl.when(fv_ref[z] == 1)
            def _():
                zero_block(z).wait()

    for g in range(0, tm, DMA_GROUP):
        dests = [(r, kk, dest_ref[0, kk, r]) for r in range(g, g + DMA_GROUP) for kk in range(TOP_K)]
        for r, kk, d in dests:
            _tile_copy(xt_ref, r, xs_hbm, d, sem.at[0]).start(priority=kk)
    for kk in range(TOP_K):
        pltpu.make_async_copy(xt_ref, xs_hbm.at[pl.ds(0, tm * SUBLANES), :], sem.at[0]).wait()


def _route_spec(dest, tm, extra_args):
    per = dest.shape[2] // tm
    nt = dest.shape[0] * per
    if extra_args == 0:
        index = lambda i: (jnp.minimum(i, nt - 1) // per, 0, jnp.minimum(i, nt - 1) % per)
    else:
        index = lambda i, *_: (jnp.minimum(i, nt - 1) // per, 0, jnp.minimum(i, nt - 1) % per)
    return nt, pl.BlockSpec((1, SUBLANES, tm), index, memory_space=pltpu.SMEM)


def _dispatch(xt_rows, dest, fill_blocks, fill_valid, nb):
    tm = ROUTE_TILE
    bm = EXPERT_ROWS
    nt, dest_spec = _route_spec(dest, tm, 2)
    grid_spec = pltpu.PrefetchScalarGridSpec(
        num_scalar_prefetch=2,
        grid=(nt,),
        in_specs=[dest_spec, pl.BlockSpec((tm * SUBLANES, LANES), lambda i, fz, fv: (i, 0))],
        out_specs=pl.BlockSpec(memory_space=pl.ANY),
        scratch_shapes=[pltpu.VMEM((bm * SUBLANES, LANES), F32), pltpu.SemaphoreType.DMA((1,)),
                        pltpu.SemaphoreType.DMA((1,))],
    )
    return pl.pallas_call(
        functools.partial(_dispatch_kernel, tm, bm),
        grid_spec=grid_spec,
        out_shape=jax.ShapeDtypeStruct((nb * bm * SUBLANES, LANES), F32),
        compiler_params=pltpu.CompilerParams(
            dimension_semantics=("arbitrary",), vmem_limit_bytes=VMEM_LIMIT),
        name="moe_dispatch",
    )(fill_blocks, fill_valid, dest, xt_rows)


def _expert_kernel(bm, be_ref, nu_ref, first_ref, nxt_ref, slot_ref, xs_ref, wg_hbm, wu_hbm, wd_hbm, ys_ref,
                   wg_buf, wu_buf, wd_buf, wg_bf, wu_bf, wd_bf, wsem):
    j = pl.program_id(0)

    def weight_copies(e, slot):
        return [pltpu.make_async_copy(src.at[e], buf.at[slot], wsem.at[m, slot])
                for m, (src, buf) in enumerate(((wg_hbm, wg_buf), (wu_hbm, wu_buf), (wd_hbm, wd_buf)))]

    @pl.when(j == 0)
    def _():
        for c in weight_copies(be_ref[0], 0):
            c.start()

    @pl.when(j < nu_ref[0])
    def _():
        slot = slot_ref[j]

        @pl.when(first_ref[j] == 1)
        def _():
            for c in weight_copies(0, slot):
                c.wait()

            @pl.when(nxt_ref[j] >= 0)
            def _():
                for c in weight_copies(nxt_ref[j], 1 - slot):
                    c.start()

            wg_bf[...] = wg_buf[slot].astype(BF16)
            wu_bf[...] = wu_buf[slot].astype(BF16)
            wd_bf[...] = wd_buf[slot].astype(BF16)

        x = jnp.concatenate([xs_ref[pl.ds(sl, bm, stride=SUBLANES), :] for sl in range(SUBLANES)],
                            axis=1).astype(BF16)
        hg = jnp.dot(x, wg_bf[...], preferred_element_type=F32)
        hu = jnp.dot(x, wu_bf[...], preferred_element_type=F32)
        hdn = hg * (1.0 / (1.0 + jnp.exp(-hg))) * hu
        y = jnp.dot(hdn.astype(BF16), wd_bf[...], preferred_element_type=F32)
        for sl in range(SUBLANES):
            ys_ref[pl.ds(sl, bm, stride=SUBLANES), :] = y[:, sl * LANES:(sl + 1) * LANES]

    @pl.when(j >= nu_ref[0])
    def _():
        ys_ref[...] = jnp.zeros(ys_ref.shape, F32)


def _experts(xs_rows, block_expert, n_used, w_gate, w_up, w_down):
    bm = EXPERT_ROWS
    nb = block_expert.shape[0]
    blk = jnp.arange(nb, dtype=jnp.int32)
    live = blk < n_used[0]
    first = live & ((blk == 0) | (block_expert != jnp.roll(block_expert, 1)))
    slot = (jnp.cumsum(first.astype(jnp.int32)) - 1) % 2
    later_first = first[None, :] & (blk[None, :] > blk[:, None])
    nxt_start = jnp.min(jnp.where(later_first, blk[None, :], nb), axis=1)
    nxt = jnp.sum(jnp.where(blk[None, :] == nxt_start[:, None], block_expert[None, :], 0), axis=1)
    nxt = jnp.where(nxt_start < nb, nxt, -1)
    grid_spec = pltpu.PrefetchScalarGridSpec(
        num_scalar_prefetch=5,
        grid=(nb,),
        in_specs=[
            pl.BlockSpec((bm * SUBLANES, LANES), lambda j, be, nu, *_: (jnp.minimum(j, nu[0] - 1), 0)),
            pl.BlockSpec(memory_space=pl.ANY),
            pl.BlockSpec(memory_space=pl.ANY),
            pl.BlockSpec(memory_space=pl.ANY),
        ],
        out_specs=pl.BlockSpec((bm * SUBLANES, LANES), lambda j, *_: (j, 0)),
        scratch_shapes=[pltpu.VMEM((2, D_MODEL, D_EXPERT), F32), pltpu.VMEM((2, D_MODEL, D_EXPERT), F32),
                        pltpu.VMEM((2, D_EXPERT, D_MODEL), F32),
                        pltpu.VMEM((D_MODEL, D_EXPERT), BF16), pltpu.VMEM((D_MODEL, D_EXPERT), BF16),
                        pltpu.VMEM((D_EXPERT, D_MODEL), BF16), pltpu.SemaphoreType.DMA((3, 2))],
    )
    return pl.pallas_call(
        functools.partial(_expert_kernel, bm),
        grid_spec=grid_spec,
        out_shape=jax.ShapeDtypeStruct((nb * bm * SUBLANES, LANES), F32),
        compiler_params=pltpu.CompilerParams(
            dimension_semantics=("arbitrary",), vmem_limit_bytes=VMEM_LIMIT),
        name="moe_experts",
    )(block_expert, n_used, first.astype(jnp.int32), nxt.astype(jnp.int32), slot.astype(jnp.int32),
      xs_rows, w_gate, w_up, w_down)


def _combine_kernel(tm, dest_ref, ys_hbm, h_ref, rf_ref, o_ref, ybuf, sem):
    i = pl.program_id(0)
    nt = pl.num_programs(0) - 1

    @pl.when(i < nt)
    def _():
        slot = i % 2
        for g in range(0, tm, DMA_GROUP):
            dests = [(r, kk, dest_ref[0, kk, r]) for r in range(g, g + DMA_GROUP) for kk in range(TOP_K)]
            for r, kk, d in dests:
                _tile_copy(ys_hbm, d, ybuf.at[kk, slot], r, sem.at[kk, slot]).start(priority=kk)

    @pl.when(i >= 1)
    def _():
        slot = (i - 1) % 2
        for kk in range(TOP_K):
            pltpu.make_async_copy(ys_hbm.at[pl.ds(0, tm * SUBLANES), :], ybuf.at[kk, slot], sem.at[kk, slot]).wait()
        rf = rf_ref[...]
        for sl in range(SUBLANES):
            cols = slice(sl * LANES, (sl + 1) * LANES)
            o_ref[:, cols] = (h_ref[:, cols]
                              + rf[:, 0:1] * ybuf[0, slot, pl.ds(sl, tm, stride=SUBLANES), :]
                              + rf[:, 1:2] * ybuf[1, slot, pl.ds(sl, tm, stride=SUBLANES), :])


def _combine(ys_rows, dest, h, rf):
    n = h.shape[0]
    tm = ROUTE_TILE
    nt, dest_spec = _route_spec(dest, tm, 0)
    prev = lambda i: jnp.maximum(i - 1, 0)
    return pl.pallas_call(
        functools.partial(_combine_kernel, tm),
        grid=(nt + 1,),
        in_specs=[
            dest_spec,
            pl.BlockSpec(memory_space=pl.ANY),
            pl.BlockSpec((tm, D_MODEL), lambda i: (prev(i), 0)),
            pl.BlockSpec((tm, SUBLANES), lambda i: (prev(i), 0)),
        ],
        out_specs=pl.BlockSpec((tm, D_MODEL), lambda i: (prev(i), 0)),
        out_shape=jax.ShapeDtypeStruct((n, D_MODEL), F32),
        scratch_shapes=[pltpu.VMEM((TOP_K, 2, tm * SUBLANES, LANES), F32), pltpu.SemaphoreType.DMA((TOP_K, 2))],
        compiler_params=pltpu.CompilerParams(
            dimension_semantics=("arbitrary",), vmem_limit_bytes=VMEM_LIMIT),
        name="moe_combine",
    )(dest, ys_rows, h, rf)


def _pad_lanes(a, width=LANES):
    return jnp.pad(a, ((0, 0), (0, width - a.shape[-1])))


def _layer(x, meta_tokens, g_mix, w_in, b_forget, q_norm_g, k_norm_g, conv_w, attn_out_g, conv_out_g,
           w_out, g_ffn, w_rg, b_rg, w_re, b_re, w_gate, w_up, w_down):
    bsz, slen, _ = x.shape
    n = bsz * slen
    w = ATTN_WIDTH

    wq, wk, wv, wf, wb, wc, whc = jnp.split(
        w_in, [w, 2 * w, 3 * w, 3 * w + ATTN_HEADS, 4 * w + ATTN_HEADS, 5 * w + ATTN_HEADS], axis=1)
    w1 = jnp.concatenate([wq, wk, wb, wc, whc, _pad_lanes(wf)], axis=1).astype(BF16)
    head_of = jnp.arange(w) // HEAD_DIM
    gmat = (head_of[:, None] == head_of[None, :]).astype(BF16)
    consts = (
        g_mix.reshape(1, D_MODEL), w1, wv.T.astype(BF16), _pad_lanes(b_forget.reshape(1, ATTN_HEADS)),
        jnp.tile(q_norm_g, ATTN_HEADS).reshape(1, w) * (HEAD_DIM ** -0.5 * LOG2E),
        jnp.tile(k_norm_g, ATTN_HEADS).reshape(1, w),
        jnp.pad(conv_w, ((0, SUBLANES - CONV_K), (0, 0))), conv_out_g.reshape(1, w), gmat)

    xm = jnp.pad(meta_tokens, ((0, META_TILE - N_META), (0, 0)))[None]
    km, vmt, kaugm, cum_m, uc_m = _inproj(xm, True, META_TILE, consts,
                                          jnp.zeros((SUBLANES, w), F32), jnp.zeros((1, LANES), F32))
    km, vmt, kaugm = km[0, :N_META], vmt[0, 0, :, :N_META], kaugm[0, :N_META]
    cum_m = cum_m[0, :N_META]
    halo = jnp.zeros((SUBLANES, w), F32).at[SUBLANES - 2:].set(uc_m[0, N_META - 2:N_META])
    cum0 = cum_m[N_META - 1:N_META]

    q, k, vt, conv, qaug, kaug = _inproj(x, False, IN_TILE, consts, halo, cum0)
    attn = _attention(q, qaug, k, kaug, vt, km, kaugm, vmt, attn_out_g.reshape(w, 1))

    wr = _pad_lanes(jnp.concatenate([w_rg, w_re], axis=1))
    wr_hi = wr.astype(BF16)
    wr_lo = (wr - wr_hi.astype(F32)).astype(BF16)
    wr = jnp.concatenate([jnp.concatenate([wr_hi, wr_lo], axis=1),
                          jnp.concatenate([wr_hi, jnp.zeros_like(wr_lo)], axis=1)], axis=0)
    br = _pad_lanes(jnp.concatenate([b_rg, b_re]).reshape(1, -1))
    wo = w_out.astype(BF16)
    h, xt, ri, rf, cnt = _outproj(attn.reshape(n, w), conv.reshape(n, w), x.reshape(n, D_MODEL),
                                  wo[:w], wo[w:], g_ffn.reshape(1, D_MODEL), wr, br)

    bm = EXPERT_ROWS
    nb = (n * TOP_K) // bm + N_EXPERTS
    counts = cnt[0, :N_EXPERTS].astype(jnp.int32)
    blocks_e = (counts + bm - 1) // bm
    blk_end = jnp.cumsum(blocks_e)
    row_start = (blk_end - blocks_e) * bm
    blk = jnp.arange(nb, dtype=jnp.int32)
    block_expert = jnp.minimum(jnp.sum(blk_end[None, :] <= blk[:, None], axis=1), N_EXPERTS - 1).astype(jnp.int32)
    n_used = blk_end[-1:].astype(jnp.int32)
    tail = n_used[0] + jnp.arange(N_EXPERTS, dtype=jnp.int32)
    fill_blocks = jnp.clip(jnp.concatenate([blk_end - 1, tail]), 0, nb - 1).astype(jnp.int32)
    fill_valid = jnp.concatenate([counts % bm != 0, tail < nb]).astype(jnp.int32)

    dest = _dest_rows(ri, row_start.astype(jnp.int32))
    xs = _dispatch(xt, dest, fill_blocks, fill_valid, nb)
    ys = _experts(xs, block_expert, n_used, w_gate, w_up, w_down)
    out = _combine(ys, dest, h, rf)
    return out.reshape(bsz, slen, D_MODEL)


def kernel(x, meta_tokens, norm_mix_g, w_in, b_forget, q_norm_g, k_norm_g, conv_w, attn_out_g, conv_out_g,
           w_out, norm_ffn_g, w_router_group, b_router_group, w_router_expert, b_router_expert,
           w_gate, w_up, w_down):
    assert norm_mix_g.shape[0] == 1, "single-layer block"
    return _layer(x, meta_tokens, norm_mix_g[0], w_in[0], b_forget[0], q_norm_g[0], k_norm_g[0], conv_w[0],
                  attn_out_g[0], conv_out_g[0], w_out[0], norm_ffn_g[0], w_router_group[0],
                  b_router_group[0], w_router_expert[0], b_router_expert[0], w_gate[0], w_up[0], w_down[0])
```

```python
import functools

import jax
import jax.numpy as jnp
from jax import lax
from jax.experimental import pallas as pl
from jax.experimental.pallas import tpu as pltpu

D_MODEL = 1024
N_META = 16
HEAD_DIM = 64
ATTN_HEADS = 8
ATTN_WIDTH = ATTN_HEADS * HEAD_DIM
CONV_WIDTH = D_MODEL - ATTN_WIDTH
CONV_K = 3
N_EXPERT_GROUPS = 4
EXPERTS_PER_GROUP = 8
N_EXPERTS = N_EXPERT_GROUPS * EXPERTS_PER_GROUP
TOP_K = 2
D_EXPERT = 512
EPS = 1e-6
MASK_VALUE = -1e30
LOG2E = 1.4426950408889634
AUG = 6

LANES = 128
SUBLANES = 8
MXU_DIM = 256
HEAD_PAIRS = ATTN_HEADS * HEAD_DIM // LANES
PROJ_PAD_COLS = 5 * ATTN_WIDTH + LANES
VMEM_LIMIT = 56 * 1024 * 1024

IN_TILE = 512
META_TILE = 128
ATTN_TILE = 512
OUT_TILE = 512
EXPERT_ROWS = 256
ROUTE_TILE = 256
DMA_GROUP = 8

F32 = jnp.float32
BF16 = jnp.bfloat16
ROW_SUBLANES = D_MODEL // LANES


def _lane_iota(shape):
    return lax.broadcasted_iota(jnp.int32, shape, len(shape) - 1)


def _store_token_rows(ref, x, rows):
    for sl in range(ROW_SUBLANES):
        ref[pl.ds(sl, rows, stride=ROW_SUBLANES), :] = x[:, sl * LANES:(sl + 1) * LANES]


def _load_token_rows(ref, rows):
    return jnp.concatenate([ref[pl.ds(sl, rows, stride=ROW_SUBLANES), :] for sl in range(ROW_SUBLANES)], axis=1)


def _inproj_kernel(is_meta, tm, x_ref, gmix_ref, w1_ref, wvt_ref, bf_ref, gq_ref, gk_ref, cw_ref, gco_ref,
                   gmat_ref, tri_ref, eq_ref, ek_ref, halo_ref, cum0_ref, *rest):
    if is_meta:
        k_ref, vt_ref, kaug_ref, cum_ref, uc_ref, ucbuf, carry = rest
    else:
        q_ref, k_ref, vt_ref, conv_ref, qaug_ref, kaug_ref, ucbuf, carry = rest
    t = pl.program_id(1)

    @pl.when(t == 0)
    def _():
        ucbuf[0:SUBLANES, :] = halo_ref[...]
        carry[0:1, :] = cum0_ref[...]

    x = x_ref[0]
    ms = jnp.mean(x * x, axis=-1, keepdims=True)
    u = (x * lax.rsqrt(ms + EPS)) * gmix_ref[...]
    ub = u.astype(BF16)
    proj = jnp.dot(ub, w1_ref[...], preferred_element_type=F32)
    vt_ref[0, 0] = lax.dot_general(wvt_ref[...], ub, (((1,), (1,)), ((), ())),
                                   preferred_element_type=F32).astype(BF16)

    def head_norm(z, g):
        z2 = (z * z).astype(BF16)
        half = gmat_ref.shape[0]
        ssq = jnp.concatenate([jnp.dot(z2[:, c:c + half], gmat_ref[...], preferred_element_type=F32)
                               for c in range(0, z.shape[1], half)], axis=1)
        return z * lax.rsqrt(ssq * (1.0 / HEAD_DIM) + EPS) * g

    w = ATTN_WIDTH
    kn = head_norm(proj[:, w:2 * w], gk_ref[...])
    k_ref[0] = kn.astype(BF16)

    z = proj[:, 5 * w:5 * w + LANES] + bf_ref[...]
    ls = jnp.minimum(z, 0.0) - jnp.log1p(jnp.exp(-jnp.abs(z)))
    lane_c = _lane_iota(ls.shape)
    is_head = lane_c < ATTN_HEADS
    ls = jnp.where(is_head, ls, 0.0)

    def pieces(val):
        p_hi = val.astype(BF16).astype(F32)
        rem = val - p_hi
        p_mid = rem.astype(BF16).astype(F32)
        p_lo = (rem - p_mid).astype(BF16).astype(F32)
        return jnp.where(is_head, p_hi,
                         jnp.where(lane_c < 2 * ATTN_HEADS, pltpu.roll(p_mid, ATTN_HEADS, axis=1),
                                   pltpu.roll(p_lo, 2 * ATTN_HEADS, axis=1))).astype(BF16)

    cs3 = jnp.dot(tri_ref[...], pieces(ls), preferred_element_type=F32)
    cs = cs3 + pltpu.roll(cs3, LANES - ATTN_HEADS, axis=1) + pltpu.roll(cs3, LANES - 2 * ATTN_HEADS, axis=1)
    cum = jnp.where(is_head, cs, 0.0) + carry[0:1, :]
    carry[0:1, :] = cum[tm - 1:tm, :]

    packed = pieces(cum * LOG2E)
    in_aug = lane_c < AUG * ATTN_HEADS
    ones_k = in_aug & ((lane_c % AUG) < AUG // 2)
    kaug = jnp.where(ones_k, 1.0, -jnp.dot(packed, ek_ref[...], preferred_element_type=F32))
    kaug_ref[0] = kaug.astype(BF16)

    uc = proj[:, 3 * w:4 * w] * proj[:, 4 * w:5 * w]
    ucbuf[SUBLANES:SUBLANES + tm, :] = uc
    uc1 = ucbuf[SUBLANES - 1:SUBLANES - 1 + tm, :]
    uc2 = ucbuf[SUBLANES - 2:SUBLANES - 2 + tm, :]
    ucbuf[0:SUBLANES, :] = uc[tm - SUBLANES:tm, :]

    if is_meta:
        cum_ref[0] = cum
        uc_ref[0] = uc
        return

    qn = head_norm(proj[:, 0:w], gq_ref[...])
    q_ref[0] = qn.astype(BF16)
    y = cw_ref[0:1, :] * uc2 + cw_ref[1:2, :] * uc1 + cw_ref[2:3, :] * uc
    conv = proj[:, 2 * w:3 * w] * y
    conv_ref[0] = head_norm(conv, gco_ref[...]).astype(BF16)
    ones_q = in_aug & ((lane_c % AUG) >= AUG // 2)
    qaug = jnp.where(ones_q, 1.0, jnp.dot(packed, eq_ref[...], preferred_element_type=F32))
    qaug_ref[0] = qaug.astype(BF16)


def _inproj(x, is_meta, tm, consts, halo, cum0):
    bsz, tlen, _ = x.shape
    nt = tlen // tm
    gmix, w1, wvt, bfp, gq, gk, cw, gco, gmat = consts
    tri = jnp.tril(jnp.ones((tm, tm), F32)).astype(BF16)
    w = ATTN_WIDTH
    src = jnp.arange(LANES)
    dst = jnp.arange(LANES)
    piece, head = src // ATTN_HEADS, src % ATTN_HEADS
    valid = src < 3 * ATTN_HEADS
    eq = (valid[:, None] & (dst[None, :] == (AUG * head + piece)[:, None])).astype(BF16)
    ek = (valid[:, None] & (dst[None, :] == (AUG * head + AUG // 2 + piece)[:, None])).astype(BF16)

    def full(a):
        return pl.BlockSpec(a.shape, lambda b, t: (0,) * a.ndim)

    in_specs = [pl.BlockSpec((1, tm, D_MODEL), lambda b, t: (b, t, 0))] + [
        full(a) for a in (gmix, w1, wvt, bfp, gq, gk, cw, gco, gmat, tri, eq, ek, halo, cum0)]
    tok = lambda width: pl.BlockSpec((1, tm, width), lambda b, t: (b, t, 0))
    vt_spec = pl.BlockSpec((1, 1, w, tm), lambda b, t: (b, t, 0, 0))
    if is_meta:
        out_shape = [jax.ShapeDtypeStruct((bsz, tlen, w), BF16),
                     jax.ShapeDtypeStruct((bsz, nt, w, tm), BF16),
                     jax.ShapeDtypeStruct((bsz, tlen, LANES), BF16),
                     jax.ShapeDtypeStruct((bsz, tlen, LANES), F32),
                     jax.ShapeDtypeStruct((bsz, tlen, w), F32)]
        out_specs = [tok(w), vt_spec, tok(LANES), tok(LANES), tok(w)]
    else:
        out_shape = [jax.ShapeDtypeStruct((bsz, tlen, w), BF16)] * 2 + [
            jax.ShapeDtypeStruct((bsz, nt, w, tm), BF16),
            jax.ShapeDtypeStruct((bsz, tlen, w), BF16),
            jax.ShapeDtypeStruct((bsz, tlen, LANES), BF16),
            jax.ShapeDtypeStruct((bsz, tlen, LANES), BF16)]
        out_specs = [tok(w), tok(w), vt_spec, tok(w), tok(LANES), tok(LANES)]
    return pl.pallas_call(
        functools.partial(_inproj_kernel, is_meta, tm),
        grid=(bsz, nt),
        in_specs=in_specs,
        out_specs=out_specs,
        out_shape=out_shape,
        scratch_shapes=[pltpu.VMEM((tm + SUBLANES, w), F32), pltpu.VMEM((SUBLANES, LANES), F32)],
        compiler_params=pltpu.CompilerParams(
            dimension_semantics=("arbitrary", "arbitrary"), vmem_limit_bytes=VMEM_LIMIT),
        name="inproj_meta" if is_meta else "inproj",
    )(x, gmix, w1, wvt, bfp, gq, gk, cw, gco, gmat, tri, eq, ek, halo, cum0)


def _attn_kernel(tq, q_ref, qaug_ref, k_ref, kaug_ref, vt_ref, km_ref, kaugm_ref, vmt_ref, gao_ref,
                 o_ref, m_sc, l_sc, acc_sc, sa_sc, sb_sc):
    hp = pl.program_id(1)
    qi = pl.program_id(2)
    q = q_ref[0]
    qaug = qaug_ref[0]
    lane = _lane_iota(q.shape)
    first = lane < HEAD_DIM
    zero = jnp.zeros_like(q)
    qcat = []
    for hh in range(2):
        qh = jnp.where(first, q, zero) if hh == 0 else jnp.where(first, zero, q)
        lo = AUG * (2 * hp + hh)
        qa = jnp.where((lane >= lo) & (lane < lo + AUG), qaug, zero)
        qcat.append(jnp.concatenate([qh, qa], axis=1))

    m_sc[...] = jnp.full(m_sc.shape, MASK_VALUE, F32)
    l_sc[...] = jnp.zeros(l_sc.shape, F32)
    acc_sc[...] = jnp.zeros(acc_sc.shape, F32)

    def scores_t(kcat):
        return tuple(lax.dot_general(kcat, qcat[hh], (((1,), (1,)), ((), ())), preferred_element_type=F32)
                     for hh in range(2))

    def update(hh, st, vtb):
        m_old = m_sc[hh]
        m_new = jnp.maximum(m_old, jnp.max(st, axis=0, keepdims=True))
        alpha = jnp.exp2(m_old - m_new)
        p = jnp.exp2(st - m_new)
        l_sc[hh] = alpha * l_sc[hh] + jnp.sum(p, axis=0, keepdims=True)
        vh = vtb[hh * HEAD_DIM:(hh + 1) * HEAD_DIM, :]
        acc_sc[hh] = alpha * acc_sc[hh] + jnp.dot(vh, p.astype(BF16), preferred_element_type=F32)
        m_sc[hh] = m_new

    def put_scores(j, buf):
        start = pl.multiple_of(j * tq, tq)
        sts = scores_t(jnp.concatenate([k_ref[0, pl.ds(start, tq), :], kaug_ref[0, pl.ds(start, tq), :]], axis=1))
        for hh in range(2):
            buf[hh] = sts[hh]

    def consume(buf, j, masked):
        vtb = vt_ref[0, j]
        for hh in range(2):
            st = buf[hh]
            if masked:
                key = lax.broadcasted_iota(jnp.int32, st.shape, 0)
                qry = lax.broadcasted_iota(jnp.int32, st.shape, 1)
                st = jnp.where(key <= qry, st, MASK_VALUE)
            update(hh, st, vtb)

    st_m = scores_t(jnp.concatenate([km_ref[...], kaugm_ref[...]], axis=1))
    put_scores(0, sa_sc)
    vmt = vmt_ref[...]
    for hh in range(2):
        update(hh, st_m[hh], vmt)

    def body(t, carry):
        j = 2 * t
        put_scores(j + 1, sb_sc)
        consume(sa_sc, j, False)
        put_scores(j + 2, sa_sc)
        consume(sb_sc, j + 1, False)
        return carry

    lax.fori_loop(0, lax.shift_right_logical(qi, 1), body, 0)
    odd = (qi & 1) == 1

    @pl.when(odd)
    def _():
        put_scores(qi, sb_sc)
        consume(sa_sc, qi - 1, False)
        consume(sb_sc, qi, True)

    @pl.when(jnp.logical_not(odd))
    def _():
        consume(sa_sc, qi, True)

    ot = jnp.concatenate([acc_sc[0] / l_sc[0], acc_sc[1] / l_sc[1]], axis=0)
    o2 = ot * ot
    ms0 = jnp.sum(o2[0:HEAD_DIM], axis=0, keepdims=True) * (1.0 / HEAD_DIM)
    ms1 = jnp.sum(o2[HEAD_DIM:], axis=0, keepdims=True) * (1.0 / HEAD_DIM)
    inv = jnp.concatenate([jnp.broadcast_to(lax.rsqrt(ms0 + EPS), (HEAD_DIM, tq)),
                           jnp.broadcast_to(lax.rsqrt(ms1 + EPS), (HEAD_DIM, tq))], axis=0)
    o_ref[0] = (ot * inv * gao_ref[...]).T.astype(BF16)


def _attention(q, qaug, k, kaug, vt, km, kaugm, vmt, gao):
    bsz, slen, w = q.shape
    tq = ATTN_TILE
    nq = slen // tq
    return pl.pallas_call(
        functools.partial(_attn_kernel, tq),
        grid=(bsz, HEAD_PAIRS, nq),
        in_specs=[
            pl.BlockSpec((1, tq, LANES), lambda b, p, i: (b, i, p)),
            pl.BlockSpec((1, tq, LANES), lambda b, p, i: (b, i, 0)),
            pl.BlockSpec((1, slen, LANES), lambda b, p, i: (b, 0, p)),
            pl.BlockSpec((1, slen, LANES), lambda b, p, i: (b, 0, 0)),
            pl.BlockSpec((1, nq, LANES, tq), lambda b, p, i: (b, 0, p, 0)),
            pl.BlockSpec((N_META, LANES), lambda b, p, i: (0, p)),
            pl.BlockSpec((N_META, LANES), lambda b, p, i: (0, 0)),
            pl.BlockSpec((LANES, N_META), lambda b, p, i: (p, 0)),
            pl.BlockSpec((LANES, 1), lambda b, p, i: (p, 0)),
        ],
        out_specs=pl.BlockSpec((1, tq, LANES), lambda b, p, i: (b, i, p)),
        out_shape=jax.ShapeDtypeStruct((bsz, slen, w), BF16),
        scratch_shapes=[pltpu.VMEM((2, 1, tq), F32), pltpu.VMEM((2, 1, tq), F32),
                        pltpu.VMEM((2, HEAD_DIM, tq), F32),
                        pltpu.VMEM((2, tq, tq), F32), pltpu.VMEM((2, tq, tq), F32)],
        compiler_params=pltpu.CompilerParams(
            dimension_semantics=("arbitrary", "arbitrary", "arbitrary"), vmem_limit_bytes=VMEM_LIMIT),
        name="fox_attention",
    )(q, qaug, k, kaug, vt, km, kaugm, vmt, gao)


ROUTE_ROWS = 64


def _outproj_kernel(tm, attn_ref, conv_ref, x_ref, wo_ref, gffn_ref, wrt_ref, brc_ref, tri_ref,
                    h_ref, xt_ref, ri_ref, rf_ref, cnt_ref, carry):
    i = pl.program_id(0)

    @pl.when(i == 0)
    def _():
        carry[...] = jnp.zeros(carry.shape, F32)

    mixed = jnp.concatenate([attn_ref[...], conv_ref[...]], axis=1)
    h = x_ref[...] + jnp.dot(mixed, wo_ref[...], preferred_element_type=F32)
    h_ref[...] = h
    ms = jnp.mean(h * h, axis=-1, keepdims=True)
    xt = (h * lax.rsqrt(ms + EPS)) * gffn_ref[...]
    _store_token_rows(xt_ref, xt, tm)

    x_hi = xt.astype(BF16)
    x_lo = (xt - x_hi.astype(F32)).astype(BF16)
    parts = lax.dot_general(wrt_ref[...], jnp.concatenate([x_hi, x_lo], axis=1), (((1,), (1,)), ((), ())),
                            preferred_element_type=F32)
    logits = (parts[0:ROUTE_ROWS] + parts[LANES:LANES + ROUTE_ROWS]) + brc_ref[0:ROUTE_ROWS, 0:1]
    row = lax.broadcasted_iota(jnp.int32, logits.shape, 0)
    rowf = row.astype(F32)
    big = float(LANES)

    def first_argmax(vals, vmax):
        return jnp.min(jnp.where(vals == vmax, rowf, big), axis=0, keepdims=True)

    is_g = row < N_EXPERT_GROUPS
    gl = jnp.where(is_g, logits, MASK_VALUE)
    gmax = jnp.max(gl, axis=0, keepdims=True)
    gidx = first_argmax(gl, gmax)
    g_p = 1.0 / jnp.sum(jnp.where(is_g, jnp.exp(gl - gmax), 0.0), axis=0, keepdims=True)

    base = N_EXPERT_GROUPS + EXPERTS_PER_GROUP * gidx
    in_grp = (rowf >= base) & (rowf < base + EXPERTS_PER_GROUP)
    el = jnp.where(in_grp, logits, MASK_VALUE)
    l1 = jnp.max(el, axis=0, keepdims=True)
    e1 = first_argmax(el, l1)
    el2 = jnp.where(rowf == e1, MASK_VALUE, el)
    l2 = jnp.max(el2, axis=0, keepdims=True)
    e2 = first_argmax(el2, l2)
    zsum = jnp.sum(jnp.where(in_grp, jnp.exp(el - l1), 0.0), axis=0, keepdims=True)
    p1 = 1.0 / zsum
    p2 = jnp.exp(l2 - l1) / zsum
    den = p1 + p2
    w1 = g_p * p1 / den
    w2 = g_p * p2 / den
    id1 = e1 - N_EXPERT_GROUPS
    id2 = e2 - N_EXPERT_GROUPS

    oh1 = rowf == id1
    oh2 = rowf == id2
    oh = jnp.where(oh1 | oh2, 1.0, 0.0)
    before = jnp.dot(oh.astype(BF16), tri_ref[...], preferred_element_type=F32) + carry[:, 0:1]
    rank1 = jnp.sum(jnp.where(oh1, before, 0.0), axis=0, keepdims=True)
    rank2 = jnp.sum(jnp.where(oh2, before, 0.0), axis=0, keepdims=True)
    carry[...] = carry[...] + jnp.sum(oh, axis=1, keepdims=True)
    cnt_ref[...] = carry[...]

    pad_i = jnp.zeros((SUBLANES - 2 * TOP_K, tm), F32)
    ri_ref[0] = jnp.concatenate([id1, id2, rank1, rank2, pad_i], axis=0).astype(jnp.int32)
    pad_f = jnp.zeros((LANES - TOP_K, tm), F32)
    rf_ref[...] = jnp.concatenate([w1, w2, pad_f], axis=0).T[:, 0:SUBLANES]


def _outproj(attn, conv, x, wo, gffn, wrt, brc):
    n = x.shape[0]
    tm = OUT_TILE
    tri = jnp.triu(jnp.ones((tm, tm), F32), k=1).astype(BF16)

    def full(a):
        return pl.BlockSpec(a.shape, lambda i: (0,) * a.ndim)

    rows = lambda width: pl.BlockSpec((tm, width), lambda i: (i, 0))
    return pl.pallas_call(
        functools.partial(_outproj_kernel, tm),
        grid=(n // tm,),
        in_specs=[rows(ATTN_WIDTH), rows(CONV_WIDTH), rows(D_MODEL)] + [
            full(a) for a in (wo, gffn, wrt, brc, tri)],
        out_specs=[rows(D_MODEL), pl.BlockSpec((tm * ROW_SUBLANES, LANES), lambda i: (i, 0)),
                   pl.BlockSpec((1, SUBLANES, tm), lambda i: (i, 0, 0)), rows(SUBLANES),
                   pl.BlockSpec((ROUTE_ROWS, LANES), lambda i: (0, 0))],
        out_shape=[jax.ShapeDtypeStruct((n, D_MODEL), F32),
                   jax.ShapeDtypeStruct((n * ROW_SUBLANES, LANES), F32),
                   jax.ShapeDtypeStruct((n // tm, SUBLANES, tm), jnp.int32),
                   jax.ShapeDtypeStruct((n, SUBLANES), F32),
                   jax.ShapeDtypeStruct((ROUTE_ROWS, LANES), F32)],
        scratch_shapes=[pltpu.VMEM((ROUTE_ROWS, LANES), F32)],
        compiler_params=pltpu.CompilerParams(
            dimension_semantics=("arbitrary",), vmem_limit_bytes=VMEM_LIMIT),
        name="outproj_router",
    )(attn, conv, x, wo, gffn, wrt, brc, tri)


def _dest_kernel(rs_ref, ri_ref, dest_ref):
    ri = ri_ref[0]
    experts = ri[0:TOP_K, :]
    start = jnp.zeros_like(experts)
    for e in range(N_EXPERTS):
        start = jnp.where(experts == e, rs_ref[e], start)
    dest_ref[0] = jnp.concatenate([start + ri[TOP_K:2 * TOP_K, :],
                                   jnp.zeros((SUBLANES - TOP_K, ri.shape[1]), jnp.int32)], axis=0)


def _dest_rows(ri_t, row_start):
    nt, _, tm = ri_t.shape
    grid_spec = pltpu.PrefetchScalarGridSpec(
        num_scalar_prefetch=1,
        grid=(nt,),
        in_specs=[pl.BlockSpec((1, SUBLANES, tm), lambda i, rs: (i, 0, 0))],
        out_specs=pl.BlockSpec((1, SUBLANES, tm), lambda i, rs: (i, 0, 0)),
    )
    return pl.pallas_call(
        _dest_kernel,
        grid_spec=grid_spec,
        out_shape=jax.ShapeDtypeStruct(ri_t.shape, jnp.int32),
        compiler_params=pltpu.CompilerParams(dimension_semantics=("arbitrary",)),
        name="moe_dest",
    )(row_start, ri_t)


def _tile_copy(src, src_row, dst, dst_row, sem):
    rs = ROW_SUBLANES
    return pltpu.make_async_copy(src.at[pl.ds(pl.multiple_of(src_row * rs, rs), rs), :],
                                 dst.at[pl.ds(pl.multiple_of(dst_row * rs, rs), rs), :], sem)


def _dispatch_kernel(tm, bm, fz_ref, fv_ref, dest_ref, xt_ref, xs_hbm, zeros, sem, zsem):
    i = pl.program_id(0)
    block_rows = bm * ROW_SUBLANES

    def zero_block(z):
        start = pl.multiple_of(fz_ref[z] * block_rows, block_rows)
        return pltpu.make_async_copy(zeros, xs_hbm.at[pl.ds(start, block_rows), :], zsem.at[0])

    @pl.when(i == 0)
    def _():
        zeros[...] = jnp.zeros(zeros.shape, F32)
        for z in range(fz_ref.shape[0]):
            @pl.when(fv_ref[z] == 1)
            def _():
                zero_block(z).start()
        for z in range(fz_ref.shape[0]):
            @pl.when(fv_ref[z] == 1)
            def _():
                zero_block(z).wait()

    for g in range(0, tm, DMA_GROUP):
        dests = [(r, kk, dest_ref[0, kk, r]) for r in range(g, g + DMA_GROUP) for kk in range(TOP_K)]
        for r, kk, d in dests:
            _tile_copy(xt_ref, r, xs_hbm, d, sem.at[0]).start(priority=kk)
    for kk in range(TOP_K):
        pltpu.make_async_copy(xt_ref, xs_hbm.at[pl.ds(0, tm * ROW_SUBLANES), :], sem.at[0]).wait()


def _route_spec(dest, tm, extra_args):
    per = dest.shape[2] // tm
    nt = dest.shape[0] * per
    if extra_args == 0:
        index = lambda i: (jnp.minimum(i, nt - 1) // per, 0, jnp.minimum(i, nt - 1) % per)
    else:
        index = lambda i, *_: (jnp.minimum(i, nt - 1) // per, 0, jnp.minimum(i, nt - 1) % per)
    return nt, pl.BlockSpec((1, SUBLANES, tm), index, memory_space=pltpu.SMEM)


def _dispatch(xt_rows, dest, fill_blocks, fill_valid, nb):
    tm = ROUTE_TILE
    bm = EXPERT_ROWS
    nt, dest_spec = _route_spec(dest, tm, 2)
    grid_spec = pltpu.PrefetchScalarGridSpec(
        num_scalar_prefetch=2,
        grid=(nt,),
        in_specs=[dest_spec, pl.BlockSpec((tm * ROW_SUBLANES, LANES), lambda i, fz, fv: (i, 0))],
        out_specs=pl.BlockSpec(memory_space=pl.ANY),
        scratch_shapes=[pltpu.VMEM((bm * ROW_SUBLANES, LANES), F32), pltpu.SemaphoreType.DMA((1,)),
                        pltpu.SemaphoreType.DMA((1,))],
    )
    return pl.pallas_call(
        functools.partial(_dispatch_kernel, tm, bm),
        grid_spec=grid_spec,
        out_shape=jax.ShapeDtypeStruct((nb * bm * ROW_SUBLANES, LANES), F32),
        compiler_params=pltpu.CompilerParams(
            dimension_semantics=("arbitrary",), vmem_limit_bytes=VMEM_LIMIT),
        name="moe_dispatch",
    )(fill_blocks, fill_valid, dest, xt_rows)


def _expert_kernel(bm, be_ref, nu_ref, first_ref, nxt_ref, slot_ref, xs_ref, wg_hbm, wu_hbm, wd_hbm, ys_ref,
                   wg_buf, wu_buf, wd_buf, wg_bf, wu_bf, wd_bf, wsem):
    j = pl.program_id(0)

    def weight_copies(e, slot):
        return [pltpu.make_async_copy(src.at[e], buf.at[slot], wsem.at[m, slot])
                for m, (src, buf) in enumerate(((wg_hbm, wg_buf), (wu_hbm, wu_buf), (wd_hbm, wd_buf)))]

    @pl.when(j == 0)
    def _():
        for c in weight_copies(be_ref[0], 0):
            c.start()

    @pl.when(j < nu_ref[0])
    def _():
        slot = slot_ref[j]

        @pl.when(first_ref[j] == 1)
        def _():
            for c in weight_copies(0, slot):
                c.wait()

            @pl.when(nxt_ref[j] >= 0)
            def _():
                for c in weight_copies(nxt_ref[j], 1 - slot):
                    c.start()

            wg_bf[...] = wg_buf[slot].astype(BF16)
            wu_bf[...] = wu_buf[slot].astype(BF16)
            wd_bf[...] = wd_buf[slot].astype(BF16)

        x = _load_token_rows(xs_ref, bm).astype(BF16)
        hg = jnp.dot(x, wg_bf[...], preferred_element_type=F32)
        hu = jnp.dot(x, wu_bf[...], preferred_element_type=F32)
        hdn = hg * (1.0 / (1.0 + jnp.exp(-hg))) * hu
        y = jnp.dot(hdn.astype(BF16), wd_bf[...], preferred_element_type=F32)
        _store_token_rows(ys_ref, y, bm)

    @pl.when(j >= nu_ref[0])
    def _():
        ys_ref[...] = jnp.zeros(ys_ref.shape, F32)


def _experts(xs_rows, block_expert, n_used, w_gate, w_up, w_down):
    bm = EXPERT_ROWS
    nb = block_expert.shape[0]
    blk = jnp.arange(nb, dtype=jnp.int32)
    live = blk < n_used[0]
    first = live & ((blk == 0) | (block_expert != jnp.roll(block_expert, 1)))
    slot = (jnp.cumsum(first.astype(jnp.int32)) - 1) % 2
    later_first = first[None, :] & (blk[None, :] > blk[:, None])
    nxt_start = jnp.min(jnp.where(later_first, blk[None, :], nb), axis=1)
    nxt = jnp.sum(jnp.where(blk[None, :] == nxt_start[:, None], block_expert[None, :], 0), axis=1)
    nxt = jnp.where(nxt_start < nb, nxt, -1)
    grid_spec = pltpu.PrefetchScalarGridSpec(
        num_scalar_prefetch=5,
        grid=(nb,),
        in_specs=[
            pl.BlockSpec((bm * ROW_SUBLANES, LANES), lambda j, be, nu, *_: (jnp.minimum(j, nu[0] - 1), 0)),
            pl.BlockSpec(memory_space=pl.ANY),
            pl.BlockSpec(memory_space=pl.ANY),
            pl.BlockSpec(memory_space=pl.ANY),
        ],
        out_specs=pl.BlockSpec((bm * ROW_SUBLANES, LANES), lambda j, *_: (j, 0)),
        scratch_shapes=[pltpu.VMEM((2, D_MODEL, D_EXPERT), F32), pltpu.VMEM((2, D_MODEL, D_EXPERT), F32),
                        pltpu.VMEM((2, D_EXPERT, D_MODEL), F32),
                        pltpu.VMEM((D_MODEL, D_EXPERT), BF16), pltpu.VMEM((D_MODEL, D_EXPERT), BF16),
                        pltpu.VMEM((D_EXPERT, D_MODEL), BF16), pltpu.SemaphoreType.DMA((3, 2))],
    )
    return pl.pallas_call(
        functools.partial(_expert_kernel, bm),
        grid_spec=grid_spec,
        out_shape=jax.ShapeDtypeStruct((nb * bm * ROW_SUBLANES, LANES), F32),
        compiler_params=pltpu.CompilerParams(
            dimension_semantics=("arbitrary",), vmem_limit_bytes=VMEM_LIMIT),
        name="moe_experts",
    )(block_expert, n_used, first.astype(jnp.int32), nxt.astype(jnp.int32), slot.astype(jnp.int32),
      xs_rows, w_gate, w_up, w_down)


def _combine_kernel(tm, dest_ref, ys_hbm, h_ref, rf_ref, o_ref, ybuf, sem):
    i = pl.program_id(0)
    nt = pl.num_programs(0) - 1

    @pl.when(i < nt)
    def _():
        slot = i % 2
        for g in range(0, tm, DMA_GROUP):
            dests = [(r, kk, dest_ref[0, kk, r]) for r in range(g, g + DMA_GROUP) for kk in range(TOP_K)]
            for r, kk, d in dests:
                _tile_copy(ys_hbm, d, ybuf.at[kk, slot], r, sem.at[kk, slot]).start(priority=kk)

    @pl.when(i >= 1)
    def _():
        slot = (i - 1) % 2
        for kk in range(TOP_K):
            pltpu.make_async_copy(ys_hbm.at[pl.ds(0, tm * ROW_SUBLANES), :], ybuf.at[kk, slot],
                                  sem.at[kk, slot]).wait()
        rf = rf_ref[...]
        o_ref[...] = (h_ref[...] + rf[:, 0:1] * _load_token_rows(ybuf.at[0, slot], tm)
                      + rf[:, 1:2] * _load_token_rows(ybuf.at[1, slot], tm))


def _combine(ys_rows, dest, h, rf):
    n = h.shape[0]
    tm = ROUTE_TILE
    nt, dest_spec = _route_spec(dest, tm, 0)
    prev = lambda i: jnp.maximum(i - 1, 0)
    return pl.pallas_call(
        functools.partial(_combine_kernel, tm),
        grid=(nt + 1,),
        in_specs=[
            dest_spec,
            pl.BlockSpec(memory_space=pl.ANY),
            pl.BlockSpec((tm, D_MODEL), lambda i: (prev(i), 0)),
            pl.BlockSpec((tm, SUBLANES), lambda i: (prev(i), 0)),
        ],
        out_specs=pl.BlockSpec((tm, D_MODEL), lambda i: (prev(i), 0)),
        out_shape=jax.ShapeDtypeStruct((n, D_MODEL), F32),
        scratch_shapes=[pltpu.VMEM((TOP_K, 2, tm * ROW_SUBLANES, LANES), F32),
                        pltpu.SemaphoreType.DMA((TOP_K, 2))],
        compiler_params=pltpu.CompilerParams(
            dimension_semantics=("arbitrary",), vmem_limit_bytes=VMEM_LIMIT),
        name="moe_combine",
    )(dest, ys_rows, h, rf)


def _pad_lanes(a, width=LANES):
    return jnp.pad(a, ((0, 0), (0, width - a.shape[-1])))


def _layer(x, meta_tokens, g_mix, w_in, b_forget, q_norm_g, k_norm_g, conv_w, attn_out_g, conv_out_g,
           w_out, g_ffn, w_rg, b_rg, w_re, b_re, w_gate, w_up, w_down):
    bsz, slen, _ = x.shape
    n = bsz * slen
    w = ATTN_WIDTH

    wq, wk, wv, wf, wb, wc, whc = jnp.split(
        w_in, [w, 2 * w, 3 * w, 3 * w + ATTN_HEADS, 4 * w + ATTN_HEADS, 5 * w + ATTN_HEADS], axis=1)
    w1 = jnp.concatenate([wq, wk, wb, wc, whc, _pad_lanes(wf)], axis=1).astype(BF16)
    head_of = jnp.arange(MXU_DIM) // HEAD_DIM
    gmat = (head_of[:, None] == head_of[None, :]).astype(BF16)
    consts = (
        g_mix.reshape(1, D_MODEL), w1, wv.T.astype(BF16), _pad_lanes(b_forget.reshape(1, ATTN_HEADS)),
        jnp.tile(q_norm_g, ATTN_HEADS).reshape(1, w) * (HEAD_DIM ** -0.5 * LOG2E),
        jnp.tile(k_norm_g, ATTN_HEADS).reshape(1, w),
        jnp.pad(conv_w, ((0, SUBLANES - CONV_K), (0, 0))), conv_out_g.reshape(1, w), gmat)

    xm = jnp.pad(meta_tokens, ((0, META_TILE - N_META), (0, 0)))[None]
    km, vmt, kaugm, cum_m, uc_m = _inproj(xm, True, META_TILE, consts,
                                          jnp.zeros((SUBLANES, w), F32), jnp.zeros((1, LANES), F32))
    km, vmt, kaugm = km[0, :N_META], vmt[0, 0, :, :N_META], kaugm[0, :N_META]
    cum_m = cum_m[0, :N_META]
    halo = jnp.zeros((SUBLANES, w), F32).at[SUBLANES - 2:].set(uc_m[0, N_META - 2:N_META])
    cum0 = cum_m[N_META - 1:N_META]

    q, k, vt, conv, qaug, kaug = _inproj(x, False, IN_TILE, consts, halo, cum0)
    attn = _attention(q, qaug, k, kaug, vt, km, kaugm, vmt, attn_out_g.reshape(w, 1))

    wr = _pad_lanes(jnp.concatenate([w_rg, w_re], axis=1)).T
    wr_hi = wr.astype(BF16)
    wr_lo = (wr - wr_hi.astype(F32)).astype(BF16)
    wrt = jnp.concatenate([jnp.concatenate([wr_hi, wr_hi], axis=1),
                           jnp.concatenate([wr_lo, jnp.zeros_like(wr_lo)], axis=1)], axis=0)
    brc = jnp.broadcast_to(_pad_lanes(jnp.concatenate([b_rg, b_re]).reshape(1, -1)).T, (LANES, LANES))
    h, xt, ri, rf, cnt = _outproj(attn.reshape(n, w), conv.reshape(n, w), x.reshape(n, D_MODEL),
                                  w_out.astype(BF16), g_ffn.reshape(1, D_MODEL), wrt, brc)

    bm = EXPERT_ROWS
    nb = (n * TOP_K) // bm + N_EXPERTS
    counts = cnt[:N_EXPERTS, 0].astype(jnp.int32)
    blocks_e = (counts + bm - 1) // bm
    blk_end = jnp.cumsum(blocks_e)
    row_start = (blk_end - blocks_e) * bm
    blk = jnp.arange(nb, dtype=jnp.int32)
    block_expert = jnp.minimum(jnp.sum(blk_end[None, :] <= blk[:, None], axis=1), N_EXPERTS - 1).astype(jnp.int32)
    n_used = blk_end[-1:].astype(jnp.int32)
    tail = n_used[0] + jnp.arange(N_EXPERTS, dtype=jnp.int32)
    fill_blocks = jnp.clip(jnp.concatenate([blk_end - 1, tail]), 0, nb - 1).astype(jnp.int32)
    fill_valid = jnp.concatenate([counts % bm != 0, tail < nb]).astype(jnp.int32)

    dest = _dest_rows(ri, row_start.astype(jnp.int32))
    xs = _dispatch(xt, dest, fill_blocks, fill_valid, nb)
    ys = _experts(xs, block_expert, n_used, w_gate, w_up, w_down)
    out = _combine(ys, dest, h, rf)
    return out.reshape(bsz, slen, D_MODEL)


def kernel(x, meta_tokens, norm_mix_g, w_in, b_forget, q_norm_g, k_norm_g, conv_w, attn_out_g, conv_out_g,
           w_out, norm_ffn_g, w_router_group, b_router_group, w_router_expert, b_router_expert,
           w_gate, w_up, w_down):
    assert norm_mix_g.shape[0] == 1, "single-layer block"
    return _layer(x, meta_tokens, norm_mix_g[0], w_in[0], b_forget[0], q_norm_g[0], k_norm_g[0], conv_w[0],
                  attn_out_g[0], conv_out_g[0], w_out[0], norm_ffn_g[0], w_router_group[0],
                  b_router_group[0], w_router_expert[0], b_router_expert[0], w_gate[0], w_up[0], w_down[0])
```

```python
import functools

import jax
import jax.numpy as jnp
from jax import lax
from jax.experimental import pallas as pl
from jax.experimental.pallas import tpu as pltpu

D_MODEL = 1024
N_META = 16
HEAD_DIM = 64
ATTN_HEADS = 8
ATTN_WIDTH = ATTN_HEADS * HEAD_DIM
CONV_WIDTH = D_MODEL - ATTN_WIDTH
CONV_K = 3
N_EXPERT_GROUPS = 4
EXPERTS_PER_GROUP = 8
N_EXPERTS = N_EXPERT_GROUPS * EXPERTS_PER_GROUP
TOP_K = 2
D_EXPERT = 512
EPS = 1e-6
MASK_VALUE = -1e30
LOG2E = 1.4426950408889634
AUG = 6

LANES = 128
SUBLANES = 8
MXU_DIM = 256
HEAD_PAIRS = ATTN_HEADS * HEAD_DIM // LANES
PROJ_PAD_COLS = 5 * ATTN_WIDTH + LANES
VMEM_LIMIT = 56 * 1024 * 1024

IN_TILE = 512
META_TILE = 128
ATTN_Q_TILE = 1024
ATTN_K_TILE = 512
OUT_TILE = 512
EXPERT_ROWS = 256
ROUTE_TILE = 256
DMA_GROUP = 8

F32 = jnp.float32
BF16 = jnp.bfloat16
ROW_SUBLANES = D_MODEL // LANES


def _lane_iota(shape):
    return lax.broadcasted_iota(jnp.int32, shape, len(shape) - 1)


def _store_token_rows(ref, x, rows):
    for sl in range(ROW_SUBLANES):
        ref[pl.ds(sl, rows, stride=ROW_SUBLANES), :] = x[:, sl * LANES:(sl + 1) * LANES]


def _load_token_rows(ref, rows):
    return jnp.concatenate([ref[pl.ds(sl, rows, stride=ROW_SUBLANES), :] for sl in range(ROW_SUBLANES)], axis=1)


def _inproj_kernel(is_meta, tm, x_ref, gmix_ref, w1_ref, wvt_ref, bf_ref, gq_ref, gk_ref, cw_ref, gco_ref,
                   gmat_ref, tri_ref, eq_ref, ek_ref, halo_ref, cum0_ref, *rest):
    if is_meta:
        k_ref, vt_ref, kaug_ref, cum_ref, uc_ref, ucbuf, carry = rest
    else:
        q_ref, k_ref, vt_ref, conv_ref, qaug_ref, kaug_ref, ucbuf, carry = rest
    t = pl.program_id(1)

    @pl.when(t == 0)
    def _():
        ucbuf[0:SUBLANES, :] = halo_ref[...]
        carry[0:1, :] = cum0_ref[...]

    x = x_ref[0]
    ms = jnp.mean(x * x, axis=-1, keepdims=True)
    u = (x * lax.rsqrt(ms + EPS)) * gmix_ref[...]
    ub = u.astype(BF16)
    proj = jnp.dot(ub, w1_ref[...], preferred_element_type=F32)
    vt_ref[0, 0] = lax.dot_general(wvt_ref[...], ub, (((1,), (1,)), ((), ())),
                                   preferred_element_type=F32).astype(BF16)

    def head_norm(z, g):
        z2 = (z * z).astype(BF16)
        half = gmat_ref.shape[0]
        ssq = jnp.concatenate([jnp.dot(z2[:, c:c + half], gmat_ref[...], preferred_element_type=F32)
                               for c in range(0, z.shape[1], half)], axis=1)
        return z * lax.rsqrt(ssq * (1.0 / HEAD_DIM) + EPS) * g

    w = ATTN_WIDTH
    kn = head_norm(proj[:, w:2 * w], gk_ref[...])
    k_ref[0] = kn.astype(BF16)

    z = proj[:, 5 * w:5 * w + LANES] + bf_ref[...]
    ls = jnp.minimum(z, 0.0) - jnp.log1p(jnp.exp(-jnp.abs(z)))
    lane_c = _lane_iota(ls.shape)
    is_head = lane_c < ATTN_HEADS
    ls = jnp.where(is_head, ls, 0.0)

    def pieces(val):
        p_hi = val.astype(BF16).astype(F32)
        rem = val - p_hi
        p_mid = rem.astype(BF16).astype(F32)
        p_lo = (rem - p_mid).astype(BF16).astype(F32)
        return jnp.where(is_head, p_hi,
                         jnp.where(lane_c < 2 * ATTN_HEADS, pltpu.roll(p_mid, ATTN_HEADS, axis=1),
                                   pltpu.roll(p_lo, 2 * ATTN_HEADS, axis=1))).astype(BF16)

    cs3 = jnp.dot(tri_ref[...], pieces(ls), preferred_element_type=F32)
    cs = cs3 + pltpu.roll(cs3, LANES - ATTN_HEADS, axis=1) + pltpu.roll(cs3, LANES - 2 * ATTN_HEADS, axis=1)
    cum = jnp.where(is_head, cs, 0.0) + carry[0:1, :]
    carry[0:1, :] = cum[tm - 1:tm, :]

    packed = pieces(cum * LOG2E)
    in_aug = lane_c < AUG * ATTN_HEADS
    ones_k = in_aug & ((lane_c % AUG) < AUG // 2)
    kaug = jnp.where(ones_k, 1.0, -jnp.dot(packed, ek_ref[...], preferred_element_type=F32))
    kaug_ref[0] = kaug.astype(BF16)

    uc = proj[:, 3 * w:4 * w] * proj[:, 4 * w:5 * w]
    ucbuf[SUBLANES:SUBLANES + tm, :] = uc
    uc1 = ucbuf[SUBLANES - 1:SUBLANES - 1 + tm, :]
    uc2 = ucbuf[SUBLANES - 2:SUBLANES - 2 + tm, :]
    ucbuf[0:SUBLANES, :] = uc[tm - SUBLANES:tm, :]

    if is_meta:
        cum_ref[0] = cum
        uc_ref[0] = uc
        return

    qn = head_norm(proj[:, 0:w], gq_ref[...])
    q_ref[0] = qn.astype(BF16)
    y = cw_ref[0:1, :] * uc2 + cw_ref[1:2, :] * uc1 + cw_ref[2:3, :] * uc
    conv = proj[:, 2 * w:3 * w] * y
    conv_ref[0] = head_norm(conv, gco_ref[...]).astype(BF16)
    ones_q = in_aug & ((lane_c % AUG) >= AUG // 2)
    qaug = jnp.where(ones_q, 1.0, jnp.dot(packed, eq_ref[...], preferred_element_type=F32))
    qaug_ref[0] = qaug.astype(BF16)


def _inproj(x, is_meta, tm, consts, halo, cum0):
    bsz, tlen, _ = x.shape
    nt = tlen // tm
    gmix, w1, wvt, bfp, gq, gk, cw, gco, gmat = consts
    tri = jnp.tril(jnp.ones((tm, tm), F32)).astype(BF16)
    w = ATTN_WIDTH
    src = jnp.arange(LANES)
    dst = jnp.arange(LANES)
    piece, head = src // ATTN_HEADS, src % ATTN_HEADS
    valid = src < 3 * ATTN_HEADS
    eq = (valid[:, None] & (dst[None, :] == (AUG * head + piece)[:, None])).astype(BF16)
    ek = (valid[:, None] & (dst[None, :] == (AUG * head + AUG // 2 + piece)[:, None])).astype(BF16)

    def full(a):
        return pl.BlockSpec(a.shape, lambda b, t: (0,) * a.ndim)

    in_specs = [pl.BlockSpec((1, tm, D_MODEL), lambda b, t: (b, t, 0))] + [
        full(a) for a in (gmix, w1, wvt, bfp, gq, gk, cw, gco, gmat, tri, eq, ek, halo, cum0)]
    tok = lambda width: pl.BlockSpec((1, tm, width), lambda b, t: (b, t, 0))
    vt_spec = pl.BlockSpec((1, 1, w, tm), lambda b, t: (b, t, 0, 0))
    if is_meta:
        out_shape = [jax.ShapeDtypeStruct((bsz, tlen, w), BF16),
                     jax.ShapeDtypeStruct((bsz, nt, w, tm), BF16),
                     jax.ShapeDtypeStruct((bsz, tlen, LANES), BF16),
                     jax.ShapeDtypeStruct((bsz, tlen, LANES), F32),
                     jax.ShapeDtypeStruct((bsz, tlen, w), F32)]
        out_specs = [tok(w), vt_spec, tok(LANES), tok(LANES), tok(w)]
    else:
        out_shape = [jax.ShapeDtypeStruct((bsz, tlen, w), BF16)] * 2 + [
            jax.ShapeDtypeStruct((bsz, nt, w, tm), BF16),
            jax.ShapeDtypeStruct((bsz, tlen, w), BF16),
            jax.ShapeDtypeStruct((bsz, tlen, LANES), BF16),
            jax.ShapeDtypeStruct((bsz, tlen, LANES), BF16)]
        out_specs = [tok(w), tok(w), vt_spec, tok(w), tok(LANES), tok(LANES)]
    return pl.pallas_call(
        functools.partial(_inproj_kernel, is_meta, tm),
        grid=(bsz, nt),
        in_specs=in_specs,
        out_specs=out_specs,
        out_shape=out_shape,
        scratch_shapes=[pltpu.VMEM((tm + SUBLANES, w), F32), pltpu.VMEM((SUBLANES, LANES), F32)],
        compiler_params=pltpu.CompilerParams(
            dimension_semantics=("arbitrary", "arbitrary"), vmem_limit_bytes=VMEM_LIMIT),
        name="inproj_meta" if is_meta else "inproj",
    )(x, gmix, w1, wvt, bfp, gq, gk, cw, gco, gmat, tri, eq, ek, halo, cum0)


def _attn_kernel(tq, tk, q_ref, qaug_ref, k_ref, kaug_ref, vt_ref, km_ref, kaugm_ref, vmt_ref, gao_ref,
                 o_ref, m_sc, l_sc, acc_sc, sa_sc, sb_sc):
    hp = pl.program_id(1)
    qi = pl.program_id(2)
    q = q_ref[0]
    qaug = qaug_ref[0]
    lane = _lane_iota(q.shape)
    first = lane < HEAD_DIM
    zero = jnp.zeros_like(q)
    qcat = []
    for hh in range(2):
        qh = jnp.where(first, q, zero) if hh == 0 else jnp.where(first, zero, q)
        lo = AUG * (2 * hp + hh)
        qa = jnp.where((lane >= lo) & (lane < lo + AUG), qaug, zero)
        qcat.append(jnp.concatenate([qh, qa], axis=1))

    m_sc[...] = jnp.full(m_sc.shape, MASK_VALUE, F32)
    l_sc[...] = jnp.zeros(l_sc.shape, F32)
    acc_sc[...] = jnp.zeros(acc_sc.shape, F32)

    def scores_t(kcat, q_lo=0):
        return tuple(lax.dot_general(kcat, qcat[hh][q_lo:], (((1,), (1,)), ((), ())), preferred_element_type=F32)
                     for hh in range(2))

    def update(hh, st, vtb, q_lo=0):
        m_old = m_sc[hh, :, q_lo:]
        m_new = jnp.maximum(m_old, jnp.max(st, axis=0, keepdims=True))
        alpha = jnp.exp2(m_old - m_new)
        p = jnp.exp2(st - m_new)
        l_sc[hh, :, q_lo:] = alpha * l_sc[hh, :, q_lo:] + jnp.sum(p, axis=0, keepdims=True)
        vh = vtb[hh * HEAD_DIM:(hh + 1) * HEAD_DIM, :]
        acc_sc[hh, :, q_lo:] = alpha * acc_sc[hh, :, q_lo:] + jnp.dot(vh, p.astype(BF16),
                                                                     preferred_element_type=F32)
        m_sc[hh, :, q_lo:] = m_new

    def key_block(j):
        start = pl.multiple_of(j * tk, tk)
        return jnp.concatenate([k_ref[0, pl.ds(start, tk), :], kaug_ref[0, pl.ds(start, tk), :]], axis=1)

    def put_scores(j, buf, q_lo=0):
        sts = scores_t(key_block(j), q_lo)
        for hh in range(2):
            buf[hh, :, q_lo:] = sts[hh]

    def consume(buf, j, diagonal=False, q_lo=0):
        vtb = vt_ref[0, j]
        for hh in range(2):
            st = buf[hh, :, q_lo:]
            if diagonal:
                key = lax.broadcasted_iota(jnp.int32, st.shape, 0)
                qry = lax.broadcasted_iota(jnp.int32, st.shape, 1)
                st = jnp.where(key <= qry, st, MASK_VALUE)
            update(hh, st, vtb, q_lo)

    assert tq == 2 * tk
    st_m = scores_t(jnp.concatenate([km_ref[...], kaugm_ref[...]], axis=1))
    put_scores(0, sa_sc)
    vmt = vmt_ref[...]
    for hh in range(2):
        update(hh, st_m[hh], vmt)

    def body(t, carry):
        j = 2 * t
        put_scores(j + 1, sb_sc)
        consume(sa_sc, j)
        put_scores(j + 2, sa_sc)
        consume(sb_sc, j + 1)
        return carry

    lax.fori_loop(0, qi, body, 0)
    put_scores(2 * qi + 1, sb_sc, tk)
    consume(sa_sc, 2 * qi, diagonal=True)
    consume(sb_sc, 2 * qi + 1, diagonal=True, q_lo=tk)

    ot = jnp.concatenate([acc_sc[0] / l_sc[0], acc_sc[1] / l_sc[1]], axis=0)
    o2 = ot * ot
    ms0 = jnp.sum(o2[0:HEAD_DIM], axis=0, keepdims=True) * (1.0 / HEAD_DIM)
    ms1 = jnp.sum(o2[HEAD_DIM:], axis=0, keepdims=True) * (1.0 / HEAD_DIM)
    inv = jnp.concatenate([jnp.broadcast_to(lax.rsqrt(ms0 + EPS), (HEAD_DIM, tq)),
                           jnp.broadcast_to(lax.rsqrt(ms1 + EPS), (HEAD_DIM, tq))], axis=0)
    o_ref[0] = (ot * inv * gao_ref[...]).T.astype(BF16)


def _attention(q, qaug, k, kaug, vt, km, kaugm, vmt, gao):
    bsz, slen, w = q.shape
    tq, tk = ATTN_Q_TILE, ATTN_K_TILE
    nq, nk = slen // tq, slen // tk
    return pl.pallas_call(
        functools.partial(_attn_kernel, tq, tk),
        grid=(bsz, HEAD_PAIRS, nq),
        in_specs=[
            pl.BlockSpec((1, tq, LANES), lambda b, p, i: (b, i, p)),
            pl.BlockSpec((1, tq, LANES), lambda b, p, i: (b, i, 0)),
            pl.BlockSpec((1, slen, LANES), lambda b, p, i: (b, 0, p)),
            pl.BlockSpec((1, slen, LANES), lambda b, p, i: (b, 0, 0)),
            pl.BlockSpec((1, nk, LANES, tk), lambda b, p, i: (b, 0, p, 0)),
            pl.BlockSpec((N_META, LANES), lambda b, p, i: (0, p)),
            pl.BlockSpec((N_META, LANES), lambda b, p, i: (0, 0)),
            pl.BlockSpec((LANES, N_META), lambda b, p, i: (p, 0)),
            pl.BlockSpec((LANES, 1), lambda b, p, i: (p, 0)),
        ],
        out_specs=pl.BlockSpec((1, tq, LANES), lambda b, p, i: (b, i, p)),
        out_shape=jax.ShapeDtypeStruct((bsz, slen, w), BF16),
        scratch_shapes=[pltpu.VMEM((2, 1, tq), F32), pltpu.VMEM((2, 1, tq), F32),
                        pltpu.VMEM((2, HEAD_DIM, tq), F32),
                        pltpu.VMEM((2, tk, tq), F32), pltpu.VMEM((2, tk, tq), F32)],
        compiler_params=pltpu.CompilerParams(
            dimension_semantics=("arbitrary", "arbitrary", "arbitrary"), vmem_limit_bytes=VMEM_LIMIT),
        name="fox_attention",
    )(q, qaug, k, kaug, vt, km, kaugm, vmt, gao)


ROUTE_ROWS = 64


def _outproj_kernel(tm, attn_ref, conv_ref, x_ref, wo_ref, gffn_ref, wrt_ref, brc_ref, tri_ref,
                    h_ref, xt_ref, ri_ref, rf_ref, cnt_ref, carry):
    i = pl.program_id(0)

    @pl.when(i == 0)
    def _():
        carry[...] = jnp.zeros(carry.shape, F32)

    mixed = jnp.concatenate([attn_ref[...], conv_ref[...]], axis=1)
    h = x_ref[...] + jnp.dot(mixed, wo_ref[...], preferred_element_type=F32)
    h_ref[...] = h
    ms = jnp.mean(h * h, axis=-1, keepdims=True)
    xt = (h * lax.rsqrt(ms + EPS)) * gffn_ref[...]
    _store_token_rows(xt_ref, xt, tm)

    x_hi = xt.astype(BF16)
    x_lo = (xt - x_hi.astype(F32)).astype(BF16)
    parts = lax.dot_general(wrt_ref[...], jnp.concatenate([x_hi, x_lo], axis=1), (((1,), (1,)), ((), ())),
                            preferred_element_type=F32)
    logits = (parts[0:ROUTE_ROWS] + parts[LANES:LANES + ROUTE_ROWS]) + brc_ref[0:ROUTE_ROWS, 0:1]
    row = lax.broadcasted_iota(jnp.int32, logits.shape, 0)
    rowf = row.astype(F32)
    big = float(LANES)

    def first_argmax(vals, vmax):
        return jnp.min(jnp.where(vals == vmax, rowf, big), axis=0, keepdims=True)

    is_g = row < N_EXPERT_GROUPS
    gl = jnp.where(is_g, logits, MASK_VALUE)
    gmax = jnp.max(gl, axis=0, keepdims=True)
    gidx = first_argmax(gl, gmax)
    g_p = 1.0 / jnp.sum(jnp.where(is_g, jnp.exp(gl - gmax), 0.0), axis=0, keepdims=True)

    base = N_EXPERT_GROUPS + EXPERTS_PER_GROUP * gidx
    in_grp = (rowf >= base) & (rowf < base + EXPERTS_PER_GROUP)
    el = jnp.where(in_grp, logits, MASK_VALUE)
    l1 = jnp.max(el, axis=0, keepdims=True)
    e1 = first_argmax(el, l1)
    el2 = jnp.where(rowf == e1, MASK_VALUE, el)
    l2 = jnp.max(el2, axis=0, keepdims=True)
    e2 = first_argmax(el2, l2)
    zsum = jnp.sum(jnp.where(in_grp, jnp.exp(el - l1), 0.0), axis=0, keepdims=True)
    p1 = 1.0 / zsum
    p2 = jnp.exp(l2 - l1) / zsum
    den = p1 + p2
    w1 = g_p * p1 / den
    w2 = g_p * p2 / den
    id1 = e1 - N_EXPERT_GROUPS
    id2 = e2 - N_EXPERT_GROUPS

    oh1 = rowf == id1
    oh2 = rowf == id2
    oh = jnp.where(oh1 | oh2, 1.0, 0.0)
    before = jnp.dot(oh.astype(BF16), tri_ref[...], preferred_element_type=F32) + carry[:, 0:1]
    rank1 = jnp.sum(jnp.where(oh1, before, 0.0), axis=0, keepdims=True)
    rank2 = jnp.sum(jnp.where(oh2, before, 0.0), axis=0, keepdims=True)
    carry[...] = carry[...] + jnp.sum(oh, axis=1, keepdims=True)
    cnt_ref[...] = carry[...]

    pad_i = jnp.zeros((SUBLANES - 2 * TOP_K, tm), F32)
    ri_ref[0] = jnp.concatenate([id1, id2, rank1, rank2, pad_i], axis=0).astype(jnp.int32)
    pad_f = jnp.zeros((LANES - TOP_K, tm), F32)
    rf_ref[...] = jnp.concatenate([w1, w2, pad_f], axis=0).T[:, 0:SUBLANES]


def _outproj(attn, conv, x, wo, gffn, wrt, brc):
    n = x.shape[0]
    tm = OUT_TILE
    tri = jnp.triu(jnp.ones((tm, tm), F32), k=1).astype(BF16)

    def full(a):
        return pl.BlockSpec(a.shape, lambda i: (0,) * a.ndim)

    rows = lambda width: pl.BlockSpec((tm, width), lambda i: (i, 0))
    return pl.pallas_call(
        functools.partial(_outproj_kernel, tm),
        grid=(n // tm,),
        in_specs=[rows(ATTN_WIDTH), rows(CONV_WIDTH), rows(D_MODEL)] + [
            full(a) for a in (wo, gffn, wrt, brc, tri)],
        out_specs=[rows(D_MODEL), pl.BlockSpec((tm * ROW_SUBLANES, LANES), lambda i: (i, 0)),
                   pl.BlockSpec((1, SUBLANES, tm), lambda i: (i, 0, 0)), rows(SUBLANES),
                   pl.BlockSpec((ROUTE_ROWS, LANES), lambda i: (0, 0))],
        out_shape=[jax.ShapeDtypeStruct((n, D_MODEL), F32),
                   jax.ShapeDtypeStruct((n * ROW_SUBLANES, LANES), F32),
                   jax.ShapeDtypeStruct((n // tm, SUBLANES, tm), jnp.int32),
                   jax.ShapeDtypeStruct((n, SUBLANES), F32),
                   jax.ShapeDtypeStruct((ROUTE_ROWS, LANES), F32)],
        scratch_shapes=[pltpu.VMEM((ROUTE_ROWS, LANES), F32)],
        compiler_params=pltpu.CompilerParams(
            dimension_semantics=("arbitrary",), vmem_limit_bytes=VMEM_LIMIT),
        name="outproj_router",
    )(attn, conv, x, wo, gffn, wrt, brc, tri)


def _dest_kernel(rs_ref, ri_ref, dest_ref):
    ri = ri_ref[...]
    experts = ri[:, 0:TOP_K, :]
    start = jnp.zeros_like(experts)
    for e in range(N_EXPERTS):
        start = jnp.where(experts == e, rs_ref[e], start)
    dest_ref[...] = jnp.concatenate([start + ri[:, TOP_K:2 * TOP_K, :],
                                     jnp.zeros((ri.shape[0], SUBLANES - TOP_K, ri.shape[2]), jnp.int32)], axis=1)


def _dest_rows(ri_t, row_start):
    grid_spec = pltpu.PrefetchScalarGridSpec(
        num_scalar_prefetch=1,
        grid=(1,),
        in_specs=[pl.BlockSpec(ri_t.shape, lambda i, rs: (0, 0, 0))],
        out_specs=pl.BlockSpec(ri_t.shape, lambda i, rs: (0, 0, 0)),
    )
    return pl.pallas_call(
        _dest_kernel,
        grid_spec=grid_spec,
        out_shape=jax.ShapeDtypeStruct(ri_t.shape, jnp.int32),
        compiler_params=pltpu.CompilerParams(dimension_semantics=("arbitrary",)),
        name="moe_dest",
    )(row_start, ri_t)


def _tile_copy(src, src_row, dst, dst_row, sem):
    rs = ROW_SUBLANES
    return pltpu.make_async_copy(src.at[pl.ds(pl.multiple_of(src_row * rs, rs), rs), :],
                                 dst.at[pl.ds(pl.multiple_of(dst_row * rs, rs), rs), :], sem)


def _dispatch_kernel(tm, bm, fz_ref, fv_ref, dest_ref, xt_ref, xs_hbm, zeros, sem, zsem):
    i = pl.program_id(0)
    block_rows = bm * ROW_SUBLANES

    def zero_block(z):
        start = pl.multiple_of(fz_ref[z] * block_rows, block_rows)
        return pltpu.make_async_copy(zeros, xs_hbm.at[pl.ds(start, block_rows), :], zsem.at[0])

    @pl.when(i == 0)
    def _():
        zeros[...] = jnp.zeros(zeros.shape, F32)
        for z in range(fz_ref.shape[0]):
            @pl.when(fv_ref[z] == 1)
            def _():
                zero_block(z).start()
        for z in range(fz_ref.shape[0]):
            @pl.when(fv_ref[z] == 1)
            def _():
                zero_block(z).wait()

    for g in range(0, tm, DMA_GROUP):
        dests = [(r, kk, dest_ref[0, kk, r]) for r in range(g, g + DMA_GROUP) for kk in range(TOP_K)]
        for r, kk, d in dests:
            _tile_copy(xt_ref, r, xs_hbm, d, sem.at[0]).start(priority=kk)
    for kk in range(TOP_K):
        pltpu.make_async_copy(xt_ref, xs_hbm.at[pl.ds(0, tm * ROW_SUBLANES), :], sem.at[0]).wait()


def _route_spec(dest, tm, extra_args):
    per = dest.shape[2] // tm
    nt = dest.shape[0] * per
    if extra_args == 0:
        index = lambda i: (jnp.minimum(i, nt - 1) // per, 0, jnp.minimum(i, nt - 1) % per)
    else:
        index = lambda i, *_: (jnp.minimum(i, nt - 1) // per, 0, jnp.minimum(i, nt - 1) % per)
    return nt, pl.BlockSpec((1, SUBLANES, tm), index, memory_space=pltpu.SMEM)


def _dispatch(xt_rows, dest, fill_blocks, fill_valid, nb):
    tm = ROUTE_TILE
    bm = EXPERT_ROWS
    nt, dest_spec = _route_spec(dest, tm, 2)
    grid_spec = pltpu.PrefetchScalarGridSpec(
        num_scalar_prefetch=2,
        grid=(nt,),
        in_specs=[dest_spec, pl.BlockSpec((tm * ROW_SUBLANES, LANES), lambda i, fz, fv: (i, 0))],
        out_specs=pl.BlockSpec(memory_space=pl.ANY),
        scratch_shapes=[pltpu.VMEM((bm * ROW_SUBLANES, LANES), F32), pltpu.SemaphoreType.DMA((1,)),
                        pltpu.SemaphoreType.DMA((1,))],
    )
    return pl.pallas_call(
        functools.partial(_dispatch_kernel, tm, bm),
        grid_spec=grid_spec,
        out_shape=jax.ShapeDtypeStruct((nb * bm * ROW_SUBLANES, LANES), F32),
        compiler_params=pltpu.CompilerParams(
            dimension_semantics=("arbitrary",), vmem_limit_bytes=VMEM_LIMIT),
        name="moe_dispatch",
    )(fill_blocks, fill_valid, dest, xt_rows)


def _expert_kernel(bm, be_ref, nu_ref, first_ref, nxt_ref, slot_ref, xs_ref, wg_hbm, wu_hbm, wd_hbm, ys_ref,
                   wg_buf, wu_buf, wd_buf, wg_bf, wu_bf, wd_bf, wsem):
    j = pl.program_id(0)

    def weight_copies(e, slot):
        return [pltpu.make_async_copy(src.at[e], buf.at[slot], wsem.at[m, slot])
                for m, (src, buf) in enumerate(((wg_hbm, wg_buf), (wu_hbm, wu_buf), (wd_hbm, wd_buf)))]

    @pl.when(j == 0)
    def _():
        for c in weight_copies(be_ref[0], 0):
            c.start()

    @pl.when(j < nu_ref[0])
    def _():
        slot = slot_ref[j]

        @pl.when(first_ref[j] == 1)
        def _():
            for c in weight_copies(0, slot):
                c.wait()

            @pl.when(nxt_ref[j] >= 0)
            def _():
                for c in weight_copies(nxt_ref[j], 1 - slot):
                    c.start()

            wg_bf[...] = wg_buf[slot].astype(BF16)
            wu_bf[...] = wu_buf[slot].astype(BF16)
            wd_bf[...] = wd_buf[slot].astype(BF16)

        x = _load_token_rows(xs_ref, bm).astype(BF16)
        hg = jnp.dot(x, wg_bf[...], preferred_element_type=F32)
        hu = jnp.dot(x, wu_bf[...], preferred_element_type=F32)
        hdn = hg * (1.0 / (1.0 + jnp.exp(-hg))) * hu
        y = jnp.dot(hdn.astype(BF16), wd_bf[...], preferred_element_type=F32)
        _store_token_rows(ys_ref, y, bm)

    @pl.when(j >= nu_ref[0])
    def _():
        ys_ref[...] = jnp.zeros(ys_ref.shape, F32)


def _experts(xs_rows, block_expert, n_used, w_gate, w_up, w_down):
    bm = EXPERT_ROWS
    nb = block_expert.shape[0]
    blk = jnp.arange(nb, dtype=jnp.int32)
    live = blk < n_used[0]
    first = live & ((blk == 0) | (block_expert != jnp.roll(block_expert, 1)))
    slot = (jnp.cumsum(first.astype(jnp.int32)) - 1) % 2
    later_first = first[None, :] & (blk[None, :] > blk[:, None])
    nxt_start = jnp.min(jnp.where(later_first, blk[None, :], nb), axis=1)
    nxt = jnp.sum(jnp.where(blk[None, :] == nxt_start[:, None], block_expert[None, :], 0), axis=1)
    nxt = jnp.where(nxt_start < nb, nxt, -1)
    grid_spec = pltpu.PrefetchScalarGridSpec(
        num_scalar_prefetch=5,
        grid=(nb,),
        in_specs=[
            pl.BlockSpec((bm * ROW_SUBLANES, LANES), lambda j, be, nu, *_: (jnp.minimum(j, nu[0] - 1), 0)),
            pl.BlockSpec(memory_space=pl.ANY),
            pl.BlockSpec(memory_space=pl.ANY),
            pl.BlockSpec(memory_space=pl.ANY),
        ],
        out_specs=pl.BlockSpec((bm * ROW_SUBLANES, LANES), lambda j, *_: (j, 0)),
        scratch_shapes=[pltpu.VMEM((2, D_MODEL, D_EXPERT), F32), pltpu.VMEM((2, D_MODEL, D_EXPERT), F32),
                        pltpu.VMEM((2, D_EXPERT, D_MODEL), F32),
                        pltpu.VMEM((D_MODEL, D_EXPERT), BF16), pltpu.VMEM((D_MODEL, D_EXPERT), BF16),
                        pltpu.VMEM((D_EXPERT, D_MODEL), BF16), pltpu.SemaphoreType.DMA((3, 2))],
    )
    return pl.pallas_call(
        functools.partial(_expert_kernel, bm),
        grid_spec=grid_spec,
        out_shape=jax.ShapeDtypeStruct((nb * bm * ROW_SUBLANES, LANES), F32),
        compiler_params=pltpu.CompilerParams(
            dimension_semantics=("arbitrary",), vmem_limit_bytes=VMEM_LIMIT),
        name="moe_experts",
    )(block_expert, n_used, first.astype(jnp.int32), nxt.astype(jnp.int32), slot.astype(jnp.int32),
      xs_rows, w_gate, w_up, w_down)


def _combine_kernel(tm, dest_ref, ys_hbm, h_ref, rf_ref, o_ref, ybuf, sem):
    i = pl.program_id(0)
    nt = pl.num_programs(0) - 1

    @pl.when(i < nt)
    def _():
        slot = i % 2
        for g in range(0, tm, DMA_GROUP):
            dests = [(r, kk, dest_ref[0, kk, r]) for r in range(g, g + DMA_GROUP) for kk in range(TOP_K)]
            for r, kk, d in dests:
                _tile_copy(ys_hbm, d, ybuf.at[kk, slot], r, sem.at[kk, slot]).start(priority=kk)

    @pl.when(i >= 1)
    def _():
        slot = (i - 1) % 2
        for kk in range(TOP_K):
            pltpu.make_async_copy(ys_hbm.at[pl.ds(0, tm * ROW_SUBLANES), :], ybuf.at[kk, slot],
                                  sem.at[kk, slot]).wait()
        rf = rf_ref[...]
        o_ref[...] = (h_ref[...] + rf[:, 0:1] * _load_token_rows(ybuf.at[0, slot], tm)
                      + rf[:, 1:2] * _load_token_rows(ybuf.at[1, slot], tm))


def _combine(ys_rows, dest, h, rf):
    n = h.shape[0]
    tm = ROUTE_TILE
    nt, dest_spec = _route_spec(dest, tm, 0)
    prev = lambda i: jnp.maximum(i - 1, 0)
    return pl.pallas_call(
        functools.partial(_combine_kernel, tm),
        grid=(nt + 1,),
        in_specs=[
            dest_spec,
            pl.BlockSpec(memory_space=pl.ANY),
            pl.BlockSpec((tm, D_MODEL), lambda i: (prev(i), 0)),
            pl.BlockSpec((tm, SUBLANES), lambda i: (prev(i), 0)),
        ],
        out_specs=pl.BlockSpec((tm, D_MODEL), lambda i: (prev(i), 0)),
        out_shape=jax.ShapeDtypeStruct((n, D_MODEL), F32),
        scratch_shapes=[pltpu.VMEM((TOP_K, 2, tm * ROW_SUBLANES, LANES), F32),
                        pltpu.SemaphoreType.DMA((TOP_K, 2))],
        compiler_params=pltpu.CompilerParams(
            dimension_semantics=("arbitrary",), vmem_limit_bytes=VMEM_LIMIT),
        name="moe_combine",
    )(dest, ys_rows, h, rf)


def _pad_lanes(a, width=LANES):
    return jnp.pad(a, ((0, 0), (0, width - a.shape[-1])))


def _layer(x, meta_tokens, g_mix, w_in, b_forget, q_norm_g, k_norm_g, conv_w, attn_out_g, conv_out_g,
           w_out, g_ffn, w_rg, b_rg, w_re, b_re, w_gate, w_up, w_down):
    bsz, slen, _ = x.shape
    n = bsz * slen
    w = ATTN_WIDTH

    wq, wk, wv, wf, wb, wc, whc = jnp.split(
        w_in, [w, 2 * w, 3 * w, 3 * w + ATTN_HEADS, 4 * w + ATTN_HEADS, 5 * w + ATTN_HEADS], axis=1)
    w1 = jnp.concatenate([wq, wk, wb, wc, whc, _pad_lanes(wf)], axis=1).astype(BF16)
    head_of = jnp.arange(MXU_DIM) // HEAD_DIM
    gmat = (head_of[:, None] == head_of[None, :]).astype(BF16)
    consts = (
        g_mix.reshape(1, D_MODEL), w1, wv.T.astype(BF16), _pad_lanes(b_forget.reshape(1, ATTN_HEADS)),
        jnp.tile(q_norm_g, ATTN_HEADS).reshape(1, w) * (HEAD_DIM ** -0.5 * LOG2E),
        jnp.tile(k_norm_g, ATTN_HEADS).reshape(1, w),
        jnp.pad(conv_w, ((0, SUBLANES - CONV_K), (0, 0))), conv_out_g.reshape(1, w), gmat)

    xm = jnp.pad(meta_tokens, ((0, META_TILE - N_META), (0, 0)))[None]
    km, vmt, kaugm, cum_m, uc_m = _inproj(xm, True, META_TILE, consts,
                                          jnp.zeros((SUBLANES, w), F32), jnp.zeros((1, LANES), F32))
    km, vmt, kaugm = km[0, :N_META], vmt[0, 0, :, :N_META], kaugm[0, :N_META]
    cum_m = cum_m[0, :N_META]
    halo = jnp.zeros((SUBLANES, w), F32).at[SUBLANES - 2:].set(uc_m[0, N_META - 2:N_META])
    cum0 = cum_m[N_META - 1:N_META]

    q, k, vt, conv, qaug, kaug = _inproj(x, False, IN_TILE, consts, halo, cum0)
    attn = _attention(q, qaug, k, kaug, vt, km, kaugm, vmt, attn_out_g.reshape(w, 1))

    wr = _pad_lanes(jnp.concatenate([w_rg, w_re], axis=1)).T
    wr_hi = wr.astype(BF16)
    wr_lo = (wr - wr_hi.astype(F32)).astype(BF16)
    wrt = jnp.concatenate([jnp.concatenate([wr_hi, wr_hi], axis=1),
                           jnp.concatenate([wr_lo, jnp.zeros_like(wr_lo)], axis=1)], axis=0)
    brc = jnp.broadcast_to(_pad_lanes(jnp.concatenate([b_rg, b_re]).reshape(1, -1)).T, (LANES, LANES))
    h, xt, ri, rf, cnt = _outproj(attn.reshape(n, w), conv.reshape(n, w), x.reshape(n, D_MODEL),
                                  w_out.astype(BF16), g_ffn.reshape(1, D_MODEL), wrt, brc)

    bm = EXPERT_ROWS
    nb = (n * TOP_K) // bm + N_EXPERTS
    counts = cnt[:N_EXPERTS, 0].astype(jnp.int32)
    blocks_e = (counts + bm - 1) // bm
    blk_end = jnp.cumsum(blocks_e)
    row_start = (blk_end - blocks_e) * bm
    blk = jnp.arange(nb, dtype=jnp.int32)
    block_expert = jnp.minimum(jnp.sum(blk_end[None, :] <= blk[:, None], axis=1), N_EXPERTS - 1).astype(jnp.int32)
    n_used = blk_end[-1:].astype(jnp.int32)
    tail = n_used[0] + jnp.arange(N_EXPERTS, dtype=jnp.int32)
    fill_blocks = jnp.clip(jnp.concatenate([blk_end - 1, tail]), 0, nb - 1).astype(jnp.int32)
    fill_valid = jnp.concatenate([counts % bm != 0, tail < nb]).astype(jnp.int32)

    dest = _dest_rows(ri, row_start.astype(jnp.int32))
    xs = _dispatch(xt, dest, fill_blocks, fill_valid, nb)
    ys = _experts(xs, block_expert, n_used, w_gate, w_up, w_down)
    out = _combine(ys, dest, h, rf)
    return out.reshape(bsz, slen, D_MODEL)


def kernel(x, meta_tokens, norm_mix_g, w_in, b_forget, q_norm_g, k_norm_g, conv_w, attn_out_g, conv_out_g,
           w_out, norm_ffn_g, w_router_group, b_router_group, w_router_expert, b_router_expert,
           w_gate, w_up, w_down):
    assert norm_mix_g.shape[0] == 1, "single-layer block"
    return _layer(x, meta_tokens, norm_mix_g[0], w_in[0], b_forget[0], q_norm_g[0], k_norm_g[0], conv_w[0],
                  attn_out_g[0], conv_out_g[0], w_out[0], norm_ffn_g[0], w_router_group[0],
                  b_router_group[0], w_router_expert[0], b_router_expert[0], w_gate[0], w_up[0], w_down[0])
```

```python
import functools

import jax
import jax.numpy as jnp
from jax import lax
from jax.experimental import pallas as pl
from jax.experimental.pallas import tpu as pltpu

D_MODEL = 1024
N_META = 16
HEAD_DIM = 64
ATTN_HEADS = 8
ATTN_WIDTH = ATTN_HEADS * HEAD_DIM
CONV_WIDTH = D_MODEL - ATTN_WIDTH
CONV_K = 3
N_EXPERT_GROUPS = 4
EXPERTS_PER_GROUP = 8
N_EXPERTS = N_EXPERT_GROUPS * EXPERTS_PER_GROUP
TOP_K = 2
D_EXPERT = 512
EPS = 1e-6
MASK_VALUE = -1e30
LOG2E = 1.4426950408889634
AUG = 6

LANES = 128
SUBLANES = 8
MXU_DIM = 256
HEAD_PAIRS = ATTN_HEADS * HEAD_DIM // LANES
PROJ_PAD_COLS = 5 * ATTN_WIDTH + LANES
VMEM_LIMIT = 56 * 1024 * 1024

IN_TILE = 512
META_TILE = 128
ATTN_Q_TILE = 2048
ATTN_K_TILE = 512
OUT_TILE = 512
EXPERT_ROWS = 256
ROUTE_TILE = 512
DMA_GROUP = 8

F32 = jnp.float32
BF16 = jnp.bfloat16
ROW_SUBLANES = D_MODEL // LANES


def _lane_iota(shape):
    return lax.broadcasted_iota(jnp.int32, shape, len(shape) - 1)


def _store_token_rows(ref, x, rows):
    for sl in range(ROW_SUBLANES):
        ref[pl.ds(sl, rows, stride=ROW_SUBLANES), :] = x[:, sl * LANES:(sl + 1) * LANES]


def _load_token_rows(ref, rows):
    return jnp.concatenate([ref[pl.ds(sl, rows, stride=ROW_SUBLANES), :] for sl in range(ROW_SUBLANES)], axis=1)


def _inproj_kernel(is_meta, tm, x_ref, gmix_ref, w1_ref, wvt_ref, bf_ref, gq_ref, gk_ref, cw_ref, gco_ref,
                   gmat_ref, tri_ref, eq_ref, ek_ref, halo_ref, cum0_ref, *rest):
    if is_meta:
        k_ref, vt_ref, kaug_ref, cum_ref, uc_ref, ucbuf, carry = rest
    else:
        q_ref, k_ref, vt_ref, conv_ref, qaug_ref, kaug_ref, ucbuf, carry = rest
    t = pl.program_id(1)

    @pl.when(t == 0)
    def _():
        ucbuf[0:SUBLANES, :] = halo_ref[...]
        carry[0:1, :] = cum0_ref[...]

    x = x_ref[0]
    ms = jnp.mean(x * x, axis=-1, keepdims=True)
    u = (x * lax.rsqrt(ms + EPS)) * gmix_ref[...]
    ub = u.astype(BF16)
    proj = jnp.dot(ub, w1_ref[...], preferred_element_type=F32)
    vt_ref[0, 0] = lax.dot_general(wvt_ref[...], ub, (((1,), (1,)), ((), ())),
                                   preferred_element_type=F32).astype(BF16)

    def head_norm(z, g):
        z2 = (z * z).astype(BF16)
        half = gmat_ref.shape[0]
        ssq = jnp.concatenate([jnp.dot(z2[:, c:c + half], gmat_ref[...], preferred_element_type=F32)
                               for c in range(0, z.shape[1], half)], axis=1)
        return z * lax.rsqrt(ssq * (1.0 / HEAD_DIM) + EPS) * g

    w = ATTN_WIDTH
    kn = head_norm(proj[:, w:2 * w], gk_ref[...])
    k_ref[0] = kn.astype(BF16)

    z = proj[:, 5 * w:5 * w + LANES] + bf_ref[...]
    ls = jnp.minimum(z, 0.0) - jnp.log1p(jnp.exp(-jnp.abs(z)))
    lane_c = _lane_iota(ls.shape)
    is_head = lane_c < ATTN_HEADS
    ls = jnp.where(is_head, ls, 0.0)

    def pieces(val):
        p_hi = val.astype(BF16).astype(F32)
        rem = val - p_hi
        p_mid = rem.astype(BF16).astype(F32)
        p_lo = (rem - p_mid).astype(BF16).astype(F32)
        return jnp.where(is_head, p_hi,
                         jnp.where(lane_c < 2 * ATTN_HEADS, pltpu.roll(p_mid, ATTN_HEADS, axis=1),
                                   pltpu.roll(p_lo, 2 * ATTN_HEADS, axis=1))).astype(BF16)

    cs3 = jnp.dot(tri_ref[...], pieces(ls), preferred_element_type=F32)
    cs = cs3 + pltpu.roll(cs3, LANES - ATTN_HEADS, axis=1) + pltpu.roll(cs3, LANES - 2 * ATTN_HEADS, axis=1)
    cum = jnp.where(is_head, cs, 0.0) + carry[0:1, :]
    carry[0:1, :] = cum[tm - 1:tm, :]

    packed = pieces(cum * LOG2E)
    in_aug = lane_c < AUG * ATTN_HEADS
    ones_k = in_aug & ((lane_c % AUG) < AUG // 2)
    kaug = jnp.where(ones_k, 1.0, -jnp.dot(packed, ek_ref[...], preferred_element_type=F32))
    kaug_ref[0] = kaug.astype(BF16)

    uc = proj[:, 3 * w:4 * w] * proj[:, 4 * w:5 * w]
    ucbuf[SUBLANES:SUBLANES + tm, :] = uc
    uc1 = ucbuf[SUBLANES - 1:SUBLANES - 1 + tm, :]
    uc2 = ucbuf[SUBLANES - 2:SUBLANES - 2 + tm, :]
    ucbuf[0:SUBLANES, :] = uc[tm - SUBLANES:tm, :]

    if is_meta:
        cum_ref[0] = cum
        uc_ref[0] = uc
        return

    qn = head_norm(proj[:, 0:w], gq_ref[...])
    q_ref[0] = qn.astype(BF16)
    y = cw_ref[0:1, :] * uc2 + cw_ref[1:2, :] * uc1 + cw_ref[2:3, :] * uc
    conv = proj[:, 2 * w:3 * w] * y
    conv_ref[0] = head_norm(conv, gco_ref[...]).astype(BF16)
    ones_q = in_aug & ((lane_c % AUG) >= AUG // 2)
    qaug = jnp.where(ones_q, 1.0, jnp.dot(packed, eq_ref[...], preferred_element_type=F32))
    qaug_ref[0] = qaug.astype(BF16)


def _inproj(x, is_meta, tm, consts, halo, cum0):
    bsz, tlen, _ = x.shape
    nt = tlen // tm
    gmix, w1, wvt, bfp, gq, gk, cw, gco, gmat = consts
    tri = jnp.tril(jnp.ones((tm, tm), F32)).astype(BF16)
    w = ATTN_WIDTH
    src = jnp.arange(LANES)
    dst = jnp.arange(LANES)
    piece, head = src // ATTN_HEADS, src % ATTN_HEADS
    valid = src < 3 * ATTN_HEADS
    eq = (valid[:, None] & (dst[None, :] == (AUG * head + piece)[:, None])).astype(BF16)
    ek = (valid[:, None] & (dst[None, :] == (AUG * head + AUG // 2 + piece)[:, None])).astype(BF16)

    def full(a):
        return pl.BlockSpec(a.shape, lambda b, t: (0,) * a.ndim)

    in_specs = [pl.BlockSpec((1, tm, D_MODEL), lambda b, t: (b, t, 0))] + [
        full(a) for a in (gmix, w1, wvt, bfp, gq, gk, cw, gco, gmat, tri, eq, ek, halo, cum0)]
    tok = lambda width: pl.BlockSpec((1, tm, width), lambda b, t: (b, t, 0))
    vt_spec = pl.BlockSpec((1, 1, w, tm), lambda b, t: (b, t, 0, 0))
    if is_meta:
        out_shape = [jax.ShapeDtypeStruct((bsz, tlen, w), BF16),
                     jax.ShapeDtypeStruct((bsz, nt, w, tm), BF16),
                     jax.ShapeDtypeStruct((bsz, tlen, LANES), BF16),
                     jax.ShapeDtypeStruct((bsz, tlen, LANES), F32),
                     jax.ShapeDtypeStruct((bsz, tlen, w), F32)]
        out_specs = [tok(w), vt_spec, tok(LANES), tok(LANES), tok(w)]
    else:
        out_shape = [jax.ShapeDtypeStruct((bsz, tlen, w), BF16)] * 2 + [
            jax.ShapeDtypeStruct((bsz, nt, w, tm), BF16),
            jax.ShapeDtypeStruct((bsz, tlen, w), BF16),
            jax.ShapeDtypeStruct((bsz, tlen, LANES), BF16),
            jax.ShapeDtypeStruct((bsz, tlen, LANES), BF16)]
        out_specs = [tok(w), tok(w), vt_spec, tok(w), tok(LANES), tok(LANES)]
    return pl.pallas_call(
        functools.partial(_inproj_kernel, is_meta, tm),
        grid=(bsz, nt),
        in_specs=in_specs,
        out_specs=out_specs,
        out_shape=out_shape,
        scratch_shapes=[pltpu.VMEM((tm + SUBLANES, w), F32), pltpu.VMEM((SUBLANES, LANES), F32)],
        compiler_params=pltpu.CompilerParams(
            dimension_semantics=("arbitrary", "arbitrary"), vmem_limit_bytes=VMEM_LIMIT),
        name="inproj_meta" if is_meta else "inproj",
    )(x, gmix, w1, wvt, bfp, gq, gk, cw, gco, gmat, tri, eq, ek, halo, cum0)


def _attn_kernel(tq, tk, q_ref, qaug_ref, k_ref, kaug_ref, vt_ref, km_ref, kaugm_ref, vmt_ref, gao_ref,
                 o_ref, m_sc, l_sc, acc_sc, sa_sc, sb_sc):
    hp = pl.program_id(1)
    qi = pl.program_id(2)
    q = q_ref[0]
    qaug = qaug_ref[0]
    lane = _lane_iota(q.shape)
    first = lane < HEAD_DIM
    zero = jnp.zeros_like(q)
    qcat = []
    for hh in range(2):
        qh = jnp.where(first, q, zero) if hh == 0 else jnp.where(first, zero, q)
        lo = AUG * (2 * hp + hh)
        qa = jnp.where((lane >= lo) & (lane < lo + AUG), qaug, zero)
        qcat.append(jnp.concatenate([qh, qa], axis=1))

    m_sc[...] = jnp.full(m_sc.shape, MASK_VALUE, F32)
    l_sc[...] = jnp.zeros(l_sc.shape, F32)
    acc_sc[...] = jnp.zeros(acc_sc.shape, F32)

    def scores_t(kcat, q_lo=0):
        return tuple(lax.dot_general(kcat, qcat[hh][q_lo:], (((1,), (1,)), ((), ())), preferred_element_type=F32)
                     for hh in range(2))

    def update(hh, st, vtb, q_lo=0):
        m_old = m_sc[hh, :, q_lo:]
        m_new = jnp.maximum(m_old, jnp.max(st, axis=0, keepdims=True))
        alpha = jnp.exp2(m_old - m_new)
        p = jnp.exp2(st - m_new)
        l_sc[hh, :, q_lo:] = alpha * l_sc[hh, :, q_lo:] + jnp.sum(p, axis=0, keepdims=True)
        vh = vtb[hh * HEAD_DIM:(hh + 1) * HEAD_DIM, :]
        acc_sc[hh, :, q_lo:] = alpha * acc_sc[hh, :, q_lo:] + jnp.dot(vh, p.astype(BF16),
                                                                     preferred_element_type=F32)
        m_sc[hh, :, q_lo:] = m_new

    def key_block(j):
        start = pl.multiple_of(j * tk, tk)
        return jnp.concatenate([k_ref[0, pl.ds(start, tk), :], kaug_ref[0, pl.ds(start, tk), :]], axis=1)

    def put_scores(j, buf, q_lo=0):
        sts = scores_t(key_block(j), q_lo)
        for hh in range(2):
            buf[hh, :, q_lo:] = sts[hh]

    def consume(buf, j, diagonal=False, q_lo=0):
        vtb = vt_ref[0, j]
        for hh in range(2):
            st = buf[hh, :, q_lo:]
            if diagonal:
                key = lax.broadcasted_iota(jnp.int32, st.shape, 0)
                qry = lax.broadcasted_iota(jnp.int32, st.shape, 1)
                st = jnp.where(key <= qry, st, MASK_VALUE)
            update(hh, st, vtb, q_lo)

    per = tq // tk
    assert per % 2 == 0
    st_m = scores_t(jnp.concatenate([km_ref[...], kaugm_ref[...]], axis=1))
    put_scores(0, sa_sc)
    vmt = vmt_ref[...]
    for hh in range(2):
        update(hh, st_m[hh], vmt)

    def body(t, carry):
        j = 2 * t
        put_scores(j + 1, sb_sc)
        consume(sa_sc, j)
        put_scores(j + 2, sa_sc)
        consume(sb_sc, j + 1)
        return carry

    lax.fori_loop(0, (per // 2) * qi, body, 0)
    bufs = (sa_sc, sb_sc)
    for d in range(per):
        if d + 1 < per:
            put_scores(per * qi + d + 1, bufs[(d + 1) % 2], (d + 1) * tk)
        consume(bufs[d % 2], per * qi + d, diagonal=True, q_lo=d * tk)

    ot = jnp.concatenate([acc_sc[0] / l_sc[0], acc_sc[1] / l_sc[1]], axis=0)
    o2 = ot * ot
    ms0 = jnp.sum(o2[0:HEAD_DIM], axis=0, keepdims=True) * (1.0 / HEAD_DIM)
    ms1 = jnp.sum(o2[HEAD_DIM:], axis=0, keepdims=True) * (1.0 / HEAD_DIM)
    inv = jnp.concatenate([jnp.broadcast_to(lax.rsqrt(ms0 + EPS), (HEAD_DIM, tq)),
                           jnp.broadcast_to(lax.rsqrt(ms1 + EPS), (HEAD_DIM, tq))], axis=0)
    o_ref[0] = (ot * inv * gao_ref[...]).T.astype(BF16)


def _attention(q, qaug, k, kaug, vt, km, kaugm, vmt, gao):
    bsz, slen, w = q.shape
    tq, tk = ATTN_Q_TILE, ATTN_K_TILE
    nq, nk = slen // tq, slen // tk
    return pl.pallas_call(
        functools.partial(_attn_kernel, tq, tk),
        grid=(bsz, HEAD_PAIRS, nq),
        in_specs=[
            pl.BlockSpec((1, tq, LANES), lambda b, p, i: (b, i, p)),
            pl.BlockSpec((1, tq, LANES), lambda b, p, i: (b, i, 0)),
            pl.BlockSpec((1, slen, LANES), lambda b, p, i: (b, 0, p)),
            pl.BlockSpec((1, slen, LANES), lambda b, p, i: (b, 0, 0)),
            pl.BlockSpec((1, nk, LANES, tk), lambda b, p, i: (b, 0, p, 0)),
            pl.BlockSpec((N_META, LANES), lambda b, p, i: (0, p)),
            pl.BlockSpec((N_META, LANES), lambda b, p, i: (0, 0)),
            pl.BlockSpec((LANES, N_META), lambda b, p, i: (p, 0)),
            pl.BlockSpec((LANES, 1), lambda b, p, i: (p, 0)),
        ],
        out_specs=pl.BlockSpec((1, tq, LANES), lambda b, p, i: (b, i, p)),
        out_shape=jax.ShapeDtypeStruct((bsz, slen, w), BF16),
        scratch_shapes=[pltpu.VMEM((2, 1, tq), F32), pltpu.VMEM((2, 1, tq), F32),
                        pltpu.VMEM((2, HEAD_DIM, tq), F32),
                        pltpu.VMEM((2, tk, tq), F32), pltpu.VMEM((2, tk, tq), F32)],
        compiler_params=pltpu.CompilerParams(
            dimension_semantics=("arbitrary", "arbitrary", "arbitrary"), vmem_limit_bytes=VMEM_LIMIT),
        name="fox_attention",
    )(q, qaug, k, kaug, vt, km, kaugm, vmt, gao)


ROUTE_ROWS = 64


def _outproj_kernel(tm, attn_ref, conv_ref, x_ref, wo_ref, gffn_ref, wrt_ref, brc_ref, tri_ref,
                    h_ref, xt_ref, ri_ref, rf_ref, cnt_ref, carry):
    i = pl.program_id(0)

    @pl.when(i == 0)
    def _():
        carry[...] = jnp.zeros(carry.shape, F32)

    mixed = jnp.concatenate([attn_ref[...], conv_ref[...]], axis=1)
    h = x_ref[...] + jnp.dot(mixed, wo_ref[...], preferred_element_type=F32)
    h_ref[...] = h
    ms = jnp.mean(h * h, axis=-1, keepdims=True)
    xt = (h * lax.rsqrt(ms + EPS)) * gffn_ref[...]
    _store_token_rows(xt_ref, xt, tm)

    x_hi = xt.astype(BF16)
    x_lo = (xt - x_hi.astype(F32)).astype(BF16)
    parts = lax.dot_general(wrt_ref[...], jnp.concatenate([x_hi, x_lo], axis=1), (((1,), (1,)), ((), ())),
                            preferred_element_type=F32)
    logits = (parts[0:ROUTE_ROWS] + parts[LANES:LANES + ROUTE_ROWS]) + brc_ref[0:ROUTE_ROWS, 0:1]
    row = lax.broadcasted_iota(jnp.int32, logits.shape, 0)
    rowf = row.astype(F32)
    big = float(LANES)

    def first_argmax(vals, vmax):
        return jnp.min(jnp.where(vals == vmax, rowf, big), axis=0, keepdims=True)

    is_g = row < N_EXPERT_GROUPS
    gl = jnp.where(is_g, logits, MASK_VALUE)
    gmax = jnp.max(gl, axis=0, keepdims=True)
    gidx = first_argmax(gl, gmax)
    g_p = 1.0 / jnp.sum(jnp.where(is_g, jnp.exp(gl - gmax), 0.0), axis=0, keepdims=True)

    base = N_EXPERT_GROUPS + EXPERTS_PER_GROUP * gidx
    in_grp = (rowf >= base) & (rowf < base + EXPERTS_PER_GROUP)
    el = jnp.where(in_grp, logits, MASK_VALUE)
    l1 = jnp.max(el, axis=0, keepdims=True)
    e1 = first_argmax(el, l1)
    el2 = jnp.where(rowf == e1, MASK_VALUE, el)
    l2 = jnp.max(el2, axis=0, keepdims=True)
    e2 = first_argmax(el2, l2)
    zsum = jnp.sum(jnp.where(in_grp, jnp.exp(el - l1), 0.0), axis=0, keepdims=True)
    p1 = 1.0 / zsum
    p2 = jnp.exp(l2 - l1) / zsum
    den = p1 + p2
    w1 = g_p * p1 / den
    w2 = g_p * p2 / den
    id1 = e1 - N_EXPERT_GROUPS
    id2 = e2 - N_EXPERT_GROUPS

    oh1 = rowf == id1
    oh2 = rowf == id2
    oh = jnp.where(oh1 | oh2, 1.0, 0.0)
    before = jnp.dot(oh.astype(BF16), tri_ref[...], preferred_element_type=F32) + carry[:, 0:1]
    rank1 = jnp.sum(jnp.where(oh1, before, 0.0), axis=0, keepdims=True)
    rank2 = jnp.sum(jnp.where(oh2, before, 0.0), axis=0, keepdims=True)
    carry[...] = carry[...] + jnp.sum(oh, axis=1, keepdims=True)
    cnt_ref[...] = carry[...]

    pad_i = jnp.zeros((SUBLANES - 2 * TOP_K, tm), F32)
    ri_ref[0] = jnp.concatenate([id1, id2, rank1, rank2, pad_i], axis=0).astype(jnp.int32)
    pad_f = jnp.zeros((LANES - TOP_K, tm), F32)
    rf_ref[...] = jnp.concatenate([w1, w2, pad_f], axis=0).T[:, 0:SUBLANES]


def _outproj(attn, conv, x, wo, gffn, wrt, brc):
    n = x.shape[0]
    tm = OUT_TILE
    tri = jnp.triu(jnp.ones((tm, tm), F32), k=1).astype(BF16)

    def full(a):
        return pl.BlockSpec(a.shape, lambda i: (0,) * a.ndim)

    rows = lambda width: pl.BlockSpec((tm, width), lambda i: (i, 0))
    return pl.pallas_call(
        functools.partial(_outproj_kernel, tm),
        grid=(n // tm,),
        in_specs=[rows(ATTN_WIDTH), rows(CONV_WIDTH), rows(D_MODEL)] + [
            full(a) for a in (wo, gffn, wrt, brc, tri)],
        out_specs=[rows(D_MODEL), pl.BlockSpec((tm * ROW_SUBLANES, LANES), lambda i: (i, 0)),
                   pl.BlockSpec((1, SUBLANES, tm), lambda i: (i, 0, 0)), rows(SUBLANES),
                   pl.BlockSpec((ROUTE_ROWS, LANES), lambda i: (0, 0))],
        out_shape=[jax.ShapeDtypeStruct((n, D_MODEL), F32),
                   jax.ShapeDtypeStruct((n * ROW_SUBLANES, LANES), F32),
                   jax.ShapeDtypeStruct((n // tm, SUBLANES, tm), jnp.int32),
                   jax.ShapeDtypeStruct((n, SUBLANES), F32),
                   jax.ShapeDtypeStruct((ROUTE_ROWS, LANES), F32)],
        scratch_shapes=[pltpu.VMEM((ROUTE_ROWS, LANES), F32)],
        compiler_params=pltpu.CompilerParams(
            dimension_semantics=("arbitrary",), vmem_limit_bytes=VMEM_LIMIT),
        name="outproj_router",
    )(attn, conv, x, wo, gffn, wrt, brc, tri)


def _dest_kernel(rs_ref, ri_ref, dest_ref):
    ri = ri_ref[...]
    experts = ri[:, 0:TOP_K, :]
    start = jnp.zeros_like(experts)
    for e in range(N_EXPERTS):
        start = jnp.where(experts == e, rs_ref[e], start)
    dest_ref[...] = jnp.concatenate([start + ri[:, TOP_K:2 * TOP_K, :],
                                     jnp.zeros((ri.shape[0], SUBLANES - TOP_K, ri.shape[2]), jnp.int32)], axis=1)


def _dest_rows(ri_t, row_start):
    grid_spec = pltpu.PrefetchScalarGridSpec(
        num_scalar_prefetch=1,
        grid=(1,),
        in_specs=[pl.BlockSpec(ri_t.shape, lambda i, rs: (0, 0, 0))],
        out_specs=pl.BlockSpec(ri_t.shape, lambda i, rs: (0, 0, 0)),
    )
    return pl.pallas_call(
        _dest_kernel,
        grid_spec=grid_spec,
        out_shape=jax.ShapeDtypeStruct(ri_t.shape, jnp.int32),
        compiler_params=pltpu.CompilerParams(dimension_semantics=("arbitrary",)),
        name="moe_dest",
    )(row_start, ri_t)


def _tile_copy(src, src_row, dst, dst_row, sem):
    rs = ROW_SUBLANES
    return pltpu.make_async_copy(src.at[pl.ds(pl.multiple_of(src_row * rs, rs), rs), :],
                                 dst.at[pl.ds(pl.multiple_of(dst_row * rs, rs), rs), :], sem)


def _dispatch_kernel(tm, bm, fz_ref, fv_ref, dest_ref, xt_ref, xs_hbm, zeros, sem, zsem):
    i = pl.program_id(0)
    block_rows = bm * ROW_SUBLANES

    def zero_block(z):
        start = pl.multiple_of(fz_ref[z] * block_rows, block_rows)
        return pltpu.make_async_copy(zeros, xs_hbm.at[pl.ds(start, block_rows), :], zsem.at[0])

    @pl.when(i == 0)
    def _():
        zeros[...] = jnp.zeros(zeros.shape, F32)
        for z in range(fz_ref.shape[0]):
            @pl.when(fv_ref[z] == 1)
            def _():
                zero_block(z).start()
        for z in range(fz_ref.shape[0]):
            @pl.when(fv_ref[z] == 1)
            def _():
                zero_block(z).wait()

    for g in range(0, tm, DMA_GROUP):
        dests = [(r, kk, dest_ref[0, kk, r]) for r in range(g, g + DMA_GROUP) for kk in range(TOP_K)]
        for r, kk, d in dests:
            _tile_copy(xt_ref, r, xs_hbm, d, sem.at[0]).start(priority=kk)
    for kk in range(TOP_K):
        pltpu.make_async_copy(xt_ref, xs_hbm.at[pl.ds(0, tm * ROW_SUBLANES), :], sem.at[0]).wait()


def _route_spec(dest, tm, extra_args):
    per = dest.shape[2] // tm
    nt = dest.shape[0] * per
    if extra_args == 0:
        index = lambda i: (jnp.minimum(i, nt - 1) // per, 0, jnp.minimum(i, nt - 1) % per)
    else:
        index = lambda i, *_: (jnp.minimum(i, nt - 1) // per, 0, jnp.minimum(i, nt - 1) % per)
    return nt, pl.BlockSpec((1, SUBLANES, tm), index, memory_space=pltpu.SMEM)


def _dispatch(xt_rows, dest, fill_blocks, fill_valid, nb):
    tm = ROUTE_TILE
    bm = EXPERT_ROWS
    nt, dest_spec = _route_spec(dest, tm, 2)
    grid_spec = pltpu.PrefetchScalarGridSpec(
        num_scalar_prefetch=2,
        grid=(nt,),
        in_specs=[dest_spec, pl.BlockSpec((tm * ROW_SUBLANES, LANES), lambda i, fz, fv: (i, 0))],
        out_specs=pl.BlockSpec(memory_space=pl.ANY),
        scratch_shapes=[pltpu.VMEM((bm * ROW_SUBLANES, LANES), F32), pltpu.SemaphoreType.DMA((1,)),
                        pltpu.SemaphoreType.DMA((1,))],
    )
    return pl.pallas_call(
        functools.partial(_dispatch_kernel, tm, bm),
        grid_spec=grid_spec,
        out_shape=jax.ShapeDtypeStruct((nb * bm * ROW_SUBLANES, LANES), F32),
        compiler_params=pltpu.CompilerParams(
            dimension_semantics=("arbitrary",), vmem_limit_bytes=VMEM_LIMIT),
        name="moe_dispatch",
    )(fill_blocks, fill_valid, dest, xt_rows)


def _expert_kernel(bm, be_ref, nu_ref, first_ref, nxt_ref, slot_ref, xs_ref, wg_hbm, wu_hbm, wd_hbm, ys_ref,
                   wg_buf, wu_buf, wd_buf, wg_bf, wu_bf, wd_bf, wsem):
    j = pl.program_id(0)

    def weight_copies(e, slot):
        return [pltpu.make_async_copy(src.at[e], buf.at[slot], wsem.at[m, slot])
                for m, (src, buf) in enumerate(((wg_hbm, wg_buf), (wu_hbm, wu_buf), (wd_hbm, wd_buf)))]

    @pl.when(j == 0)
    def _():
        for c in weight_copies(be_ref[0], 0):
            c.start()

    @pl.when(j < nu_ref[0])
    def _():
        slot = slot_ref[j]

        @pl.when(first_ref[j] == 1)
        def _():
            for c in weight_copies(0, slot):
                c.wait()

            @pl.when(nxt_ref[j] >= 0)
            def _():
                for c in weight_copies(nxt_ref[j], 1 - slot):
                    c.start()

            wg_bf[...] = wg_buf[slot].astype(BF16)
            wu_bf[...] = wu_buf[slot].astype(BF16)
            wd_bf[...] = wd_buf[slot].astype(BF16)

        x = _load_token_rows(xs_ref, bm).astype(BF16)
        hg = jnp.dot(x, wg_bf[...], preferred_element_type=F32)
        hu = jnp.dot(x, wu_bf[...], preferred_element_type=F32)
        hdn = hg * (1.0 / (1.0 + jnp.exp(-hg))) * hu
        y = jnp.dot(hdn.astype(BF16), wd_bf[...], preferred_element_type=F32)
        _store_token_rows(ys_ref, y, bm)

    @pl.when(j >= nu_ref[0])
    def _():
        ys_ref[...] = jnp.zeros(ys_ref.shape, F32)


def _experts(xs_rows, block_expert, n_used, w_gate, w_up, w_down):
    bm = EXPERT_ROWS
    nb = block_expert.shape[0]
    blk = jnp.arange(nb, dtype=jnp.int32)
    live = blk < n_used[0]
    first = live & ((blk == 0) | (block_expert != jnp.roll(block_expert, 1)))
    slot = (jnp.cumsum(first.astype(jnp.int32)) - 1) % 2
    later_first = first[None, :] & (blk[None, :] > blk[:, None])
    nxt_start = jnp.min(jnp.where(later_first, blk[None, :], nb), axis=1)
    nxt = jnp.sum(jnp.where(blk[None, :] == nxt_start[:, None], block_expert[None, :], 0), axis=1)
    nxt = jnp.where(nxt_start < nb, nxt, -1)
    grid_spec = pltpu.PrefetchScalarGridSpec(
        num_scalar_prefetch=5,
        grid=(nb,),
        in_specs=[
            pl.BlockSpec((bm * ROW_SUBLANES, LANES), lambda j, be, nu, *_: (jnp.minimum(j, nu[0] - 1), 0)),
            pl.BlockSpec(memory_space=pl.ANY),
            pl.BlockSpec(memory_space=pl.ANY),
            pl.BlockSpec(memory_space=pl.ANY),
        ],
        out_specs=pl.BlockSpec((bm * ROW_SUBLANES, LANES), lambda j, *_: (j, 0)),
        scratch_shapes=[pltpu.VMEM((2, D_MODEL, D_EXPERT), F32), pltpu.VMEM((2, D_MODEL, D_EXPERT), F32),
                        pltpu.VMEM((2, D_EXPERT, D_MODEL), F32),
                        pltpu.VMEM((D_MODEL, D_EXPERT), BF16), pltpu.VMEM((D_MODEL, D_EXPERT), BF16),
                        pltpu.VMEM((D_EXPERT, D_MODEL), BF16), pltpu.SemaphoreType.DMA((3, 2))],
    )
    return pl.pallas_call(
        functools.partial(_expert_kernel, bm),
        grid_spec=grid_spec,
        out_shape=jax.ShapeDtypeStruct((nb * bm * ROW_SUBLANES, LANES), F32),
        compiler_params=pltpu.CompilerParams(
            dimension_semantics=("arbitrary",), vmem_limit_bytes=VMEM_LIMIT),
        name="moe_experts",
    )(block_expert, n_used, first.astype(jnp.int32), nxt.astype(jnp.int32), slot.astype(jnp.int32),
      xs_rows, w_gate, w_up, w_down)


def _combine_kernel(tm, dest_ref, ys_hbm, h_ref, rf_ref, o_ref, ybuf, sem):
    i = pl.program_id(0)
    nt = pl.num_programs(0) - 1

    @pl.when(i < nt)
    def _():
        slot = i % 2
        for g in range(0, tm, DMA_GROUP):
            dests = [(r, kk, dest_ref[0, kk, r]) for r in range(g, g + DMA_GROUP) for kk in range(TOP_K)]
            for r, kk, d in dests:
                _tile_copy(ys_hbm, d, ybuf.at[kk, slot], r, sem.at[kk, slot]).start(priority=kk)

    @pl.when(i >= 1)
    def _():
        slot = (i - 1) % 2
        for kk in range(TOP_K):
            pltpu.make_async_copy(ys_hbm.at[pl.ds(0, tm * ROW_SUBLANES), :], ybuf.at[kk, slot],
                                  sem.at[kk, slot]).wait()
        rf = rf_ref[...]
        o_ref[...] = (h_ref[...] + rf[:, 0:1] * _load_token_rows(ybuf.at[0, slot], tm)
                      + rf[:, 1:2] * _load_token_rows(ybuf.at[1, slot], tm))


def _combine(ys_rows, dest, h, rf):
    n = h.shape[0]
    tm = ROUTE_TILE
    nt, dest_spec = _route_spec(dest, tm, 0)
    prev = lambda i: jnp.maximum(i - 1, 0)
    return pl.pallas_call(
        functools.partial(_combine_kernel, tm),
        grid=(nt + 1,),
        in_specs=[
            dest_spec,
            pl.BlockSpec(memory_space=pl.ANY),
            pl.BlockSpec((tm, D_MODEL), lambda i: (prev(i), 0)),
            pl.BlockSpec((tm, SUBLANES), lambda i: (prev(i), 0)),
        ],
        out_specs=pl.BlockSpec((tm, D_MODEL), lambda i: (prev(i), 0)),
        out_shape=jax.ShapeDtypeStruct((n, D_MODEL), F32),
        scratch_shapes=[pltpu.VMEM((TOP_K, 2, tm * ROW_SUBLANES, LANES), F32),
                        pltpu.SemaphoreType.DMA((TOP_K, 2))],
        compiler_params=pltpu.CompilerParams(
            dimension_semantics=("arbitrary",), vmem_limit_bytes=VMEM_LIMIT),
        name="moe_combine",
    )(dest, ys_rows, h, rf)


def _pad_lanes(a, width=LANES):
    return jnp.pad(a, ((0, 0), (0, width - a.shape[-1])))


def _layer(x, meta_tokens, g_mix, w_in, b_forget, q_norm_g, k_norm_g, conv_w, attn_out_g, conv_out_g,
           w_out, g_ffn, w_rg, b_rg, w_re, b_re, w_gate, w_up, w_down):
    bsz, slen, _ = x.shape
    n = bsz * slen
    w = ATTN_WIDTH

    wq, wk, wv, wf, wb, wc, whc = jnp.split(
        w_in, [w, 2 * w, 3 * w, 3 * w + ATTN_HEADS, 4 * w + ATTN_HEADS, 5 * w + ATTN_HEADS], axis=1)
    w1 = jnp.concatenate([wq, wk, wb, wc, whc, _pad_lanes(wf)], axis=1).astype(BF16)
    head_of = jnp.arange(MXU_DIM) // HEAD_DIM
    gmat = (head_of[:, None] == head_of[None, :]).astype(BF16)
    consts = (
        g_mix.reshape(1, D_MODEL), w1, wv.T.astype(BF16), _pad_lanes(b_forget.reshape(1, ATTN_HEADS)),
        jnp.tile(q_norm_g, ATTN_HEADS).reshape(1, w) * (HEAD_DIM ** -0.5 * LOG2E),
        jnp.tile(k_norm_g, ATTN_HEADS).reshape(1, w),
        jnp.pad(conv_w, ((0, SUBLANES - CONV_K), (0, 0))), conv_out_g.reshape(1, w), gmat)

    xm = jnp.pad(meta_tokens, ((0, META_TILE - N_META), (0, 0)))[None]
    km, vmt, kaugm, cum_m, uc_m = _inproj(xm, True, META_TILE, consts,
                                          jnp.zeros((SUBLANES, w), F32), jnp.zeros((1, LANES), F32))
    km, vmt, kaugm = km[0, :N_META], vmt[0, 0, :, :N_META], kaugm[0, :N_META]
    cum_m = cum_m[0, :N_META]
    halo = jnp.zeros((SUBLANES, w), F32).at[SUBLANES - 2:].set(uc_m[0, N_META - 2:N_META])
    cum0 = cum_m[N_META - 1:N_META]

    q, k, vt, conv, qaug, kaug = _inproj(x, False, IN_TILE, consts, halo, cum0)
    attn = _attention(q, qaug, k, kaug, vt, km, kaugm, vmt, attn_out_g.reshape(w, 1))

    wr = _pad_lanes(jnp.concatenate([w_rg, w_re], axis=1)).T
    wr_hi = wr.astype(BF16)
    wr_lo = (wr - wr_hi.astype(F32)).astype(BF16)
    wrt = jnp.concatenate([jnp.concatenate([wr_hi, wr_hi], axis=1),
                           jnp.concatenate([wr_lo, jnp.zeros_like(wr_lo)], axis=1)], axis=0)
    brc = jnp.broadcast_to(_pad_lanes(jnp.concatenate([b_rg, b_re]).reshape(1, -1)).T, (LANES, LANES))
    h, xt, ri, rf, cnt = _outproj(attn.reshape(n, w), conv.reshape(n, w), x.reshape(n, D_MODEL),
                                  w_out.astype(BF16), g_ffn.reshape(1, D_MODEL), wrt, brc)

    bm = EXPERT_ROWS
    nb = (n * TOP_K) // bm + N_EXPERTS
    counts = cnt[:N_EXPERTS, 0].astype(jnp.int32)
    blocks_e = (counts + bm - 1) // bm
    blk_end = jnp.cumsum(blocks_e)
    row_start = (blk_end - blocks_e) * bm
    blk = jnp.arange(nb, dtype=jnp.int32)
    block_expert = jnp.minimum(jnp.sum(blk_end[None, :] <= blk[:, None], axis=1), N_EXPERTS - 1).astype(jnp.int32)
    n_used = blk_end[-1:].astype(jnp.int32)
    tail = n_used[0] + jnp.arange(N_EXPERTS, dtype=jnp.int32)
    fill_blocks = jnp.clip(jnp.concatenate([blk_end - 1, tail]), 0, nb - 1).astype(jnp.int32)
    fill_valid = jnp.concatenate([counts % bm != 0, tail < nb]).astype(jnp.int32)

    dest = _dest_rows(ri, row_start.astype(jnp.int32))
    xs = _dispatch(xt, dest, fill_blocks, fill_valid, nb)
    ys = _experts(xs, block_expert, n_used, w_gate, w_up, w_down)
    out = _combine(ys, dest, h, rf)
    return out.reshape(bsz, slen, D_MODEL)


def kernel(x, meta_tokens, norm_mix_g, w_in, b_forget, q_norm_g, k_norm_g, conv_w, attn_out_g, conv_out_g,
           w_out, norm_ffn_g, w_router_group, b_router_group, w_router_expert, b_router_expert,
           w_gate, w_up, w_down):
    assert norm_mix_g.shape[0] == 1, "single-layer block"
    return _layer(x, meta_tokens, norm_mix_g[0], w_in[0], b_forget[0], q_norm_g[0], k_norm_g[0], conv_w[0],
                  attn_out_g[0], conv_out_g[0], w_out[0], norm_ffn_g[0], w_router_group[0],
                  b_router_group[0], w_router_expert[0], b_router_expert[0], w_gate[0], w_up[0], w_down[0])
```

```python
import functools

import jax
import jax.numpy as jnp
from jax import lax
from jax.experimental import pallas as pl
from jax.experimental.pallas import tpu as pltpu

D_MODEL = 1024
N_META = 16
HEAD_DIM = 64
ATTN_HEADS = 8
ATTN_WIDTH = ATTN_HEADS * HEAD_DIM
CONV_WIDTH = D_MODEL - ATTN_WIDTH
CONV_K = 3
N_EXPERT_GROUPS = 4
EXPERTS_PER_GROUP = 8
N_EXPERTS = N_EXPERT_GROUPS * EXPERTS_PER_GROUP
TOP_K = 2
D_EXPERT = 512
EPS = 1e-6
MASK_VALUE = -1e30
LOG2E = 1.4426950408889634
AUG = 6

LANES = 128
SUBLANES = 8
MXU_DIM = 256
HEAD_PAIRS = ATTN_HEADS * HEAD_DIM // LANES
PROJ_PAD_COLS = 5 * ATTN_WIDTH + LANES
VMEM_LIMIT = 56 * 1024 * 1024

IN_TILE = 512
META_TILE = 128
ATTN_Q_TILE = 2048
ATTN_K_TILE = 512
OUT_TILE = 1024
EXPERT_ROWS = 256
EXPERT_BLOCKS_PER_STEP = 8
DISPATCH_TILE = 1024
COMBINE_TILE = 256
DMA_GROUP = 8

F32 = jnp.float32
BF16 = jnp.bfloat16
ROW_SUBLANES = D_MODEL // LANES


def _lane_iota(shape):
    return lax.broadcasted_iota(jnp.int32, shape, len(shape) - 1)


def _store_token_rows(ref, x, rows):
    for sl in range(ROW_SUBLANES):
        ref[pl.ds(sl, rows, stride=ROW_SUBLANES), :] = x[:, sl * LANES:(sl + 1) * LANES]


def _load_token_rows(ref, rows):
    return jnp.concatenate([ref[pl.ds(sl, rows, stride=ROW_SUBLANES), :] for sl in range(ROW_SUBLANES)], axis=1)


def _inproj_kernel(is_meta, tm, x_ref, gmix_ref, w1_ref, wvt_ref, bf_ref, gq_ref, gk_ref, cw_ref, gco_ref,
                   gmat_ref, tri_ref, eq_ref, ek_ref, halo_ref, cum0_ref, *rest):
    if is_meta:
        k_ref, vt_ref, kaug_ref, cum_ref, uc_ref, ucbuf, carry = rest
    else:
        q_ref, k_ref, vt_ref, conv_ref, qaug_ref, kaug_ref, ucbuf, carry = rest
    t = pl.program_id(1)

    @pl.when(t == 0)
    def _():
        ucbuf[0:SUBLANES, :] = halo_ref[...]
        carry[0:1, :] = cum0_ref[...]

    x = x_ref[0]
    ms = jnp.mean(x * x, axis=-1, keepdims=True)
    u = (x * lax.rsqrt(ms + EPS)) * gmix_ref[...]
    ub = u.astype(BF16)
    proj = jnp.dot(ub, w1_ref[...], preferred_element_type=F32)
    vt_ref[0, 0] = lax.dot_general(wvt_ref[...], ub, (((1,), (1,)), ((), ())),
                                   preferred_element_type=F32).astype(BF16)

    def head_norm(z, g):
        z2 = (z * z).astype(BF16)
        half = gmat_ref.shape[0]
        ssq = jnp.concatenate([jnp.dot(z2[:, c:c + half], gmat_ref[...], preferred_element_type=F32)
                               for c in range(0, z.shape[1], half)], axis=1)
        return z * lax.rsqrt(ssq * (1.0 / HEAD_DIM) + EPS) * g

    w = ATTN_WIDTH
    kn = head_norm(proj[:, w:2 * w], gk_ref[...])
    k_ref[0] = kn.astype(BF16)

    z = proj[:, 5 * w:5 * w + LANES] + bf_ref[...]
    ls = jnp.minimum(z, 0.0) - jnp.log1p(jnp.exp(-jnp.abs(z)))
    lane_c = _lane_iota(ls.shape)
    is_head = lane_c < ATTN_HEADS
    ls = jnp.where(is_head, ls, 0.0)

    def pieces(val):
        p_hi = val.astype(BF16).astype(F32)
        rem = val - p_hi
        p_mid = rem.astype(BF16).astype(F32)
        p_lo = (rem - p_mid).astype(BF16).astype(F32)
        return jnp.where(is_head, p_hi,
                         jnp.where(lane_c < 2 * ATTN_HEADS, pltpu.roll(p_mid, ATTN_HEADS, axis=1),
                                   pltpu.roll(p_lo, 2 * ATTN_HEADS, axis=1))).astype(BF16)

    cs3 = jnp.dot(tri_ref[...], pieces(ls), preferred_element_type=F32)
    cs = cs3 + pltpu.roll(cs3, LANES - ATTN_HEADS, axis=1) + pltpu.roll(cs3, LANES - 2 * ATTN_HEADS, axis=1)
    cum = jnp.where(is_head, cs, 0.0) + carry[0:1, :]
    carry[0:1, :] = cum[tm - 1:tm, :]

    packed = pieces(cum * LOG2E)
    in_aug = lane_c < AUG * ATTN_HEADS
    ones_k = in_aug & ((lane_c % AUG) < AUG // 2)
    kaug = jnp.where(ones_k, 1.0, -jnp.dot(packed, ek_ref[...], preferred_element_type=F32))
    kaug_ref[0] = kaug.astype(BF16)

    uc = proj[:, 3 * w:4 * w] * proj[:, 4 * w:5 * w]
    ucbuf[SUBLANES:SUBLANES + tm, :] = uc
    uc1 = ucbuf[SUBLANES - 1:SUBLANES - 1 + tm, :]
    uc2 = ucbuf[SUBLANES - 2:SUBLANES - 2 + tm, :]
    ucbuf[0:SUBLANES, :] = uc[tm - SUBLANES:tm, :]

    if is_meta:
        cum_ref[0] = cum
        uc_ref[0] = uc
        return

    qn = head_norm(proj[:, 0:w], gq_ref[...])
    q_ref[0] = qn.astype(BF16)
    y = cw_ref[0:1, :] * uc2 + cw_ref[1:2, :] * uc1 + cw_ref[2:3, :] * uc
    conv = proj[:, 2 * w:3 * w] * y
    conv_ref[0] = head_norm(conv, gco_ref[...]).astype(BF16)
    ones_q = in_aug & ((lane_c % AUG) >= AUG // 2)
    qaug = jnp.where(ones_q, 1.0, jnp.dot(packed, eq_ref[...], preferred_element_type=F32))
    qaug_ref[0] = qaug.astype(BF16)


def _inproj(x, is_meta, tm, consts, halo, cum0):
    bsz, tlen, _ = x.shape
    nt = tlen // tm
    gmix, w1, wvt, bfp, gq, gk, cw, gco, gmat = consts
    tri = jnp.tril(jnp.ones((tm, tm), F32)).astype(BF16)
    w = ATTN_WIDTH
    src = jnp.arange(LANES)
    dst = jnp.arange(LANES)
    piece, head = src // ATTN_HEADS, src % ATTN_HEADS
    valid = src < 3 * ATTN_HEADS
    eq = (valid[:, None] & (dst[None, :] == (AUG * head + piece)[:, None])).astype(BF16)
    ek = (valid[:, None] & (dst[None, :] == (AUG * head + AUG // 2 + piece)[:, None])).astype(BF16)

    def full(a):
        return pl.BlockSpec(a.shape, lambda b, t: (0,) * a.ndim)

    in_specs = [pl.BlockSpec((1, tm, D_MODEL), lambda b, t: (b, t, 0))] + [
        full(a) for a in (gmix, w1, wvt, bfp, gq, gk, cw, gco, gmat, tri, eq, ek, halo, cum0)]
    tok = lambda width: pl.BlockSpec((1, tm, width), lambda b, t: (b, t, 0))
    vt_spec = pl.BlockSpec((1, 1, w, tm), lambda b, t: (b, t, 0, 0))
    if is_meta:
        out_shape = [jax.ShapeDtypeStruct((bsz, tlen, w), BF16),
                     jax.ShapeDtypeStruct((bsz, nt, w, tm), BF16),
                     jax.ShapeDtypeStruct((bsz, tlen, LANES), BF16),
                     jax.ShapeDtypeStruct((bsz, tlen, LANES), F32),
                     jax.ShapeDtypeStruct((bsz, tlen, w), F32)]
        out_specs = [tok(w), vt_spec, tok(LANES), tok(LANES), tok(w)]
    else:
        out_shape = [jax.ShapeDtypeStruct((bsz, tlen, w), BF16)] * 2 + [
            jax.ShapeDtypeStruct((bsz, nt, w, tm), BF16),
            jax.ShapeDtypeStruct((bsz, tlen, w), BF16),
            jax.ShapeDtypeStruct((bsz, tlen, LANES), BF16),
            jax.ShapeDtypeStruct((bsz, tlen, LANES), BF16)]
        out_specs = [tok(w), tok(w), vt_spec, tok(w), tok(LANES), tok(LANES)]
    return pl.pallas_call(
        functools.partial(_inproj_kernel, is_meta, tm),
        grid=(bsz, nt),
        in_specs=in_specs,
        out_specs=out_specs,
        out_shape=out_shape,
        scratch_shapes=[pltpu.VMEM((tm + SUBLANES, w), F32), pltpu.VMEM((SUBLANES, LANES), F32)],
        compiler_params=pltpu.CompilerParams(
            dimension_semantics=("arbitrary", "arbitrary"), vmem_limit_bytes=VMEM_LIMIT),
        name="inproj_meta" if is_meta else "inproj",
    )(x, gmix, w1, wvt, bfp, gq, gk, cw, gco, gmat, tri, eq, ek, halo, cum0)


def _attn_kernel(tq, tk, q_ref, qaug_ref, k_ref, kaug_ref, vt_ref, km_ref, kaugm_ref, vmt_ref, gao_ref,
                 o_ref, m_sc, l_sc, acc_sc, sa_sc, sb_sc):
    hp = pl.program_id(1)
    qi = pl.program_id(2)
    q = q_ref[0]
    qaug = qaug_ref[0]
    lane = _lane_iota(q.shape)
    first = lane < HEAD_DIM
    zero = jnp.zeros_like(q)
    qcat = []
    for hh in range(2):
        qh = jnp.where(first, q, zero) if hh == 0 else jnp.where(first, zero, q)
        lo = AUG * (2 * hp + hh)
        qa = jnp.where((lane >= lo) & (lane < lo + AUG), qaug, zero)
        qcat.append(jnp.concatenate([qh, qa], axis=1))

    m_sc[...] = jnp.full(m_sc.shape, MASK_VALUE, F32)
    l_sc[...] = jnp.zeros(l_sc.shape, F32)
    acc_sc[...] = jnp.zeros(acc_sc.shape, F32)

    def scores_t(kcat, q_lo=0):
        return tuple(lax.dot_general(kcat, qcat[hh][q_lo:], (((1,), (1,)), ((), ())), preferred_element_type=F32)
                     for hh in range(2))

    def update(hh, st, vtb, q_lo=0):
        m_old = m_sc[hh, :, q_lo:]
        m_new = jnp.maximum(m_old, jnp.max(st, axis=0, keepdims=True))
        alpha = jnp.exp2(m_old - m_new)
        p = jnp.exp2(st - m_new)
        l_sc[hh, :, q_lo:] = alpha * l_sc[hh, :, q_lo:] + jnp.sum(p, axis=0, keepdims=True)
        vh = vtb[hh * HEAD_DIM:(hh + 1) * HEAD_DIM, :]
        acc_sc[hh, :, q_lo:] = alpha * acc_sc[hh, :, q_lo:] + jnp.dot(vh, p.astype(BF16),
                                                                     preferred_element_type=F32)
        m_sc[hh, :, q_lo:] = m_new

    def key_block(j):
        start = pl.multiple_of(j * tk, tk)
        return jnp.concatenate([k_ref[0, pl.ds(start, tk), :], kaug_ref[0, pl.ds(start, tk), :]], axis=1)

    def put_scores(j, buf, q_lo=0):
        sts = scores_t(key_block(j), q_lo)
        for hh in range(2):
            buf[hh, :, q_lo:] = sts[hh]

    def consume(buf, j, diagonal=False, q_lo=0):
        vtb = vt_ref[0, j]
        for hh in range(2):
            st = buf[hh, :, q_lo:]
            if diagonal:
                key = lax.broadcasted_iota(jnp.int32, st.shape, 0)
                qry = lax.broadcasted_iota(jnp.int32, st.shape, 1)
                st = jnp.where(key <= qry, st, MASK_VALUE)
            update(hh, st, vtb, q_lo)

    per = tq // tk
    assert per % 2 == 0
    st_m = scores_t(jnp.concatenate([km_ref[...], kaugm_ref[...]], axis=1))
    put_scores(0, sa_sc)
    vmt = vmt_ref[...]
    for hh in range(2):
        update(hh, st_m[hh], vmt)

    def body(t, carry):
        j = 2 * t
        put_scores(j + 1, sb_sc)
        consume(sa_sc, j)
        put_scores(j + 2, sa_sc)
        consume(sb_sc, j + 1)
        return carry

    lax.fori_loop(0, (per // 2) * qi, body, 0)
    bufs = (sa_sc, sb_sc)
    for d in range(per):
        if d + 1 < per:
            put_scores(per * qi + d + 1, bufs[(d + 1) % 2], (d + 1) * tk)
        consume(bufs[d % 2], per * qi + d, diagonal=True, q_lo=d * tk)

    ot = jnp.concatenate([acc_sc[0] / l_sc[0], acc_sc[1] / l_sc[1]], axis=0)
    o2 = ot * ot
    ms0 = jnp.sum(o2[0:HEAD_DIM], axis=0, keepdims=True) * (1.0 / HEAD_DIM)
    ms1 = jnp.sum(o2[HEAD_DIM:], axis=0, keepdims=True) * (1.0 / HEAD_DIM)
    inv = jnp.concatenate([jnp.broadcast_to(lax.rsqrt(ms0 + EPS), (HEAD_DIM, tq)),
                           jnp.broadcast_to(lax.rsqrt(ms1 + EPS), (HEAD_DIM, tq))], axis=0)
    o_ref[0] = (ot * inv * gao_ref[...]).T.astype(BF16)


def _attention(q, qaug, k, kaug, vt, km, kaugm, vmt, gao):
    bsz, slen, w = q.shape
    tq, tk = ATTN_Q_TILE, ATTN_K_TILE
    nq, nk = slen // tq, slen // tk
    return pl.pallas_call(
        functools.partial(_attn_kernel, tq, tk),
        grid=(bsz, HEAD_PAIRS, nq),
        in_specs=[
            pl.BlockSpec((1, tq, LANES), lambda b, p, i: (b, i, p)),
            pl.BlockSpec((1, tq, LANES), lambda b, p, i: (b, i, 0)),
            pl.BlockSpec((1, slen, LANES), lambda b, p, i: (b, 0, p)),
            pl.BlockSpec((1, slen, LANES), lambda b, p, i: (b, 0, 0)),
            pl.BlockSpec((1, nk, LANES, tk), lambda b, p, i: (b, 0, p, 0)),
            pl.BlockSpec((N_META, LANES), lambda b, p, i: (0, p)),
            pl.BlockSpec((N_META, LANES), lambda b, p, i: (0, 0)),
            pl.BlockSpec((LANES, N_META), lambda b, p, i: (p, 0)),
            pl.BlockSpec((LANES, 1), lambda b, p, i: (p, 0)),
        ],
        out_specs=pl.BlockSpec((1, tq, LANES), lambda b, p, i: (b, i, p)),
        out_shape=jax.ShapeDtypeStruct((bsz, slen, w), BF16),
        scratch_shapes=[pltpu.VMEM((2, 1, tq), F32), pltpu.VMEM((2, 1, tq), F32),
                        pltpu.VMEM((2, HEAD_DIM, tq), F32),
                        pltpu.VMEM((2, tk, tq), F32), pltpu.VMEM((2, tk, tq), F32)],
        compiler_params=pltpu.CompilerParams(
            dimension_semantics=("arbitrary", "arbitrary", "arbitrary"), vmem_limit_bytes=VMEM_LIMIT),
        name="fox_attention",
    )(q, qaug, k, kaug, vt, km, kaugm, vmt, gao)


ROUTE_ROWS = 64


def _outproj_kernel(tm, attn_ref, conv_ref, x_ref, wo_ref, gffn_ref, wrt_ref, brc_ref, tri_ref,
                    h_ref, xt_ref, ri_ref, rf_ref, cnt_ref, carry):
    i = pl.program_id(0)

    @pl.when(i == 0)
    def _():
        carry[...] = jnp.zeros(carry.shape, F32)

    mixed = jnp.concatenate([attn_ref[...], conv_ref[...]], axis=1)
    h = x_ref[...] + jnp.dot(mixed, wo_ref[...], preferred_element_type=F32)
    h_ref[...] = h
    ms = jnp.mean(h * h, axis=-1, keepdims=True)
    xt = (h * lax.rsqrt(ms + EPS)) * gffn_ref[...]
    _store_token_rows(xt_ref, xt, tm)

    x_hi = xt.astype(BF16)
    x_lo = (xt - x_hi.astype(F32)).astype(BF16)
    parts = lax.dot_general(wrt_ref[...], jnp.concatenate([x_hi, x_lo], axis=1), (((1,), (1,)), ((), ())),
                            preferred_element_type=F32)
    logits = (parts[0:ROUTE_ROWS] + parts[LANES:LANES + ROUTE_ROWS]) + brc_ref[0:ROUTE_ROWS, 0:1]
    row = lax.broadcasted_iota(jnp.int32, logits.shape, 0)
    rowf = row.astype(F32)
    big = float(LANES)

    def first_argmax(vals, vmax):
        return jnp.min(jnp.where(vals == vmax, rowf, big), axis=0, keepdims=True)

    is_g = row < N_EXPERT_GROUPS
    gl = jnp.where(is_g, logits, MASK_VALUE)
    gmax = jnp.max(gl, axis=0, keepdims=True)
    gidx = first_argmax(gl, gmax)
    g_p = 1.0 / jnp.sum(jnp.where(is_g, jnp.exp(gl - gmax), 0.0), axis=0, keepdims=True)

    base = N_EXPERT_GROUPS + EXPERTS_PER_GROUP * gidx
    in_grp = (rowf >= base) & (rowf < base + EXPERTS_PER_GROUP)
    el = jnp.where(in_grp, logits, MASK_VALUE)
    l1 = jnp.max(el, axis=0, keepdims=True)
    e1 = first_argmax(el, l1)
    el2 = jnp.where(rowf == e1, MASK_VALUE, el)
    l2 = jnp.max(el2, axis=0, keepdims=True)
    e2 = first_argmax(el2, l2)
    zsum = jnp.sum(jnp.where(in_grp, jnp.exp(el - l1), 0.0), axis=0, keepdims=True)
    p1 = 1.0 / zsum
    p2 = jnp.exp(l2 - l1) / zsum
    den = p1 + p2
    w1 = g_p * p1 / den
    w2 = g_p * p2 / den
    id1 = e1 - N_EXPERT_GROUPS
    id2 = e2 - N_EXPERT_GROUPS

    oh1 = rowf == id1
    oh2 = rowf == id2
    oh = jnp.where(oh1 | oh2, 1.0, 0.0)
    before = jnp.dot(oh.astype(BF16), tri_ref[...], preferred_element_type=F32) + carry[:, 0:1]
    rank1 = jnp.sum(jnp.where(oh1, before, 0.0), axis=0, keepdims=True)
    rank2 = jnp.sum(jnp.where(oh2, before, 0.0), axis=0, keepdims=True)
    carry[...] = carry[...] + jnp.sum(oh, axis=1, keepdims=True)
    cnt_ref[...] = carry[...]

    pad_i = jnp.zeros((SUBLANES - 2 * TOP_K, tm), F32)
    ri_ref[0] = jnp.concatenate([id1, id2, rank1, rank2, pad_i], axis=0).astype(jnp.int32)
    pad_f = jnp.zeros((LANES - TOP_K, tm), F32)
    rf_ref[...] = jnp.concatenate([w1, w2, pad_f], axis=0).T[:, 0:SUBLANES]


def _outproj(attn, conv, x, wo, gffn, wrt, brc):
    n = x.shape[0]
    tm = OUT_TILE
    tri = jnp.triu(jnp.ones((tm, tm), F32), k=1).astype(BF16)

    def full(a):
        return pl.BlockSpec(a.shape, lambda i: (0,) * a.ndim)

    rows = lambda width: pl.BlockSpec((tm, width), lambda i: (i, 0))
    return pl.pallas_call(
        functools.partial(_outproj_kernel, tm),
        grid=(n // tm,),
        in_specs=[rows(ATTN_WIDTH), rows(CONV_WIDTH), rows(D_MODEL)] + [
            full(a) for a in (wo, gffn, wrt, brc, tri)],
        out_specs=[rows(D_MODEL), pl.BlockSpec((tm * ROW_SUBLANES, LANES), lambda i: (i, 0)),
                   pl.BlockSpec((1, SUBLANES, tm), lambda i: (i, 0, 0)), rows(SUBLANES),
                   pl.BlockSpec((ROUTE_ROWS, LANES), lambda i: (0, 0))],
        out_shape=[jax.ShapeDtypeStruct((n, D_MODEL), F32),
                   jax.ShapeDtypeStruct((n * ROW_SUBLANES, LANES), F32),
                   jax.ShapeDtypeStruct((n // tm, SUBLANES, tm), jnp.int32),
                   jax.ShapeDtypeStruct((n, SUBLANES), F32),
                   jax.ShapeDtypeStruct((ROUTE_ROWS, LANES), F32)],
        scratch_shapes=[pltpu.VMEM((ROUTE_ROWS, LANES), F32)],
        compiler_params=pltpu.CompilerParams(
            dimension_semantics=("arbitrary",), vmem_limit_bytes=VMEM_LIMIT),
        name="outproj_router",
    )(attn, conv, x, wo, gffn, wrt, brc, tri)


def _dest_kernel(rs_ref, ri_ref, dest_ref):
    ri = ri_ref[...]
    experts = ri[:, 0:TOP_K, :]
    start = jnp.zeros_like(experts)
    for e in range(N_EXPERTS):
        start = jnp.where(experts == e, rs_ref[e], start)
    dest_ref[...] = jnp.concatenate([start + ri[:, TOP_K:2 * TOP_K, :],
                                     jnp.zeros((ri.shape[0], SUBLANES - TOP_K, ri.shape[2]), jnp.int32)], axis=1)


def _dest_rows(ri_t, row_start):
    grid_spec = pltpu.PrefetchScalarGridSpec(
        num_scalar_prefetch=1,
        grid=(1,),
        in_specs=[pl.BlockSpec(ri_t.shape, lambda i, rs: (0, 0, 0))],
        out_specs=pl.BlockSpec(ri_t.shape, lambda i, rs: (0, 0, 0)),
    )
    return pl.pallas_call(
        _dest_kernel,
        grid_spec=grid_spec,
        out_shape=jax.ShapeDtypeStruct(ri_t.shape, jnp.int32),
        compiler_params=pltpu.CompilerParams(dimension_semantics=("arbitrary",)),
        name="moe_dest",
    )(row_start, ri_t)


def _tile_copy(src, src_row, dst, dst_row, sem):
    rs = ROW_SUBLANES
    return pltpu.make_async_copy(src.at[pl.ds(pl.multiple_of(src_row * rs, rs), rs), :],
                                 dst.at[pl.ds(pl.multiple_of(dst_row * rs, rs), rs), :], sem)


def _dispatch_kernel(tm, bm, fz_ref, fv_ref, dest_ref, xt_ref, xs_hbm, zeros, sem, zsem):
    i = pl.program_id(0)
    block_rows = bm * ROW_SUBLANES

    def zero_block(z):
        start = pl.multiple_of(fz_ref[z] * block_rows, block_rows)
        return pltpu.make_async_copy(zeros, xs_hbm.at[pl.ds(start, block_rows), :], zsem.at[0])

    @pl.when(i == 0)
    def _():
        zeros[...] = jnp.zeros(zeros.shape, F32)
        for z in range(fz_ref.shape[0]):
            @pl.when(fv_ref[z] == 1)
            def _():
                zero_block(z).start()
        for z in range(fz_ref.shape[0]):
            @pl.when(fv_ref[z] == 1)
            def _():
                zero_block(z).wait()

    for g in range(0, tm, DMA_GROUP):
        dests = [(r, kk, dest_ref[0, kk, r]) for r in range(g, g + DMA_GROUP) for kk in range(TOP_K)]
        for r, kk, d in dests:
            _tile_copy(xt_ref, r, xs_hbm, d, sem.at[0]).start(priority=kk)
    for kk in range(TOP_K):
        pltpu.make_async_copy(xt_ref, xs_hbm.at[pl.ds(0, tm * ROW_SUBLANES), :], sem.at[0]).wait()


def _route_spec(dest, tm, extra_args):
    per = dest.shape[2] // tm
    nt = dest.shape[0] * per
    if extra_args == 0:
        index = lambda i: (jnp.minimum(i, nt - 1) // per, 0, jnp.minimum(i, nt - 1) % per)
    else:
        index = lambda i, *_: (jnp.minimum(i, nt - 1) // per, 0, jnp.minimum(i, nt - 1) % per)
    return nt, pl.BlockSpec((1, SUBLANES, tm), index, memory_space=pltpu.SMEM)


def _dispatch(xt_rows, dest, fill_blocks, fill_valid, nb):
    tm = DISPATCH_TILE
    bm = EXPERT_ROWS
    nt, dest_spec = _route_spec(dest, tm, 2)
    grid_spec = pltpu.PrefetchScalarGridSpec(
        num_scalar_prefetch=2,
        grid=(nt,),
        in_specs=[dest_spec, pl.BlockSpec((tm * ROW_SUBLANES, LANES), lambda i, fz, fv: (i, 0))],
        out_specs=pl.BlockSpec(memory_space=pl.ANY),
        scratch_shapes=[pltpu.VMEM((bm * ROW_SUBLANES, LANES), F32), pltpu.SemaphoreType.DMA((1,)),
                        pltpu.SemaphoreType.DMA((1,))],
    )
    return pl.pallas_call(
        functools.partial(_dispatch_kernel, tm, bm),
        grid_spec=grid_spec,
        out_shape=jax.ShapeDtypeStruct((nb * bm * ROW_SUBLANES, LANES), F32),
        compiler_params=pltpu.CompilerParams(
            dimension_semantics=("arbitrary",), vmem_limit_bytes=VMEM_LIMIT),
        name="moe_dispatch",
    )(fill_blocks, fill_valid, dest, xt_rows)


def _expert_kernel(bm, be_ref, nu_ref, first_ref, nxt_ref, slot_ref, xs_ref, wg_hbm, wu_hbm, wd_hbm, ys_ref,
                   wg_buf, wu_buf, wd_buf, wg_bf, wu_bf, wd_bf, wsem):
    step = pl.program_id(0)

    def weight_copies(e, slot):
        return [pltpu.make_async_copy(src.at[e], buf.at[slot], wsem.at[m, slot])
                for m, (src, buf) in enumerate(((wg_hbm, wg_buf), (wu_hbm, wu_buf), (wd_hbm, wd_buf)))]

    @pl.when(step == 0)
    def _():
        for c in weight_copies(be_ref[0], 0):
            c.start()

    def one_block(j, xs_blk, ys_blk):
        @pl.when(j < nu_ref[0])
        def _():
            slot = slot_ref[j]

            @pl.when(first_ref[j] == 1)
            def _():
                for c in weight_copies(0, slot):
                    c.wait()

                @pl.when(nxt_ref[j] >= 0)
                def _():
                    for c in weight_copies(nxt_ref[j], 1 - slot):
                        c.start()

                wg_bf[...] = wg_buf[slot].astype(BF16)
                wu_bf[...] = wu_buf[slot].astype(BF16)
                wd_bf[...] = wd_buf[slot].astype(BF16)

            x = _load_token_rows(xs_blk, bm).astype(BF16)
            hg = jnp.dot(x, wg_bf[...], preferred_element_type=F32)
            hu = jnp.dot(x, wu_bf[...], preferred_element_type=F32)
            hdn = hg * (1.0 / (1.0 + jnp.exp(-hg))) * hu
            y = jnp.dot(hdn.astype(BF16), wd_bf[...], preferred_element_type=F32)
            _store_token_rows(ys_blk, y, bm)

        @pl.when(j >= nu_ref[0])
        def _():
            ys_blk[...] = jnp.zeros(ys_blk.shape, F32)

    rows = bm * ROW_SUBLANES
    for sub in range(EXPERT_BLOCKS_PER_STEP):
        view = pl.ds(sub * rows, rows)
        one_block(step * EXPERT_BLOCKS_PER_STEP + sub, xs_ref.at[view, :], ys_ref.at[view, :])


def _experts(xs_rows, block_expert, n_used, w_gate, w_up, w_down):
    bm = EXPERT_ROWS
    nb = block_expert.shape[0]
    blk = jnp.arange(nb, dtype=jnp.int32)
    live = blk < n_used[0]
    first = live & ((blk == 0) | (block_expert != jnp.roll(block_expert, 1)))
    slot = (jnp.cumsum(first.astype(jnp.int32)) - 1) % 2
    later_first = first[None, :] & (blk[None, :] > blk[:, None])
    nxt_start = jnp.min(jnp.where(later_first, blk[None, :], nb), axis=1)
    nxt = jnp.sum(jnp.where(blk[None, :] == nxt_start[:, None], block_expert[None, :], 0), axis=1)
    nxt = jnp.where(nxt_start < nb, nxt, -1)
    per = EXPERT_BLOCKS_PER_STEP
    assert nb % per == 0
    step_rows = per * bm * ROW_SUBLANES
    grid_spec = pltpu.PrefetchScalarGridSpec(
        num_scalar_prefetch=5,
        grid=(nb // per,),
        in_specs=[
            pl.BlockSpec((step_rows, LANES), lambda s, be, nu, *_: (jnp.minimum(s, (nu[0] - 1) // per), 0)),
            pl.BlockSpec(memory_space=pl.ANY),
            pl.BlockSpec(memory_space=pl.ANY),
            pl.BlockSpec(memory_space=pl.ANY),
        ],
        out_specs=pl.BlockSpec((step_rows, LANES), lambda s, *_: (s, 0)),
        scratch_shapes=[pltpu.VMEM((2, D_MODEL, D_EXPERT), F32), pltpu.VMEM((2, D_MODEL, D_EXPERT), F32),
                        pltpu.VMEM((2, D_EXPERT, D_MODEL), F32),
                        pltpu.VMEM((D_MODEL, D_EXPERT), BF16), pltpu.VMEM((D_MODEL, D_EXPERT), BF16),
                        pltpu.VMEM((D_EXPERT, D_MODEL), BF16), pltpu.SemaphoreType.DMA((3, 2))],
    )
    return pl.pallas_call(
        functools.partial(_expert_kernel, bm),
        grid_spec=grid_spec,
        out_shape=jax.ShapeDtypeStruct((nb * bm * ROW_SUBLANES, LANES), F32),
        compiler_params=pltpu.CompilerParams(
            dimension_semantics=("arbitrary",), vmem_limit_bytes=VMEM_LIMIT),
        name="moe_experts",
    )(block_expert, n_used, first.astype(jnp.int32), nxt.astype(jnp.int32), slot.astype(jnp.int32),
      xs_rows, w_gate, w_up, w_down)


def _combine_kernel(tm, dest_ref, ys_hbm, h_ref, rf_ref, o_ref, ybuf, sem):
    i = pl.program_id(0)
    nt = pl.num_programs(0) - 1

    @pl.when(i < nt)
    def _():
        slot = i % 2
        for g in range(0, tm, DMA_GROUP):
            dests = [(r, kk, dest_ref[0, kk, r]) for r in range(g, g + DMA_GROUP) for kk in range(TOP_K)]
            for r, kk, d in dests:
                _tile_copy(ys_hbm, d, ybuf.at[kk, slot], r, sem.at[kk, slot]).start(priority=kk)

    @pl.when(i >= 1)
    def _():
        slot = (i - 1) % 2
        for kk in range(TOP_K):
            pltpu.make_async_copy(ys_hbm.at[pl.ds(0, tm * ROW_SUBLANES), :], ybuf.at[kk, slot],
                                  sem.at[kk, slot]).wait()
        rf = rf_ref[...]
        o_ref[...] = (h_ref[...] + rf[:, 0:1] * _load_token_rows(ybuf.at[0, slot], tm)
                      + rf[:, 1:2] * _load_token_rows(ybuf.at[1, slot], tm))


def _combine(ys_rows, dest, h, rf):
    n = h.shape[0]
    tm = COMBINE_TILE
    nt, dest_spec = _route_spec(dest, tm, 0)
    prev = lambda i: jnp.maximum(i - 1, 0)
    return pl.pallas_call(
        functools.partial(_combine_kernel, tm),
        grid=(nt + 1,),
        in_specs=[
            dest_spec,
            pl.BlockSpec(memory_space=pl.ANY),
            pl.BlockSpec((tm, D_MODEL), lambda i: (prev(i), 0)),
            pl.BlockSpec((tm, SUBLANES), lambda i: (prev(i), 0)),
        ],
        out_specs=pl.BlockSpec((tm, D_MODEL), lambda i: (prev(i), 0)),
        out_shape=jax.ShapeDtypeStruct((n, D_MODEL), F32),
        scratch_shapes=[pltpu.VMEM((TOP_K, 2, tm * ROW_SUBLANES, LANES), F32),
                        pltpu.SemaphoreType.DMA((TOP_K, 2))],
        compiler_params=pltpu.CompilerParams(
            dimension_semantics=("arbitrary",), vmem_limit_bytes=VMEM_LIMIT),
        name="moe_combine",
    )(dest, ys_rows, h, rf)


def _pad_lanes(a, width=LANES):
    return jnp.pad(a, ((0, 0), (0, width - a.shape[-1])))


def _layer(x, meta_tokens, g_mix, w_in, b_forget, q_norm_g, k_norm_g, conv_w, attn_out_g, conv_out_g,
           w_out, g_ffn, w_rg, b_rg, w_re, b_re, w_gate, w_up, w_down):
    bsz, slen, _ = x.shape
    n = bsz * slen
    w = ATTN_WIDTH

    wq, wk, wv, wf, wb, wc, whc = jnp.split(
        w_in, [w, 2 * w, 3 * w, 3 * w + ATTN_HEADS, 4 * w + ATTN_HEADS, 5 * w + ATTN_HEADS], axis=1)
    w1 = jnp.concatenate([wq, wk, wb, wc, whc, _pad_lanes(wf)], axis=1).astype(BF16)
    head_of = jnp.arange(MXU_DIM) // HEAD_DIM
    gmat = (head_of[:, None] == head_of[None, :]).astype(BF16)
    consts = (
        g_mix.reshape(1, D_MODEL), w1, wv.T.astype(BF16), _pad_lanes(b_forget.reshape(1, ATTN_HEADS)),
        jnp.tile(q_norm_g, ATTN_HEADS).reshape(1, w) * (HEAD_DIM ** -0.5 * LOG2E),
        jnp.tile(k_norm_g, ATTN_HEADS).reshape(1, w),
        jnp.pad(conv_w, ((0, SUBLANES - CONV_K), (0, 0))), conv_out_g.reshape(1, w), gmat)

    xm = jnp.pad(meta_tokens, ((0, META_TILE - N_META), (0, 0)))[None]
    km, vmt, kaugm, cum_m, uc_m = _inproj(xm, True, META_TILE, consts,
                                          jnp.zeros((SUBLANES, w), F32), jnp.zeros((1, LANES), F32))
    km, vmt, kaugm = km[0, :N_META], vmt[0, 0, :, :N_META], kaugm[0, :N_META]
    cum_m = cum_m[0, :N_META]
    halo = jnp.zeros((SUBLANES, w), F32).at[SUBLANES - 2:].set(uc_m[0, N_META - 2:N_META])
    cum0 = cum_m[N_META - 1:N_META]

    q, k, vt, conv, qaug, kaug = _inproj(x, False, IN_TILE, consts, halo, cum0)
    attn = _attention(q, qaug, k, kaug, vt, km, kaugm, vmt, attn_out_g.reshape(w, 1))

    wr = _pad_lanes(jnp.concatenate([w_rg, w_re], axis=1)).T
    wr_hi = wr.astype(BF16)
    wr_lo = (wr - wr_hi.astype(F32)).astype(BF16)
    wrt = jnp.concatenate([jnp.concatenate([wr_hi, wr_hi], axis=1),
                           jnp.concatenate([wr_lo, jnp.zeros_like(wr_lo)], axis=1)], axis=0)
    brc = jnp.broadcast_to(_pad_lanes(jnp.concatenate([b_rg, b_re]).reshape(1, -1)).T, (LANES, LANES))
    h, xt, ri, rf, cnt = _outproj(attn.reshape(n, w), conv.reshape(n, w), x.reshape(n, D_MODEL),
                                  w_out.astype(BF16), g_ffn.reshape(1, D_MODEL), wrt, brc)

    bm = EXPERT_ROWS
    nb = (n * TOP_K) // bm + N_EXPERTS
    counts = cnt[:N_EXPERTS, 0].astype(jnp.int32)
    blocks_e = (counts + bm - 1) // bm
    blk_end = jnp.cumsum(blocks_e)
    row_start = (blk_end - blocks_e) * bm
    blk = jnp.arange(nb, dtype=jnp.int32)
    block_expert = jnp.minimum(jnp.sum(blk_end[None, :] <= blk[:, None], axis=1), N_EXPERTS - 1).astype(jnp.int32)
    n_used = blk_end[-1:].astype(jnp.int32)
    tail = n_used[0] + jnp.arange(N_EXPERTS, dtype=jnp.int32)
    fill_blocks = jnp.clip(jnp.concatenate([blk_end - 1, tail]), 0, nb - 1).astype(jnp.int32)
    fill_valid = jnp.concatenate([counts % bm != 0, tail < nb]).astype(jnp.int32)

    dest = _dest_rows(ri, row_start.astype(jnp.int32))
    xs = _dispatch(xt, dest, fill_blocks, fill_valid, nb)
    ys = _experts(xs, block_expert, n_used, w_gate, w_up, w_down)
    out = _combine(ys, dest, h, rf)
    return out.reshape(bsz, slen, D_MODEL)


def kernel(x, meta_tokens, norm_mix_g, w_in, b_forget, q_norm_g, k_norm_g, conv_w, attn_out_g, conv_out_g,
           w_out, norm_ffn_g, w_router_group, b_router_group, w_router_expert, b_router_expert,
           w_gate, w_up, w_down):
    assert norm_mix_g.shape[0] == 1, "single-layer block"
    return _layer(x, meta_tokens, norm_mix_g[0], w_in[0], b_forget[0], q_norm_g[0], k_norm_g[0], conv_w[0],
                  attn_out_g[0], conv_out_g[0], w_out[0], norm_ffn_g[0], w_router_group[0],
                  b_router_group[0], w_router_expert[0], b_router_expert[0], w_gate[0], w_up[0], w_down[0])
```

```python
import functools

import jax
import jax.numpy as jnp
from jax import lax
from jax.experimental import pallas as pl
from jax.experimental.pallas import tpu as pltpu

D_MODEL = 1024
N_META = 16
HEAD_DIM = 64
ATTN_HEADS = 8
ATTN_WIDTH = ATTN_HEADS * HEAD_DIM
CONV_WIDTH = D_MODEL - ATTN_WIDTH
CONV_K = 3
N_EXPERT_GROUPS = 4
EXPERTS_PER_GROUP = 8
N_EXPERTS = N_EXPERT_GROUPS * EXPERTS_PER_GROUP
TOP_K = 2
D_EXPERT = 512
EPS = 1e-6
MASK_VALUE = -1e30
LOG2E = 1.4426950408889634
AUG = 6

LANES = 128
SUBLANES = 8
MXU_DIM = 256
HEAD_PAIRS = ATTN_HEADS * HEAD_DIM // LANES
PROJ_PAD_COLS = 5 * ATTN_WIDTH + LANES
VMEM_LIMIT = 56 * 1024 * 1024

IN_TILE = 512
META_TILE = 128
ATTN_Q_TILE = 2048
ATTN_K_TILE = 512
OUT_TILE = 1024
EXPERT_ROWS = 256
EXPERT_BLOCKS_PER_STEP = 4
DISPATCH_TILE = 1024
COMBINE_TILE = 256
DMA_GROUP = 8

F32 = jnp.float32
BF16 = jnp.bfloat16
ROW_SUBLANES = D_MODEL // LANES


def _lane_iota(shape):
    return lax.broadcasted_iota(jnp.int32, shape, len(shape) - 1)


def _store_token_rows(ref, x, rows):
    for sl in range(ROW_SUBLANES):
        ref[pl.ds(sl, rows, stride=ROW_SUBLANES), :] = x[:, sl * LANES:(sl + 1) * LANES]


def _load_token_rows(ref, rows):
    return jnp.concatenate([ref[pl.ds(sl, rows, stride=ROW_SUBLANES), :] for sl in range(ROW_SUBLANES)], axis=1)


def _inproj_kernel(is_meta, tm, x_ref, gmix_ref, w1_ref, wvt_ref, bf_ref, gq_ref, gk_ref, cw_ref, gco_ref,
                   gmat_ref, tri_ref, eq_ref, ek_ref, halo_ref, cum0_ref, *rest):
    if is_meta:
        k_ref, vt_ref, kaug_ref, cum_ref, uc_ref, ucbuf, carry = rest
    else:
        q_ref, k_ref, vt_ref, conv_ref, qaug_ref, kaug_ref, ucbuf, carry = rest
    t = pl.program_id(1)

    @pl.when(t == 0)
    def _():
        ucbuf[0:SUBLANES, :] = halo_ref[...]
        carry[0:1, :] = cum0_ref[...]

    x = x_ref[0]
    ms = jnp.mean(x * x, axis=-1, keepdims=True)
    u = (x * lax.rsqrt(ms + EPS)) * gmix_ref[...]
    ub = u.astype(BF16)
    proj = jnp.dot(ub, w1_ref[...], preferred_element_type=F32)
    vt_ref[0, 0] = lax.dot_general(wvt_ref[...], ub, (((1,), (1,)), ((), ())),
                                   preferred_element_type=F32).astype(BF16)

    def head_norm(z, g):
        z2 = (z * z).astype(BF16)
        half = gmat_ref.shape[0]
        ssq = jnp.concatenate([jnp.dot(z2[:, c:c + half], gmat_ref[...], preferred_element_type=F32)
                               for c in range(0, z.shape[1], half)], axis=1)
        return z * lax.rsqrt(ssq * (1.0 / HEAD_DIM) + EPS) * g

    w = ATTN_WIDTH
    kn = head_norm(proj[:, w:2 * w], gk_ref[...])
    k_ref[0] = kn.astype(BF16)

    z = proj[:, 5 * w:5 * w + LANES] + bf_ref[...]
    ls = jnp.minimum(z, 0.0) - jnp.log1p(jnp.exp(-jnp.abs(z)))
    lane_c = _lane_iota(ls.shape)
    is_head = lane_c < ATTN_HEADS
    ls = jnp.where(is_head, ls, 0.0)

    def pieces(val):
        p_hi = val.astype(BF16).astype(F32)
        rem = val - p_hi
        p_mid = rem.astype(BF16).astype(F32)
        p_lo = (rem - p_mid).astype(BF16).astype(F32)
        return jnp.where(is_head, p_hi,
                         jnp.where(lane_c < 2 * ATTN_HEADS, pltpu.roll(p_mid, ATTN_HEADS, axis=1),
                                   pltpu.roll(p_lo, 2 * ATTN_HEADS, axis=1))).astype(BF16)

    cs3 = jnp.dot(tri_ref[...], pieces(ls), preferred_element_type=F32)
    cs = cs3 + pltpu.roll(cs3, LANES - ATTN_HEADS, axis=1) + pltpu.roll(cs3, LANES - 2 * ATTN_HEADS, axis=1)
    cum = jnp.where(is_head, cs, 0.0) + carry[0:1, :]
    carry[0:1, :] = cum[tm - 1:tm, :]

    packed = pieces(cum * LOG2E)
    in_aug = lane_c < AUG * ATTN_HEADS
    ones_k = in_aug & ((lane_c % AUG) < AUG // 2)
    kaug = jnp.where(ones_k, 1.0, -jnp.dot(packed, ek_ref[...], preferred_element_type=F32))
    kaug_ref[0] = kaug.astype(BF16)

    uc = proj[:, 3 * w:4 * w] * proj[:, 4 * w:5 * w]
    ucbuf[SUBLANES:SUBLANES + tm, :] = uc
    uc1 = ucbuf[SUBLANES - 1:SUBLANES - 1 + tm, :]
    uc2 = ucbuf[SUBLANES - 2:SUBLANES - 2 + tm, :]
    ucbuf[0:SUBLANES, :] = uc[tm - SUBLANES:tm, :]

    if is_meta:
        cum_ref[0] = cum
        uc_ref[0] = uc
        return

    qn = head_norm(proj[:, 0:w], gq_ref[...])
    q_ref[0] = qn.astype(BF16)
    y = cw_ref[0:1, :] * uc2 + cw_ref[1:2, :] * uc1 + cw_ref[2:3, :] * uc
    conv = proj[:, 2 * w:3 * w] * y
    conv_ref[0] = head_norm(conv, gco_ref[...]).astype(BF16)
    ones_q = in_aug & ((lane_c % AUG) >= AUG // 2)
    qaug = jnp.where(ones_q, 1.0, jnp.dot(packed, eq_ref[...], preferred_element_type=F32))
    qaug_ref[0] = qaug.astype(BF16)


def _inproj(x, is_meta, tm, consts, halo, cum0):
    bsz, tlen, _ = x.shape
    nt = tlen // tm
    gmix, w1, wvt, bfp, gq, gk, cw, gco, gmat = consts
    tri = jnp.tril(jnp.ones((tm, tm), F32)).astype(BF16)
    w = ATTN_WIDTH
    src = jnp.arange(LANES)
    dst = jnp.arange(LANES)
    piece, head = src // ATTN_HEADS, src % ATTN_HEADS
    valid = src < 3 * ATTN_HEADS
    eq = (valid[:, None] & (dst[None, :] == (AUG * head + piece)[:, None])).astype(BF16)
    ek = (valid[:, None] & (dst[None, :] == (AUG * head + AUG // 2 + piece)[:, None])).astype(BF16)

    def full(a):
        return pl.BlockSpec(a.shape, lambda b, t: (0,) * a.ndim)

    in_specs = [pl.BlockSpec((1, tm, D_MODEL), lambda b, t: (b, t, 0))] + [
        full(a) for a in (gmix, w1, wvt, bfp, gq, gk, cw, gco, gmat, tri, eq, ek, halo, cum0)]
    tok = lambda width: pl.BlockSpec((1, tm, width), lambda b, t: (b, t, 0))
    vt_spec = pl.BlockSpec((1, 1, w, tm), lambda b, t: (b, t, 0, 0))
    if is_meta:
        out_shape = [jax.ShapeDtypeStruct((bsz, tlen, w), BF16),
                     jax.ShapeDtypeStruct((bsz, nt, w, tm), BF16),
                     jax.ShapeDtypeStruct((bsz, tlen, LANES), BF16),
                     jax.ShapeDtypeStruct((bsz, tlen, LANES), F32),
                     jax.ShapeDtypeStruct((bsz, tlen, w), F32)]
        out_specs = [tok(w), vt_spec, tok(LANES), tok(LANES), tok(w)]
    else:
        out_shape = [jax.ShapeDtypeStruct((bsz, tlen, w), BF16)] * 2 + [
            jax.ShapeDtypeStruct((bsz, nt, w, tm), BF16),
            jax.ShapeDtypeStruct((bsz, tlen, w), BF16),
            jax.ShapeDtypeStruct((bsz, tlen, LANES), BF16),
            jax.ShapeDtypeStruct((bsz, tlen, LANES), BF16)]
        out_specs = [tok(w), tok(w), vt_spec, tok(w), tok(LANES), tok(LANES)]
    return pl.pallas_call(
        functools.partial(_inproj_kernel, is_meta, tm),
        grid=(bsz, nt),
        in_specs=in_specs,
        out_specs=out_specs,
        out_shape=out_shape,
        scratch_shapes=[pltpu.VMEM((tm + SUBLANES, w), F32), pltpu.VMEM((SUBLANES, LANES), F32)],
        compiler_params=pltpu.CompilerParams(
            dimension_semantics=("arbitrary", "arbitrary"), vmem_limit_bytes=VMEM_LIMIT),
        name="inproj_meta" if is_meta else "inproj",
    )(x, gmix, w1, wvt, bfp, gq, gk, cw, gco, gmat, tri, eq, ek, halo, cum0)


def _attn_kernel(tq, tk, q_ref, qaug_ref, k_ref, kaug_ref, vt_ref, km_ref, kaugm_ref, vmt_ref, gao_ref,
                 o_ref, m_sc, l_sc, acc_sc, sa_sc, sb_sc, ma_sc, mb_sc):
    hp = pl.program_id(1)
    qi = pl.program_id(2)
    q = q_ref[0]
    qaug = qaug_ref[0]
    lane = _lane_iota(q.shape)
    first = lane < HEAD_DIM
    zero = jnp.zeros_like(q)
    qcat = []
    for hh in range(2):
        qh = jnp.where(first, q, zero) if hh == 0 else jnp.where(first, zero, q)
        lo = AUG * (2 * hp + hh)
        qa = jnp.where((lane >= lo) & (lane < lo + AUG), qaug, zero)
        qcat.append(jnp.concatenate([qh, qa], axis=1))

    m_sc[...] = jnp.full(m_sc.shape, MASK_VALUE, F32)
    l_sc[...] = jnp.zeros(l_sc.shape, F32)
    acc_sc[...] = jnp.zeros(acc_sc.shape, F32)

    def scores_t(kcat, q_lo=0):
        return tuple(lax.dot_general(kcat, qcat[hh][q_lo:], (((1,), (1,)), ((), ())), preferred_element_type=F32)
                     for hh in range(2))

    def update(hh, st, vtb, q_lo=0, block_max=None):
        m_old = m_sc[hh, :, q_lo:]
        if block_max is None:
            block_max = jnp.max(st, axis=0, keepdims=True)
        m_new = jnp.maximum(m_old, block_max)
        alpha = jnp.exp2(m_old - m_new)
        p = jnp.exp2(st - m_new)
        l_sc[hh, :, q_lo:] = alpha * l_sc[hh, :, q_lo:] + jnp.sum(p, axis=0, keepdims=True)
        vh = vtb[hh * HEAD_DIM:(hh + 1) * HEAD_DIM, :]
        acc_sc[hh, :, q_lo:] = alpha * acc_sc[hh, :, q_lo:] + jnp.dot(vh, p.astype(BF16),
                                                                     preferred_element_type=F32)
        m_sc[hh, :, q_lo:] = m_new

    def key_block(j):
        start = pl.multiple_of(j * tk, tk)
        return jnp.concatenate([k_ref[0, pl.ds(start, tk), :], kaug_ref[0, pl.ds(start, tk), :]], axis=1)

    def put_scores(j, bufs, q_lo=0):
        buf, mbuf = bufs
        sts = scores_t(key_block(j), q_lo)
        for hh in range(2):
            buf[hh, :, q_lo:] = sts[hh]
            mbuf[hh, :, q_lo:] = jnp.max(sts[hh], axis=0, keepdims=True)

    def consume(bufs, j, diagonal=False, q_lo=0):
        buf, mbuf = bufs
        vtb = vt_ref[0, j]
        for hh in range(2):
            st = buf[hh, :, q_lo:]
            if diagonal:
                key = lax.broadcasted_iota(jnp.int32, st.shape, 0)
                qry = lax.broadcasted_iota(jnp.int32, st.shape, 1)
                update(hh, jnp.where(key <= qry, st, MASK_VALUE), vtb, q_lo)
            else:
                update(hh, st, vtb, q_lo, mbuf[hh, :, q_lo:])

    per = tq // tk
    assert per % 2 == 0
    st_m = scores_t(jnp.concatenate([km_ref[...], kaugm_ref[...]], axis=1))
    buf_a, buf_b = (sa_sc, ma_sc), (sb_sc, mb_sc)
    put_scores(0, buf_a)
    vmt = vmt_ref[...]
    for hh in range(2):
        update(hh, st_m[hh], vmt)

    def body(t, carry):
        j = 2 * t
        put_scores(j + 1, buf_b)
        consume(buf_a, j)
        put_scores(j + 2, buf_a)
        consume(buf_b, j + 1)
        return carry

    lax.fori_loop(0, (per // 2) * qi, body, 0)
    bufs = (buf_a, buf_b)
    for d in range(per):
        if d + 1 < per:
            put_scores(per * qi + d + 1, bufs[(d + 1) % 2], (d + 1) * tk)
        consume(bufs[d % 2], per * qi + d, diagonal=True, q_lo=d * tk)

    ot = jnp.concatenate([acc_sc[0] / l_sc[0], acc_sc[1] / l_sc[1]], axis=0)
    o2 = ot * ot
    ms0 = jnp.sum(o2[0:HEAD_DIM], axis=0, keepdims=True) * (1.0 / HEAD_DIM)
    ms1 = jnp.sum(o2[HEAD_DIM:], axis=0, keepdims=True) * (1.0 / HEAD_DIM)
    inv = jnp.concatenate([jnp.broadcast_to(lax.rsqrt(ms0 + EPS), (HEAD_DIM, tq)),
                           jnp.broadcast_to(lax.rsqrt(ms1 + EPS), (HEAD_DIM, tq))], axis=0)
    o_ref[0] = (ot * inv * gao_ref[...]).T.astype(BF16)


def _attention(q, qaug, k, kaug, vt, km, kaugm, vmt, gao):
    bsz, slen, w = q.shape
    tq, tk = ATTN_Q_TILE, ATTN_K_TILE
    nq, nk = slen // tq, slen // tk
    return pl.pallas_call(
        functools.partial(_attn_kernel, tq, tk),
        grid=(bsz, HEAD_PAIRS, nq),
        in_specs=[
            pl.BlockSpec((1, tq, LANES), lambda b, p, i: (b, i, p)),
            pl.BlockSpec((1, tq, LANES), lambda b, p, i: (b, i, 0)),
            pl.BlockSpec((1, slen, LANES), lambda b, p, i: (b, 0, p)),
            pl.BlockSpec((1, slen, LANES), lambda b, p, i: (b, 0, 0)),
            pl.BlockSpec((1, nk, LANES, tk), lambda b, p, i: (b, 0, p, 0)),
            pl.BlockSpec((N_META, LANES), lambda b, p, i: (0, p)),
            pl.BlockSpec((N_META, LANES), lambda b, p, i: (0, 0)),
            pl.BlockSpec((LANES, N_META), lambda b, p, i: (p, 0)),
            pl.BlockSpec((LANES, 1), lambda b, p, i: (p, 0)),
        ],
        out_specs=pl.BlockSpec((1, tq, LANES), lambda b, p, i: (b, i, p)),
        out_shape=jax.ShapeDtypeStruct((bsz, slen, w), BF16),
        scratch_shapes=[pltpu.VMEM((2, 1, tq), F32), pltpu.VMEM((2, 1, tq), F32),
                        pltpu.VMEM((2, HEAD_DIM, tq), F32),
                        pltpu.VMEM((2, tk, tq), F32), pltpu.VMEM((2, tk, tq), F32),
                        pltpu.VMEM((2, 1, tq), F32), pltpu.VMEM((2, 1, tq), F32)],
        compiler_params=pltpu.CompilerParams(
            dimension_semantics=("arbitrary", "arbitrary", "arbitrary"), vmem_limit_bytes=VMEM_LIMIT),
        name="fox_attention",
    )(q, qaug, k, kaug, vt, km, kaugm, vmt, gao)


ROUTE_ROWS = 64


def _outproj_kernel(tm, attn_ref, conv_ref, x_ref, wo_ref, gffn_ref, wrt_ref, brc_ref, tri_ref,
                    h_ref, xt_ref, ri_ref, rf_ref, cnt_ref, carry):
    i = pl.program_id(0)

    @pl.when(i == 0)
    def _():
        carry[...] = jnp.zeros(carry.shape, F32)

    mixed = jnp.concatenate([attn_ref[...], conv_ref[...]], axis=1)
    h = x_ref[...] + jnp.dot(mixed, wo_ref[...], preferred_element_type=F32)
    h_ref[...] = h
    ms = jnp.mean(h * h, axis=-1, keepdims=True)
    xt = (h * lax.rsqrt(ms + EPS)) * gffn_ref[...]
    _store_token_rows(xt_ref, xt, tm)

    x_hi = xt.astype(BF16)
    x_lo = (xt - x_hi.astype(F32)).astype(BF16)
    parts = lax.dot_general(wrt_ref[...], jnp.concatenate([x_hi, x_lo], axis=1), (((1,), (1,)), ((), ())),
                            preferred_element_type=F32)
    logits = (parts[0:ROUTE_ROWS] + parts[LANES:LANES + ROUTE_ROWS]) + brc_ref[0:ROUTE_ROWS, 0:1]
    row = lax.broadcasted_iota(jnp.int32, logits.shape, 0)
    rowf = row.astype(F32)
    big = float(LANES)

    def first_argmax(vals, vmax):
        return jnp.min(jnp.where(vals == vmax, rowf, big), axis=0, keepdims=True)

    is_g = row < N_EXPERT_GROUPS
    gl = jnp.where(is_g, logits, MASK_VALUE)
    gmax = jnp.max(gl, axis=0, keepdims=True)
    gidx = first_argmax(gl, gmax)
    g_p = 1.0 / jnp.sum(jnp.where(is_g, jnp.exp(gl - gmax), 0.0), axis=0, keepdims=True)

    base = N_EXPERT_GROUPS + EXPERTS_PER_GROUP * gidx
    in_grp = (rowf >= base) & (rowf < base + EXPERTS_PER_GROUP)
    el = jnp.where(in_grp, logits, MASK_VALUE)
    l1 = jnp.max(el, axis=0, keepdims=True)
    e1 = first_argmax(el, l1)
    el2 = jnp.where(rowf == e1, MASK_VALUE, el)
    l2 = jnp.max(el2, axis=0, keepdims=True)
    e2 = first_argmax(el2, l2)
    zsum = jnp.sum(jnp.where(in_grp, jnp.exp(el - l1), 0.0), axis=0, keepdims=True)
    p1 = 1.0 / zsum
    p2 = jnp.exp(l2 - l1) / zsum
    den = p1 + p2
    w1 = g_p * p1 / den
    w2 = g_p * p2 / den
    id1 = e1 - N_EXPERT_GROUPS
    id2 = e2 - N_EXPERT_GROUPS

    oh1 = rowf == id1
    oh2 = rowf == id2
    oh = jnp.where(oh1 | oh2, 1.0, 0.0)
    before = jnp.dot(oh.astype(BF16), tri_ref[...], preferred_element_type=F32) + carry[:, 0:1]
    rank1 = jnp.sum(jnp.where(oh1, before, 0.0), axis=0, keepdims=True)
    rank2 = jnp.sum(jnp.where(oh2, before, 0.0), axis=0, keepdims=True)
    carry[...] = carry[...] + jnp.sum(oh, axis=1, keepdims=True)
    cnt_ref[...] = carry[...]

    pad_i = jnp.zeros((SUBLANES - 2 * TOP_K, tm), F32)
    ri_ref[0] = jnp.concatenate([id1, id2, rank1, rank2, pad_i], axis=0).astype(jnp.int32)
    pad_f = jnp.zeros((LANES - TOP_K, tm), F32)
    rf_ref[...] = jnp.concatenate([w1, w2, pad_f], axis=0).T[:, 0:SUBLANES]


def _outproj(attn, conv, x, wo, gffn, wrt, brc):
    n = x.shape[0]
    tm = OUT_TILE
    tri = jnp.triu(jnp.ones((tm, tm), F32), k=1).astype(BF16)

    def full(a):
        return pl.BlockSpec(a.shape, lambda i: (0,) * a.ndim)

    rows = lambda width: pl.BlockSpec((tm, width), lambda i: (i, 0))
    return pl.pallas_call(
        functools.partial(_outproj_kernel, tm),
        grid=(n // tm,),
        in_specs=[rows(ATTN_WIDTH), rows(CONV_WIDTH), rows(D_MODEL)] + [
            full(a) for a in (wo, gffn, wrt, brc, tri)],
        out_specs=[rows(D_MODEL), pl.BlockSpec((tm * ROW_SUBLANES, LANES), lambda i: (i, 0)),
                   pl.BlockSpec((1, SUBLANES, tm), lambda i: (i, 0, 0)), rows(SUBLANES),
                   pl.BlockSpec((ROUTE_ROWS, LANES), lambda i: (0, 0))],
        out_shape=[jax.ShapeDtypeStruct((n, D_MODEL), F32),
                   jax.ShapeDtypeStruct((n * ROW_SUBLANES, LANES), F32),
                   jax.ShapeDtypeStruct((n // tm, SUBLANES, tm), jnp.int32),
                   jax.ShapeDtypeStruct((n, SUBLANES), F32),
                   jax.ShapeDtypeStruct((ROUTE_ROWS, LANES), F32)],
        scratch_shapes=[pltpu.VMEM((ROUTE_ROWS, LANES), F32)],
        compiler_params=pltpu.CompilerParams(
            dimension_semantics=("arbitrary",), vmem_limit_bytes=VMEM_LIMIT),
        name="outproj_router",
    )(attn, conv, x, wo, gffn, wrt, brc, tri)


def _dest_kernel(rs_ref, ri_ref, dest_ref):
    ri = ri_ref[...]
    experts = ri[:, 0:TOP_K, :]
    start = jnp.zeros_like(experts)
    for e in range(N_EXPERTS):
        start = jnp.where(experts == e, rs_ref[e], start)
    dest_ref[...] = jnp.concatenate([start + ri[:, TOP_K:2 * TOP_K, :],
                                     jnp.zeros((ri.shape[0], SUBLANES - TOP_K, ri.shape[2]), jnp.int32)], axis=1)


def _dest_rows(ri_t, row_start):
    grid_spec = pltpu.PrefetchScalarGridSpec(
        num_scalar_prefetch=1,
        grid=(1,),
        in_specs=[pl.BlockSpec(ri_t.shape, lambda i, rs: (0, 0, 0))],
        out_specs=pl.BlockSpec(ri_t.shape, lambda i, rs: (0, 0, 0)),
    )
    return pl.pallas_call(
        _dest_kernel,
        grid_spec=grid_spec,
        out_shape=jax.ShapeDtypeStruct(ri_t.shape, jnp.int32),
        compiler_params=pltpu.CompilerParams(dimension_semantics=("arbitrary",)),
        name="moe_dest",
    )(row_start, ri_t)


def _tile_copy(src, src_row, dst, dst_row, sem):
    rs = ROW_SUBLANES
    return pltpu.make_async_copy(src.at[pl.ds(pl.multiple_of(src_row * rs, rs), rs), :],
                                 dst.at[pl.ds(pl.multiple_of(dst_row * rs, rs), rs), :], sem)


def _dispatch_kernel(tm, bm, fz_ref, fv_ref, dest_ref, xt_ref, xs_hbm, zeros, sem, zsem):
    i = pl.program_id(0)
    block_rows = bm * ROW_SUBLANES

    def zero_block(z):
        start = pl.multiple_of(fz_ref[z] * block_rows, block_rows)
        return pltpu.make_async_copy(zeros, xs_hbm.at[pl.ds(start, block_rows), :], zsem.at[0])

    @pl.when(i == 0)
    def _():
        zeros[...] = jnp.zeros(zeros.shape, F32)
        for z in range(fz_ref.shape[0]):
            @pl.when(fv_ref[z] == 1)
            def _():
                zero_block(z).start()
        for z in range(fz_ref.shape[0]):
            @pl.when(fv_ref[z] == 1)
            def _():
                zero_block(z).wait()

    for g in range(0, tm, DMA_GROUP):
        dests = [(r, kk, dest_ref[0, kk, r]) for r in range(g, g + DMA_GROUP) for kk in range(TOP_K)]
        for r, kk, d in dests:
            _tile_copy(xt_ref, r, xs_hbm, d, sem.at[0]).start(priority=kk)
    for kk in range(TOP_K):
        pltpu.make_async_copy(xt_ref, xs_hbm.at[pl.ds(0, tm * ROW_SUBLANES), :], sem.at[0]).wait()


def _route_spec(dest, tm, extra_args):
    per = dest.shape[2] // tm
    nt = dest.shape[0] * per
    if extra_args == 0:
        index = lambda i: (jnp.minimum(i, nt - 1) // per, 0, jnp.minimum(i, nt - 1) % per)
    else:
        index = lambda i, *_: (jnp.minimum(i, nt - 1) // per, 0, jnp.minimum(i, nt - 1) % per)
    return nt, pl.BlockSpec((1, SUBLANES, tm), index, memory_space=pltpu.SMEM)


def _dispatch(xt_rows, dest, fill_blocks, fill_valid, nb):
    tm = DISPATCH_TILE
    bm = EXPERT_ROWS
    nt, dest_spec = _route_spec(dest, tm, 2)
    grid_spec = pltpu.PrefetchScalarGridSpec(
        num_scalar_prefetch=2,
        grid=(nt,),
        in_specs=[dest_spec, pl.BlockSpec((tm * ROW_SUBLANES, LANES), lambda i, fz, fv: (i, 0))],
        out_specs=pl.BlockSpec(memory_space=pl.ANY),
        scratch_shapes=[pltpu.VMEM((bm * ROW_SUBLANES, LANES), F32), pltpu.SemaphoreType.DMA((1,)),
                        pltpu.SemaphoreType.DMA((1,))],
    )
    return pl.pallas_call(
        functools.partial(_dispatch_kernel, tm, bm),
        grid_spec=grid_spec,
        out_shape=jax.ShapeDtypeStruct((nb * bm * ROW_SUBLANES, LANES), F32),
        compiler_params=pltpu.CompilerParams(
            dimension_semantics=("arbitrary",), vmem_limit_bytes=VMEM_LIMIT),
        name="moe_dispatch",
    )(fill_blocks, fill_valid, dest, xt_rows)


def _expert_kernel(bm, be_ref, nu_ref, first_ref, nxt_ref, slot_ref, xs_ref, wg_hbm, wu_hbm, wd_hbm, ys_ref,
                   wg_buf, wu_buf, wd_buf, wg_bf, wu_bf, wd_bf, wsem):
    step = pl.program_id(0)

    def weight_copies(e, slot):
        return [pltpu.make_async_copy(src.at[e], buf.at[slot], wsem.at[m, slot])
                for m, (src, buf) in enumerate(((wg_hbm, wg_buf), (wu_hbm, wu_buf), (wd_hbm, wd_buf)))]

    @pl.when(step == 0)
    def _():
        for c in weight_copies(be_ref[0], 0):
            c.start()

    def one_block(j, xs_blk, ys_blk):
        @pl.when(j < nu_ref[0])
        def _():
            slot = slot_ref[j]

            @pl.when(first_ref[j] == 1)
            def _():
                for c in weight_copies(0, slot):
                    c.wait()

                @pl.when(nxt_ref[j] >= 0)
                def _():
                    for c in weight_copies(nxt_ref[j], 1 - slot):
                        c.start()

                wg_bf[...] = wg_buf[slot].astype(BF16)
                wu_bf[...] = wu_buf[slot].astype(BF16)
                wd_bf[...] = wd_buf[slot].astype(BF16)

            x = _load_token_rows(xs_blk, bm).astype(BF16)
            hg = jnp.dot(x, wg_bf[...], preferred_element_type=F32)
            hu = jnp.dot(x, wu_bf[...], preferred_element_type=F32)
            hdn = hg * (1.0 / (1.0 + jnp.exp(-hg))) * hu
            y = jnp.dot(hdn.astype(BF16), wd_bf[...], preferred_element_type=F32)
            _store_token_rows(ys_blk, y, bm)

        @pl.when(j >= nu_ref[0])
        def _():
            ys_blk[...] = jnp.zeros(ys_blk.shape, F32)

    rows = bm * ROW_SUBLANES
    for sub in range(EXPERT_BLOCKS_PER_STEP):
        view = pl.ds(sub * rows, rows)
        one_block(step * EXPERT_BLOCKS_PER_STEP + sub, xs_ref.at[view, :], ys_ref.at[view, :])


def _experts(xs_rows, block_expert, n_used, w_gate, w_up, w_down):
    bm = EXPERT_ROWS
    nb = block_expert.shape[0]
    blk = jnp.arange(nb, dtype=jnp.int32)
    live = blk < n_used[0]
    first = live & ((blk == 0) | (block_expert != jnp.roll(block_expert, 1)))
    slot = (jnp.cumsum(first.astype(jnp.int32)) - 1) % 2
    later_first = first[None, :] & (blk[None, :] > blk[:, None])
    nxt_start = jnp.min(jnp.where(later_first, blk[None, :], nb), axis=1)
    nxt = jnp.sum(jnp.where(blk[None, :] == nxt_start[:, None], block_expert[None, :], 0), axis=1)
    nxt = jnp.where(nxt_start < nb, nxt, -1)
    per = EXPERT_BLOCKS_PER_STEP
    assert nb % per == 0
    step_rows = per * bm * ROW_SUBLANES
    grid_spec = pltpu.PrefetchScalarGridSpec(
        num_scalar_prefetch=5,
        grid=(nb // per,),
        in_specs=[
            pl.BlockSpec((step_rows, LANES), lambda s, be, nu, *_: (jnp.minimum(s, (nu[0] - 1) // per), 0)),
            pl.BlockSpec(memory_space=pl.ANY),
            pl.BlockSpec(memory_space=pl.ANY),
            pl.BlockSpec(memory_space=pl.ANY),
        ],
        out_specs=pl.BlockSpec((step_rows, LANES), lambda s, *_: (s, 0)),
        scratch_shapes=[pltpu.VMEM((2, D_MODEL, D_EXPERT), F32), pltpu.VMEM((2, D_MODEL, D_EXPERT), F32),
                        pltpu.VMEM((2, D_EXPERT, D_MODEL), F32),
                        pltpu.VMEM((D_MODEL, D_EXPERT), BF16), pltpu.VMEM((D_MODEL, D_EXPERT), BF16),
                        pltpu.VMEM((D_EXPERT, D_MODEL), BF16), pltpu.SemaphoreType.DMA((3, 2))],
    )
    return pl.pallas_call(
        functools.partial(_expert_kernel, bm),
        grid_spec=grid_spec,
        out_shape=jax.ShapeDtypeStruct((nb * bm * ROW_SUBLANES, LANES), F32),
        compiler_params=pltpu.CompilerParams(
            dimension_semantics=("arbitrary",), vmem_limit_bytes=VMEM_LIMIT),
        name="moe_experts",
    )(block_expert, n_used, first.astype(jnp.int32), nxt.astype(jnp.int32), slot.astype(jnp.int32),
      xs_rows, w_gate, w_up, w_down)


def _combine_kernel(tm, dest_ref, ys_hbm, h_ref, rf_ref, o_ref, ybuf, sem):
    i = pl.program_id(0)
    nt = pl.num_programs(0) - 1

    @pl.when(i < nt)
    def _():
        slot = i % 2
        for g in range(0, tm, DMA_GROUP):
            dests = [(r, kk, dest_ref[0, kk, r]) for r in range(g, g + DMA_GROUP) for kk in range(TOP_K)]
            for r, kk, d in dests:
                _tile_copy(ys_hbm, d, ybuf.at[kk, slot], r, sem.at[kk, slot]).start(priority=kk)

    @pl.when(i >= 1)
    def _():
        slot = (i - 1) % 2
        for kk in range(TOP_K):
            pltpu.make_async_copy(ys_hbm.at[pl.ds(0, tm * ROW_SUBLANES), :], ybuf.at[kk, slot],
                                  sem.at[kk, slot]).wait()
        rf = rf_ref[...]
        o_ref[...] = (h_ref[...] + rf[:, 0:1] * _load_token_rows(ybuf.at[0, slot], tm)
                      + rf[:, 1:2] * _load_token_rows(ybuf.at[1, slot], tm))


def _combine(ys_rows, dest, h, rf):
    n = h.shape[0]
    tm = COMBINE_TILE
    nt, dest_spec = _route_spec(dest, tm, 0)
    prev = lambda i: jnp.maximum(i - 1, 0)
    return pl.pallas_call(
        functools.partial(_combine_kernel, tm),
        grid=(nt + 1,),
        in_specs=[
            dest_spec,
            pl.BlockSpec(memory_space=pl.ANY),
            pl.BlockSpec((tm, D_MODEL), lambda i: (prev(i), 0)),
            pl.BlockSpec((tm, SUBLANES), lambda i: (prev(i), 0)),
        ],
        out_specs=pl.BlockSpec((tm, D_MODEL), lambda i: (prev(i), 0)),
        out_shape=jax.ShapeDtypeStruct((n, D_MODEL), F32),
        scratch_shapes=[pltpu.VMEM((TOP_K, 2, tm * ROW_SUBLANES, LANES), F32),
                        pltpu.SemaphoreType.DMA((TOP_K, 2))],
        compiler_params=pltpu.CompilerParams(
            dimension_semantics=("arbitrary",), vmem_limit_bytes=VMEM_LIMIT),
        name="moe_combine",
    )(dest, ys_rows, h, rf)


def _pad_lanes(a, width=LANES):
    return jnp.pad(a, ((0, 0), (0, width - a.shape[-1])))


def _layer(x, meta_tokens, g_mix, w_in, b_forget, q_norm_g, k_norm_g, conv_w, attn_out_g, conv_out_g,
           w_out, g_ffn, w_rg, b_rg, w_re, b_re, w_gate, w_up, w_down):
    bsz, slen, _ = x.shape
    n = bsz * slen
    w = ATTN_WIDTH

    wq, wk, wv, wf, wb, wc, whc = jnp.split(
        w_in, [w, 2 * w, 3 * w, 3 * w + ATTN_HEADS, 4 * w + ATTN_HEADS, 5 * w + ATTN_HEADS], axis=1)
    w1 = jnp.concatenate([wq, wk, wb, wc, whc, _pad_lanes(wf)], axis=1).astype(BF16)
    head_of = jnp.arange(MXU_DIM) // HEAD_DIM
    gmat = (head_of[:, None] == head_of[None, :]).astype(BF16)
    consts = (
        g_mix.reshape(1, D_MODEL), w1, wv.T.astype(BF16), _pad_lanes(b_forget.reshape(1, ATTN_HEADS)),
        jnp.tile(q_norm_g, ATTN_HEADS).reshape(1, w) * (HEAD_DIM ** -0.5 * LOG2E),
        jnp.tile(k_norm_g, ATTN_HEADS).reshape(1, w),
        jnp.pad(conv_w, ((0, SUBLANES - CONV_K), (0, 0))), conv_out_g.reshape(1, w), gmat)

    xm = jnp.pad(meta_tokens, ((0, META_TILE - N_META), (0, 0)))[None]
    km, vmt, kaugm, cum_m, uc_m = _inproj(xm, True, META_TILE, consts,
                                          jnp.zeros((SUBLANES, w), F32), jnp.zeros((1, LANES), F32))
    km, vmt, kaugm = km[0, :N_META], vmt[0, 0, :, :N_META], kaugm[0, :N_META]
    cum_m = cum_m[0, :N_META]
    halo = jnp.zeros((SUBLANES, w), F32).at[SUBLANES - 2:].set(uc_m[0, N_META - 2:N_META])
    cum0 = cum_m[N_META - 1:N_META]

    q, k, vt, conv, qaug, kaug = _inproj(x, False, IN_TILE, consts, halo, cum0)
    attn = _attention(q, qaug, k, kaug, vt, km, kaugm, vmt, attn_out_g.reshape(w, 1))

    wr = _pad_lanes(jnp.concatenate([w_rg, w_re], axis=1)).T
    wr_hi = wr.astype(BF16)
    wr_lo = (wr - wr_hi.astype(F32)).astype(BF16)
    wrt = jnp.concatenate([jnp.concatenate([wr_hi, wr_hi], axis=1),
                           jnp.concatenate([wr_lo, jnp.zeros_like(wr_lo)], axis=1)], axis=0)
    brc = jnp.broadcast_to(_pad_lanes(jnp.concatenate([b_rg, b_re]).reshape(1, -1)).T, (LANES, LANES))
    h, xt, ri, rf, cnt = _outproj(attn.reshape(n, w), conv.reshape(n, w), x.reshape(n, D_MODEL),
                                  w_out.astype(BF16), g_ffn.reshape(1, D_MODEL), wrt, brc)

    bm = EXPERT_ROWS
    nb = (n * TOP_K) // bm + N_EXPERTS
    counts = cnt[:N_EXPERTS, 0].astype(jnp.int32)
    blocks_e = (counts + bm - 1) // bm
    blk_end = jnp.cumsum(blocks_e)
    row_start = (blk_end - blocks_e) * bm
    blk = jnp.arange(nb, dtype=jnp.int32)
    block_expert = jnp.minimum(jnp.sum(blk_end[None, :] <= blk[:, None], axis=1), N_EXPERTS - 1).astype(jnp.int32)
    n_used = blk_end[-1:].astype(jnp.int32)
    tail = n_used[0] + jnp.arange(N_EXPERTS, dtype=jnp.int32)
    fill_blocks = jnp.clip(jnp.concatenate([blk_end - 1, tail]), 0, nb - 1).astype(jnp.int32)
    fill_valid = jnp.concatenate([counts % bm != 0, tail < nb]).astype(jnp.int32)

    dest = _dest_rows(ri, row_start.astype(jnp.int32))
    xs = _dispatch(xt, dest, fill_blocks, fill_valid, nb)
    ys = _experts(xs, block_expert, n_used, w_gate, w_up, w_down)
    out = _combine(ys, dest, h, rf)
    return out.reshape(bsz, slen, D_MODEL)


def kernel(x, meta_tokens, norm_mix_g, w_in, b_forget, q_norm_g, k_norm_g, conv_w, attn_out_g, conv_out_g,
           w_out, norm_ffn_g, w_router_group, b_router_group, w_router_expert, b_router_expert,
           w_gate, w_up, w_down):
    assert norm_mix_g.shape[0] == 1, "single-layer block"
    return _layer(x, meta_tokens, norm_mix_g[0], w_in[0], b_forget[0], q_norm_g[0], k_norm_g[0], conv_w[0],
                  attn_out_g[0], conv_out_g[0], w_out[0], norm_ffn_g[0], w_router_group[0],
                  b_router_group[0], w_router_expert[0], b_router_expert[0], w_gate[0], w_up[0], w_down[0])
```

```python
import functools

import jax
import jax.numpy as jnp
import numpy as np
from jax import lax
from jax.experimental import pallas as pl
from jax.experimental.pallas import tpu as pltpu

D_MODEL = 1024
N_META = 16
HEAD_DIM = 64
ATTN_HEADS = 8
ATTN_WIDTH = ATTN_HEADS * HEAD_DIM
CONV_WIDTH = D_MODEL - ATTN_WIDTH
CONV_K = 3
N_EXPERT_GROUPS = 4
EXPERTS_PER_GROUP = 8
N_EXPERTS = N_EXPERT_GROUPS * EXPERTS_PER_GROUP
TOP_K = 2
D_EXPERT = 512
EPS = 1e-6
MASK_VALUE = -1e30
LOG2E = 1.4426950408889634
AUG = 6

LANES = 128
SUBLANES = 8
MXU_DIM = 256
HEAD_PAIRS = ATTN_HEADS * HEAD_DIM // LANES
PROJ_PAD_COLS = 5 * ATTN_WIDTH + LANES
VMEM_LIMIT = 56 * 1024 * 1024

IN_TILE = 512
META_TILE = 128
ATTN_Q_TILE = 2048
ATTN_K_TILE = 512
OUT_TILE = 1024
EXPERT_ROWS = 256
EXPERT_BLOCKS_PER_STEP = 4
DISPATCH_TILE = 1024
COMBINE_TILE = 256
DMA_GROUP = 8

F32 = jnp.float32
BF16 = jnp.bfloat16
ROW_SUBLANES = D_MODEL // LANES


def _lane_iota(shape):
    return lax.broadcasted_iota(jnp.int32, shape, len(shape) - 1)


def _store_token_rows(ref, x, rows):
    for sl in range(ROW_SUBLANES):
        ref[pl.ds(sl, rows, stride=ROW_SUBLANES), :] = x[:, sl * LANES:(sl + 1) * LANES]


def _load_token_rows(ref, rows):
    return jnp.concatenate([ref[pl.ds(sl, rows, stride=ROW_SUBLANES), :] for sl in range(ROW_SUBLANES)], axis=1)


def _inproj_kernel(is_meta, tm, x_ref, gmix_ref, w1_ref, wvt_ref, bf_ref, gq_ref, gk_ref, cw_ref, gco_ref,
                   gmat_ref, tri_ref, eq_ref, ek_ref, halo_ref, cum0_ref, *rest):
    if is_meta:
        k_ref, vt_ref, kaug_ref, cum_ref, uc_ref, ucbuf, carry = rest
    else:
        q_ref, k_ref, vt_ref, conv_ref, qaug_ref, kaug_ref, ucbuf, carry = rest
    t = pl.program_id(1)

    @pl.when(t == 0)
    def _():
        ucbuf[0:SUBLANES, :] = halo_ref[...]
        carry[0:1, :] = cum0_ref[...]

    x = x_ref[0]
    ms = jnp.mean(x * x, axis=-1, keepdims=True)
    u = (x * lax.rsqrt(ms + EPS)) * gmix_ref[...]
    ub = u.astype(BF16)
    proj = jnp.dot(ub, w1_ref[...], preferred_element_type=F32)
    vt_ref[0, 0] = lax.dot_general(wvt_ref[...], ub, (((1,), (1,)), ((), ())),
                                   preferred_element_type=F32).astype(BF16)

    def head_norm(z, g):
        z2 = (z * z).astype(BF16)
        half = gmat_ref.shape[0]
        ssq = jnp.concatenate([jnp.dot(z2[:, c:c + half], gmat_ref[...], preferred_element_type=F32)
                               for c in range(0, z.shape[1], half)], axis=1)
        return z * lax.rsqrt(ssq * (1.0 / HEAD_DIM) + EPS) * g

    w = ATTN_WIDTH
    kn = head_norm(proj[:, w:2 * w], gk_ref[...])
    k_ref[0] = kn.astype(BF16)

    z = proj[:, 5 * w:5 * w + LANES] + bf_ref[...]
    ls = jnp.minimum(z, 0.0) - jnp.log1p(jnp.exp(-jnp.abs(z)))
    lane_c = _lane_iota(ls.shape)
    is_head = lane_c < ATTN_HEADS
    ls = jnp.where(is_head, ls, 0.0)

    def pieces(val):
        p_hi = val.astype(BF16).astype(F32)
        rem = val - p_hi
        p_mid = rem.astype(BF16).astype(F32)
        p_lo = (rem - p_mid).astype(BF16).astype(F32)
        return jnp.where(is_head, p_hi,
                         jnp.where(lane_c < 2 * ATTN_HEADS, pltpu.roll(p_mid, ATTN_HEADS, axis=1),
                                   pltpu.roll(p_lo, 2 * ATTN_HEADS, axis=1))).astype(BF16)

    cs3 = jnp.dot(tri_ref[...], pieces(ls), preferred_element_type=F32)
    cs = cs3 + pltpu.roll(cs3, LANES - ATTN_HEADS, axis=1) + pltpu.roll(cs3, LANES - 2 * ATTN_HEADS, axis=1)
    cum = jnp.where(is_head, cs, 0.0) + carry[0:1, :]
    carry[0:1, :] = cum[tm - 1:tm, :]

    packed = pieces(cum * LOG2E)
    in_aug = lane_c < AUG * ATTN_HEADS
    ones_k = in_aug & ((lane_c % AUG) < AUG // 2)
    kaug = jnp.where(ones_k, 1.0, -jnp.dot(packed, ek_ref[...], preferred_element_type=F32))
    kaug_ref[0] = kaug.astype(BF16)

    uc = proj[:, 3 * w:4 * w] * proj[:, 4 * w:5 * w]
    ucbuf[SUBLANES:SUBLANES + tm, :] = uc
    uc1 = ucbuf[SUBLANES - 1:SUBLANES - 1 + tm, :]
    uc2 = ucbuf[SUBLANES - 2:SUBLANES - 2 + tm, :]
    ucbuf[0:SUBLANES, :] = uc[tm - SUBLANES:tm, :]

    if is_meta:
        cum_ref[0] = cum
        uc_ref[0] = uc
        return

    qn = head_norm(proj[:, 0:w], gq_ref[...])
    q_ref[0] = qn.astype(BF16)
    y = cw_ref[0:1, :] * uc2 + cw_ref[1:2, :] * uc1 + cw_ref[2:3, :] * uc
    conv = proj[:, 2 * w:3 * w] * y
    conv_ref[0] = head_norm(conv, gco_ref[...]).astype(BF16)
    ones_q = in_aug & ((lane_c % AUG) >= AUG // 2)
    qaug = jnp.where(ones_q, 1.0, jnp.dot(packed, eq_ref[...], preferred_element_type=F32))
    qaug_ref[0] = qaug.astype(BF16)


def _inproj(x, is_meta, tm, consts, halo, cum0):
    bsz, tlen, _ = x.shape
    nt = tlen // tm
    gmix, w1, wvt, bfp, gq, gk, cw, gco, gmat = consts
    tri = jnp.asarray(np.tril(np.ones((tm, tm), np.float32)), BF16)
    w = ATTN_WIDTH
    src = np.arange(LANES)
    dst = np.arange(LANES)
    piece, head = src // ATTN_HEADS, src % ATTN_HEADS
    valid = src < 3 * ATTN_HEADS
    eq = jnp.asarray(valid[:, None] & (dst[None, :] == (AUG * head + piece)[:, None]), BF16)
    ek = jnp.asarray(valid[:, None] & (dst[None, :] == (AUG * head + AUG // 2 + piece)[:, None]), BF16)

    def full(a):
        return pl.BlockSpec(a.shape, lambda b, t: (0,) * a.ndim)

    in_specs = [pl.BlockSpec((1, tm, D_MODEL), lambda b, t: (b, t, 0))] + [
        full(a) for a in (gmix, w1, wvt, bfp, gq, gk, cw, gco, gmat, tri, eq, ek, halo, cum0)]
    tok = lambda width: pl.BlockSpec((1, tm, width), lambda b, t: (b, t, 0))
    vt_spec = pl.BlockSpec((1, 1, w, tm), lambda b, t: (b, t, 0, 0))
    if is_meta:
        out_shape = [jax.ShapeDtypeStruct((bsz, tlen, w), BF16),
                     jax.ShapeDtypeStruct((bsz, nt, w, tm), BF16),
                     jax.ShapeDtypeStruct((bsz, tlen, LANES), BF16),
                     jax.ShapeDtypeStruct((bsz, tlen, LANES), F32),
                     jax.ShapeDtypeStruct((bsz, tlen, w), F32)]
        out_specs = [tok(w), vt_spec, tok(LANES), tok(LANES), tok(w)]
    else:
        out_shape = [jax.ShapeDtypeStruct((bsz, tlen, w), BF16)] * 2 + [
            jax.ShapeDtypeStruct((bsz, nt, w, tm), BF16),
            jax.ShapeDtypeStruct((bsz, tlen, w), BF16),
            jax.ShapeDtypeStruct((bsz, tlen, LANES), BF16),
            jax.ShapeDtypeStruct((bsz, tlen, LANES), BF16)]
        out_specs = [tok(w), tok(w), vt_spec, tok(w), tok(LANES), tok(LANES)]
    return pl.pallas_call(
        functools.partial(_inproj_kernel, is_meta, tm),
        grid=(bsz, nt),
        in_specs=in_specs,
        out_specs=out_specs,
        out_shape=out_shape,
        scratch_shapes=[pltpu.VMEM((tm + SUBLANES, w), F32), pltpu.VMEM((SUBLANES, LANES), F32)],
        compiler_params=pltpu.CompilerParams(
            dimension_semantics=("arbitrary", "arbitrary"), vmem_limit_bytes=VMEM_LIMIT),
        name="inproj_meta" if is_meta else "inproj",
    )(x, gmix, w1, wvt, bfp, gq, gk, cw, gco, gmat, tri, eq, ek, halo, cum0)


def _attn_kernel(tq, tk, q_ref, qaug_ref, k_ref, kaug_ref, vt_ref, km_ref, kaugm_ref, vmt_ref, gao_ref,
                 o_ref, m_sc, l_sc, acc_sc, sa_sc, sb_sc, ma_sc, mb_sc):
    hp = pl.program_id(1)
    qi = pl.program_id(2)
    q = q_ref[0]
    qaug = qaug_ref[0]
    lane = _lane_iota(q.shape)
    first = lane < HEAD_DIM
    zero = jnp.zeros_like(q)
    qcat = []
    for hh in range(2):
        qh = jnp.where(first, q, zero) if hh == 0 else jnp.where(first, zero, q)
        lo = AUG * (2 * hp + hh)
        qa = jnp.where((lane >= lo) & (lane < lo + AUG), qaug, zero)
        qcat.append(jnp.concatenate([qh, qa], axis=1))

    m_sc[...] = jnp.full(m_sc.shape, MASK_VALUE, F32)
    l_sc[...] = jnp.zeros(l_sc.shape, F32)
    acc_sc[...] = jnp.zeros(acc_sc.shape, F32)

    def scores_t(kcat, q_lo=0):
        return tuple(lax.dot_general(kcat, qcat[hh][q_lo:], (((1,), (1,)), ((), ())), preferred_element_type=F32)
                     for hh in range(2))

    def update(hh, st, vtb, q_lo=0, block_max=None):
        m_old = m_sc[hh, :, q_lo:]
        if block_max is None:
            block_max = jnp.max(st, axis=0, keepdims=True)
        m_new = jnp.maximum(m_old, block_max)
        alpha = jnp.exp2(m_old - m_new)
        p = jnp.exp2(st - m_new)
        l_sc[hh, :, q_lo:] = alpha * l_sc[hh, :, q_lo:] + jnp.sum(p, axis=0, keepdims=True)
        vh = vtb[hh * HEAD_DIM:(hh + 1) * HEAD_DIM, :]
        acc_sc[hh, :, q_lo:] = alpha * acc_sc[hh, :, q_lo:] + jnp.dot(vh, p.astype(BF16),
                                                                     preferred_element_type=F32)
        m_sc[hh, :, q_lo:] = m_new

    def key_block(j):
        start = pl.multiple_of(j * tk, tk)
        return jnp.concatenate([k_ref[0, pl.ds(start, tk), :], kaug_ref[0, pl.ds(start, tk), :]], axis=1)

    def put_scores(j, bufs, q_lo=0):
        buf, mbuf = bufs
        sts = scores_t(key_block(j), q_lo)
        for hh in range(2):
            buf[hh, :, q_lo:] = sts[hh]
            mbuf[hh, :, q_lo:] = jnp.max(sts[hh], axis=0, keepdims=True)

    def consume(bufs, j, diagonal=False, q_lo=0):
        buf, mbuf = bufs
        vtb = vt_ref[0, j]
        for hh in range(2):
            st = buf[hh, :, q_lo:]
            if diagonal:
                key = lax.broadcasted_iota(jnp.int32, st.shape, 0)
                qry = lax.broadcasted_iota(jnp.int32, st.shape, 1)
                update(hh, jnp.where(key <= qry, st, MASK_VALUE), vtb, q_lo)
            else:
                update(hh, st, vtb, q_lo, mbuf[hh, :, q_lo:])

    per = tq // tk
    assert per % 2 == 0
    st_m = scores_t(jnp.concatenate([km_ref[...], kaugm_ref[...]], axis=1))
    buf_a, buf_b = (sa_sc, ma_sc), (sb_sc, mb_sc)
    put_scores(0, buf_a)
    vmt = vmt_ref[...]
    for hh in range(2):
        update(hh, st_m[hh], vmt)

    def body(t, carry):
        j = 2 * t
        put_scores(j + 1, buf_b)
        consume(buf_a, j)
        put_scores(j + 2, buf_a)
        consume(buf_b, j + 1)
        return carry

    lax.fori_loop(0, (per // 2) * qi, body, 0)
    bufs = (buf_a, buf_b)
    for d in range(per):
        if d + 1 < per:
            put_scores(per * qi + d + 1, bufs[(d + 1) % 2], (d + 1) * tk)
        consume(bufs[d % 2], per * qi + d, diagonal=True, q_lo=d * tk)

    ot = jnp.concatenate([acc_sc[0] / l_sc[0], acc_sc[1] / l_sc[1]], axis=0)
    o2 = ot * ot
    ms0 = jnp.sum(o2[0:HEAD_DIM], axis=0, keepdims=True) * (1.0 / HEAD_DIM)
    ms1 = jnp.sum(o2[HEAD_DIM:], axis=0, keepdims=True) * (1.0 / HEAD_DIM)
    inv = jnp.concatenate([jnp.broadcast_to(lax.rsqrt(ms0 + EPS), (HEAD_DIM, tq)),
                           jnp.broadcast_to(lax.rsqrt(ms1 + EPS), (HEAD_DIM, tq))], axis=0)
    o_ref[0] = (ot * inv * gao_ref[...]).T.astype(BF16)


def _attention(q, qaug, k, kaug, vt, km, kaugm, vmt, gao):
    bsz, slen, w = q.shape
    tq, tk = ATTN_Q_TILE, ATTN_K_TILE
    nq, nk = slen // tq, slen // tk
    return pl.pallas_call(
        functools.partial(_attn_kernel, tq, tk),
        grid=(bsz, HEAD_PAIRS, nq),
        in_specs=[
            pl.BlockSpec((1, tq, LANES), lambda b, p, i: (b, i, p)),
            pl.BlockSpec((1, tq, LANES), lambda b, p, i: (b, i, 0)),
            pl.BlockSpec((1, slen, LANES), lambda b, p, i: (b, 0, p)),
            pl.BlockSpec((1, slen, LANES), lambda b, p, i: (b, 0, 0)),
            pl.BlockSpec((1, nk, LANES, tk), lambda b, p, i: (b, 0, p, 0)),
            pl.BlockSpec((N_META, LANES), lambda b, p, i: (0, p)),
            pl.BlockSpec((N_META, LANES), lambda b, p, i: (0, 0)),
            pl.BlockSpec((LANES, N_META), lambda b, p, i: (p, 0)),
            pl.BlockSpec((LANES, 1), lambda b, p, i: (p, 0)),
        ],
        out_specs=pl.BlockSpec((1, tq, LANES), lambda b, p, i: (b, i, p)),
        out_shape=jax.ShapeDtypeStruct((bsz, slen, w), BF16),
        scratch_shapes=[pltpu.VMEM((2, 1, tq), F32), pltpu.VMEM((2, 1, tq), F32),
                        pltpu.VMEM((2, HEAD_DIM, tq), F32),
                        pltpu.VMEM((2, tk, tq), F32), pltpu.VMEM((2, tk, tq), F32),
                        pltpu.VMEM((2, 1, tq), F32), pltpu.VMEM((2, 1, tq), F32)],
        compiler_params=pltpu.CompilerParams(
            dimension_semantics=("arbitrary", "arbitrary", "arbitrary"), vmem_limit_bytes=VMEM_LIMIT),
        name="fox_attention",
    )(q, qaug, k, kaug, vt, km, kaugm, vmt, gao)


ROUTE_ROWS = 64


def _outproj_kernel(tm, attn_ref, conv_ref, x_ref, wo_ref, gffn_ref, wrt_ref, brc_ref, tri_ref,
                    h_ref, xt_ref, ri_ref, rf_ref, cnt_ref, carry):
    i = pl.program_id(0)

    @pl.when(i == 0)
    def _():
        carry[...] = jnp.zeros(carry.shape, F32)

    mixed = jnp.concatenate([attn_ref[...], conv_ref[...]], axis=1)
    h = x_ref[...] + jnp.dot(mixed, wo_ref[...], preferred_element_type=F32)
    h_ref[...] = h
    ms = jnp.mean(h * h, axis=-1, keepdims=True)
    xt = (h * lax.rsqrt(ms + EPS)) * gffn_ref[...]
    _store_token_rows(xt_ref, xt, tm)

    x_hi = xt.astype(BF16)
    x_lo = (xt - x_hi.astype(F32)).astype(BF16)
    parts = lax.dot_general(wrt_ref[...], jnp.concatenate([x_hi, x_lo], axis=1), (((1,), (1,)), ((), ())),
                            preferred_element_type=F32)
    logits = (parts[0:ROUTE_ROWS] + parts[LANES:LANES + ROUTE_ROWS]) + brc_ref[0:ROUTE_ROWS, 0:1]
    row = lax.broadcasted_iota(jnp.int32, logits.shape, 0)
    rowf = row.astype(F32)
    big = float(LANES)

    def first_argmax(vals, vmax):
        return jnp.min(jnp.where(vals == vmax, rowf, big), axis=0, keepdims=True)

    is_g = row < N_EXPERT_GROUPS
    gl = jnp.where(is_g, logits, MASK_VALUE)
    gmax = jnp.max(gl, axis=0, keepdims=True)
    gidx = first_argmax(gl, gmax)
    g_p = 1.0 / jnp.sum(jnp.where(is_g, jnp.exp(gl - gmax), 0.0), axis=0, keepdims=True)

    base = N_EXPERT_GROUPS + EXPERTS_PER_GROUP * gidx
    in_grp = (rowf >= base) & (rowf < base + EXPERTS_PER_GROUP)
    el = jnp.where(in_grp, logits, MASK_VALUE)
    l1 = jnp.max(el, axis=0, keepdims=True)
    e1 = first_argmax(el, l1)
    el2 = jnp.where(rowf == e1, MASK_VALUE, el)
    l2 = jnp.max(el2, axis=0, keepdims=True)
    e2 = first_argmax(el2, l2)
    zsum = jnp.sum(jnp.where(in_grp, jnp.exp(el - l1), 0.0), axis=0, keepdims=True)
    p1 = 1.0 / zsum
    p2 = jnp.exp(l2 - l1) / zsum
    den = p1 + p2
    w1 = g_p * p1 / den
    w2 = g_p * p2 / den
    id1 = e1 - N_EXPERT_GROUPS
    id2 = e2 - N_EXPERT_GROUPS

    oh1 = rowf == id1
    oh2 = rowf == id2
    oh = jnp.where(oh1 | oh2, 1.0, 0.0)
    before = jnp.dot(oh.astype(BF16), tri_ref[...], preferred_element_type=F32) + carry[:, 0:1]
    rank1 = jnp.sum(jnp.where(oh1, before, 0.0), axis=0, keepdims=True)
    rank2 = jnp.sum(jnp.where(oh2, before, 0.0), axis=0, keepdims=True)
    carry[...] = carry[...] + jnp.sum(oh, axis=1, keepdims=True)
    cnt_ref[...] = carry[...]

    pad_i = jnp.zeros((SUBLANES - 2 * TOP_K, tm), F32)
    ri_ref[0] = jnp.concatenate([id1, id2, rank1, rank2, pad_i], axis=0).astype(jnp.int32)
    pad_f = jnp.zeros((LANES - TOP_K, tm), F32)
    rf_ref[...] = jnp.concatenate([w1, w2, pad_f], axis=0).T[:, 0:SUBLANES]


def _outproj(attn, conv, x, wo, gffn, wrt, brc):
    n = x.shape[0]
    tm = OUT_TILE
    tri = jnp.asarray(np.triu(np.ones((tm, tm), np.float32), k=1), BF16)

    def full(a):
        return pl.BlockSpec(a.shape, lambda i: (0,) * a.ndim)

    rows = lambda width: pl.BlockSpec((tm, width), lambda i: (i, 0))
    return pl.pallas_call(
        functools.partial(_outproj_kernel, tm),
        grid=(n // tm,),
        in_specs=[rows(ATTN_WIDTH), rows(CONV_WIDTH), rows(D_MODEL)] + [
            full(a) for a in (wo, gffn, wrt, brc, tri)],
        out_specs=[rows(D_MODEL), pl.BlockSpec((tm * ROW_SUBLANES, LANES), lambda i: (i, 0)),
                   pl.BlockSpec((1, SUBLANES, tm), lambda i: (i, 0, 0)), rows(SUBLANES),
                   pl.BlockSpec((ROUTE_ROWS, LANES), lambda i: (0, 0))],
        out_shape=[jax.ShapeDtypeStruct((n, D_MODEL), F32),
                   jax.ShapeDtypeStruct((n * ROW_SUBLANES, LANES), F32),
                   jax.ShapeDtypeStruct((n // tm, SUBLANES, tm), jnp.int32),
                   jax.ShapeDtypeStruct((n, SUBLANES), F32),
                   jax.ShapeDtypeStruct((ROUTE_ROWS, LANES), F32)],
        scratch_shapes=[pltpu.VMEM((ROUTE_ROWS, LANES), F32)],
        compiler_params=pltpu.CompilerParams(
            dimension_semantics=("arbitrary",), vmem_limit_bytes=VMEM_LIMIT),
        name="outproj_router",
    )(attn, conv, x, wo, gffn, wrt, brc, tri)


def _dest_kernel(rs_ref, ri_ref, dest_ref):
    ri = ri_ref[...]
    experts = ri[:, 0:TOP_K, :]
    start = jnp.zeros_like(experts)
    for e in range(N_EXPERTS):
        start = jnp.where(experts == e, rs_ref[e], start)
    dest_ref[...] = jnp.concatenate([start + ri[:, TOP_K:2 * TOP_K, :],
                                     jnp.zeros((ri.shape[0], SUBLANES - TOP_K, ri.shape[2]), jnp.int32)], axis=1)


def _dest_rows(ri_t, row_start):
    grid_spec = pltpu.PrefetchScalarGridSpec(
        num_scalar_prefetch=1,
        grid=(1,),
        in_specs=[pl.BlockSpec(ri_t.shape, lambda i, rs: (0, 0, 0))],
        out_specs=pl.BlockSpec(ri_t.shape, lambda i, rs: (0, 0, 0)),
    )
    return pl.pallas_call(
        _dest_kernel,
        grid_spec=grid_spec,
        out_shape=jax.ShapeDtypeStruct(ri_t.shape, jnp.int32),
        compiler_params=pltpu.CompilerParams(dimension_semantics=("arbitrary",)),
        name="moe_dest",
    )(row_start, ri_t)


def _tile_copy(src, src_row, dst, dst_row, sem):
    rs = ROW_SUBLANES
    return pltpu.make_async_copy(src.at[pl.ds(pl.multiple_of(src_row * rs, rs), rs), :],
                                 dst.at[pl.ds(pl.multiple_of(dst_row * rs, rs), rs), :], sem)


def _dispatch_kernel(tm, bm, fz_ref, fv_ref, dest_ref, xt_ref, xs_hbm, zeros, sem, zsem):
    i = pl.program_id(0)
    block_rows = bm * ROW_SUBLANES

    def zero_block(z):
        start = pl.multiple_of(fz_ref[z] * block_rows, block_rows)
        return pltpu.make_async_copy(zeros, xs_hbm.at[pl.ds(start, block_rows), :], zsem.at[0])

    @pl.when(i == 0)
    def _():
        zeros[...] = jnp.zeros(zeros.shape, F32)
        for z in range(fz_ref.shape[0]):
            @pl.when(fv_ref[z] == 1)
            def _():
                zero_block(z).start()
        for z in range(fz_ref.shape[0]):
            @pl.when(fv_ref[z] == 1)
            def _():
                zero_block(z).wait()

    for g in range(0, tm, DMA_GROUP):
        dests = [(r, kk, dest_ref[0, kk, r]) for r in range(g, g + DMA_GROUP) for kk in range(TOP_K)]
        for r, kk, d in dests:
            _tile_copy(xt_ref, r, xs_hbm, d, sem.at[0]).start(priority=kk)
    for kk in range(TOP_K):
        pltpu.make_async_copy(xt_ref, xs_hbm.at[pl.ds(0, tm * ROW_SUBLANES), :], sem.at[0]).wait()


def _route_spec(dest, tm, extra_args):
    per = dest.shape[2] // tm
    nt = dest.shape[0] * per
    if extra_args == 0:
        index = lambda i: (jnp.minimum(i, nt - 1) // per, 0, jnp.minimum(i, nt - 1) % per)
    else:
        index = lambda i, *_: (jnp.minimum(i, nt - 1) // per, 0, jnp.minimum(i, nt - 1) % per)
    return nt, pl.BlockSpec((1, SUBLANES, tm), index, memory_space=pltpu.SMEM)


def _dispatch(xt_rows, dest, fill_blocks, fill_valid, nb):
    tm = DISPATCH_TILE
    bm = EXPERT_ROWS
    nt, dest_spec = _route_spec(dest, tm, 2)
    grid_spec = pltpu.PrefetchScalarGridSpec(
        num_scalar_prefetch=2,
        grid=(nt,),
        in_specs=[dest_spec, pl.BlockSpec((tm * ROW_SUBLANES, LANES), lambda i, fz, fv: (i, 0))],
        out_specs=pl.BlockSpec(memory_space=pl.ANY),
        scratch_shapes=[pltpu.VMEM((bm * ROW_SUBLANES, LANES), F32), pltpu.SemaphoreType.DMA((1,)),
                        pltpu.SemaphoreType.DMA((1,))],
    )
    return pl.pallas_call(
        functools.partial(_dispatch_kernel, tm, bm),
        grid_spec=grid_spec,
        out_shape=jax.ShapeDtypeStruct((nb * bm * ROW_SUBLANES, LANES), F32),
        compiler_params=pltpu.CompilerParams(
            dimension_semantics=("arbitrary",), vmem_limit_bytes=VMEM_LIMIT),
        name="moe_dispatch",
    )(fill_blocks, fill_valid, dest, xt_rows)


def _expert_kernel(bm, be_ref, nu_ref, first_ref, nxt_ref, slot_ref, xs_ref, wg_hbm, wu_hbm, wd_hbm, ys_ref,
                   wg_buf, wu_buf, wd_buf, wg_bf, wu_bf, wd_bf, wsem):
    step = pl.program_id(0)

    def weight_copies(e, slot):
        return [pltpu.make_async_copy(src.at[e], buf.at[slot], wsem.at[m, slot])
                for m, (src, buf) in enumerate(((wg_hbm, wg_buf), (wu_hbm, wu_buf), (wd_hbm, wd_buf)))]

    @pl.when(step == 0)
    def _():
        for c in weight_copies(be_ref[0], 0):
            c.start()

    def one_block(j, xs_blk, ys_blk):
        @pl.when(j < nu_ref[0])
        def _():
            slot = slot_ref[j]

            @pl.when(first_ref[j] == 1)
            def _():
                for c in weight_copies(0, slot):
                    c.wait()

                @pl.when(nxt_ref[j] >= 0)
                def _():
                    for c in weight_copies(nxt_ref[j], 1 - slot):
                        c.start()

                wg_bf[...] = wg_buf[slot].astype(BF16)
                wu_bf[...] = wu_buf[slot].astype(BF16)
                wd_bf[...] = wd_buf[slot].astype(BF16)

            x = _load_token_rows(xs_blk, bm).astype(BF16)
            hg = jnp.dot(x, wg_bf[...], preferred_element_type=F32)
            hu = jnp.dot(x, wu_bf[...], preferred_element_type=F32)
            hdn = hg * (1.0 / (1.0 + jnp.exp(-hg))) * hu
            y = jnp.dot(hdn.astype(BF16), wd_bf[...], preferred_element_type=F32)
            _store_token_rows(ys_blk, y, bm)

        @pl.when(j >= nu_ref[0])
        def _():
            ys_blk[...] = jnp.zeros(ys_blk.shape, F32)

    rows = bm * ROW_SUBLANES
    for sub in range(EXPERT_BLOCKS_PER_STEP):
        view = pl.ds(sub * rows, rows)
        one_block(step * EXPERT_BLOCKS_PER_STEP + sub, xs_ref.at[view, :], ys_ref.at[view, :])


def _experts(xs_rows, block_expert, n_used, w_gate, w_up, w_down):
    bm = EXPERT_ROWS
    nb = block_expert.shape[0]
    blk = jnp.arange(nb, dtype=jnp.int32)
    live = blk < n_used[0]
    first = live & ((blk == 0) | (block_expert != jnp.roll(block_expert, 1)))
    slot = (jnp.cumsum(first.astype(jnp.int32)) - 1) % 2
    later_first = first[None, :] & (blk[None, :] > blk[:, None])
    nxt_start = jnp.min(jnp.where(later_first, blk[None, :], nb), axis=1)
    nxt = jnp.sum(jnp.where(blk[None, :] == nxt_start[:, None], block_expert[None, :], 0), axis=1)
    nxt = jnp.where(nxt_start < nb, nxt, -1)
    per = EXPERT_BLOCKS_PER_STEP
    assert nb % per == 0
    step_rows = per * bm * ROW_SUBLANES
    grid_spec = pltpu.PrefetchScalarGridSpec(
        num_scalar_prefetch=5,
        grid=(nb // per,),
        in_specs=[
            pl.BlockSpec((step_rows, LANES), lambda s, be, nu, *_: (jnp.minimum(s, (nu[0] - 1) // per), 0)),
            pl.BlockSpec(memory_space=pl.ANY),
            pl.BlockSpec(memory_space=pl.ANY),
            pl.BlockSpec(memory_space=pl.ANY),
        ],
        out_specs=pl.BlockSpec((step_rows, LANES), lambda s, *_: (s, 0)),
        scratch_shapes=[pltpu.VMEM((2, D_MODEL, D_EXPERT), F32), pltpu.VMEM((2, D_MODEL, D_EXPERT), F32),
                        pltpu.VMEM((2, D_EXPERT, D_MODEL), F32),
                        pltpu.VMEM((D_MODEL, D_EXPERT), BF16), pltpu.VMEM((D_MODEL, D_EXPERT), BF16),
                        pltpu.VMEM((D_EXPERT, D_MODEL), BF16), pltpu.SemaphoreType.DMA((3, 2))],
    )
    return pl.pallas_call(
        functools.partial(_expert_kernel, bm),
        grid_spec=grid_spec,
        out_shape=jax.ShapeDtypeStruct((nb * bm * ROW_SUBLANES, LANES), F32),
        compiler_params=pltpu.CompilerParams(
            dimension_semantics=("arbitrary",), vmem_limit_bytes=VMEM_LIMIT),
        name="moe_experts",
    )(block_expert, n_used, first.astype(jnp.int32), nxt.astype(jnp.int32), slot.astype(jnp.int32),
      xs_rows, w_gate, w_up, w_down)


def _combine_kernel(tm, dest_ref, ys_hbm, h_ref, rf_ref, o_ref, ybuf, sem):
    i = pl.program_id(0)
    nt = pl.num_programs(0) - 1

    @pl.when(i < nt)
    def _():
        slot = i % 2
        for g in range(0, tm, DMA_GROUP):
            dests = [(r, kk, dest_ref[0, kk, r]) for r in range(g, g + DMA_GROUP) for kk in range(TOP_K)]
            for r, kk, d in dests:
                _tile_copy(ys_hbm, d, ybuf.at[kk, slot], r, sem.at[kk, slot]).start(priority=kk)

    @pl.when(i >= 1)
    def _():
        slot = (i - 1) % 2
        for kk in range(TOP_K):
            pltpu.make_async_copy(ys_hbm.at[pl.ds(0, tm * ROW_SUBLANES), :], ybuf.at[kk, slot],
                                  sem.at[kk, slot]).wait()
        rf = rf_ref[...]
        o_ref[...] = (h_ref[...] + rf[:, 0:1] * _load_token_rows(ybuf.at[0, slot], tm)
                      + rf[:, 1:2] * _load_token_rows(ybuf.at[1, slot], tm))


def _combine(ys_rows, dest, h, rf):
    n = h.shape[0]
    tm = COMBINE_TILE
    nt, dest_spec = _route_spec(dest, tm, 0)
    prev = lambda i: jnp.maximum(i - 1, 0)
    return pl.pallas_call(
        functools.partial(_combine_kernel, tm),
        grid=(nt + 1,),
        in_specs=[
            dest_spec,
            pl.BlockSpec(memory_space=pl.ANY),
            pl.BlockSpec((tm, D_MODEL), lambda i: (prev(i), 0)),
            pl.BlockSpec((tm, SUBLANES), lambda i: (prev(i), 0)),
        ],
        out_specs=pl.BlockSpec((tm, D_MODEL), lambda i: (prev(i), 0)),
        out_shape=jax.ShapeDtypeStruct((n, D_MODEL), F32),
        scratch_shapes=[pltpu.VMEM((TOP_K, 2, tm * ROW_SUBLANES, LANES), F32),
                        pltpu.SemaphoreType.DMA((TOP_K, 2))],
        compiler_params=pltpu.CompilerParams(
            dimension_semantics=("arbitrary",), vmem_limit_bytes=VMEM_LIMIT),
        name="moe_combine",
    )(dest, ys_rows, h, rf)


def _pad_lanes(a, width=LANES):
    return jnp.pad(a, ((0, 0), (0, width - a.shape[-1])))


def _layer(x, meta_tokens, g_mix, w_in, b_forget, q_norm_g, k_norm_g, conv_w, attn_out_g, conv_out_g,
           w_out, g_ffn, w_rg, b_rg, w_re, b_re, w_gate, w_up, w_down):
    bsz, slen, _ = x.shape
    n = bsz * slen
    w = ATTN_WIDTH

    wq, wk, wv, wf, wb, wc, whc = jnp.split(
        w_in, [w, 2 * w, 3 * w, 3 * w + ATTN_HEADS, 4 * w + ATTN_HEADS, 5 * w + ATTN_HEADS], axis=1)
    w1 = jnp.concatenate([wq, wk, wb, wc, whc, _pad_lanes(wf)], axis=1).astype(BF16)
    head_of = np.arange(MXU_DIM) // HEAD_DIM
    gmat = jnp.asarray(head_of[:, None] == head_of[None, :], BF16)
    consts = (
        g_mix.reshape(1, D_MODEL), w1, wv.T.astype(BF16), _pad_lanes(b_forget.reshape(1, ATTN_HEADS)),
        jnp.tile(q_norm_g, ATTN_HEADS).reshape(1, w) * (HEAD_DIM ** -0.5 * LOG2E),
        jnp.tile(k_norm_g, ATTN_HEADS).reshape(1, w),
        jnp.pad(conv_w, ((0, SUBLANES - CONV_K), (0, 0))), conv_out_g.reshape(1, w), gmat)

    xm = jnp.pad(meta_tokens, ((0, META_TILE - N_META), (0, 0)))[None]
    km, vmt, kaugm, cum_m, uc_m = _inproj(xm, True, META_TILE, consts,
                                          jnp.zeros((SUBLANES, w), F32), jnp.zeros((1, LANES), F32))
    km, vmt, kaugm = km[0, :N_META], vmt[0, 0, :, :N_META], kaugm[0, :N_META]
    cum_m = cum_m[0, :N_META]
    halo = jnp.zeros((SUBLANES, w), F32).at[SUBLANES - 2:].set(uc_m[0, N_META - 2:N_META])
    cum0 = cum_m[N_META - 1:N_META]

    q, k, vt, conv, qaug, kaug = _inproj(x, False, IN_TILE, consts, halo, cum0)
    attn = _attention(q, qaug, k, kaug, vt, km, kaugm, vmt, attn_out_g.reshape(w, 1))

    wr = _pad_lanes(jnp.concatenate([w_rg, w_re], axis=1)).T
    wr_hi = wr.astype(BF16)
    wr_lo = (wr - wr_hi.astype(F32)).astype(BF16)
    wrt = jnp.concatenate([jnp.concatenate([wr_hi, wr_hi], axis=1),
                           jnp.concatenate([wr_lo, jnp.zeros_like(wr_lo)], axis=1)], axis=0)
    brc = jnp.broadcast_to(_pad_lanes(jnp.concatenate([b_rg, b_re]).reshape(1, -1)).T, (LANES, LANES))
    h, xt, ri, rf, cnt = _outproj(attn.reshape(n, w), conv.reshape(n, w), x.reshape(n, D_MODEL),
                                  w_out.astype(BF16), g_ffn.reshape(1, D_MODEL), wrt, brc)

    bm = EXPERT_ROWS
    nb = (n * TOP_K) // bm + N_EXPERTS
    counts = cnt[:N_EXPERTS, 0].astype(jnp.int32)
    blocks_e = (counts + bm - 1) // bm
    blk_end = jnp.cumsum(blocks_e)
    row_start = (blk_end - blocks_e) * bm
    blk = jnp.arange(nb, dtype=jnp.int32)
    block_expert = jnp.minimum(jnp.sum(blk_end[None, :] <= blk[:, None], axis=1), N_EXPERTS - 1).astype(jnp.int32)
    n_used = blk_end[-1:].astype(jnp.int32)
    tail = n_used[0] + jnp.arange(N_EXPERTS, dtype=jnp.int32)
    fill_blocks = jnp.clip(jnp.concatenate([blk_end - 1, tail]), 0, nb - 1).astype(jnp.int32)
    fill_valid = jnp.concatenate([counts % bm != 0, tail < nb]).astype(jnp.int32)

    dest = _dest_rows(ri, row_start.astype(jnp.int32))
    xs = _dispatch(xt, dest, fill_blocks, fill_valid, nb)
    ys = _experts(xs, block_expert, n_used, w_gate, w_up, w_down)
    out = _combine(ys, dest, h, rf)
    return out.reshape(bsz, slen, D_MODEL)


def kernel(x, meta_tokens, norm_mix_g, w_in, b_forget, q_norm_g, k_norm_g, conv_w, attn_out_g, conv_out_g,
           w_out, norm_ffn_g, w_router_group, b_router_group, w_router_expert, b_router_expert,
           w_gate, w_up, w_down):
    assert norm_mix_g.shape[0] == 1, "single-layer block"
    return _layer(x, meta_tokens, norm_mix_g[0], w_in[0], b_forget[0], q_norm_g[0], k_norm_g[0], conv_w[0],
                  attn_out_g[0], conv_out_g[0], w_out[0], norm_ffn_g[0], w_router_group[0],
                  b_router_group[0], w_router_expert[0], b_router_expert[0], w_gate[0], w_up[0], w_down[0])
```

```python
import functools

import jax
import jax.numpy as jnp
import numpy as np
from jax import lax
from jax.experimental import pallas as pl
from jax.experimental.pallas import tpu as pltpu

D_MODEL = 1024
N_META = 16
HEAD_DIM = 64
ATTN_HEADS = 8
ATTN_WIDTH = ATTN_HEADS * HEAD_DIM
CONV_WIDTH = D_MODEL - ATTN_WIDTH
CONV_K = 3
N_EXPERT_GROUPS = 4
EXPERTS_PER_GROUP = 8
N_EXPERTS = N_EXPERT_GROUPS * EXPERTS_PER_GROUP
TOP_K = 2
D_EXPERT = 512
EPS = 1e-6
MASK_VALUE = -1e30
LOG2E = 1.4426950408889634
AUG = 6

LANES = 128
SUBLANES = 8
MXU_DIM = 256
HEAD_PAIRS = ATTN_HEADS * HEAD_DIM // LANES
PROJ_PAD_COLS = 5 * ATTN_WIDTH + LANES
VMEM_LIMIT = 56 * 1024 * 1024

IN_TILE = 512
META_TILE = 128
ATTN_Q_TILE = 2048
ATTN_K_TILE = 512
OUT_TILE = 1024
EXPERT_ROWS = 256
EXPERT_BLOCKS_PER_STEP = 4
DISPATCH_TILE = 1024
COMBINE_TILE = 256
DMA_GROUP = 8

F32 = jnp.float32
BF16 = jnp.bfloat16
ROW_SUBLANES = D_MODEL // LANES


def _lane_iota(shape):
    return lax.broadcasted_iota(jnp.int32, shape, len(shape) - 1)


def _store_token_rows(ref, x, rows):
    for sl in range(ROW_SUBLANES):
        ref[pl.ds(sl, rows, stride=ROW_SUBLANES), :] = x[:, sl * LANES:(sl + 1) * LANES]


def _load_token_rows(ref, rows):
    return jnp.concatenate([ref[pl.ds(sl, rows, stride=ROW_SUBLANES), :] for sl in range(ROW_SUBLANES)], axis=1)


def _inproj_kernel(is_meta, tm, x_ref, gmix_ref, w1_ref, wvt_ref, bf_ref, gq_ref, gk_ref, cw_ref, gco_ref,
                   gmat_ref, tri_ref, eq_ref, ek_ref, halo_ref, cum0_ref, *rest):
    if is_meta:
        k_ref, vt_ref, kaug_ref, cum_ref, uc_ref, ucbuf, carry = rest
    else:
        q_ref, k_ref, vt_ref, conv_ref, qaug_ref, kaug_ref, ucbuf, carry = rest
    t = pl.program_id(1)

    @pl.when(t == 0)
    def _():
        ucbuf[0:SUBLANES, :] = halo_ref[...]
        carry[0:1, :] = cum0_ref[...]

    x = x_ref[0]
    ms = jnp.mean(x * x, axis=-1, keepdims=True)
    u = (x * lax.rsqrt(ms + EPS)) * gmix_ref[...]
    ub = u.astype(BF16)
    proj = jnp.dot(ub, w1_ref[...], preferred_element_type=F32)
    vt_ref[0, 0] = lax.dot_general(wvt_ref[...], ub, (((1,), (1,)), ((), ())),
                                   preferred_element_type=F32).astype(BF16)

    def head_norm(z, g):
        z2 = (z * z).astype(BF16)
        half = gmat_ref.shape[0]
        ssq = jnp.concatenate([jnp.dot(z2[:, c:c + half], gmat_ref[...], preferred_element_type=F32)
                               for c in range(0, z.shape[1], half)], axis=1)
        return z * lax.rsqrt(ssq * (1.0 / HEAD_DIM) + EPS) * g

    w = ATTN_WIDTH
    kn = head_norm(proj[:, w:2 * w], gk_ref[...])
    k_ref[0] = kn.astype(BF16)

    z = proj[:, 5 * w:5 * w + LANES] + bf_ref[...]
    ls = jnp.minimum(z, 0.0) - jnp.log1p(jnp.exp(-jnp.abs(z)))
    lane_c = _lane_iota(ls.shape)
    is_head = lane_c < ATTN_HEADS
    ls = jnp.where(is_head, ls, 0.0)

    def pieces(val):
        p_hi = val.astype(BF16).astype(F32)
        rem = val - p_hi
        p_mid = rem.astype(BF16).astype(F32)
        p_lo = (rem - p_mid).astype(BF16).astype(F32)
        return jnp.where(is_head, p_hi,
                         jnp.where(lane_c < 2 * ATTN_HEADS, pltpu.roll(p_mid, ATTN_HEADS, axis=1),
                                   pltpu.roll(p_lo, 2 * ATTN_HEADS, axis=1))).astype(BF16)

    cs3 = jnp.dot(tri_ref[...], pieces(ls), preferred_element_type=F32)
    cs = cs3 + pltpu.roll(cs3, LANES - ATTN_HEADS, axis=1) + pltpu.roll(cs3, LANES - 2 * ATTN_HEADS, axis=1)
    cum = jnp.where(is_head, cs, 0.0) + carry[0:1, :]
    carry[0:1, :] = cum[tm - 1:tm, :]

    packed = pieces(cum * LOG2E)
    in_aug = lane_c < AUG * ATTN_HEADS
    ones_k = in_aug & ((lane_c % AUG) < AUG // 2)
    kaug = jnp.where(ones_k, 1.0, -jnp.dot(packed, ek_ref[...], preferred_element_type=F32))
    kaug_ref[0] = kaug.astype(BF16)

    uc = proj[:, 3 * w:4 * w] * proj[:, 4 * w:5 * w]
    ucbuf[SUBLANES:SUBLANES + tm, :] = uc
    uc1 = ucbuf[SUBLANES - 1:SUBLANES - 1 + tm, :]
    uc2 = ucbuf[SUBLANES - 2:SUBLANES - 2 + tm, :]
    ucbuf[0:SUBLANES, :] = uc[tm - SUBLANES:tm, :]

    if is_meta:
        cum_ref[0] = cum
        uc_ref[0] = uc
        return

    qn = head_norm(proj[:, 0:w], gq_ref[...])
    q_ref[0] = qn.astype(BF16)
    y = cw_ref[0:1, :] * uc2 + cw_ref[1:2, :] * uc1 + cw_ref[2:3, :] * uc
    conv = proj[:, 2 * w:3 * w] * y
    conv_ref[0] = head_norm(conv, gco_ref[...]).astype(BF16)
    ones_q = in_aug & ((lane_c % AUG) >= AUG // 2)
    qaug = jnp.where(ones_q, 1.0, jnp.dot(packed, eq_ref[...], preferred_element_type=F32))
    qaug_ref[0] = qaug.astype(BF16)


def _inproj(x, is_meta, tm, consts, halo, cum0):
    bsz, tlen, _ = x.shape
    nt = tlen // tm
    gmix, w1, wvt, bfp, gq, gk, cw, gco, gmat = consts
    tri = jnp.asarray(np.tril(np.ones((tm, tm), np.float32)), BF16)
    w = ATTN_WIDTH
    src = np.arange(LANES)
    dst = np.arange(LANES)
    piece, head = src // ATTN_HEADS, src % ATTN_HEADS
    valid = src < 3 * ATTN_HEADS
    eq = jnp.asarray(valid[:, None] & (dst[None, :] == (AUG * head + piece)[:, None]), BF16)
    ek = jnp.asarray(valid[:, None] & (dst[None, :] == (AUG * head + AUG // 2 + piece)[:, None]), BF16)

    def full(a):
        return pl.BlockSpec(a.shape, lambda b, t: (0,) * a.ndim)

    in_specs = [pl.BlockSpec((1, tm, D_MODEL), lambda b, t: (b, t, 0))] + [
        full(a) for a in (gmix, w1, wvt, bfp, gq, gk, cw, gco, gmat, tri, eq, ek, halo, cum0)]
    tok = lambda width: pl.BlockSpec((1, tm, width), lambda b, t: (b, t, 0))
    vt_spec = pl.BlockSpec((1, 1, w, tm), lambda b, t: (b, t, 0, 0))
    if is_meta:
        out_shape = [jax.ShapeDtypeStruct((bsz, tlen, w), BF16),
                     jax.ShapeDtypeStruct((bsz, nt, w, tm), BF16),
                     jax.ShapeDtypeStruct((bsz, tlen, LANES), BF16),
                     jax.ShapeDtypeStruct((bsz, tlen, LANES), F32),
                     jax.ShapeDtypeStruct((bsz, tlen, w), F32)]
        out_specs = [tok(w), vt_spec, tok(LANES), tok(LANES), tok(w)]
    else:
        out_shape = [jax.ShapeDtypeStruct((bsz, tlen, w), BF16)] * 2 + [
            jax.ShapeDtypeStruct((bsz, nt, w, tm), BF16),
            jax.ShapeDtypeStruct((bsz, tlen, w), BF16),
            jax.ShapeDtypeStruct((bsz, tlen, LANES), BF16),
            jax.ShapeDtypeStruct((bsz, tlen, LANES), BF16)]
        out_specs = [tok(w), tok(w), vt_spec, tok(w), tok(LANES), tok(LANES)]
    return pl.pallas_call(
        functools.partial(_inproj_kernel, is_meta, tm),
        grid=(bsz, nt),
        in_specs=in_specs,
        out_specs=out_specs,
        out_shape=out_shape,
        scratch_shapes=[pltpu.VMEM((tm + SUBLANES, w), F32), pltpu.VMEM((SUBLANES, LANES), F32)],
        compiler_params=pltpu.CompilerParams(
            dimension_semantics=("arbitrary", "arbitrary"), vmem_limit_bytes=VMEM_LIMIT),
        name="inproj_meta" if is_meta else "inproj",
    )(x, gmix, w1, wvt, bfp, gq, gk, cw, gco, gmat, tri, eq, ek, halo, cum0)


def _attn_kernel(tq, tk, q_ref, qaug_ref, k_ref, kaug_ref, vt_ref, km_ref, kaugm_ref, vmt_ref, gao_ref,
                 o_ref, m_sc, l_sc, acc_sc, sa_sc, sb_sc, ma_sc, mb_sc):
    hp = pl.program_id(1)
    qi = pl.program_id(2)
    q = q_ref[0]
    qaug = qaug_ref[0]
    lane = _lane_iota(q.shape)
    first = lane < HEAD_DIM
    zero = jnp.zeros_like(q)
    qcat = []
    for hh in range(2):
        qh = jnp.where(first, q, zero) if hh == 0 else jnp.where(first, zero, q)
        lo = AUG * (2 * hp + hh)
        qa = jnp.where((lane >= lo) & (lane < lo + AUG), qaug, zero)
        qcat.append(jnp.concatenate([qh, qa], axis=1))

    m_sc[...] = jnp.full(m_sc.shape, MASK_VALUE, F32)
    l_sc[...] = jnp.zeros(l_sc.shape, F32)
    acc_sc[...] = jnp.zeros(acc_sc.shape, F32)

    def scores_t(kcat, q_lo=0):
        return tuple(lax.dot_general(kcat, qcat[hh][q_lo:], (((1,), (1,)), ((), ())), preferred_element_type=F32)
                     for hh in range(2))

    def update(hh, st, vtb, q_lo=0, q_hi=None, block_max=None):
        qs = slice(q_lo, q_hi)
        m_old = m_sc[hh, :, qs]
        if block_max is None:
            block_max = jnp.max(st, axis=0, keepdims=True)
        m_new = jnp.maximum(m_old, block_max)
        alpha = jnp.exp2(m_old - m_new)
        p = jnp.exp2(st - m_new)
        l_sc[hh, :, qs] = alpha * l_sc[hh, :, qs] + jnp.sum(p, axis=0, keepdims=True)
        vh = vtb[hh * HEAD_DIM:(hh + 1) * HEAD_DIM, :]
        acc_sc[hh, :, qs] = alpha * acc_sc[hh, :, qs] + jnp.dot(vh, p.astype(BF16), preferred_element_type=F32)
        m_sc[hh, :, qs] = m_new

    def key_block(j):
        start = pl.multiple_of(j * tk, tk)
        return jnp.concatenate([k_ref[0, pl.ds(start, tk), :], kaug_ref[0, pl.ds(start, tk), :]], axis=1)

    def put_scores(j, bufs, q_lo=0):
        buf, mbuf = bufs
        sts = scores_t(key_block(j), q_lo)
        for hh in range(2):
            buf[hh, :, q_lo:] = sts[hh]
            mbuf[hh, :, q_lo:] = jnp.max(sts[hh], axis=0, keepdims=True)

    def consume(bufs, j, diagonal=False, q_lo=0):
        buf, mbuf = bufs
        vtb = vt_ref[0, j]
        for hh in range(2):
            if diagonal:
                st = buf[hh, :, q_lo:q_lo + tk]
                key = lax.broadcasted_iota(jnp.int32, st.shape, 0)
                qry = lax.broadcasted_iota(jnp.int32, st.shape, 1)
                update(hh, jnp.where(key <= qry, st, MASK_VALUE), vtb, q_lo, q_lo + tk)
                if q_lo + tk < tq:
                    update(hh, buf[hh, :, q_lo + tk:], vtb, q_lo + tk, None, mbuf[hh, :, q_lo + tk:])
            else:
                update(hh, buf[hh, :, q_lo:], vtb, q_lo, None, mbuf[hh, :, q_lo:])

    per = tq // tk
    assert per % 2 == 0
    st_m = scores_t(jnp.concatenate([km_ref[...], kaugm_ref[...]], axis=1))
    buf_a, buf_b = (sa_sc, ma_sc), (sb_sc, mb_sc)
    put_scores(0, buf_a)
    vmt = vmt_ref[...]
    for hh in range(2):
        update(hh, st_m[hh], vmt)

    def body(t, carry):
        j = 2 * t
        put_scores(j + 1, buf_b)
        consume(buf_a, j)
        put_scores(j + 2, buf_a)
        consume(buf_b, j + 1)
        return carry

    lax.fori_loop(0, (per // 2) * qi, body, 0)
    bufs = (buf_a, buf_b)
    for d in range(per):
        if d + 1 < per:
            put_scores(per * qi + d + 1, bufs[(d + 1) % 2], (d + 1) * tk)
        consume(bufs[d % 2], per * qi + d, diagonal=True, q_lo=d * tk)

    ot = jnp.concatenate([acc_sc[0] / l_sc[0], acc_sc[1] / l_sc[1]], axis=0)
    o2 = ot * ot
    ms0 = jnp.sum(o2[0:HEAD_DIM], axis=0, keepdims=True) * (1.0 / HEAD_DIM)
    ms1 = jnp.sum(o2[HEAD_DIM:], axis=0, keepdims=True) * (1.0 / HEAD_DIM)
    inv = jnp.concatenate([jnp.broadcast_to(lax.rsqrt(ms0 + EPS), (HEAD_DIM, tq)),
                           jnp.broadcast_to(lax.rsqrt(ms1 + EPS), (HEAD_DIM, tq))], axis=0)
    o_ref[0] = (ot * inv * gao_ref[...]).T.astype(BF16)


def _attention(q, qaug, k, kaug, vt, km, kaugm, vmt, gao):
    bsz, slen, w = q.shape
    tq, tk = ATTN_Q_TILE, ATTN_K_TILE
    nq, nk = slen // tq, slen // tk
    return pl.pallas_call(
        functools.partial(_attn_kernel, tq, tk),
        grid=(bsz, HEAD_PAIRS, nq),
        in_specs=[
            pl.BlockSpec((1, tq, LANES), lambda b, p, i: (b, i, p)),
            pl.BlockSpec((1, tq, LANES), lambda b, p, i: (b, i, 0)),
            pl.BlockSpec((1, slen, LANES), lambda b, p, i: (b, 0, p)),
            pl.BlockSpec((1, slen, LANES), lambda b, p, i: (b, 0, 0)),
            pl.BlockSpec((1, nk, LANES, tk), lambda b, p, i: (b, 0, p, 0)),
            pl.BlockSpec((N_META, LANES), lambda b, p, i: (0, p)),
            pl.BlockSpec((N_META, LANES), lambda b, p, i: (0, 0)),
            pl.BlockSpec((LANES, N_META), lambda b, p, i: (p, 0)),
            pl.BlockSpec((LANES, 1), lambda b, p, i: (p, 0)),
        ],
        out_specs=pl.BlockSpec((1, tq, LANES), lambda b, p, i: (b, i, p)),
        out_shape=jax.ShapeDtypeStruct((bsz, slen, w), BF16),
        scratch_shapes=[pltpu.VMEM((2, 1, tq), F32), pltpu.VMEM((2, 1, tq), F32),
                        pltpu.VMEM((2, HEAD_DIM, tq), F32),
                        pltpu.VMEM((2, tk, tq), F32), pltpu.VMEM((2, tk, tq), F32),
                        pltpu.VMEM((2, 1, tq), F32), pltpu.VMEM((2, 1, tq), F32)],
        compiler_params=pltpu.CompilerParams(
            dimension_semantics=("arbitrary", "arbitrary", "arbitrary"), vmem_limit_bytes=VMEM_LIMIT),
        name="fox_attention",
    )(q, qaug, k, kaug, vt, km, kaugm, vmt, gao)


ROUTE_ROWS = 64


def _outproj_kernel(tm, attn_ref, conv_ref, x_ref, wo_ref, gffn_ref, wrt_ref, brc_ref, tri_ref,
                    h_ref, xt_ref, ri_ref, rf_ref, cnt_ref, carry):
    i = pl.program_id(0)

    @pl.when(i == 0)
    def _():
        carry[...] = jnp.zeros(carry.shape, F32)

    mixed = jnp.concatenate([attn_ref[...], conv_ref[...]], axis=1)
    h = x_ref[...] + jnp.dot(mixed, wo_ref[...], preferred_element_type=F32)
    h_ref[...] = h
    ms = jnp.mean(h * h, axis=-1, keepdims=True)
    xt = (h * lax.rsqrt(ms + EPS)) * gffn_ref[...]
    _store_token_rows(xt_ref, xt, tm)

    x_hi = xt.astype(BF16)
    x_lo = (xt - x_hi.astype(F32)).astype(BF16)
    parts = lax.dot_general(wrt_ref[...], jnp.concatenate([x_hi, x_lo], axis=1), (((1,), (1,)), ((), ())),
                            preferred_element_type=F32)
    logits = (parts[0:ROUTE_ROWS] + parts[LANES:LANES + ROUTE_ROWS]) + brc_ref[0:ROUTE_ROWS, 0:1]
    row = lax.broadcasted_iota(jnp.int32, logits.shape, 0)
    rowf = row.astype(F32)
    big = float(LANES)

    def first_argmax(vals, vmax):
        return jnp.min(jnp.where(vals == vmax, rowf, big), axis=0, keepdims=True)

    is_g = row < N_EXPERT_GROUPS
    gl = jnp.where(is_g, logits, MASK_VALUE)
    gmax = jnp.max(gl, axis=0, keepdims=True)
    gidx = first_argmax(gl, gmax)
    g_p = 1.0 / jnp.sum(jnp.where(is_g, jnp.exp(gl - gmax), 0.0), axis=0, keepdims=True)

    base = N_EXPERT_GROUPS + EXPERTS_PER_GROUP * gidx
    in_grp = (rowf >= base) & (rowf < base + EXPERTS_PER_GROUP)
    el = jnp.where(in_grp, logits, MASK_VALUE)
    l1 = jnp.max(el, axis=0, keepdims=True)
    e1 = first_argmax(el, l1)
    el2 = jnp.where(rowf == e1, MASK_VALUE, el)
    l2 = jnp.max(el2, axis=0, keepdims=True)
    e2 = first_argmax(el2, l2)
    zsum = jnp.sum(jnp.where(in_grp, jnp.exp(el - l1), 0.0), axis=0, keepdims=True)
    p1 = 1.0 / zsum
    p2 = jnp.exp(l2 - l1) / zsum
    den = p1 + p2
    w1 = g_p * p1 / den
    w2 = g_p * p2 / den
    id1 = e1 - N_EXPERT_GROUPS
    id2 = e2 - N_EXPERT_GROUPS

    oh1 = rowf == id1
    oh2 = rowf == id2
    oh = jnp.where(oh1 | oh2, 1.0, 0.0)
    before = jnp.dot(oh.astype(BF16), tri_ref[...], preferred_element_type=F32) + carry[:, 0:1]
    rank1 = jnp.sum(jnp.where(oh1, before, 0.0), axis=0, keepdims=True)
    rank2 = jnp.sum(jnp.where(oh2, before, 0.0), axis=0, keepdims=True)
    carry[...] = carry[...] + jnp.sum(oh, axis=1, keepdims=True)
    cnt_ref[...] = carry[...]

    pad_i = jnp.zeros((SUBLANES - 2 * TOP_K, tm), F32)
    ri_ref[0] = jnp.concatenate([id1, id2, rank1, rank2, pad_i], axis=0).astype(jnp.int32)
    pad_f = jnp.zeros((LANES - TOP_K, tm), F32)
    rf_ref[...] = jnp.concatenate([w1, w2, pad_f], axis=0).T[:, 0:SUBLANES]


def _outproj(attn, conv, x, wo, gffn, wrt, brc):
    n = x.shape[0]
    tm = OUT_TILE
    tri = jnp.asarray(np.triu(np.ones((tm, tm), np.float32), k=1), BF16)

    def full(a):
        return pl.BlockSpec(a.shape, lambda i: (0,) * a.ndim)

    rows = lambda width: pl.BlockSpec((tm, width), lambda i: (i, 0))
    return pl.pallas_call(
        functools.partial(_outproj_kernel, tm),
        grid=(n // tm,),
        in_specs=[rows(ATTN_WIDTH), rows(CONV_WIDTH), rows(D_MODEL)] + [
            full(a) for a in (wo, gffn, wrt, brc, tri)],
        out_specs=[rows(D_MODEL), pl.BlockSpec((tm * ROW_SUBLANES, LANES), lambda i: (i, 0)),
                   pl.BlockSpec((1, SUBLANES, tm), lambda i: (i, 0, 0)), rows(SUBLANES),
                   pl.BlockSpec((ROUTE_ROWS, LANES), lambda i: (0, 0))],
        out_shape=[jax.ShapeDtypeStruct((n, D_MODEL), F32),
                   jax.ShapeDtypeStruct((n * ROW_SUBLANES, LANES), F32),
                   jax.ShapeDtypeStruct((n // tm, SUBLANES, tm), jnp.int32),
                   jax.ShapeDtypeStruct((n, SUBLANES), F32),
                   jax.ShapeDtypeStruct((ROUTE_ROWS, LANES), F32)],
        scratch_shapes=[pltpu.VMEM((ROUTE_ROWS, LANES), F32)],
        compiler_params=pltpu.CompilerParams(
            dimension_semantics=("arbitrary",), vmem_limit_bytes=VMEM_LIMIT),
        name="outproj_router",
    )(attn, conv, x, wo, gffn, wrt, brc, tri)


def _dest_kernel(rs_ref, ri_ref, dest_ref):
    ri = ri_ref[...]
    experts = ri[:, 0:TOP_K, :]
    start = jnp.zeros_like(experts)
    for e in range(N_EXPERTS):
        start = jnp.where(experts == e, rs_ref[e], start)
    dest_ref[...] = jnp.concatenate([start + ri[:, TOP_K:2 * TOP_K, :],
                                     jnp.zeros((ri.shape[0], SUBLANES - TOP_K, ri.shape[2]), jnp.int32)], axis=1)


def _dest_rows(ri_t, row_start):
    grid_spec = pltpu.PrefetchScalarGridSpec(
        num_scalar_prefetch=1,
        grid=(1,),
        in_specs=[pl.BlockSpec(ri_t.shape, lambda i, rs: (0, 0, 0))],
        out_specs=pl.BlockSpec(ri_t.shape, lambda i, rs: (0, 0, 0)),
    )
    return pl.pallas_call(
        _dest_kernel,
        grid_spec=grid_spec,
        out_shape=jax.ShapeDtypeStruct(ri_t.shape, jnp.int32),
        compiler_params=pltpu.CompilerParams(dimension_semantics=("arbitrary",)),
        name="moe_dest",
    )(row_start, ri_t)


def _tile_copy(src, src_row, dst, dst_row, sem):
    rs = ROW_SUBLANES
    return pltpu.make_async_copy(src.at[pl.ds(pl.multiple_of(src_row * rs, rs), rs), :],
                                 dst.at[pl.ds(pl.multiple_of(dst_row * rs, rs), rs), :], sem)


def _dispatch_kernel(tm, bm, fz_ref, fv_ref, dest_ref, xt_ref, xs_hbm, zeros, sem, zsem):
    i = pl.program_id(0)
    block_rows = bm * ROW_SUBLANES

    def zero_block(z):
        start = pl.multiple_of(fz_ref[z] * block_rows, block_rows)
        return pltpu.make_async_copy(zeros, xs_hbm.at[pl.ds(start, block_rows), :], zsem.at[0])

    @pl.when(i == 0)
    def _():
        zeros[...] = jnp.zeros(zeros.shape, F32)
        for z in range(fz_ref.shape[0]):
            @pl.when(fv_ref[z] == 1)
            def _():
                zero_block(z).start()
        for z in range(fz_ref.shape[0]):
            @pl.when(fv_ref[z] == 1)
            def _():
                zero_block(z).wait()

    for g in range(0, tm, DMA_GROUP):
        dests = [(r, kk, dest_ref[0, kk, r]) for r in range(g, g + DMA_GROUP) for kk in range(TOP_K)]
        for r, kk, d in dests:
            _tile_copy(xt_ref, r, xs_hbm, d, sem.at[0]).start(priority=kk)
    for kk in range(TOP_K):
        pltpu.make_async_copy(xt_ref, xs_hbm.at[pl.ds(0, tm * ROW_SUBLANES), :], sem.at[0]).wait()


def _route_spec(dest, tm, extra_args):
    per = dest.shape[2] // tm
    nt = dest.shape[0] * per
    if extra_args == 0:
        index = lambda i: (jnp.minimum(i, nt - 1) // per, 0, jnp.minimum(i, nt - 1) % per)
    else:
        index = lambda i, *_: (jnp.minimum(i, nt - 1) // per, 0, jnp.minimum(i, nt - 1) % per)
    return nt, pl.BlockSpec((1, SUBLANES, tm), index, memory_space=pltpu.SMEM)


def _dispatch(xt_rows, dest, fill_blocks, fill_valid, nb):
    tm = DISPATCH_TILE
    bm = EXPERT_ROWS
    nt, dest_spec = _route_spec(dest, tm, 2)
    grid_spec = pltpu.PrefetchScalarGridSpec(
        num_scalar_prefetch=2,
        grid=(nt,),
        in_specs=[dest_spec, pl.BlockSpec((tm * ROW_SUBLANES, LANES), lambda i, fz, fv: (i, 0))],
        out_specs=pl.BlockSpec(memory_space=pl.ANY),
        scratch_shapes=[pltpu.VMEM((bm * ROW_SUBLANES, LANES), F32), pltpu.SemaphoreType.DMA((1,)),
                        pltpu.SemaphoreType.DMA((1,))],
    )
    return pl.pallas_call(
        functools.partial(_dispatch_kernel, tm, bm),
        grid_spec=grid_spec,
        out_shape=jax.ShapeDtypeStruct((nb * bm * ROW_SUBLANES, LANES), F32),
        compiler_params=pltpu.CompilerParams(
            dimension_semantics=("arbitrary",), vmem_limit_bytes=VMEM_LIMIT),
        name="moe_dispatch",
    )(fill_blocks, fill_valid, dest, xt_rows)


def _expert_kernel(bm, be_ref, nu_ref, first_ref, nxt_ref, slot_ref, xs_ref, wg_hbm, wu_hbm, wd_hbm, ys_ref,
                   wg_buf, wu_buf, wd_buf, wg_bf, wu_bf, wd_bf, wsem):
    step = pl.program_id(0)

    def weight_copies(e, slot):
        return [pltpu.make_async_copy(src.at[e], buf.at[slot], wsem.at[m, slot])
                for m, (src, buf) in enumerate(((wg_hbm, wg_buf), (wu_hbm, wu_buf), (wd_hbm, wd_buf)))]

    @pl.when(step == 0)
    def _():
        for c in weight_copies(be_ref[0], 0):
            c.start()

    def one_block(j, xs_blk, ys_blk):
        @pl.when(j < nu_ref[0])
        def _():
            slot = slot_ref[j]

            @pl.when(first_ref[j] == 1)
            def _():
                for c in weight_copies(0, slot):
                    c.wait()

                @pl.when(nxt_ref[j] >= 0)
                def _():
                    for c in weight_copies(nxt_ref[j], 1 - slot):
                        c.start()

                wg_bf[...] = wg_buf[slot].astype(BF16)
                wu_bf[...] = wu_buf[slot].astype(BF16)
                wd_bf[...] = wd_buf[slot].astype(BF16)

            x = _load_token_rows(xs_blk, bm).astype(BF16)
            hg = jnp.dot(x, wg_bf[...], preferred_element_type=F32)
            hu = jnp.dot(x, wu_bf[...], preferred_element_type=F32)
            hdn = hg * (1.0 / (1.0 + jnp.exp(-hg))) * hu
            y = jnp.dot(hdn.astype(BF16), wd_bf[...], preferred_element_type=F32)
            _store_token_rows(ys_blk, y, bm)

        @pl.when(j >= nu_ref[0])
        def _():
            ys_blk[...] = jnp.zeros(ys_blk.shape, F32)

    rows = bm * ROW_SUBLANES
    for sub in range(EXPERT_BLOCKS_PER_STEP):
        view = pl.ds(sub * rows, rows)
        one_block(step * EXPERT_BLOCKS_PER_STEP + sub, xs_ref.at[view, :], ys_ref.at[view, :])


def _experts(xs_rows, block_expert, n_used, w_gate, w_up, w_down):
    bm = EXPERT_ROWS
    nb = block_expert.shape[0]
    blk = jnp.arange(nb, dtype=jnp.int32)
    live = blk < n_used[0]
    first = live & ((blk == 0) | (block_expert != jnp.roll(block_expert, 1)))
    slot = (jnp.cumsum(first.astype(jnp.int32)) - 1) % 2
    later_first = first[None, :] & (blk[None, :] > blk[:, None])
    nxt_start = jnp.min(jnp.where(later_first, blk[None, :], nb), axis=1)
    nxt = jnp.sum(jnp.where(blk[None, :] == nxt_start[:, None], block_expert[None, :], 0), axis=1)
    nxt = jnp.where(nxt_start < nb, nxt, -1)
    per = EXPERT_BLOCKS_PER_STEP
    assert nb % per == 0
    step_rows = per * bm * ROW_SUBLANES
    grid_spec = pltpu.PrefetchScalarGridSpec(
        num_scalar_prefetch=5,
        grid=(nb // per,),
        in_specs=[
            pl.BlockSpec((step_rows, LANES), lambda s, be, nu, *_: (jnp.minimum(s, (nu[0] - 1) // per), 0)),
            pl.BlockSpec(memory_space=pl.ANY),
            pl.BlockSpec(memory_space=pl.ANY),
            pl.BlockSpec(memory_space=pl.ANY),
        ],
        out_specs=pl.BlockSpec((step_rows, LANES), lambda s, *_: (s, 0)),
        scratch_shapes=[pltpu.VMEM((2, D_MODEL, D_EXPERT), F32), pltpu.VMEM((2, D_MODEL, D_EXPERT), F32),
                        pltpu.VMEM((2, D_EXPERT, D_MODEL), F32),
                        pltpu.VMEM((D_MODEL, D_EXPERT), BF16), pltpu.VMEM((D_MODEL, D_EXPERT), BF16),
                        pltpu.VMEM((D_EXPERT, D_MODEL), BF16), pltpu.SemaphoreType.DMA((3, 2))],
    )
    return pl.pallas_call(
        functools.partial(_expert_kernel, bm),
        grid_spec=grid_spec,
        out_shape=jax.ShapeDtypeStruct((nb * bm * ROW_SUBLANES, LANES), F32),
        compiler_params=pltpu.CompilerParams(
            dimension_semantics=("arbitrary",), vmem_limit_bytes=VMEM_LIMIT),
        name="moe_experts",
    )(block_expert, n_used, first.astype(jnp.int32), nxt.astype(jnp.int32), slot.astype(jnp.int32),
      xs_rows, w_gate, w_up, w_down)


def _combine_kernel(tm, dest_ref, ys_hbm, h_ref, rf_ref, o_ref, ybuf, sem):
    i = pl.program_id(0)
    nt = pl.num_programs(0) - 1

    @pl.when(i < nt)
    def _():
        slot = i % 2
        for g in range(0, tm, DMA_GROUP):
            dests = [(r, kk, dest_ref[0, kk, r]) for r in range(g, g + DMA_GROUP) for kk in range(TOP_K)]
            for r, kk, d in dests:
                _tile_copy(ys_hbm, d, ybuf.at[kk, slot], r, sem.at[kk, slot]).start(priority=kk)

    @pl.when(i >= 1)
    def _():
        slot = (i - 1) % 2
        for kk in range(TOP_K):
            pltpu.make_async_copy(ys_hbm.at[pl.ds(0, tm * ROW_SUBLANES), :], ybuf.at[kk, slot],
                                  sem.at[kk, slot]).wait()
        rf = rf_ref[...]
        o_ref[...] = (h_ref[...] + rf[:, 0:1] * _load_token_rows(ybuf.at[0, slot], tm)
                      + rf[:, 1:2] * _load_token_rows(ybuf.at[1, slot], tm))


def _combine(ys_rows, dest, h, rf):
    n = h.shape[0]
    tm = COMBINE_TILE
    nt, dest_spec = _route_spec(dest, tm, 0)
    prev = lambda i: jnp.maximum(i - 1, 0)
    return pl.pallas_call(
        functools.partial(_combine_kernel, tm),
        grid=(nt + 1,),
        in_specs=[
            dest_spec,
            pl.BlockSpec(memory_space=pl.ANY),
            pl.BlockSpec((tm, D_MODEL), lambda i: (prev(i), 0)),
            pl.BlockSpec((tm, SUBLANES), lambda i: (prev(i), 0)),
        ],
        out_specs=pl.BlockSpec((tm, D_MODEL), lambda i: (prev(i), 0)),
        out_shape=jax.ShapeDtypeStruct((n, D_MODEL), F32),
        scratch_shapes=[pltpu.VMEM((TOP_K, 2, tm * ROW_SUBLANES, LANES), F32),
                        pltpu.SemaphoreType.DMA((TOP_K, 2))],
        compiler_params=pltpu.CompilerParams(
            dimension_semantics=("arbitrary",), vmem_limit_bytes=VMEM_LIMIT),
        name="moe_combine",
    )(dest, ys_rows, h, rf)


def _pad_lanes(a, width=LANES):
    return jnp.pad(a, ((0, 0), (0, width - a.shape[-1])))


def _layer(x, meta_tokens, g_mix, w_in, b_forget, q_norm_g, k_norm_g, conv_w, attn_out_g, conv_out_g,
           w_out, g_ffn, w_rg, b_rg, w_re, b_re, w_gate, w_up, w_down):
    bsz, slen, _ = x.shape
    n = bsz * slen
    w = ATTN_WIDTH

    wq, wk, wv, wf, wb, wc, whc = jnp.split(
        w_in, [w, 2 * w, 3 * w, 3 * w + ATTN_HEADS, 4 * w + ATTN_HEADS, 5 * w + ATTN_HEADS], axis=1)
    w1 = jnp.concatenate([wq, wk, wb, wc, whc, _pad_lanes(wf)], axis=1).astype(BF16)
    head_of = np.arange(MXU_DIM) // HEAD_DIM
    gmat = jnp.asarray(head_of[:, None] == head_of[None, :], BF16)
    consts = (
        g_mix.reshape(1, D_MODEL), w1, wv.T.astype(BF16), _pad_lanes(b_forget.reshape(1, ATTN_HEADS)),
        jnp.tile(q_norm_g, ATTN_HEADS).reshape(1, w) * (HEAD_DIM ** -0.5 * LOG2E),
        jnp.tile(k_norm_g, ATTN_HEADS).reshape(1, w),
        jnp.pad(conv_w, ((0, SUBLANES - CONV_K), (0, 0))), conv_out_g.reshape(1, w), gmat)

    xm = jnp.pad(meta_tokens, ((0, META_TILE - N_META), (0, 0)))[None]
    km, vmt, kaugm, cum_m, uc_m = _inproj(xm, True, META_TILE, consts,
                                          jnp.zeros((SUBLANES, w), F32), jnp.zeros((1, LANES), F32))
    km, vmt, kaugm = km[0, :N_META], vmt[0, 0, :, :N_META], kaugm[0, :N_META]
    cum_m = cum_m[0, :N_META]
    halo = jnp.zeros((SUBLANES, w), F32).at[SUBLANES - 2:].set(uc_m[0, N_META - 2:N_META])
    cum0 = cum_m[N_META - 1:N_META]

    q, k, vt, conv, qaug, kaug = _inproj(x, False, IN_TILE, consts, halo, cum0)
    attn = _attention(q, qaug, k, kaug, vt, km, kaugm, vmt, attn_out_g.reshape(w, 1))

    wr = _pad_lanes(jnp.concatenate([w_rg, w_re], axis=1)).T
    wr_hi = wr.astype(BF16)
    wr_lo = (wr - wr_hi.astype(F32)).astype(BF16)
    wrt = jnp.concatenate([jnp.concatenate([wr_hi, wr_hi], axis=1),
                           jnp.concatenate([wr_lo, jnp.zeros_like(wr_lo)], axis=1)], axis=0)
    brc = jnp.broadcast_to(_pad_lanes(jnp.concatenate([b_rg, b_re]).reshape(1, -1)).T, (LANES, LANES))
    h, xt, ri, rf, cnt = _outproj(attn.reshape(n, w), conv.reshape(n, w), x.reshape(n, D_MODEL),
                                  w_out.astype(BF16), g_ffn.reshape(1, D_MODEL), wrt, brc)

    bm = EXPERT_ROWS
    nb = (n * TOP_K) // bm + N_EXPERTS
    counts = cnt[:N_EXPERTS, 0].astype(jnp.int32)
    blocks_e = (counts + bm - 1) // bm
    blk_end = jnp.cumsum(blocks_e)
    row_start = (blk_end - blocks_e) * bm
    blk = jnp.arange(nb, dtype=jnp.int32)
    block_expert = jnp.minimum(jnp.sum(blk_end[None, :] <= blk[:, None], axis=1), N_EXPERTS - 1).astype(jnp.int32)
    n_used = blk_end[-1:].astype(jnp.int32)
    tail = n_used[0] + jnp.arange(N_EXPERTS, dtype=jnp.int32)
    fill_blocks = jnp.clip(jnp.concatenate([blk_end - 1, tail]), 0, nb - 1).astype(jnp.int32)
    fill_valid = jnp.concatenate([counts % bm != 0, tail < nb]).astype(jnp.int32)

    dest = _dest_rows(ri, row_start.astype(jnp.int32))
    xs = _dispatch(xt, dest, fill_blocks, fill_valid, nb)
    ys = _experts(xs, block_expert, n_used, w_gate, w_up, w_down)
    out = _combine(ys, dest, h, rf)
    return out.reshape(bsz, slen, D_MODEL)


def kernel(x, meta_tokens, norm_mix_g, w_in, b_forget, q_norm_g, k_norm_g, conv_w, attn_out_g, conv_out_g,
           w_out, norm_ffn_g, w_router_group, b_router_group, w_router_expert, b_router_expert,
           w_gate, w_up, w_down):
    assert norm_mix_g.shape[0] == 1, "single-layer block"
    return _layer(x, meta_tokens, norm_mix_g[0], w_in[0], b_forget[0], q_norm_g[0], k_norm_g[0], conv_w[0],
                  attn_out_g[0], conv_out_g[0], w_out[0], norm_ffn_g[0], w_router_group[0],
                  b_router_group[0], w_router_expert[0], b_router_expert[0], w_gate[0], w_up[0], w_down[0])
```

```python
import functools

import jax
import jax.numpy as jnp
import numpy as np
from jax import lax
from jax.experimental import pallas as pl
from jax.experimental.pallas import tpu as pltpu

D_MODEL = 1024
N_META = 16
HEAD_DIM = 64
ATTN_HEADS = 8
ATTN_WIDTH = ATTN_HEADS * HEAD_DIM
CONV_WIDTH = D_MODEL - ATTN_WIDTH
CONV_K = 3
N_EXPERT_GROUPS = 4
EXPERTS_PER_GROUP = 8
N_EXPERTS = N_EXPERT_GROUPS * EXPERTS_PER_GROUP
TOP_K = 2
D_EXPERT = 512
EPS = 1e-6
MASK_VALUE = -1e30
LOG2E = 1.4426950408889634
AUG = 6

LANES = 128
SUBLANES = 8
MXU_DIM = 256
HEAD_PAIRS = ATTN_HEADS * HEAD_DIM // LANES
PROJ_PAD_COLS = 5 * ATTN_WIDTH + LANES
VMEM_LIMIT = 56 * 1024 * 1024

IN_TILE = 512
META_TILE = 128
ATTN_Q_TILE = 2048
ATTN_K_TILE = 512
OUT_TILE = 1024
EXPERT_ROWS = 256
EXPERT_BLOCKS_PER_STEP = 4
DISPATCH_TILE = 1024
COMBINE_TILE = 256
DMA_GROUP = 8

F32 = jnp.float32
BF16 = jnp.bfloat16
ROW_SUBLANES = D_MODEL // LANES


def _lane_iota(shape):
    return lax.broadcasted_iota(jnp.int32, shape, len(shape) - 1)


def _store_token_rows(ref, x, rows, stage):
    for sl in range(ROW_SUBLANES):
        stage[pl.ds(sl, rows, stride=ROW_SUBLANES), :] = x[:, sl * LANES:(sl + 1) * LANES]
    ref[...] = stage[...].astype(BF16)


def _load_token_rows(ref, rows, stage):
    stage[...] = ref[...].astype(F32)
    return jnp.concatenate([stage[pl.ds(sl, rows, stride=ROW_SUBLANES), :] for sl in range(ROW_SUBLANES)], axis=1)


def _inproj_kernel(is_meta, tm, x_ref, gmix_ref, w1_ref, wvt_ref, bf_ref, gq_ref, gk_ref, cw_ref, gco_ref,
                   gmat_ref, tri_ref, eq_ref, ek_ref, halo_ref, cum0_ref, *rest):
    if is_meta:
        k_ref, vt_ref, kaug_ref, cum_ref, uc_ref, ucbuf, carry = rest
    else:
        q_ref, k_ref, vt_ref, conv_ref, qaug_ref, kaug_ref, ucbuf, carry = rest
    t = pl.program_id(1)

    @pl.when(t == 0)
    def _():
        ucbuf[0:SUBLANES, :] = halo_ref[...]
        carry[0:1, :] = cum0_ref[...]

    x = x_ref[0]
    ms = jnp.mean(x * x, axis=-1, keepdims=True)
    u = (x * lax.rsqrt(ms + EPS)) * gmix_ref[...]
    ub = u.astype(BF16)
    proj = jnp.dot(ub, w1_ref[...], preferred_element_type=F32)
    vt_ref[0, 0] = lax.dot_general(wvt_ref[...], ub, (((1,), (1,)), ((), ())),
                                   preferred_element_type=F32).astype(BF16)

    def head_norm(z, g):
        z2 = (z * z).astype(BF16)
        half = gmat_ref.shape[0]
        ssq = jnp.concatenate([jnp.dot(z2[:, c:c + half], gmat_ref[...], preferred_element_type=F32)
                               for c in range(0, z.shape[1], half)], axis=1)
        return z * lax.rsqrt(ssq * (1.0 / HEAD_DIM) + EPS) * g

    w = ATTN_WIDTH
    kn = head_norm(proj[:, w:2 * w], gk_ref[...])
    k_ref[0] = kn.astype(BF16)

    z = proj[:, 5 * w:5 * w + LANES] + bf_ref[...]
    ls = jnp.minimum(z, 0.0) - jnp.log1p(jnp.exp(-jnp.abs(z)))
    lane_c = _lane_iota(ls.shape)
    is_head = lane_c < ATTN_HEADS
    ls = jnp.where(is_head, ls, 0.0)

    def pieces(val):
        p_hi = val.astype(BF16).astype(F32)
        rem = val - p_hi
        p_mid = rem.astype(BF16).astype(F32)
        p_lo = (rem - p_mid).astype(BF16).astype(F32)
        return jnp.where(is_head, p_hi,
                         jnp.where(lane_c < 2 * ATTN_HEADS, pltpu.roll(p_mid, ATTN_HEADS, axis=1),
                                   pltpu.roll(p_lo, 2 * ATTN_HEADS, axis=1))).astype(BF16)

    cs3 = jnp.dot(tri_ref[...], pieces(ls), preferred_element_type=F32)
    cs = cs3 + pltpu.roll(cs3, LANES - ATTN_HEADS, axis=1) + pltpu.roll(cs3, LANES - 2 * ATTN_HEADS, axis=1)
    cum = jnp.where(is_head, cs, 0.0) + carry[0:1, :]
    carry[0:1, :] = cum[tm - 1:tm, :]

    packed = pieces(cum * LOG2E)
    in_aug = lane_c < AUG * ATTN_HEADS
    ones_k = in_aug & ((lane_c % AUG) < AUG // 2)
    kaug = jnp.where(ones_k, 1.0, -jnp.dot(packed, ek_ref[...], preferred_element_type=F32))
    kaug_ref[0] = kaug.astype(BF16)

    uc = proj[:, 3 * w:4 * w] * proj[:, 4 * w:5 * w]
    ucbuf[SUBLANES:SUBLANES + tm, :] = uc
    uc1 = ucbuf[SUBLANES - 1:SUBLANES - 1 + tm, :]
    uc2 = ucbuf[SUBLANES - 2:SUBLANES - 2 + tm, :]
    ucbuf[0:SUBLANES, :] = uc[tm - SUBLANES:tm, :]

    if is_meta:
        cum_ref[0] = cum
        uc_ref[0] = uc
        return

    qn = head_norm(proj[:, 0:w], gq_ref[...])
    q_ref[0] = qn.astype(BF16)
    y = cw_ref[0:1, :] * uc2 + cw_ref[1:2, :] * uc1 + cw_ref[2:3, :] * uc
    conv = proj[:, 2 * w:3 * w] * y
    conv_ref[0] = head_norm(conv, gco_ref[...]).astype(BF16)
    ones_q = in_aug & ((lane_c % AUG) >= AUG // 2)
    qaug = jnp.where(ones_q, 1.0, jnp.dot(packed, eq_ref[...], preferred_element_type=F32))
    qaug_ref[0] = qaug.astype(BF16)


def _inproj(x, is_meta, tm, consts, halo, cum0):
    bsz, tlen, _ = x.shape
    nt = tlen // tm
    gmix, w1, wvt, bfp, gq, gk, cw, gco, gmat = consts
    tri = jnp.asarray(np.tril(np.ones((tm, tm), np.float32)), BF16)
    w = ATTN_WIDTH
    src = np.arange(LANES)
    dst = np.arange(LANES)
    piece, head = src // ATTN_HEADS, src % ATTN_HEADS
    valid = src < 3 * ATTN_HEADS
    eq = jnp.asarray(valid[:, None] & (dst[None, :] == (AUG * head + piece)[:, None]), BF16)
    ek = jnp.asarray(valid[:, None] & (dst[None, :] == (AUG * head + AUG // 2 + piece)[:, None]), BF16)

    def full(a):
        return pl.BlockSpec(a.shape, lambda b, t: (0,) * a.ndim)

    in_specs = [pl.BlockSpec((1, tm, D_MODEL), lambda b, t: (b, t, 0))] + [
        full(a) for a in (gmix, w1, wvt, bfp, gq, gk, cw, gco, gmat, tri, eq, ek, halo, cum0)]
    tok = lambda width: pl.BlockSpec((1, tm, width), lambda b, t: (b, t, 0))
    vt_spec = pl.BlockSpec((1, 1, w, tm), lambda b, t: (b, t, 0, 0))
    if is_meta:
        out_shape = [jax.ShapeDtypeStruct((bsz, tlen, w), BF16),
                     jax.ShapeDtypeStruct((bsz, nt, w, tm), BF16),
                     jax.ShapeDtypeStruct((bsz, tlen, LANES), BF16),
                     jax.ShapeDtypeStruct((bsz, tlen, LANES), F32),
                     jax.ShapeDtypeStruct((bsz, tlen, w), F32)]
        out_specs = [tok(w), vt_spec, tok(LANES), tok(LANES), tok(w)]
    else:
        out_shape = [jax.ShapeDtypeStruct((bsz, tlen, w), BF16)] * 2 + [
            jax.ShapeDtypeStruct((bsz, nt, w, tm), BF16),
            jax.ShapeDtypeStruct((bsz, tlen, w), BF16),
            jax.ShapeDtypeStruct((bsz, tlen, LANES), BF16),
            jax.ShapeDtypeStruct((bsz, tlen, LANES), BF16)]
        out_specs = [tok(w), tok(w), vt_spec, tok(w), tok(LANES), tok(LANES)]
    return pl.pallas_call(
        functools.partial(_inproj_kernel, is_meta, tm),
        grid=(bsz, nt),
        in_specs=in_specs,
        out_specs=out_specs,
        out_shape=out_shape,
        scratch_shapes=[pltpu.VMEM((tm + SUBLANES, w), F32), pltpu.VMEM((SUBLANES, LANES), F32)],
        compiler_params=pltpu.CompilerParams(
            dimension_semantics=("arbitrary", "arbitrary"), vmem_limit_bytes=VMEM_LIMIT),
        name="inproj_meta" if is_meta else "inproj",
    )(x, gmix, w1, wvt, bfp, gq, gk, cw, gco, gmat, tri, eq, ek, halo, cum0)


def _attn_kernel(tq, tk, q_ref, qaug_ref, k_ref, kaug_ref, vt_ref, km_ref, kaugm_ref, vmt_ref, gao_ref,
                 o_ref, m_sc, l_sc, acc_sc, sa_sc, sb_sc, ma_sc, mb_sc):
    hp = pl.program_id(1)
    qi = pl.program_id(2)
    q = q_ref[0]
    qaug = qaug_ref[0]
    lane = _lane_iota(q.shape)
    first = lane < HEAD_DIM
    zero = jnp.zeros_like(q)
    qcat = []
    for hh in range(2):
        qh = jnp.where(first, q, zero) if hh == 0 else jnp.where(first, zero, q)
        lo = AUG * (2 * hp + hh)
        qa = jnp.where((lane >= lo) & (lane < lo + AUG), qaug, zero)
        qcat.append(jnp.concatenate([qh, qa], axis=1))

    m_sc[...] = jnp.full(m_sc.shape, MASK_VALUE, F32)
    l_sc[...] = jnp.zeros(l_sc.shape, F32)
    acc_sc[...] = jnp.zeros(acc_sc.shape, F32)

    def scores_t(kcat, q_lo=0):
        return tuple(lax.dot_general(kcat, qcat[hh][q_lo:], (((1,), (1,)), ((), ())), preferred_element_type=F32)
                     for hh in range(2))

    def update(hh, st, vtb, q_lo=0, q_hi=None, block_max=None):
        qs = slice(q_lo, q_hi)
        m_old = m_sc[hh, :, qs]
        if block_max is None:
            block_max = jnp.max(st, axis=0, keepdims=True)
        m_new = jnp.maximum(m_old, block_max)
        alpha = jnp.exp2(m_old - m_new)
        p = jnp.exp2(st - m_new)
        l_sc[hh, :, qs] = alpha * l_sc[hh, :, qs] + jnp.sum(p, axis=0, keepdims=True)
        vh = vtb[hh * HEAD_DIM:(hh + 1) * HEAD_DIM, :]
        acc_sc[hh, :, qs] = alpha * acc_sc[hh, :, qs] + jnp.dot(vh, p.astype(BF16), preferred_element_type=F32)
        m_sc[hh, :, qs] = m_new

    def key_block(j):
        start = pl.multiple_of(j * tk, tk)
        return jnp.concatenate([k_ref[0, pl.ds(start, tk), :], kaug_ref[0, pl.ds(start, tk), :]], axis=1)

    def put_scores(j, bufs, q_lo=0):
        buf, mbuf = bufs
        sts = scores_t(key_block(j), q_lo)
        for hh in range(2):
            buf[hh, :, q_lo:] = sts[hh]
            mbuf[hh, :, q_lo:] = jnp.max(sts[hh], axis=0, keepdims=True)

    def consume(bufs, j, diagonal=False, q_lo=0):
        buf, mbuf = bufs
        vtb = vt_ref[0, j]
        for hh in range(2):
            if diagonal:
                st = buf[hh, :, q_lo:q_lo + tk]
                key = lax.broadcasted_iota(jnp.int32, st.shape, 0)
                qry = lax.broadcasted_iota(jnp.int32, st.shape, 1)
                update(hh, jnp.where(key <= qry, st, MASK_VALUE), vtb, q_lo, q_lo + tk)
                if q_lo + tk < tq:
                    update(hh, buf[hh, :, q_lo + tk:], vtb, q_lo + tk, None, mbuf[hh, :, q_lo + tk:])
            else:
                update(hh, buf[hh, :, q_lo:], vtb, q_lo, None, mbuf[hh, :, q_lo:])

    per = tq // tk
    assert per % 2 == 0
    st_m = scores_t(jnp.concatenate([km_ref[...], kaugm_ref[...]], axis=1))
    buf_a, buf_b = (sa_sc, ma_sc), (sb_sc, mb_sc)
    put_scores(0, buf_a)
    vmt = vmt_ref[...]
    for hh in range(2):
        update(hh, st_m[hh], vmt)

    def body(t, carry):
        j = 2 * t
        put_scores(j + 1, buf_b)
        consume(buf_a, j)
        put_scores(j + 2, buf_a)
        consume(buf_b, j + 1)
        return carry

    lax.fori_loop(0, (per // 2) * qi, body, 0)
    bufs = (buf_a, buf_b)
    for d in range(per):
        if d + 1 < per:
            put_scores(per * qi + d + 1, bufs[(d + 1) % 2], (d + 1) * tk)
        consume(bufs[d % 2], per * qi + d, diagonal=True, q_lo=d * tk)

    ot = jnp.concatenate([acc_sc[0] / l_sc[0], acc_sc[1] / l_sc[1]], axis=0)
    o2 = ot * ot
    ms0 = jnp.sum(o2[0:HEAD_DIM], axis=0, keepdims=True) * (1.0 / HEAD_DIM)
    ms1 = jnp.sum(o2[HEAD_DIM:], axis=0, keepdims=True) * (1.0 / HEAD_DIM)
    inv = jnp.concatenate([jnp.broadcast_to(lax.rsqrt(ms0 + EPS), (HEAD_DIM, tq)),
                           jnp.broadcast_to(lax.rsqrt(ms1 + EPS), (HEAD_DIM, tq))], axis=0)
    o_ref[0] = (ot * inv * gao_ref[...]).T.astype(BF16)


def _attention(q, qaug, k, kaug, vt, km, kaugm, vmt, gao):
    bsz, slen, w = q.shape
    tq, tk = ATTN_Q_TILE, ATTN_K_TILE
    nq, nk = slen // tq, slen // tk
    return pl.pallas_call(
        functools.partial(_attn_kernel, tq, tk),
        grid=(bsz, HEAD_PAIRS, nq),
        in_specs=[
            pl.BlockSpec((1, tq, LANES), lambda b, p, i: (b, i, p)),
            pl.BlockSpec((1, tq, LANES), lambda b, p, i: (b, i, 0)),
            pl.BlockSpec((1, slen, LANES), lambda b, p, i: (b, 0, p)),
            pl.BlockSpec((1, slen, LANES), lambda b, p, i: (b, 0, 0)),
            pl.BlockSpec((1, nk, LANES, tk), lambda b, p, i: (b, 0, p, 0)),
            pl.BlockSpec((N_META, LANES), lambda b, p, i: (0, p)),
            pl.BlockSpec((N_META, LANES), lambda b, p, i: (0, 0)),
            pl.BlockSpec((LANES, N_META), lambda b, p, i: (p, 0)),
            pl.BlockSpec((LANES, 1), lambda b, p, i: (p, 0)),
        ],
        out_specs=pl.BlockSpec((1, tq, LANES), lambda b, p, i: (b, i, p)),
        out_shape=jax.ShapeDtypeStruct((bsz, slen, w), BF16),
        scratch_shapes=[pltpu.VMEM((2, 1, tq), F32), pltpu.VMEM((2, 1, tq), F32),
                        pltpu.VMEM((2, HEAD_DIM, tq), F32),
                        pltpu.VMEM((2, tk, tq), F32), pltpu.VMEM((2, tk, tq), F32),
                        pltpu.VMEM((2, 1, tq), F32), pltpu.VMEM((2, 1, tq), F32)],
        compiler_params=pltpu.CompilerParams(
            dimension_semantics=("arbitrary", "arbitrary", "arbitrary"), vmem_limit_bytes=VMEM_LIMIT),
        name="fox_attention",
    )(q, qaug, k, kaug, vt, km, kaugm, vmt, gao)


ROUTE_ROWS = 64


def _outproj_kernel(tm, attn_ref, conv_ref, x_ref, wo_ref, gffn_ref, wrt_ref, brc_ref, tri_ref,
                    h_ref, xt_ref, ri_ref, rf_ref, cnt_ref, carry, stage):
    i = pl.program_id(0)

    @pl.when(i == 0)
    def _():
        carry[...] = jnp.zeros(carry.shape, F32)

    mixed = jnp.concatenate([attn_ref[...], conv_ref[...]], axis=1)
    h = x_ref[...] + jnp.dot(mixed, wo_ref[...], preferred_element_type=F32)
    h_ref[...] = h
    ms = jnp.mean(h * h, axis=-1, keepdims=True)
    xt = (h * lax.rsqrt(ms + EPS)) * gffn_ref[...]
    _store_token_rows(xt_ref, xt, tm, stage)

    x_hi = xt.astype(BF16)
    x_lo = (xt - x_hi.astype(F32)).astype(BF16)
    parts = lax.dot_general(wrt_ref[...], jnp.concatenate([x_hi, x_lo], axis=1), (((1,), (1,)), ((), ())),
                            preferred_element_type=F32)
    logits = (parts[0:ROUTE_ROWS] + parts[LANES:LANES + ROUTE_ROWS]) + brc_ref[0:ROUTE_ROWS, 0:1]
    row = lax.broadcasted_iota(jnp.int32, logits.shape, 0)
    rowf = row.astype(F32)
    big = float(LANES)

    def first_argmax(vals, vmax):
        return jnp.min(jnp.where(vals == vmax, rowf, big), axis=0, keepdims=True)

    is_g = row < N_EXPERT_GROUPS
    gl = jnp.where(is_g, logits, MASK_VALUE)
    gmax = jnp.max(gl, axis=0, keepdims=True)
    gidx = first_argmax(gl, gmax)
    g_p = 1.0 / jnp.sum(jnp.where(is_g, jnp.exp(gl - gmax), 0.0), axis=0, keepdims=True)

    base = N_EXPERT_GROUPS + EXPERTS_PER_GROUP * gidx
    in_grp = (rowf >= base) & (rowf < base + EXPERTS_PER_GROUP)
    el = jnp.where(in_grp, logits, MASK_VALUE)
    l1 = jnp.max(el, axis=0, keepdims=True)
    e1 = first_argmax(el, l1)
    el2 = jnp.where(rowf == e1, MASK_VALUE, el)
    l2 = jnp.max(el2, axis=0, keepdims=True)
    e2 = first_argmax(el2, l2)
    zsum = jnp.sum(jnp.where(in_grp, jnp.exp(el - l1), 0.0), axis=0, keepdims=True)
    p1 = 1.0 / zsum
    p2 = jnp.exp(l2 - l1) / zsum
    den = p1 + p2
    w1 = g_p * p1 / den
    w2 = g_p * p2 / den
    id1 = e1 - N_EXPERT_GROUPS
    id2 = e2 - N_EXPERT_GROUPS

    oh1 = rowf == id1
    oh2 = rowf == id2
    oh = jnp.where(oh1 | oh2, 1.0, 0.0)
    before = jnp.dot(oh.astype(BF16), tri_ref[...], preferred_element_type=F32) + carry[:, 0:1]
    rank1 = jnp.sum(jnp.where(oh1, before, 0.0), axis=0, keepdims=True)
    rank2 = jnp.sum(jnp.where(oh2, before, 0.0), axis=0, keepdims=True)
    carry[...] = carry[...] + jnp.sum(oh, axis=1, keepdims=True)
    cnt_ref[...] = carry[...]

    pad_i = jnp.zeros((SUBLANES - 2 * TOP_K, tm), F32)
    ri_ref[0] = jnp.concatenate([id1, id2, rank1, rank2, pad_i], axis=0).astype(jnp.int32)
    pad_f = jnp.zeros((LANES - TOP_K, tm), F32)
    rf_ref[...] = jnp.concatenate([w1, w2, pad_f], axis=0).T[:, 0:SUBLANES]


def _outproj(attn, conv, x, wo, gffn, wrt, brc):
    n = x.shape[0]
    tm = OUT_TILE
    tri = jnp.asarray(np.triu(np.ones((tm, tm), np.float32), k=1), BF16)

    def full(a):
        return pl.BlockSpec(a.shape, lambda i: (0,) * a.ndim)

    rows = lambda width: pl.BlockSpec((tm, width), lambda i: (i, 0))
    return pl.pallas_call(
        functools.partial(_outproj_kernel, tm),
        grid=(n // tm,),
        in_specs=[rows(ATTN_WIDTH), rows(CONV_WIDTH), rows(D_MODEL)] + [
            full(a) for a in (wo, gffn, wrt, brc, tri)],
        out_specs=[rows(D_MODEL), pl.BlockSpec((tm * ROW_SUBLANES, LANES), lambda i: (i, 0)),
                   pl.BlockSpec((1, SUBLANES, tm), lambda i: (i, 0, 0)), rows(SUBLANES),
                   pl.BlockSpec((ROUTE_ROWS, LANES), lambda i: (0, 0))],
        out_shape=[jax.ShapeDtypeStruct((n, D_MODEL), F32),
                   jax.ShapeDtypeStruct((n * ROW_SUBLANES, LANES), BF16),
                   jax.ShapeDtypeStruct((n // tm, SUBLANES, tm), jnp.int32),
                   jax.ShapeDtypeStruct((n, SUBLANES), F32),
                   jax.ShapeDtypeStruct((ROUTE_ROWS, LANES), F32)],
        scratch_shapes=[pltpu.VMEM((ROUTE_ROWS, LANES), F32), pltpu.VMEM((tm * ROW_SUBLANES, LANES), F32)],
        compiler_params=pltpu.CompilerParams(
            dimension_semantics=("arbitrary",), vmem_limit_bytes=VMEM_LIMIT),
        name="outproj_router",
    )(attn, conv, x, wo, gffn, wrt, brc, tri)


def _dest_kernel(rs_ref, ri_ref, dest_ref):
    ri = ri_ref[...]
    experts = ri[:, 0:TOP_K, :]
    start = jnp.zeros_like(experts)
    for e in range(N_EXPERTS):
        start = jnp.where(experts == e, rs_ref[e], start)
    dest_ref[...] = jnp.concatenate([start + ri[:, TOP_K:2 * TOP_K, :],
                                     jnp.zeros((ri.shape[0], SUBLANES - TOP_K, ri.shape[2]), jnp.int32)], axis=1)


def _dest_rows(ri_t, row_start):
    grid_spec = pltpu.PrefetchScalarGridSpec(
        num_scalar_prefetch=1,
        grid=(1,),
        in_specs=[pl.BlockSpec(ri_t.shape, lambda i, rs: (0, 0, 0))],
        out_specs=pl.BlockSpec(ri_t.shape, lambda i, rs: (0, 0, 0)),
    )
    return pl.pallas_call(
        _dest_kernel,
        grid_spec=grid_spec,
        out_shape=jax.ShapeDtypeStruct(ri_t.shape, jnp.int32),
        compiler_params=pltpu.CompilerParams(dimension_semantics=("arbitrary",)),
        name="moe_dest",
    )(row_start, ri_t)


def _tile_copy(src, src_row, dst, dst_row, sem):
    rs = ROW_SUBLANES
    return pltpu.make_async_copy(src.at[pl.ds(pl.multiple_of(src_row * rs, rs), rs), :],
                                 dst.at[pl.ds(pl.multiple_of(dst_row * rs, rs), rs), :], sem)


def _dispatch_kernel(tm, bm, fz_ref, fv_ref, dest_ref, xt_ref, xs_hbm, zeros, sem, zsem):
    i = pl.program_id(0)
    block_rows = bm * ROW_SUBLANES

    def zero_block(z):
        start = pl.multiple_of(fz_ref[z] * block_rows, block_rows)
        return pltpu.make_async_copy(zeros, xs_hbm.at[pl.ds(start, block_rows), :], zsem.at[0])

    @pl.when(i == 0)
    def _():
        zeros[...] = jnp.zeros(zeros.shape, BF16)
        for z in range(fz_ref.shape[0]):
            @pl.when(fv_ref[z] == 1)
            def _():
                zero_block(z).start()
        for z in range(fz_ref.shape[0]):
            @pl.when(fv_ref[z] == 1)
            def _():
                zero_block(z).wait()

    for g in range(0, tm, DMA_GROUP):
        dests = [(r, kk, dest_ref[0, kk, r]) for r in range(g, g + DMA_GROUP) for kk in range(TOP_K)]
        for r, kk, d in dests:
            _tile_copy(xt_ref, r, xs_hbm, d, sem.at[0]).start(priority=kk)
    for kk in range(TOP_K):
        pltpu.make_async_copy(xt_ref, xs_hbm.at[pl.ds(0, tm * ROW_SUBLANES), :], sem.at[0]).wait()


def _route_spec(dest, tm, extra_args):
    per = dest.shape[2] // tm
    nt = dest.shape[0] * per
    if extra_args == 0:
        index = lambda i: (jnp.minimum(i, nt - 1) // per, 0, jnp.minimum(i, nt - 1) % per)
    else:
        index = lambda i, *_: (jnp.minimum(i, nt - 1) // per, 0, jnp.minimum(i, nt - 1) % per)
    return nt, pl.BlockSpec((1, SUBLANES, tm), index, memory_space=pltpu.SMEM)


def _dispatch(xt_rows, dest, fill_blocks, fill_valid, nb):
    tm = DISPATCH_TILE
    bm = EXPERT_ROWS
    nt, dest_spec = _route_spec(dest, tm, 2)
    grid_spec = pltpu.PrefetchScalarGridSpec(
        num_scalar_prefetch=2,
        grid=(nt,),
        in_specs=[dest_spec, pl.BlockSpec((tm * ROW_SUBLANES, LANES), lambda i, fz, fv: (i, 0))],
        out_specs=pl.BlockSpec(memory_space=pl.ANY),
        scratch_shapes=[pltpu.VMEM((bm * ROW_SUBLANES, LANES), BF16), pltpu.SemaphoreType.DMA((1,)),
                        pltpu.SemaphoreType.DMA((1,))],
    )
    return pl.pallas_call(
        functools.partial(_dispatch_kernel, tm, bm),
        grid_spec=grid_spec,
        out_shape=jax.ShapeDtypeStruct((nb * bm * ROW_SUBLANES, LANES), BF16),
        compiler_params=pltpu.CompilerParams(
            dimension_semantics=("arbitrary",), vmem_limit_bytes=VMEM_LIMIT),
        name="moe_dispatch",
    )(fill_blocks, fill_valid, dest, xt_rows)


def _expert_kernel(bm, be_ref, nu_ref, first_ref, nxt_ref, slot_ref, xs_ref, wg_hbm, wu_hbm, wd_hbm, ys_ref,
                   wg_buf, wu_buf, wd_buf, wg_bf, wu_bf, wd_bf, stage_in, stage_out, wsem):
    step = pl.program_id(0)

    def weight_copies(e, slot):
        return [pltpu.make_async_copy(src.at[e], buf.at[slot], wsem.at[m, slot])
                for m, (src, buf) in enumerate(((wg_hbm, wg_buf), (wu_hbm, wu_buf), (wd_hbm, wd_buf)))]

    @pl.when(step == 0)
    def _():
        for c in weight_copies(be_ref[0], 0):
            c.start()

    def one_block(j, xs_blk, ys_blk):
        @pl.when(j < nu_ref[0])
        def _():
            slot = slot_ref[j]

            @pl.when(first_ref[j] == 1)
            def _():
                for c in weight_copies(0, slot):
                    c.wait()

                @pl.when(nxt_ref[j] >= 0)
                def _():
                    for c in weight_copies(nxt_ref[j], 1 - slot):
                        c.start()

                wg_bf[...] = wg_buf[slot].astype(BF16)
                wu_bf[...] = wu_buf[slot].astype(BF16)
                wd_bf[...] = wd_buf[slot].astype(BF16)

            x = _load_token_rows(xs_blk, bm, stage_in).astype(BF16)
            hg = jnp.dot(x, wg_bf[...], preferred_element_type=F32)
            hu = jnp.dot(x, wu_bf[...], preferred_element_type=F32)
            hdn = hg * (1.0 / (1.0 + jnp.exp(-hg))) * hu
            y = jnp.dot(hdn.astype(BF16), wd_bf[...], preferred_element_type=F32)
            _store_token_rows(ys_blk, y, bm, stage_out)

        @pl.when(j >= nu_ref[0])
        def _():
            ys_blk[...] = jnp.zeros(ys_blk.shape, BF16)

    rows = bm * ROW_SUBLANES
    for sub in range(EXPERT_BLOCKS_PER_STEP):
        view = pl.ds(sub * rows, rows)
        one_block(step * EXPERT_BLOCKS_PER_STEP + sub, xs_ref.at[view, :], ys_ref.at[view, :])


def _experts(xs_rows, block_expert, n_used, w_gate, w_up, w_down):
    bm = EXPERT_ROWS
    nb = block_expert.shape[0]
    blk = jnp.arange(nb, dtype=jnp.int32)
    live = blk < n_used[0]
    first = live & ((blk == 0) | (block_expert != jnp.roll(block_expert, 1)))
    slot = (jnp.cumsum(first.astype(jnp.int32)) - 1) % 2
    later_first = first[None, :] & (blk[None, :] > blk[:, None])
    nxt_start = jnp.min(jnp.where(later_first, blk[None, :], nb), axis=1)
    nxt = jnp.sum(jnp.where(blk[None, :] == nxt_start[:, None], block_expert[None, :], 0), axis=1)
    nxt = jnp.where(nxt_start < nb, nxt, -1)
    per = EXPERT_BLOCKS_PER_STEP
    assert nb % per == 0
    step_rows = per * bm * ROW_SUBLANES
    grid_spec = pltpu.PrefetchScalarGridSpec(
        num_scalar_prefetch=5,
        grid=(nb // per,),
        in_specs=[
            pl.BlockSpec((step_rows, LANES), lambda s, be, nu, *_: (jnp.minimum(s, (nu[0] - 1) // per), 0)),
            pl.BlockSpec(memory_space=pl.ANY),
            pl.BlockSpec(memory_space=pl.ANY),
            pl.BlockSpec(memory_space=pl.ANY),
        ],
        out_specs=pl.BlockSpec((step_rows, LANES), lambda s, *_: (s, 0)),
        scratch_shapes=[pltpu.VMEM((2, D_MODEL, D_EXPERT), F32), pltpu.VMEM((2, D_MODEL, D_EXPERT), F32),
                        pltpu.VMEM((2, D_EXPERT, D_MODEL), F32),
                        pltpu.VMEM((D_MODEL, D_EXPERT), BF16), pltpu.VMEM((D_MODEL, D_EXPERT), BF16),
                        pltpu.VMEM((D_EXPERT, D_MODEL), BF16),
                        pltpu.VMEM((bm * ROW_SUBLANES, LANES), F32), pltpu.VMEM((bm * ROW_SUBLANES, LANES), F32),
                        pltpu.SemaphoreType.DMA((3, 2))],
    )
    return pl.pallas_call(
        functools.partial(_expert_kernel, bm),
        grid_spec=grid_spec,
        out_shape=jax.ShapeDtypeStruct((nb * bm * ROW_SUBLANES, LANES), BF16),
        compiler_params=pltpu.CompilerParams(
            dimension_semantics=("arbitrary",), vmem_limit_bytes=VMEM_LIMIT),
        name="moe_experts",
    )(block_expert, n_used, first.astype(jnp.int32), nxt.astype(jnp.int32), slot.astype(jnp.int32),
      xs_rows, w_gate, w_up, w_down)


def _combine_kernel(tm, dest_ref, ys_hbm, h_ref, rf_ref, o_ref, ybuf, stage, sem):
    i = pl.program_id(0)
    nt = pl.num_programs(0) - 1

    @pl.when(i < nt)
    def _():
        slot = i % 2
        for g in range(0, tm, DMA_GROUP):
            dests = [(r, kk, dest_ref[0, kk, r]) for r in range(g, g + DMA_GROUP) for kk in range(TOP_K)]
            for r, kk, d in dests:
                _tile_copy(ys_hbm, d, ybuf.at[kk, slot], r, sem.at[kk, slot]).start(priority=kk)

    @pl.when(i >= 1)
    def _():
        slot = (i - 1) % 2
        for kk in range(TOP_K):
            pltpu.make_async_copy(ys_hbm.at[pl.ds(0, tm * ROW_SUBLANES), :], ybuf.at[kk, slot],
                                  sem.at[kk, slot]).wait()
        rf = rf_ref[...]
        o_ref[...] = h_ref[...] + rf[:, 0:1] * _load_token_rows(ybuf.at[0, slot], tm, stage)
        o_ref[...] += rf[:, 1:2] * _load_token_rows(ybuf.at[1, slot], tm, stage)


def _combine(ys_rows, dest, h, rf):
    n = h.shape[0]
    tm = COMBINE_TILE
    nt, dest_spec = _route_spec(dest, tm, 0)
    prev = lambda i: jnp.maximum(i - 1, 0)
    return pl.pallas_call(
        functools.partial(_combine_kernel, tm),
        grid=(nt + 1,),
        in_specs=[
            dest_spec,
            pl.BlockSpec(memory_space=pl.ANY),
            pl.BlockSpec((tm, D_MODEL), lambda i: (prev(i), 0)),
            pl.BlockSpec((tm, SUBLANES), lambda i: (prev(i), 0)),
        ],
        out_specs=pl.BlockSpec((tm, D_MODEL), lambda i: (prev(i), 0)),
        out_shape=jax.ShapeDtypeStruct((n, D_MODEL), F32),
        scratch_shapes=[pltpu.VMEM((TOP_K, 2, tm * ROW_SUBLANES, LANES), BF16),
                        pltpu.VMEM((tm * ROW_SUBLANES, LANES), F32),
                        pltpu.SemaphoreType.DMA((TOP_K, 2))],
        compiler_params=pltpu.CompilerParams(
            dimension_semantics=("arbitrary",), vmem_limit_bytes=VMEM_LIMIT),
        name="moe_combine",
    )(dest, ys_rows, h, rf)


def _pad_lanes(a, width=LANES):
    return jnp.pad(a, ((0, 0), (0, width - a.shape[-1])))


def _layer(x, meta_tokens, g_mix, w_in, b_forget, q_norm_g, k_norm_g, conv_w, attn_out_g, conv_out_g,
           w_out, g_ffn, w_rg, b_rg, w_re, b_re, w_gate, w_up, w_down):
    bsz, slen, _ = x.shape
    n = bsz * slen
    w = ATTN_WIDTH

    wq, wk, wv, wf, wb, wc, whc = jnp.split(
        w_in, [w, 2 * w, 3 * w, 3 * w + ATTN_HEADS, 4 * w + ATTN_HEADS, 5 * w + ATTN_HEADS], axis=1)
    w1 = jnp.concatenate([wq, wk, wb, wc, whc, _pad_lanes(wf)], axis=1).astype(BF16)
    head_of = np.arange(MXU_DIM) // HEAD_DIM
    gmat = jnp.asarray(head_of[:, None] == head_of[None, :], BF16)
    consts = (
        g_mix.reshape(1, D_MODEL), w1, wv.T.astype(BF16), _pad_lanes(b_forget.reshape(1, ATTN_HEADS)),
        jnp.tile(q_norm_g, ATTN_HEADS).reshape(1, w) * (HEAD_DIM ** -0.5 * LOG2E),
        jnp.tile(k_norm_g, ATTN_HEADS).reshape(1, w),
        jnp.pad(conv_w, ((0, SUBLANES - CONV_K), (0, 0))), conv_out_g.reshape(1, w), gmat)

    xm = jnp.pad(meta_tokens, ((0, META_TILE - N_META), (0, 0)))[None]
    km, vmt, kaugm, cum_m, uc_m = _inproj(xm, True, META_TILE, consts,
                                          jnp.zeros((SUBLANES, w), F32), jnp.zeros((1, LANES), F32))
    km, vmt, kaugm = km[0, :N_META], vmt[0, 0, :, :N_META], kaugm[0, :N_META]
    cum_m = cum_m[0, :N_META]
    halo = jnp.zeros((SUBLANES, w), F32).at[SUBLANES - 2:].set(uc_m[0, N_META - 2:N_META])
    cum0 = cum_m[N_META - 1:N_META]

    q, k, vt, conv, qaug, kaug = _inproj(x, False, IN_TILE, consts, halo, cum0)
    attn = _attention(q, qaug, k, kaug, vt, km, kaugm, vmt, attn_out_g.reshape(w, 1))

    wr = _pad_lanes(jnp.concatenate([w_rg, w_re], axis=1)).T
    wr_hi = wr.astype(BF16)
    wr_lo = (wr - wr_hi.astype(F32)).astype(BF16)
    wrt = jnp.concatenate([jnp.concatenate([wr_hi, wr_hi], axis=1),
                           jnp.concatenate([wr_lo, jnp.zeros_like(wr_lo)], axis=1)], axis=0)
    brc = jnp.broadcast_to(_pad_lanes(jnp.concatenate([b_rg, b_re]).reshape(1, -1)).T, (LANES, LANES))
    h, xt, ri, rf, cnt = _outproj(attn.reshape(n, w), conv.reshape(n, w), x.reshape(n, D_MODEL),
                                  w_out.astype(BF16), g_ffn.reshape(1, D_MODEL), wrt, brc)

    bm = EXPERT_ROWS
    nb = (n * TOP_K) // bm + N_EXPERTS
    counts = cnt[:N_EXPERTS, 0].astype(jnp.int32)
    blocks_e = (counts + bm - 1) // bm
    blk_end = jnp.cumsum(blocks_e)
    row_start = (blk_end - blocks_e) * bm
    blk = jnp.arange(nb, dtype=jnp.int32)
    block_expert = jnp.minimum(jnp.sum(blk_end[None, :] <= blk[:, None], axis=1), N_EXPERTS - 1).astype(jnp.int32)
    n_used = blk_end[-1:].astype(jnp.int32)
    tail = n_used[0] + jnp.arange(N_EXPERTS, dtype=jnp.int32)
    fill_blocks = jnp.clip(jnp.concatenate([blk_end - 1, tail]), 0, nb - 1).astype(jnp.int32)
    fill_valid = jnp.concatenate([counts % bm != 0, tail < nb]).astype(jnp.int32)

    dest = _dest_rows(ri, row_start.astype(jnp.int32))
    xs = _dispatch(xt, dest, fill_blocks, fill_valid, nb)
    ys = _experts(xs, block_expert, n_used, w_gate, w_up, w_down)
    out = _combine(ys, dest, h, rf)
    return out.reshape(bsz, slen, D_MODEL)


def kernel(x, meta_tokens, norm_mix_g, w_in, b_forget, q_norm_g, k_norm_g, conv_w, attn_out_g, conv_out_g,
           w_out, norm_ffn_g, w_router_group, b_router_group, w_router_expert, b_router_expert,
           w_gate, w_up, w_down):
    assert norm_mix_g.shape[0] == 1, "single-layer block"
    return _layer(x, meta_tokens, norm_mix_g[0], w_in[0], b_forget[0], q_norm_g[0], k_norm_g[0], conv_w[0],
                  attn_out_g[0], conv_out_g[0], w_out[0], norm_ffn_g[0], w_router_group[0],
                  b_router_group[0], w_router_expert[0], b_router_expert[0], w_gate[0], w_up[0], w_down[0])
```

```python
import functools

import jax
import jax.numpy as jnp
import numpy as np
from jax import lax
from jax.experimental import pallas as pl
from jax.experimental.pallas import tpu as pltpu

D_MODEL = 1024
N_META = 16
HEAD_DIM = 64
ATTN_HEADS = 8
ATTN_WIDTH = ATTN_HEADS * HEAD_DIM
CONV_WIDTH = D_MODEL - ATTN_WIDTH
CONV_K = 3
N_EXPERT_GROUPS = 4
EXPERTS_PER_GROUP = 8
N_EXPERTS = N_EXPERT_GROUPS * EXPERTS_PER_GROUP
TOP_K = 2
D_EXPERT = 512
EPS = 1e-6
MASK_VALUE = -1e30
LOG2E = 1.4426950408889634
AUG = 6

LANES = 128
SUBLANES = 8
MXU_DIM = 256
HEAD_PAIRS = ATTN_HEADS * HEAD_DIM // LANES
PROJ_PAD_COLS = 5 * ATTN_WIDTH + LANES
VMEM_LIMIT = 56 * 1024 * 1024

IN_TILE = 512
META_TILE = 128
ATTN_Q_TILE = 2048
ATTN_K_TILE = 512
OUT_TILE = 1024
EXPERT_ROWS = 256
EXPERT_BLOCKS_PER_STEP = 4
DISPATCH_TILE = 1024
COMBINE_TILE = 256
DMA_GROUP = 8

F32 = jnp.float32
BF16 = jnp.bfloat16
ROW_SUBLANES = D_MODEL // LANES


def _lane_iota(shape):
    return lax.broadcasted_iota(jnp.int32, shape, len(shape) - 1)


def _store_token_rows(ref, x, rows, stage=None):
    dst = ref if stage is None else stage
    for sl in range(ROW_SUBLANES):
        dst[pl.ds(sl, rows, stride=ROW_SUBLANES), :] = x[:, sl * LANES:(sl + 1) * LANES]
    if stage is not None:
        ref[...] = stage[...].astype(ref.dtype)


def _load_token_rows(ref, rows, stage=None):
    src = ref
    if stage is not None:
        stage[...] = ref[...].astype(F32)
        src = stage
    return jnp.concatenate([src[pl.ds(sl, rows, stride=ROW_SUBLANES), :] for sl in range(ROW_SUBLANES)], axis=1)


def _inproj_kernel(is_meta, tm, x_ref, gmix_ref, w1_ref, wvt_ref, bf_ref, gq_ref, gk_ref, cw_ref, gco_ref,
                   gmat_ref, tri_ref, eq_ref, ek_ref, halo_ref, cum0_ref, *rest):
    if is_meta:
        k_ref, vt_ref, kaug_ref, cum_ref, uc_ref, ucbuf, carry = rest
    else:
        q_ref, k_ref, vt_ref, conv_ref, qaug_ref, kaug_ref, ucbuf, carry = rest
    t = pl.program_id(1)

    @pl.when(t == 0)
    def _():
        ucbuf[0:SUBLANES, :] = halo_ref[...]
        carry[0:1, :] = cum0_ref[...]

    x = x_ref[0]
    ms = jnp.mean(x * x, axis=-1, keepdims=True)
    u = (x * lax.rsqrt(ms + EPS)) * gmix_ref[...]
    ub = u.astype(BF16)
    proj = jnp.dot(ub, w1_ref[...], preferred_element_type=F32)
    vt_ref[0, 0] = lax.dot_general(wvt_ref[...], ub, (((1,), (1,)), ((), ())),
                                   preferred_element_type=F32).astype(BF16)

    def head_norm(z, g):
        z2 = (z * z).astype(BF16)
        half = gmat_ref.shape[0]
        ssq = jnp.concatenate([jnp.dot(z2[:, c:c + half], gmat_ref[...], preferred_element_type=F32)
                               for c in range(0, z.shape[1], half)], axis=1)
        return z * lax.rsqrt(ssq * (1.0 / HEAD_DIM) + EPS) * g

    w = ATTN_WIDTH
    kn = head_norm(proj[:, w:2 * w], gk_ref[...])
    k_ref[0] = kn.astype(BF16)

    z = proj[:, 5 * w:5 * w + LANES] + bf_ref[...]
    ls = jnp.minimum(z, 0.0) - jnp.log1p(jnp.exp(-jnp.abs(z)))
    lane_c = _lane_iota(ls.shape)
    is_head = lane_c < ATTN_HEADS
    ls = jnp.where(is_head, ls, 0.0)

    def pieces(val):
        p_hi = val.astype(BF16).astype(F32)
        rem = val - p_hi
        p_mid = rem.astype(BF16).astype(F32)
        p_lo = (rem - p_mid).astype(BF16).astype(F32)
        return jnp.where(is_head, p_hi,
                         jnp.where(lane_c < 2 * ATTN_HEADS, pltpu.roll(p_mid, ATTN_HEADS, axis=1),
                                   pltpu.roll(p_lo, 2 * ATTN_HEADS, axis=1))).astype(BF16)

    cs3 = jnp.dot(tri_ref[...], pieces(ls), preferred_element_type=F32)
    cs = cs3 + pltpu.roll(cs3, LANES - ATTN_HEADS, axis=1) + pltpu.roll(cs3, LANES - 2 * ATTN_HEADS, axis=1)
    cum = jnp.where(is_head, cs, 0.0) + carry[0:1, :]
    carry[0:1, :] = cum[tm - 1:tm, :]

    packed = pieces(cum * LOG2E)
    in_aug = lane_c < AUG * ATTN_HEADS
    ones_k = in_aug & ((lane_c % AUG) < AUG // 2)
    kaug = jnp.where(ones_k, 1.0, -jnp.dot(packed, ek_ref[...], preferred_element_type=F32))
    kaug_ref[0] = kaug.astype(BF16)

    uc = proj[:, 3 * w:4 * w] * proj[:, 4 * w:5 * w]
    ucbuf[SUBLANES:SUBLANES + tm, :] = uc
    uc1 = ucbuf[SUBLANES - 1:SUBLANES - 1 + tm, :]
    uc2 = ucbuf[SUBLANES - 2:SUBLANES - 2 + tm, :]
    ucbuf[0:SUBLANES, :] = uc[tm - SUBLANES:tm, :]

    if is_meta:
        cum_ref[0] = cum
        uc_ref[0] = uc
        return

    qn = head_norm(proj[:, 0:w], gq_ref[...])
    q_ref[0] = qn.astype(BF16)
    y = cw_ref[0:1, :] * uc2 + cw_ref[1:2, :] * uc1 + cw_ref[2:3, :] * uc
    conv = proj[:, 2 * w:3 * w] * y
    conv_ref[0] = head_norm(conv, gco_ref[...]).astype(BF16)
    ones_q = in_aug & ((lane_c % AUG) >= AUG // 2)
    qaug = jnp.where(ones_q, 1.0, jnp.dot(packed, eq_ref[...], preferred_element_type=F32))
    qaug_ref[0] = qaug.astype(BF16)


def _inproj(x, is_meta, tm, consts, halo, cum0):
    bsz, tlen, _ = x.shape
    nt = tlen // tm
    gmix, w1, wvt, bfp, gq, gk, cw, gco, gmat = consts
    tri = jnp.asarray(np.tril(np.ones((tm, tm), np.float32)), BF16)
    w = ATTN_WIDTH
    src = np.arange(LANES)
    dst = np.arange(LANES)
    piece, head = src // ATTN_HEADS, src % ATTN_HEADS
    valid = src < 3 * ATTN_HEADS
    eq = jnp.asarray(valid[:, None] & (dst[None, :] == (AUG * head + piece)[:, None]), BF16)
    ek = jnp.asarray(valid[:, None] & (dst[None, :] == (AUG * head + AUG // 2 + piece)[:, None]), BF16)

    def full(a):
        return pl.BlockSpec(a.shape, lambda b, t: (0,) * a.ndim)

    in_specs = [pl.BlockSpec((1, tm, D_MODEL), lambda b, t: (b, t, 0))] + [
        full(a) for a in (gmix, w1, wvt, bfp, gq, gk, cw, gco, gmat, tri, eq, ek, halo, cum0)]
    tok = lambda width: pl.BlockSpec((1, tm, width), lambda b, t: (b, t, 0))
    vt_spec = pl.BlockSpec((1, 1, w, tm), lambda b, t: (b, t, 0, 0))
    if is_meta:
        out_shape = [jax.ShapeDtypeStruct((bsz, tlen, w), BF16),
                     jax.ShapeDtypeStruct((bsz, nt, w, tm), BF16),
                     jax.ShapeDtypeStruct((bsz, tlen, LANES), BF16),
                     jax.ShapeDtypeStruct((bsz, tlen, LANES), F32),
                     jax.ShapeDtypeStruct((bsz, tlen, w), F32)]
        out_specs = [tok(w), vt_spec, tok(LANES), tok(LANES), tok(w)]
    else:
        out_shape = [jax.ShapeDtypeStruct((bsz, tlen, w), BF16)] * 2 + [
            jax.ShapeDtypeStruct((bsz, nt, w, tm), BF16),
            jax.ShapeDtypeStruct((bsz, tlen, w), BF16),
            jax.ShapeDtypeStruct((bsz, tlen, LANES), BF16),
            jax.ShapeDtypeStruct((bsz, tlen, LANES), BF16)]
        out_specs = [tok(w), tok(w), vt_spec, tok(w), tok(LANES), tok(LANES)]
    return pl.pallas_call(
        functools.partial(_inproj_kernel, is_meta, tm),
        grid=(bsz, nt),
        in_specs=in_specs,
        out_specs=out_specs,
        out_shape=out_shape,
        scratch_shapes=[pltpu.VMEM((tm + SUBLANES, w), F32), pltpu.VMEM((SUBLANES, LANES), F32)],
        compiler_params=pltpu.CompilerParams(
            dimension_semantics=("arbitrary", "arbitrary"), vmem_limit_bytes=VMEM_LIMIT),
        name="inproj_meta" if is_meta else "inproj",
    )(x, gmix, w1, wvt, bfp, gq, gk, cw, gco, gmat, tri, eq, ek, halo, cum0)


def _attn_kernel(tq, tk, q_ref, qaug_ref, k_ref, kaug_ref, vt_ref, km_ref, kaugm_ref, vmt_ref, gao_ref,
                 o_ref, m_sc, l_sc, acc_sc, sa_sc, sb_sc, ma_sc, mb_sc):
    hp = pl.program_id(1)
    qi = pl.program_id(2)
    q = q_ref[0]
    qaug = qaug_ref[0]
    lane = _lane_iota(q.shape)
    first = lane < HEAD_DIM
    zero = jnp.zeros_like(q)
    qcat = []
    for hh in range(2):
        qh = jnp.where(first, q, zero) if hh == 0 else jnp.where(first, zero, q)
        lo = AUG * (2 * hp + hh)
        qa = jnp.where((lane >= lo) & (lane < lo + AUG), qaug, zero)
        qcat.append(jnp.concatenate([qh, qa], axis=1))

    m_sc[...] = jnp.full(m_sc.shape, MASK_VALUE, F32)
    l_sc[...] = jnp.zeros(l_sc.shape, F32)
    acc_sc[...] = jnp.zeros(acc_sc.shape, F32)

    def scores_t(kcat, q_lo=0):
        return tuple(lax.dot_general(kcat, qcat[hh][q_lo:], (((1,), (1,)), ((), ())), preferred_element_type=F32)
                     for hh in range(2))

    def update(hh, st, vtb, q_lo=0, q_hi=None, block_max=None):
        qs = slice(q_lo, q_hi)
        m_old = m_sc[hh, :, qs]
        if block_max is None:
            block_max = jnp.max(st, axis=0, keepdims=True)
        m_new = jnp.maximum(m_old, block_max)
        alpha = jnp.exp2(m_old - m_new)
        p = jnp.exp2(st - m_new)
        l_sc[hh, :, qs] = alpha * l_sc[hh, :, qs] + jnp.sum(p, axis=0, keepdims=True)
        vh = vtb[hh * HEAD_DIM:(hh + 1) * HEAD_DIM, :]
        acc_sc[hh, :, qs] = alpha * acc_sc[hh, :, qs] + jnp.dot(vh, p.astype(BF16), preferred_element_type=F32)
        m_sc[hh, :, qs] = m_new

    def key_block(j):
        start = pl.multiple_of(j * tk, tk)
        return jnp.concatenate([k_ref[0, pl.ds(start, tk), :], kaug_ref[0, pl.ds(start, tk), :]], axis=1)

    def put_scores(j, bufs, q_lo=0):
        buf, mbuf = bufs
        sts = scores_t(key_block(j), q_lo)
        for hh in range(2):
            buf[hh, :, q_lo:] = sts[hh]
            mbuf[hh, :, q_lo:] = jnp.max(sts[hh], axis=0, keepdims=True)

    def consume(bufs, j, diagonal=False, q_lo=0):
        buf, mbuf = bufs
        vtb = vt_ref[0, j]
        for hh in range(2):
            if diagonal:
                st = buf[hh, :, q_lo:q_lo + tk]
                key = lax.broadcasted_iota(jnp.int32, st.shape, 0)
                qry = lax.broadcasted_iota(jnp.int32, st.shape, 1)
                update(hh, jnp.where(key <= qry, st, MASK_VALUE), vtb, q_lo, q_lo + tk)
                if q_lo + tk < tq:
                    update(hh, buf[hh, :, q_lo + tk:], vtb, q_lo + tk, None, mbuf[hh, :, q_lo + tk:])
            else:
                update(hh, buf[hh, :, q_lo:], vtb, q_lo, None, mbuf[hh, :, q_lo:])

    per = tq // tk
    assert per % 2 == 0
    st_m = scores_t(jnp.concatenate([km_ref[...], kaugm_ref[...]], axis=1))
    buf_a, buf_b = (sa_sc, ma_sc), (sb_sc, mb_sc)
    put_scores(0, buf_a)
    vmt = vmt_ref[...]
    for hh in range(2):
        update(hh, st_m[hh], vmt)

    def body(t, carry):
        j = 2 * t
        put_scores(j + 1, buf_b)
        consume(buf_a, j)
        put_scores(j + 2, buf_a)
        consume(buf_b, j + 1)
        return carry

    lax.fori_loop(0, (per // 2) * qi, body, 0)
    bufs = (buf_a, buf_b)
    for d in range(per):
        if d + 1 < per:
            put_scores(per * qi + d + 1, bufs[(d + 1) % 2], (d + 1) * tk)
        consume(bufs[d % 2], per * qi + d, diagonal=True, q_lo=d * tk)

    ot = jnp.concatenate([acc_sc[0] / l_sc[0], acc_sc[1] / l_sc[1]], axis=0)
    o2 = ot * ot
    ms0 = jnp.sum(o2[0:HEAD_DIM], axis=0, keepdims=True) * (1.0 / HEAD_DIM)
    ms1 = jnp.sum(o2[HEAD_DIM:], axis=0, keepdims=True) * (1.0 / HEAD_DIM)
    inv = jnp.concatenate([jnp.broadcast_to(lax.rsqrt(ms0 + EPS), (HEAD_DIM, tq)),
                           jnp.broadcast_to(lax.rsqrt(ms1 + EPS), (HEAD_DIM, tq))], axis=0)
    o_ref[0] = (ot * inv * gao_ref[...]).T.astype(BF16)


def _attention(q, qaug, k, kaug, vt, km, kaugm, vmt, gao):
    bsz, slen, w = q.shape
    tq, tk = ATTN_Q_TILE, ATTN_K_TILE
    nq, nk = slen // tq, slen // tk
    return pl.pallas_call(
        functools.partial(_attn_kernel, tq, tk),
        grid=(bsz, HEAD_PAIRS, nq),
        in_specs=[
            pl.BlockSpec((1, tq, LANES), lambda b, p, i: (b, i, p)),
            pl.BlockSpec((1, tq, LANES), lambda b, p, i: (b, i, 0)),
            pl.BlockSpec((1, slen, LANES), lambda b, p, i: (b, 0, p)),
            pl.BlockSpec((1, slen, LANES), lambda b, p, i: (b, 0, 0)),
            pl.BlockSpec((1, nk, LANES, tk), lambda b, p, i: (b, 0, p, 0)),
            pl.BlockSpec((N_META, LANES), lambda b, p, i: (0, p)),
            pl.BlockSpec((N_META, LANES), lambda b, p, i: (0, 0)),
            pl.BlockSpec((LANES, N_META), lambda b, p, i: (p, 0)),
            pl.BlockSpec((LANES, 1), lambda b, p, i: (p, 0)),
        ],
        out_specs=pl.BlockSpec((1, tq, LANES), lambda b, p, i: (b, i, p)),
        out_shape=jax.ShapeDtypeStruct((bsz, slen, w), BF16),
        scratch_shapes=[pltpu.VMEM((2, 1, tq), F32), pltpu.VMEM((2, 1, tq), F32),
                        pltpu.VMEM((2, HEAD_DIM, tq), F32),
                        pltpu.VMEM((2, tk, tq), F32), pltpu.VMEM((2, tk, tq), F32),
                        pltpu.VMEM((2, 1, tq), F32), pltpu.VMEM((2, 1, tq), F32)],
        compiler_params=pltpu.CompilerParams(
            dimension_semantics=("arbitrary", "arbitrary", "arbitrary"), vmem_limit_bytes=VMEM_LIMIT),
        name="fox_attention",
    )(q, qaug, k, kaug, vt, km, kaugm, vmt, gao)


ROUTE_ROWS = 64


def _outproj_kernel(tm, attn_ref, conv_ref, x_ref, wo_ref, gffn_ref, wrt_ref, brc_ref, tri_ref,
                    h_ref, xt_ref, ri_ref, rf_ref, cnt_ref, carry, stage):
    i = pl.program_id(0)

    @pl.when(i == 0)
    def _():
        carry[...] = jnp.zeros(carry.shape, F32)

    mixed = jnp.concatenate([attn_ref[...], conv_ref[...]], axis=1)
    h = x_ref[...] + jnp.dot(mixed, wo_ref[...], preferred_element_type=F32)
    h_ref[...] = h
    ms = jnp.mean(h * h, axis=-1, keepdims=True)
    xt = (h * lax.rsqrt(ms + EPS)) * gffn_ref[...]
    _store_token_rows(xt_ref, xt, tm, stage)

    x_hi = xt.astype(BF16)
    x_lo = (xt - x_hi.astype(F32)).astype(BF16)
    parts = lax.dot_general(wrt_ref[...], jnp.concatenate([x_hi, x_lo], axis=1), (((1,), (1,)), ((), ())),
                            preferred_element_type=F32)
    logits = (parts[0:ROUTE_ROWS] + parts[LANES:LANES + ROUTE_ROWS]) + brc_ref[0:ROUTE_ROWS, 0:1]
    row = lax.broadcasted_iota(jnp.int32, logits.shape, 0)
    rowf = row.astype(F32)
    big = float(LANES)

    def first_argmax(vals, vmax):
        return jnp.min(jnp.where(vals == vmax, rowf, big), axis=0, keepdims=True)

    is_g = row < N_EXPERT_GROUPS
    gl = jnp.where(is_g, logits, MASK_VALUE)
    gmax = jnp.max(gl, axis=0, keepdims=True)
    gidx = first_argmax(gl, gmax)
    g_p = 1.0 / jnp.sum(jnp.where(is_g, jnp.exp(gl - gmax), 0.0), axis=0, keepdims=True)

    base = N_EXPERT_GROUPS + EXPERTS_PER_GROUP * gidx
    in_grp = (rowf >= base) & (rowf < base + EXPERTS_PER_GROUP)
    el = jnp.where(in_grp, logits, MASK_VALUE)
    l1 = jnp.max(el, axis=0, keepdims=True)
    e1 = first_argmax(el, l1)
    el2 = jnp.where(rowf == e1, MASK_VALUE, el)
    l2 = jnp.max(el2, axis=0, keepdims=True)
    e2 = first_argmax(el2, l2)
    zsum = jnp.sum(jnp.where(in_grp, jnp.exp(el - l1), 0.0), axis=0, keepdims=True)
    p1 = 1.0 / zsum
    p2 = jnp.exp(l2 - l1) / zsum
    den = p1 + p2
    w1 = g_p * p1 / den
    w2 = g_p * p2 / den
    id1 = e1 - N_EXPERT_GROUPS
    id2 = e2 - N_EXPERT_GROUPS

    oh1 = rowf == id1
    oh2 = rowf == id2
    oh = jnp.where(oh1 | oh2, 1.0, 0.0)
    before = jnp.dot(oh.astype(BF16), tri_ref[...], preferred_element_type=F32) + carry[:, 0:1]
    rank1 = jnp.sum(jnp.where(oh1, before, 0.0), axis=0, keepdims=True)
    rank2 = jnp.sum(jnp.where(oh2, before, 0.0), axis=0, keepdims=True)
    carry[...] = carry[...] + jnp.sum(oh, axis=1, keepdims=True)
    cnt_ref[...] = carry[...]

    pad_i = jnp.zeros((SUBLANES - 2 * TOP_K, tm), F32)
    ri_ref[0] = jnp.concatenate([id1, id2, rank1, rank2, pad_i], axis=0).astype(jnp.int32)
    pad_f = jnp.zeros((LANES - TOP_K, tm), F32)
    rf_ref[...] = jnp.concatenate([w1, w2, pad_f], axis=0).T[:, 0:SUBLANES]


def _outproj(attn, conv, x, wo, gffn, wrt, brc):
    n = x.shape[0]
    tm = OUT_TILE
    tri = jnp.asarray(np.triu(np.ones((tm, tm), np.float32), k=1), BF16)

    def full(a):
        return pl.BlockSpec(a.shape, lambda i: (0,) * a.ndim)

    rows = lambda width: pl.BlockSpec((tm, width), lambda i: (i, 0))
    return pl.pallas_call(
        functools.partial(_outproj_kernel, tm),
        grid=(n // tm,),
        in_specs=[rows(ATTN_WIDTH), rows(CONV_WIDTH), rows(D_MODEL)] + [
            full(a) for a in (wo, gffn, wrt, brc, tri)],
        out_specs=[rows(D_MODEL), pl.BlockSpec((tm * ROW_SUBLANES, LANES), lambda i: (i, 0)),
                   pl.BlockSpec((1, SUBLANES, tm), lambda i: (i, 0, 0)), rows(SUBLANES),
                   pl.BlockSpec((ROUTE_ROWS, LANES), lambda i: (0, 0))],
        out_shape=[jax.ShapeDtypeStruct((n, D_MODEL), F32),
                   jax.ShapeDtypeStruct((n * ROW_SUBLANES, LANES), BF16),
                   jax.ShapeDtypeStruct((n // tm, SUBLANES, tm), jnp.int32),
                   jax.ShapeDtypeStruct((n, SUBLANES), F32),
                   jax.ShapeDtypeStruct((ROUTE_ROWS, LANES), F32)],
        scratch_shapes=[pltpu.VMEM((ROUTE_ROWS, LANES), F32), pltpu.VMEM((tm * ROW_SUBLANES, LANES), F32)],
        compiler_params=pltpu.CompilerParams(
            dimension_semantics=("arbitrary",), vmem_limit_bytes=VMEM_LIMIT),
        name="outproj_router",
    )(attn, conv, x, wo, gffn, wrt, brc, tri)


def _dest_kernel(rs_ref, ri_ref, dest_ref):
    ri = ri_ref[...]
    experts = ri[:, 0:TOP_K, :]
    start = jnp.zeros_like(experts)
    for e in range(N_EXPERTS):
        start = jnp.where(experts == e, rs_ref[e], start)
    dest_ref[...] = jnp.concatenate([start + ri[:, TOP_K:2 * TOP_K, :],
                                     jnp.zeros((ri.shape[0], SUBLANES - TOP_K, ri.shape[2]), jnp.int32)], axis=1)


def _dest_rows(ri_t, row_start):
    grid_spec = pltpu.PrefetchScalarGridSpec(
        num_scalar_prefetch=1,
        grid=(1,),
        in_specs=[pl.BlockSpec(ri_t.shape, lambda i, rs: (0, 0, 0))],
        out_specs=pl.BlockSpec(ri_t.shape, lambda i, rs: (0, 0, 0)),
    )
    return pl.pallas_call(
        _dest_kernel,
        grid_spec=grid_spec,
        out_shape=jax.ShapeDtypeStruct(ri_t.shape, jnp.int32),
        compiler_params=pltpu.CompilerParams(dimension_semantics=("arbitrary",)),
        name="moe_dest",
    )(row_start, ri_t)


def _tile_copy(src, src_row, dst, dst_row, sem):
    rs = ROW_SUBLANES
    return pltpu.make_async_copy(src.at[pl.ds(pl.multiple_of(src_row * rs, rs), rs), :],
                                 dst.at[pl.ds(pl.multiple_of(dst_row * rs, rs), rs), :], sem)


def _dispatch_kernel(tm, bm, fz_ref, fv_ref, dest_ref, xt_ref, xs_hbm, zeros, sem, zsem):
    i = pl.program_id(0)
    block_rows = bm * ROW_SUBLANES

    def zero_block(z):
        start = pl.multiple_of(fz_ref[z] * block_rows, block_rows)
        return pltpu.make_async_copy(zeros, xs_hbm.at[pl.ds(start, block_rows), :], zsem.at[0])

    @pl.when(i == 0)
    def _():
        zeros[...] = jnp.zeros(zeros.shape, BF16)
        for z in range(fz_ref.shape[0]):
            @pl.when(fv_ref[z] == 1)
            def _():
                zero_block(z).start()
        for z in range(fz_ref.shape[0]):
            @pl.when(fv_ref[z] == 1)
            def _():
                zero_block(z).wait()

    for g in range(0, tm, DMA_GROUP):
        dests = [(r, kk, dest_ref[0, kk, r]) for r in range(g, g + DMA_GROUP) for kk in range(TOP_K)]
        for r, kk, d in dests:
            _tile_copy(xt_ref, r, xs_hbm, d, sem.at[0]).start(priority=kk)
    for kk in range(TOP_K):
        pltpu.make_async_copy(xt_ref, xs_hbm.at[pl.ds(0, tm * ROW_SUBLANES), :], sem.at[0]).wait()


def _route_spec(dest, tm, extra_args):
    per = dest.shape[2] // tm
    nt = dest.shape[0] * per
    if extra_args == 0:
        index = lambda i: (jnp.minimum(i, nt - 1) // per, 0, jnp.minimum(i, nt - 1) % per)
    else:
        index = lambda i, *_: (jnp.minimum(i, nt - 1) // per, 0, jnp.minimum(i, nt - 1) % per)
    return nt, pl.BlockSpec((1, SUBLANES, tm), index, memory_space=pltpu.SMEM)


def _dispatch(xt_rows, dest, fill_blocks, fill_valid, nb):
    tm = DISPATCH_TILE
    bm = EXPERT_ROWS
    nt, dest_spec = _route_spec(dest, tm, 2)
    grid_spec = pltpu.PrefetchScalarGridSpec(
        num_scalar_prefetch=2,
        grid=(nt,),
        in_specs=[dest_spec, pl.BlockSpec((tm * ROW_SUBLANES, LANES), lambda i, fz, fv: (i, 0))],
        out_specs=pl.BlockSpec(memory_space=pl.ANY),
        scratch_shapes=[pltpu.VMEM((bm * ROW_SUBLANES, LANES), BF16), pltpu.SemaphoreType.DMA((1,)),
                        pltpu.SemaphoreType.DMA((1,))],
    )
    return pl.pallas_call(
        functools.partial(_dispatch_kernel, tm, bm),
        grid_spec=grid_spec,
        out_shape=jax.ShapeDtypeStruct((nb * bm * ROW_SUBLANES, LANES), BF16),
        compiler_params=pltpu.CompilerParams(
            dimension_semantics=("arbitrary",), vmem_limit_bytes=VMEM_LIMIT),
        name="moe_dispatch",
    )(fill_blocks, fill_valid, dest, xt_rows)


def _expert_kernel(bm, be_ref, nu_ref, first_ref, nxt_ref, slot_ref, xs_ref, wg_hbm, wu_hbm, wd_hbm, ys_ref,
                   wg_buf, wu_buf, wd_buf, wg_bf, wu_bf, wd_bf, stage_in, wsem):
    step = pl.program_id(0)

    def weight_copies(e, slot):
        return [pltpu.make_async_copy(src.at[e], buf.at[slot], wsem.at[m, slot])
                for m, (src, buf) in enumerate(((wg_hbm, wg_buf), (wu_hbm, wu_buf), (wd_hbm, wd_buf)))]

    @pl.when(step == 0)
    def _():
        for c in weight_copies(be_ref[0], 0):
            c.start()

    def one_block(j, xs_blk, ys_blk):
        @pl.when(j < nu_ref[0])
        def _():
            slot = slot_ref[j]

            @pl.when(first_ref[j] == 1)
            def _():
                for c in weight_copies(0, slot):
                    c.wait()

                @pl.when(nxt_ref[j] >= 0)
                def _():
                    for c in weight_copies(nxt_ref[j], 1 - slot):
                        c.start()

                wg_bf[...] = wg_buf[slot].astype(BF16)
                wu_bf[...] = wu_buf[slot].astype(BF16)
                wd_bf[...] = wd_buf[slot].astype(BF16)

            x = _load_token_rows(xs_blk, bm, stage_in).astype(BF16)
            hg = jnp.dot(x, wg_bf[...], preferred_element_type=F32)
            hu = jnp.dot(x, wu_bf[...], preferred_element_type=F32)
            hdn = hg * (1.0 / (1.0 + jnp.exp(-hg))) * hu
            y = jnp.dot(hdn.astype(BF16), wd_bf[...], preferred_element_type=F32)
            _store_token_rows(ys_blk, y, bm)

        @pl.when(j >= nu_ref[0])
        def _():
            ys_blk[...] = jnp.zeros(ys_blk.shape, F32)

    rows = bm * ROW_SUBLANES
    for sub in range(EXPERT_BLOCKS_PER_STEP):
        view = pl.ds(sub * rows, rows)
        one_block(step * EXPERT_BLOCKS_PER_STEP + sub, xs_ref.at[view, :], ys_ref.at[view, :])


def _experts(xs_rows, block_expert, n_used, w_gate, w_up, w_down):
    bm = EXPERT_ROWS
    nb = block_expert.shape[0]
    blk = jnp.arange(nb, dtype=jnp.int32)
    live = blk < n_used[0]
    first = live & ((blk == 0) | (block_expert != jnp.roll(block_expert, 1)))
    slot = (jnp.cumsum(first.astype(jnp.int32)) - 1) % 2
    later_first = first[None, :] & (blk[None, :] > blk[:, None])
    nxt_start = jnp.min(jnp.where(later_first, blk[None, :], nb), axis=1)
    nxt = jnp.sum(jnp.where(blk[None, :] == nxt_start[:, None], block_expert[None, :], 0), axis=1)
    nxt = jnp.where(nxt_start < nb, nxt, -1)
    per = EXPERT_BLOCKS_PER_STEP
    assert nb % per == 0
    step_rows = per * bm * ROW_SUBLANES
    grid_spec = pltpu.PrefetchScalarGridSpec(
        num_scalar_prefetch=5,
        grid=(nb // per,),
        in_specs=[
            pl.BlockSpec((step_rows, LANES), lambda s, be, nu, *_: (jnp.minimum(s, (nu[0] - 1) // per), 0)),
            pl.BlockSpec(memory_space=pl.ANY),
            pl.BlockSpec(memory_space=pl.ANY),
            pl.BlockSpec(memory_space=pl.ANY),
        ],
        out_specs=pl.BlockSpec((step_rows, LANES), lambda s, *_: (s, 0)),
        scratch_shapes=[pltpu.VMEM((2, D_MODEL, D_EXPERT), F32), pltpu.VMEM((2, D_MODEL, D_EXPERT), F32),
                        pltpu.VMEM((2, D_EXPERT, D_MODEL), F32),
                        pltpu.VMEM((D_MODEL, D_EXPERT), BF16), pltpu.VMEM((D_MODEL, D_EXPERT), BF16),
                        pltpu.VMEM((D_EXPERT, D_MODEL), BF16),
                        pltpu.VMEM((bm * ROW_SUBLANES, LANES), F32), pltpu.SemaphoreType.DMA((3, 2))],
    )
    return pl.pallas_call(
        functools.partial(_expert_kernel, bm),
        grid_spec=grid_spec,
        out_shape=jax.ShapeDtypeStruct((nb * bm * ROW_SUBLANES, LANES), F32),
        compiler_params=pltpu.CompilerParams(
            dimension_semantics=("arbitrary",), vmem_limit_bytes=VMEM_LIMIT),
        name="moe_experts",
    )(block_expert, n_used, first.astype(jnp.int32), nxt.astype(jnp.int32), slot.astype(jnp.int32),
      xs_rows, w_gate, w_up, w_down)


def _combine_kernel(tm, dest_ref, ys_hbm, h_ref, rf_ref, o_ref, ybuf, sem):
    i = pl.program_id(0)
    nt = pl.num_programs(0) - 1

    @pl.when(i < nt)
    def _():
        slot = i % 2
        for g in range(0, tm, DMA_GROUP):
            dests = [(r, kk, dest_ref[0, kk, r]) for r in range(g, g + DMA_GROUP) for kk in range(TOP_K)]
            for r, kk, d in dests:
                _tile_copy(ys_hbm, d, ybuf.at[kk, slot], r, sem.at[kk, slot]).start(priority=kk)

    @pl.when(i >= 1)
    def _():
        slot = (i - 1) % 2
        for kk in range(TOP_K):
            pltpu.make_async_copy(ys_hbm.at[pl.ds(0, tm * ROW_SUBLANES), :], ybuf.at[kk, slot],
                                  sem.at[kk, slot]).wait()
        rf = rf_ref[...]
        o_ref[...] = (h_ref[...] + rf[:, 0:1] * _load_token_rows(ybuf.at[0, slot], tm)
                      + rf[:, 1:2] * _load_token_rows(ybuf.at[1, slot], tm))


def _combine(ys_rows, dest, h, rf):
    n = h.shape[0]
    tm = COMBINE_TILE
    nt, dest_spec = _route_spec(dest, tm, 0)
    prev = lambda i: jnp.maximum(i - 1, 0)
    return pl.pallas_call(
        functools.partial(_combine_kernel, tm),
        grid=(nt + 1,),
        in_specs=[
            dest_spec,
            pl.BlockSpec(memory_space=pl.ANY),
            pl.BlockSpec((tm, D_MODEL), lambda i: (prev(i), 0)),
            pl.BlockSpec((tm, SUBLANES), lambda i: (prev(i), 0)),
        ],
        out_specs=pl.BlockSpec((tm, D_MODEL), lambda i: (prev(i), 0)),
        out_shape=jax.ShapeDtypeStruct((n, D_MODEL), F32),
        scratch_shapes=[pltpu.VMEM((TOP_K, 2, tm * ROW_SUBLANES, LANES), F32),
                        pltpu.SemaphoreType.DMA((TOP_K, 2))],
        compiler_params=pltpu.CompilerParams(
            dimension_semantics=("arbitrary",), vmem_limit_bytes=VMEM_LIMIT),
        name="moe_combine",
    )(dest, ys_rows, h, rf)


def _pad_lanes(a, width=LANES):
    return jnp.pad(a, ((0, 0), (0, width - a.shape[-1])))


def _layer(x, meta_tokens, g_mix, w_in, b_forget, q_norm_g, k_norm_g, conv_w, attn_out_g, conv_out_g,
           w_out, g_ffn, w_rg, b_rg, w_re, b_re, w_gate, w_up, w_down):
    bsz, slen, _ = x.shape
    n = bsz * slen
    w = ATTN_WIDTH

    wq, wk, wv, wf, wb, wc, whc = jnp.split(
        w_in, [w, 2 * w, 3 * w, 3 * w + ATTN_HEADS, 4 * w + ATTN_HEADS, 5 * w + ATTN_HEADS], axis=1)
    w1 = jnp.concatenate([wq, wk, wb, wc, whc, _pad_lanes(wf)], axis=1).astype(BF16)
    head_of = np.arange(MXU_DIM) // HEAD_DIM
    gmat = jnp.asarray(head_of[:, None] == head_of[None, :], BF16)
    consts = (
        g_mix.reshape(1, D_MODEL), w1, wv.T.astype(BF16), _pad_lanes(b_forget.reshape(1, ATTN_HEADS)),
        jnp.tile(q_norm_g, ATTN_HEADS).reshape(1, w) * (HEAD_DIM ** -0.5 * LOG2E),
        jnp.tile(k_norm_g, ATTN_HEADS).reshape(1, w),
        jnp.pad(conv_w, ((0, SUBLANES - CONV_K), (0, 0))), conv_out_g.reshape(1, w), gmat)

    xm = jnp.pad(meta_tokens, ((0, META_TILE - N_META), (0, 0)))[None]
    km, vmt, kaugm, cum_m, uc_m = _inproj(xm, True, META_TILE, consts,
                                          jnp.zeros((SUBLANES, w), F32), jnp.zeros((1, LANES), F32))
    km, vmt, kaugm = km[0, :N_META], vmt[0, 0, :, :N_META], kaugm[0, :N_META]
    cum_m = cum_m[0, :N_META]
    halo = jnp.zeros((SUBLANES, w), F32).at[SUBLANES - 2:].set(uc_m[0, N_META - 2:N_META])
    cum0 = cum_m[N_META - 1:N_META]

    q, k, vt, conv, qaug, kaug = _inproj(x, False, IN_TILE, consts, halo, cum0)
    attn = _attention(q, qaug, k, kaug, vt, km, kaugm, vmt, attn_out_g.reshape(w, 1))

    wr = _pad_lanes(jnp.concatenate([w_rg, w_re], axis=1)).T
    wr_hi = wr.astype(BF16)
    wr_lo = (wr - wr_hi.astype(F32)).astype(BF16)
    wrt = jnp.concatenate([jnp.concatenate([wr_hi, wr_hi], axis=1),
                           jnp.concatenate([wr_lo, jnp.zeros_like(wr_lo)], axis=1)], axis=0)
    brc = jnp.broadcast_to(_pad_lanes(jnp.concatenate([b_rg, b_re]).reshape(1, -1)).T, (LANES, LANES))
    h, xt, ri, rf, cnt = _outproj(attn.reshape(n, w), conv.reshape(n, w), x.reshape(n, D_MODEL),
                                  w_out.astype(BF16), g_ffn.reshape(1, D_MODEL), wrt, brc)

    bm = EXPERT_ROWS
    nb = (n * TOP_K) // bm + N_EXPERTS
    counts = cnt[:N_EXPERTS, 0].astype(jnp.int32)
    blocks_e = (counts + bm - 1) // bm
    blk_end = jnp.cumsum(blocks_e)
    row_start = (blk_end - blocks_e) * bm
    blk = jnp.arange(nb, dtype=jnp.int32)
    block_expert = jnp.minimum(jnp.sum(blk_end[None, :] <= blk[:, None], axis=1), N_EXPERTS - 1).astype(jnp.int32)
    n_used = blk_end[-1:].astype(jnp.int32)
    tail = n_used[0] + jnp.arange(N_EXPERTS, dtype=jnp.int32)
    fill_blocks = jnp.clip(jnp.concatenate([blk_end - 1, tail]), 0, nb - 1).astype(jnp.int32)
    fill_valid = jnp.concatenate([counts % bm != 0, tail < nb]).astype(jnp.int32)

    dest = _dest_rows(ri, row_start.astype(jnp.int32))
    xs = _dispatch(xt, dest, fill_blocks, fill_valid, nb)
    ys = _experts(xs, block_expert, n_used, w_gate, w_up, w_down)
    out = _combine(ys, dest, h, rf)
    return out.reshape(bsz, slen, D_MODEL)


def kernel(x, meta_tokens, norm_mix_g, w_in, b_forget, q_norm_g, k_norm_g, conv_w, attn_out_g, conv_out_g,
           w_out, norm_ffn_g, w_router_group, b_router_group, w_router_expert, b_router_expert,
           w_gate, w_up, w_down):
    assert norm_mix_g.shape[0] == 1, "single-layer block"
    return _layer(x, meta_tokens, norm_mix_g[0], w_in[0], b_forget[0], q_norm_g[0], k_norm_g[0], conv_w[0],
                  attn_out_g[0], conv_out_g[0], w_out[0], norm_ffn_g[0], w_router_group[0],
                  b_router_group[0], w_router_expert[0], b_router_expert[0], w_gate[0], w_up[0], w_down[0])
```

```python
import functools

import jax
import jax.numpy as jnp
import numpy as np
from jax import lax
from jax.experimental import pallas as pl
from jax.experimental.pallas import tpu as pltpu

D_MODEL = 1024
N_META = 16
HEAD_DIM = 64
ATTN_HEADS = 8
ATTN_WIDTH = ATTN_HEADS * HEAD_DIM
CONV_WIDTH = D_MODEL - ATTN_WIDTH
CONV_K = 3
N_EXPERT_GROUPS = 4
EXPERTS_PER_GROUP = 8
N_EXPERTS = N_EXPERT_GROUPS * EXPERTS_PER_GROUP
TOP_K = 2
D_EXPERT = 512
EPS = 1e-6
MASK_VALUE = -1e30
LOG2E = 1.4426950408889634
AUG = 6

LANES = 128
SUBLANES = 8
MXU_DIM = 256
HEAD_PAIRS = ATTN_HEADS * HEAD_DIM // LANES
PROJ_PAD_COLS = 5 * ATTN_WIDTH + LANES
VMEM_LIMIT = 56 * 1024 * 1024

IN_TILE = 512
META_TILE = 128
ATTN_Q_TILE = 2048
ATTN_K_TILE = 512
OUT_TILE = 1024
EXPERT_ROWS = 256
EXPERT_BLOCKS_PER_STEP = 4
DISPATCH_TILE = 1024
COMBINE_TILE = 256
DMA_GROUP = 8

F32 = jnp.float32
BF16 = jnp.bfloat16
ROW_SUBLANES = D_MODEL // LANES


def _lane_iota(shape):
    return lax.broadcasted_iota(jnp.int32, shape, len(shape) - 1)


def _store_token_rows(ref, x, rows, stage=None):
    dst = ref if stage is None else stage
    for sl in range(ROW_SUBLANES):
        dst[pl.ds(sl, rows, stride=ROW_SUBLANES), :] = x[:, sl * LANES:(sl + 1) * LANES]
    if stage is not None:
        ref[...] = stage[...].astype(ref.dtype)


def _load_token_rows(ref, rows, stage=None):
    src = ref
    if stage is not None:
        stage[...] = ref[...].astype(F32)
        src = stage
    return jnp.concatenate([src[pl.ds(sl, rows, stride=ROW_SUBLANES), :] for sl in range(ROW_SUBLANES)], axis=1)


def _inproj_kernel(is_meta, tm, x_ref, gmix_ref, w1_ref, wvt_ref, bf_ref, gq_ref, gk_ref, cw_ref, gco_ref,
                   gmat_ref, tri_ref, eq_ref, ek_ref, halo_ref, cum0_ref, *rest):
    if is_meta:
        k_ref, vt_ref, kaug_ref, cum_ref, uc_ref, ucbuf, carry = rest
    else:
        q_ref, k_ref, vt_ref, conv_ref, qaug_ref, kaug_ref, ucbuf, carry = rest
    t = pl.program_id(1)

    @pl.when(t == 0)
    def _():
        ucbuf[0:SUBLANES, :] = halo_ref[...]
        carry[0:1, :] = cum0_ref[...]

    x = x_ref[0]
    ms = jnp.mean(x * x, axis=-1, keepdims=True)
    u = (x * lax.rsqrt(ms + EPS)) * gmix_ref[...]
    ub = u.astype(BF16)
    proj = jnp.dot(ub, w1_ref[...], preferred_element_type=F32)
    vt_ref[0, 0] = lax.dot_general(wvt_ref[...], ub, (((1,), (1,)), ((), ())),
                                   preferred_element_type=F32).astype(BF16)

    def head_norm(z, g):
        z2 = (z * z).astype(BF16)
        half = gmat_ref.shape[0]
        ssq = jnp.concatenate([jnp.dot(z2[:, c:c + half], gmat_ref[...], preferred_element_type=F32)
                               for c in range(0, z.shape[1], half)], axis=1)
        return z * lax.rsqrt(ssq * (1.0 / HEAD_DIM) + EPS) * g

    w = ATTN_WIDTH
    kn = head_norm(proj[:, w:2 * w], gk_ref[...])
    k_ref[0] = kn.astype(BF16)

    z = proj[:, 5 * w:5 * w + LANES] + bf_ref[...]
    ls = jnp.minimum(z, 0.0) - jnp.log1p(jnp.exp(-jnp.abs(z)))
    lane_c = _lane_iota(ls.shape)
    is_head = lane_c < ATTN_HEADS
    ls = jnp.where(is_head, ls, 0.0)

    def pieces(val):
        p_hi = val.astype(BF16).astype(F32)
        rem = val - p_hi
        p_mid = rem.astype(BF16).astype(F32)
        p_lo = (rem - p_mid).astype(BF16).astype(F32)
        return jnp.where(is_head, p_hi,
                         jnp.where(lane_c < 2 * ATTN_HEADS, pltpu.roll(p_mid, ATTN_HEADS, axis=1),
                                   pltpu.roll(p_lo, 2 * ATTN_HEADS, axis=1))).astype(BF16)

    cs3 = jnp.dot(tri_ref[...], pieces(ls), preferred_element_type=F32)
    cs = cs3 + pltpu.roll(cs3, LANES - ATTN_HEADS, axis=1) + pltpu.roll(cs3, LANES - 2 * ATTN_HEADS, axis=1)
    cum = jnp.where(is_head, cs, 0.0) + carry[0:1, :]
    carry[0:1, :] = cum[tm - 1:tm, :]

    packed = pieces(cum * LOG2E)
    in_aug = lane_c < AUG * ATTN_HEADS
    ones_k = in_aug & ((lane_c % AUG) < AUG // 2)
    kaug = jnp.where(ones_k, 1.0, -jnp.dot(packed, ek_ref[...], preferred_element_type=F32))
    kaug_ref[0] = kaug.astype(BF16)

    uc = proj[:, 3 * w:4 * w] * proj[:, 4 * w:5 * w]
    ucbuf[SUBLANES:SUBLANES + tm, :] = uc
    uc1 = ucbuf[SUBLANES - 1:SUBLANES - 1 + tm, :]
    uc2 = ucbuf[SUBLANES - 2:SUBLANES - 2 + tm, :]
    ucbuf[0:SUBLANES, :] = uc[tm - SUBLANES:tm, :]

    if is_meta:
        cum_ref[0] = cum
        uc_ref[0] = uc
        return

    qn = head_norm(proj[:, 0:w], gq_ref[...])
    q_ref[0] = qn.astype(BF16)
    y = cw_ref[0:1, :] * uc2 + cw_ref[1:2, :] * uc1 + cw_ref[2:3, :] * uc
    conv = proj[:, 2 * w:3 * w] * y
    conv_ref[0] = head_norm(conv, gco_ref[...]).astype(BF16)
    ones_q = in_aug & ((lane_c % AUG) >= AUG // 2)
    qaug = jnp.where(ones_q, 1.0, jnp.dot(packed, eq_ref[...], preferred_element_type=F32))
    qaug_ref[0] = qaug.astype(BF16)


def _inproj(x, is_meta, tm, consts, halo, cum0):
    bsz, tlen, _ = x.shape
    nt = tlen // tm
    gmix, w1, wvt, bfp, gq, gk, cw, gco, gmat = consts
    tri = jnp.asarray(np.tril(np.ones((tm, tm), np.float32)), BF16)
    w = ATTN_WIDTH
    src = np.arange(LANES)
    dst = np.arange(LANES)
    piece, head = src // ATTN_HEADS, src % ATTN_HEADS
    valid = src < 3 * ATTN_HEADS
    eq = jnp.asarray(valid[:, None] & (dst[None, :] == (AUG * head + piece)[:, None]), BF16)
    ek = jnp.asarray(valid[:, None] & (dst[None, :] == (AUG * head + AUG // 2 + piece)[:, None]), BF16)

    def full(a):
        return pl.BlockSpec(a.shape, lambda b, t: (0,) * a.ndim)

    in_specs = [pl.BlockSpec((1, tm, D_MODEL), lambda b, t: (b, t, 0))] + [
        full(a) for a in (gmix, w1, wvt, bfp, gq, gk, cw, gco, gmat, tri, eq, ek, halo, cum0)]
    tok = lambda width: pl.BlockSpec((1, tm, width), lambda b, t: (b, t, 0))
    vt_spec = pl.BlockSpec((1, 1, w, tm), lambda b, t: (b, t, 0, 0))
    if is_meta:
        out_shape = [jax.ShapeDtypeStruct((bsz, tlen, w), BF16),
                     jax.ShapeDtypeStruct((bsz, nt, w, tm), BF16),
                     jax.ShapeDtypeStruct((bsz, tlen, LANES), BF16),
                     jax.ShapeDtypeStruct((bsz, tlen, LANES), F32),
                     jax.ShapeDtypeStruct((bsz, tlen, w), F32)]
        out_specs = [tok(w), vt_spec, tok(LANES), tok(LANES), tok(w)]
    else:
        out_shape = [jax.ShapeDtypeStruct((bsz, tlen, w), BF16)] * 2 + [
            jax.ShapeDtypeStruct((bsz, nt, w, tm), BF16),
            jax.ShapeDtypeStruct((bsz, tlen, w), BF16),
            jax.ShapeDtypeStruct((bsz, tlen, LANES), BF16),
            jax.ShapeDtypeStruct((bsz, tlen, LANES), BF16)]
        out_specs = [tok(w), tok(w), vt_spec, tok(w), tok(LANES), tok(LANES)]
    return pl.pallas_call(
        functools.partial(_inproj_kernel, is_meta, tm),
        grid=(bsz, nt),
        in_specs=in_specs,
        out_specs=out_specs,
        out_shape=out_shape,
        scratch_shapes=[pltpu.VMEM((tm + SUBLANES, w), F32), pltpu.VMEM((SUBLANES, LANES), F32)],
        compiler_params=pltpu.CompilerParams(
            dimension_semantics=("arbitrary", "arbitrary"), vmem_limit_bytes=VMEM_LIMIT),
        name="inproj_meta" if is_meta else "inproj",
    )(x, gmix, w1, wvt, bfp, gq, gk, cw, gco, gmat, tri, eq, ek, halo, cum0)


def _attn_kernel(tq, tk, q_ref, qaug_ref, k_ref, kaug_ref, vt_ref, km_ref, kaugm_ref, vmt_ref, gao_ref,
                 o_ref, m_sc, l_sc, acc_sc, sa_sc, sb_sc, ma_sc, mb_sc):
    hp = pl.program_id(1)
    qi = pl.program_id(2)
    q = q_ref[0]
    qaug = qaug_ref[0]
    lane = _lane_iota(q.shape)
    first = lane < HEAD_DIM
    zero = jnp.zeros_like(q)
    qcat = []
    for hh in range(2):
        qh = jnp.where(first, q, zero) if hh == 0 else jnp.where(first, zero, q)
        lo = AUG * (2 * hp + hh)
        qa = jnp.where((lane >= lo) & (lane < lo + AUG), qaug, zero)
        qcat.append(jnp.concatenate([qh, qa], axis=1))

    m_sc[...] = jnp.full(m_sc.shape, MASK_VALUE, F32)
    l_sc[...] = jnp.zeros(l_sc.shape, F32)
    acc_sc[...] = jnp.zeros(acc_sc.shape, F32)

    def scores_t(kcat, q_lo=0):
        return tuple(lax.dot_general(kcat, qcat[hh][q_lo:], (((1,), (1,)), ((), ())), preferred_element_type=F32)
                     for hh in range(2))

    def update(hh, st, vtb, q_lo=0, q_hi=None, block_max=None):
        qs = slice(q_lo, q_hi)
        m_old = m_sc[hh, :, qs]
        if block_max is None:
            block_max = jnp.max(st, axis=0, keepdims=True)
        m_new = jnp.maximum(m_old, block_max)
        alpha = jnp.exp2(m_old - m_new)
        p = jnp.exp2(st - m_new)
        l_sc[hh, :, qs] = alpha * l_sc[hh, :, qs] + jnp.sum(p, axis=0, keepdims=True)
        vh = vtb[hh * HEAD_DIM:(hh + 1) * HEAD_DIM, :]
        acc_sc[hh, :, qs] = alpha * acc_sc[hh, :, qs] + jnp.dot(vh, p.astype(BF16), preferred_element_type=F32)
        m_sc[hh, :, qs] = m_new

    def key_block(j):
        start = pl.multiple_of(j * tk, tk)
        return jnp.concatenate([k_ref[0, pl.ds(start, tk), :], kaug_ref[0, pl.ds(start, tk), :]], axis=1)

    def put_scores(j, bufs, q_lo=0):
        buf, mbuf = bufs
        sts = scores_t(key_block(j), q_lo)
        for hh in range(2):
            buf[hh, :, q_lo:] = sts[hh]
            mbuf[hh, :, q_lo:] = jnp.max(sts[hh], axis=0, keepdims=True)

    def consume(bufs, j, diagonal=False, q_lo=0):
        buf, mbuf = bufs
        vtb = vt_ref[0, j]
        for hh in range(2):
            if diagonal:
                st = buf[hh, :, q_lo:q_lo + tk]
                key = lax.broadcasted_iota(jnp.int32, st.shape, 0)
                qry = lax.broadcasted_iota(jnp.int32, st.shape, 1)
                update(hh, jnp.where(key <= qry, st, MASK_VALUE), vtb, q_lo, q_lo + tk)
                if q_lo + tk < tq:
                    update(hh, buf[hh, :, q_lo + tk:], vtb, q_lo + tk, None, mbuf[hh, :, q_lo + tk:])
            else:
                update(hh, buf[hh, :, q_lo:], vtb, q_lo, None, mbuf[hh, :, q_lo:])

    per = tq // tk
    assert per % 2 == 0
    st_m = scores_t(jnp.concatenate([km_ref[...], kaugm_ref[...]], axis=1))
    buf_a, buf_b = (sa_sc, ma_sc), (sb_sc, mb_sc)
    put_scores(0, buf_a)
    vmt = vmt_ref[...]
    for hh in range(2):
        update(hh, st_m[hh], vmt)

    def body(t, carry):
        j = 2 * t
        put_scores(j + 1, buf_b)
        consume(buf_a, j)
        put_scores(j + 2, buf_a)
        consume(buf_b, j + 1)
        return carry

    lax.fori_loop(0, (per // 2) * qi, body, 0)
    bufs = (buf_a, buf_b)
    for d in range(per):
        if d + 1 < per:
            put_scores(per * qi + d + 1, bufs[(d + 1) % 2], (d + 1) * tk)
        consume(bufs[d % 2], per * qi + d, diagonal=True, q_lo=d * tk)

    ot = jnp.concatenate([acc_sc[0] / l_sc[0], acc_sc[1] / l_sc[1]], axis=0)
    o2 = ot * ot
    ms0 = jnp.sum(o2[0:HEAD_DIM], axis=0, keepdims=True) * (1.0 / HEAD_DIM)
    ms1 = jnp.sum(o2[HEAD_DIM:], axis=0, keepdims=True) * (1.0 / HEAD_DIM)
    inv = jnp.concatenate([jnp.broadcast_to(lax.rsqrt(ms0 + EPS), (HEAD_DIM, tq)),
                           jnp.broadcast_to(lax.rsqrt(ms1 + EPS), (HEAD_DIM, tq))], axis=0)
    o_ref[0] = (ot * inv * gao_ref[...]).T.astype(BF16)


def _attention(q, qaug, k, kaug, vt, km, kaugm, vmt, gao):
    bsz, slen, w = q.shape
    tq, tk = ATTN_Q_TILE, ATTN_K_TILE
    nq, nk = slen // tq, slen // tk
    return pl.pallas_call(
        functools.partial(_attn_kernel, tq, tk),
        grid=(bsz, HEAD_PAIRS, nq),
        in_specs=[
            pl.BlockSpec((1, tq, LANES), lambda b, p, i: (b, i, p)),
            pl.BlockSpec((1, tq, LANES), lambda b, p, i: (b, i, 0)),
            pl.BlockSpec((1, slen, LANES), lambda b, p, i: (b, 0, p)),
            pl.BlockSpec((1, slen, LANES), lambda b, p, i: (b, 0, 0)),
            pl.BlockSpec((1, nk, LANES, tk), lambda b, p, i: (b, 0, p, 0)),
            pl.BlockSpec((N_META, LANES), lambda b, p, i: (0, p)),
            pl.BlockSpec((N_META, LANES), lambda b, p, i: (0, 0)),
            pl.BlockSpec((LANES, N_META), lambda b, p, i: (p, 0)),
            pl.BlockSpec((LANES, 1), lambda b, p, i: (p, 0)),
        ],
        out_specs=pl.BlockSpec((1, tq, LANES), lambda b, p, i: (b, i, p)),
        out_shape=jax.ShapeDtypeStruct((bsz, slen, w), BF16),
        scratch_shapes=[pltpu.VMEM((2, 1, tq), F32), pltpu.VMEM((2, 1, tq), F32),
                        pltpu.VMEM((2, HEAD_DIM, tq), F32),
                        pltpu.VMEM((2, tk, tq), F32), pltpu.VMEM((2, tk, tq), F32),
                        pltpu.VMEM((2, 1, tq), F32), pltpu.VMEM((2, 1, tq), F32)],
        compiler_params=pltpu.CompilerParams(
            dimension_semantics=("arbitrary", "arbitrary", "arbitrary"), vmem_limit_bytes=VMEM_LIMIT),
        name="fox_attention",
    )(q, qaug, k, kaug, vt, km, kaugm, vmt, gao)


ROUTE_ROWS = 64


def _outproj_kernel(tm, attn_ref, conv_ref, x_ref, wo_ref, gffn_ref, wrt_ref, brc_ref, tri_ref,
                    h_ref, xt_ref, ri_ref, rf_ref, cnt_ref, carry, stage):
    i = pl.program_id(0)

    @pl.when(i == 0)
    def _():
        carry[...] = jnp.zeros(carry.shape, F32)

    mixed = jnp.concatenate([attn_ref[...], conv_ref[...]], axis=1)
    h = x_ref[...] + jnp.dot(mixed, wo_ref[...], preferred_element_type=F32)
    h_ref[...] = h
    ms = jnp.mean(h * h, axis=-1, keepdims=True)
    xt = (h * lax.rsqrt(ms + EPS)) * gffn_ref[...]
    _store_token_rows(xt_ref, xt, tm, stage)

    x_hi = xt.astype(BF16)
    x_lo = (xt - x_hi.astype(F32)).astype(BF16)
    parts = lax.dot_general(wrt_ref[...], jnp.concatenate([x_hi, x_lo], axis=1), (((1,), (1,)), ((), ())),
                            preferred_element_type=F32)
    logits = (parts[0:ROUTE_ROWS] + parts[LANES:LANES + ROUTE_ROWS]) + brc_ref[0:ROUTE_ROWS, 0:1]
    row = lax.broadcasted_iota(jnp.int32, logits.shape, 0)
    rowf = row.astype(F32)
    big = float(LANES)

    def first_argmax(vals, vmax):
        return jnp.min(jnp.where(vals == vmax, rowf, big), axis=0, keepdims=True)

    is_g = row < N_EXPERT_GROUPS
    gl = jnp.where(is_g, logits, MASK_VALUE)
    gmax = jnp.max(gl, axis=0, keepdims=True)
    gidx = first_argmax(gl, gmax)
    g_p = 1.0 / jnp.sum(jnp.where(is_g, jnp.exp(gl - gmax), 0.0), axis=0, keepdims=True)

    base = N_EXPERT_GROUPS + EXPERTS_PER_GROUP * gidx
    in_grp = (rowf >= base) & (rowf < base + EXPERTS_PER_GROUP)
    el = jnp.where(in_grp, logits, MASK_VALUE)
    l1 = jnp.max(el, axis=0, keepdims=True)
    e1 = first_argmax(el, l1)
    el2 = jnp.where(rowf == e1, MASK_VALUE, el)
    l2 = jnp.max(el2, axis=0, keepdims=True)
    e2 = first_argmax(el2, l2)
    zsum = jnp.sum(jnp.where(in_grp, jnp.exp(el - l1), 0.0), axis=0, keepdims=True)
    p1 = 1.0 / zsum
    p2 = jnp.exp(l2 - l1) / zsum
    den = p1 + p2
    w1 = g_p * p1 / den
    w2 = g_p * p2 / den
    id1 = e1 - N_EXPERT_GROUPS
    id2 = e2 - N_EXPERT_GROUPS

    oh1 = rowf == id1
    oh2 = rowf == id2
    oh = jnp.where(oh1 | oh2, 1.0, 0.0)
    before = jnp.dot(oh.astype(BF16), tri_ref[...], preferred_element_type=F32) + carry[:, 0:1]
    rank1 = jnp.sum(jnp.where(oh1, before, 0.0), axis=0, keepdims=True)
    rank2 = jnp.sum(jnp.where(oh2, before, 0.0), axis=0, keepdims=True)
    carry[...] = carry[...] + jnp.sum(oh, axis=1, keepdims=True)
    cnt_ref[...] = carry[...]

    pad_i = jnp.zeros((SUBLANES - 2 * TOP_K, tm), F32)
    ri_ref[0] = jnp.concatenate([id1, id2, rank1, rank2, pad_i], axis=0).astype(jnp.int32)
    pad_f = jnp.zeros((LANES - TOP_K, tm), F32)
    rf_ref[...] = jnp.concatenate([w1, w2, pad_f], axis=0).T[:, 0:SUBLANES]


def _outproj(attn, conv, x, wo, gffn, wrt, brc):
    n = x.shape[0]
    tm = OUT_TILE
    tri = jnp.asarray(np.triu(np.ones((tm, tm), np.float32), k=1), BF16)

    def full(a):
        return pl.BlockSpec(a.shape, lambda i: (0,) * a.ndim)

    rows = lambda width: pl.BlockSpec((tm, width), lambda i: (i, 0))
    return pl.pallas_call(
        functools.partial(_outproj_kernel, tm),
        grid=(n // tm,),
        in_specs=[rows(ATTN_WIDTH), rows(CONV_WIDTH), rows(D_MODEL)] + [
            full(a) for a in (wo, gffn, wrt, brc, tri)],
        out_specs=[rows(D_MODEL), pl.BlockSpec((tm * ROW_SUBLANES, LANES), lambda i: (i, 0)),
                   pl.BlockSpec((1, SUBLANES, tm), lambda i: (i, 0, 0)), rows(SUBLANES),
                   pl.BlockSpec((ROUTE_ROWS, LANES), lambda i: (0, 0))],
        out_shape=[jax.ShapeDtypeStruct((n, D_MODEL), F32),
                   jax.ShapeDtypeStruct((n * ROW_SUBLANES, LANES), BF16),
                   jax.ShapeDtypeStruct((n // tm, SUBLANES, tm), jnp.int32),
                   jax.ShapeDtypeStruct((n, SUBLANES), F32),
                   jax.ShapeDtypeStruct((ROUTE_ROWS, LANES), F32)],
        scratch_shapes=[pltpu.VMEM((ROUTE_ROWS, LANES), F32), pltpu.VMEM((tm * ROW_SUBLANES, LANES), F32)],
        compiler_params=pltpu.CompilerParams(
            dimension_semantics=("arbitrary",), vmem_limit_bytes=VMEM_LIMIT),
        name="outproj_router",
    )(attn, conv, x, wo, gffn, wrt, brc, tri)


def _dest_kernel(rs_ref, ri_ref, dest_ref):
    ri = ri_ref[...]
    experts = ri[:, 0:TOP_K, :]
    start = jnp.zeros_like(experts)
    for e in range(N_EXPERTS):
        start = jnp.where(experts == e, rs_ref[e], start)
    dest_ref[...] = jnp.concatenate([start + ri[:, TOP_K:2 * TOP_K, :],
                                     jnp.zeros((ri.shape[0], SUBLANES - TOP_K, ri.shape[2]), jnp.int32)], axis=1)


def _dest_rows(ri_t, row_start):
    grid_spec = pltpu.PrefetchScalarGridSpec(
        num_scalar_prefetch=1,
        grid=(1,),
        in_specs=[pl.BlockSpec(ri_t.shape, lambda i, rs: (0, 0, 0))],
        out_specs=pl.BlockSpec(ri_t.shape, lambda i, rs: (0, 0, 0)),
    )
    return pl.pallas_call(
        _dest_kernel,
        grid_spec=grid_spec,
        out_shape=jax.ShapeDtypeStruct(ri_t.shape, jnp.int32),
        compiler_params=pltpu.CompilerParams(dimension_semantics=("arbitrary",)),
        name="moe_dest",
    )(row_start, ri_t)


def _tile_copy(src, src_row, dst, dst_row, sem):
    rs = ROW_SUBLANES
    return pltpu.make_async_copy(src.at[pl.ds(pl.multiple_of(src_row * rs, rs), rs), :],
                                 dst.at[pl.ds(pl.multiple_of(dst_row * rs, rs), rs), :], sem)


def _dispatch_kernel(tm, bm, fz_ref, fv_ref, dest_ref, xt_ref, xs_hbm, zeros, sem, zsem):
    i = pl.program_id(0)
    block_rows = bm * ROW_SUBLANES

    def zero_block(z):
        start = pl.multiple_of(fz_ref[z] * block_rows, block_rows)
        return pltpu.make_async_copy(zeros, xs_hbm.at[pl.ds(start, block_rows), :], zsem.at[0])

    @pl.when(i == 0)
    def _():
        zeros[...] = jnp.zeros(zeros.shape, BF16)
        for z in range(fz_ref.shape[0]):
            @pl.when(fv_ref[z] == 1)
            def _():
                zero_block(z).start()
        for z in range(fz_ref.shape[0]):
            @pl.when(fv_ref[z] == 1)
            def _():
                zero_block(z).wait()

    for g in range(0, tm, DMA_GROUP):
        dests = [(r, kk, dest_ref[0, kk, r]) for r in range(g, g + DMA_GROUP) for kk in range(TOP_K)]
        for r, kk, d in dests:
            _tile_copy(xt_ref, r, xs_hbm, d, sem.at[0]).start(priority=kk)
    for kk in range(TOP_K):
        pltpu.make_async_copy(xt_ref, xs_hbm.at[pl.ds(0, tm * ROW_SUBLANES), :], sem.at[0]).wait()


def _route_spec(dest, tm, extra_args):
    per = dest.shape[2] // tm
    nt = dest.shape[0] * per
    if extra_args == 0:
        index = lambda i: (jnp.minimum(i, nt - 1) // per, 0, jnp.minimum(i, nt - 1) % per)
    else:
        index = lambda i, *_: (jnp.minimum(i, nt - 1) // per, 0, jnp.minimum(i, nt - 1) % per)
    return nt, pl.BlockSpec((1, SUBLANES, tm), index, memory_space=pltpu.SMEM)


def _dispatch(xt_rows, dest, fill_blocks, fill_valid, nb):
    tm = DISPATCH_TILE
    bm = EXPERT_ROWS
    nt, dest_spec = _route_spec(dest, tm, 2)
    grid_spec = pltpu.PrefetchScalarGridSpec(
        num_scalar_prefetch=2,
        grid=(nt,),
        in_specs=[dest_spec, pl.BlockSpec((tm * ROW_SUBLANES, LANES), lambda i, fz, fv: (i, 0))],
        out_specs=pl.BlockSpec(memory_space=pl.ANY),
        scratch_shapes=[pltpu.VMEM((bm * ROW_SUBLANES, LANES), BF16), pltpu.SemaphoreType.DMA((1,)),
                        pltpu.SemaphoreType.DMA((1,))],
    )
    return pl.pallas_call(
        functools.partial(_dispatch_kernel, tm, bm),
        grid_spec=grid_spec,
        out_shape=jax.ShapeDtypeStruct((nb * bm * ROW_SUBLANES, LANES), BF16),
        compiler_params=pltpu.CompilerParams(
            dimension_semantics=("arbitrary",), vmem_limit_bytes=VMEM_LIMIT),
        name="moe_dispatch",
    )(fill_blocks, fill_valid, dest, xt_rows)


def _expert_kernel(bm, be_ref, nu_ref, first_ref, nxt_ref, slot_ref, xs_ref, wg_hbm, wu_hbm, wd_hbm, ys_ref,
                   wg_buf, wu_buf, wd_buf, stage_in, wsem):
    step = pl.program_id(0)

    def weight_copies(e, slot):
        return [pltpu.make_async_copy(src.at[e], buf.at[slot], wsem.at[m, slot])
                for m, (src, buf) in enumerate(((wg_hbm, wg_buf), (wu_hbm, wu_buf), (wd_hbm, wd_buf)))]

    @pl.when(step == 0)
    def _():
        for c in weight_copies(be_ref[0], 0):
            c.start()

    def one_block(j, xs_blk, ys_blk):
        @pl.when(j < nu_ref[0])
        def _():
            slot = slot_ref[j]

            @pl.when(first_ref[j] == 1)
            def _():
                for c in weight_copies(0, slot):
                    c.wait()

                @pl.when(nxt_ref[j] >= 0)
                def _():
                    for c in weight_copies(nxt_ref[j], 1 - slot):
                        c.start()

            x = _load_token_rows(xs_blk, bm, stage_in)
            hg = jnp.dot(x, wg_buf[slot], preferred_element_type=F32)
            hu = jnp.dot(x, wu_buf[slot], preferred_element_type=F32)
            hdn = hg * (1.0 / (1.0 + jnp.exp(-hg))) * hu
            y = jnp.dot(hdn, wd_buf[slot], preferred_element_type=F32)
            _store_token_rows(ys_blk, y, bm)

        @pl.when(j >= nu_ref[0])
        def _():
            ys_blk[...] = jnp.zeros(ys_blk.shape, F32)

    rows = bm * ROW_SUBLANES
    for sub in range(EXPERT_BLOCKS_PER_STEP):
        view = pl.ds(sub * rows, rows)
        one_block(step * EXPERT_BLOCKS_PER_STEP + sub, xs_ref.at[view, :], ys_ref.at[view, :])


def _experts(xs_rows, block_expert, n_used, w_gate, w_up, w_down):
    bm = EXPERT_ROWS
    nb = block_expert.shape[0]
    blk = jnp.arange(nb, dtype=jnp.int32)
    live = blk < n_used[0]
    first = live & ((blk == 0) | (block_expert != jnp.roll(block_expert, 1)))
    slot = (jnp.cumsum(first.astype(jnp.int32)) - 1) % 2
    later_first = first[None, :] & (blk[None, :] > blk[:, None])
    nxt_start = jnp.min(jnp.where(later_first, blk[None, :], nb), axis=1)
    nxt = jnp.sum(jnp.where(blk[None, :] == nxt_start[:, None], block_expert[None, :], 0), axis=1)
    nxt = jnp.where(nxt_start < nb, nxt, -1)
    per = EXPERT_BLOCKS_PER_STEP
    assert nb % per == 0
    step_rows = per * bm * ROW_SUBLANES
    grid_spec = pltpu.PrefetchScalarGridSpec(
        num_scalar_prefetch=5,
        grid=(nb // per,),
        in_specs=[
            pl.BlockSpec((step_rows, LANES), lambda s, be, nu, *_: (jnp.minimum(s, (nu[0] - 1) // per), 0)),
            pl.BlockSpec(memory_space=pl.ANY),
            pl.BlockSpec(memory_space=pl.ANY),
            pl.BlockSpec(memory_space=pl.ANY),
        ],
        out_specs=pl.BlockSpec((step_rows, LANES), lambda s, *_: (s, 0)),
        scratch_shapes=[pltpu.VMEM((2, D_MODEL, D_EXPERT), F32), pltpu.VMEM((2, D_MODEL, D_EXPERT), F32),
                        pltpu.VMEM((2, D_EXPERT, D_MODEL), F32),
                        pltpu.VMEM((bm * ROW_SUBLANES, LANES), F32), pltpu.SemaphoreType.DMA((3, 2))],
    )
    return pl.pallas_call(
        functools.partial(_expert_kernel, bm),
        grid_spec=grid_spec,
        out_shape=jax.ShapeDtypeStruct((nb * bm * ROW_SUBLANES, LANES), F32),
        compiler_params=pltpu.CompilerParams(
            dimension_semantics=("arbitrary",), vmem_limit_bytes=VMEM_LIMIT),
        name="moe_experts",
    )(block_expert, n_used, first.astype(jnp.int32), nxt.astype(jnp.int32), slot.astype(jnp.int32),
      xs_rows, w_gate, w_up, w_down)


def _combine_kernel(tm, dest_ref, ys_hbm, h_ref, rf_ref, o_ref, ybuf, sem):
    i = pl.program_id(0)
    nt = pl.num_programs(0) - 1

    @pl.when(i < nt)
    def _():
        slot = i % 2
        for g in range(0, tm, DMA_GROUP):
            dests = [(r, kk, dest_ref[0, kk, r]) for r in range(g, g + DMA_GROUP) for kk in range(TOP_K)]
            for r, kk, d in dests:
                _tile_copy(ys_hbm, d, ybuf.at[kk, slot], r, sem.at[kk, slot]).start(priority=kk)

    @pl.when(i >= 1)
    def _():
        slot = (i - 1) % 2
        for kk in range(TOP_K):
            pltpu.make_async_copy(ys_hbm.at[pl.ds(0, tm * ROW_SUBLANES), :], ybuf.at[kk, slot],
                                  sem.at[kk, slot]).wait()
        rf = rf_ref[...]
        o_ref[...] = (h_ref[...] + rf[:, 0:1] * _load_token_rows(ybuf.at[0, slot], tm)
                      + rf[:, 1:2] * _load_token_rows(ybuf.at[1, slot], tm))


def _combine(ys_rows, dest, h, rf):
    n = h.shape[0]
    tm = COMBINE_TILE
    nt, dest_spec = _route_spec(dest, tm, 0)
    prev = lambda i: jnp.maximum(i - 1, 0)
    return pl.pallas_call(
        functools.partial(_combine_kernel, tm),
        grid=(nt + 1,),
        in_specs=[
            dest_spec,
            pl.BlockSpec(memory_space=pl.ANY),
            pl.BlockSpec((tm, D_MODEL), lambda i: (prev(i), 0)),
            pl.BlockSpec((tm, SUBLANES), lambda i: (prev(i), 0)),
        ],
        out_specs=pl.BlockSpec((tm, D_MODEL), lambda i: (prev(i), 0)),
        out_shape=jax.ShapeDtypeStruct((n, D_MODEL), F32),
        scratch_shapes=[pltpu.VMEM((TOP_K, 2, tm * ROW_SUBLANES, LANES), F32),
                        pltpu.SemaphoreType.DMA((TOP_K, 2))],
        compiler_params=pltpu.CompilerParams(
            dimension_semantics=("arbitrary",), vmem_limit_bytes=VMEM_LIMIT),
        name="moe_combine",
    )(dest, ys_rows, h, rf)


def _pad_lanes(a, width=LANES):
    return jnp.pad(a, ((0, 0), (0, width - a.shape[-1])))


def _layer(x, meta_tokens, g_mix, w_in, b_forget, q_norm_g, k_norm_g, conv_w, attn_out_g, conv_out_g,
           w_out, g_ffn, w_rg, b_rg, w_re, b_re, w_gate, w_up, w_down):
    bsz, slen, _ = x.shape
    n = bsz * slen
    w = ATTN_WIDTH

    wq, wk, wv, wf, wb, wc, whc = jnp.split(
        w_in, [w, 2 * w, 3 * w, 3 * w + ATTN_HEADS, 4 * w + ATTN_HEADS, 5 * w + ATTN_HEADS], axis=1)
    w1 = jnp.concatenate([wq, wk, wb, wc, whc, _pad_lanes(wf)], axis=1).astype(BF16)
    head_of = np.arange(MXU_DIM) // HEAD_DIM
    gmat = jnp.asarray(head_of[:, None] == head_of[None, :], BF16)
    consts = (
        g_mix.reshape(1, D_MODEL), w1, wv.T.astype(BF16), _pad_lanes(b_forget.reshape(1, ATTN_HEADS)),
        jnp.tile(q_norm_g, ATTN_HEADS).reshape(1, w) * (HEAD_DIM ** -0.5 * LOG2E),
        jnp.tile(k_norm_g, ATTN_HEADS).reshape(1, w),
        jnp.pad(conv_w, ((0, SUBLANES - CONV_K), (0, 0))), conv_out_g.reshape(1, w), gmat)

    xm = jnp.pad(meta_tokens, ((0, META_TILE - N_META), (0, 0)))[None]
    km, vmt, kaugm, cum_m, uc_m = _inproj(xm, True, META_TILE, consts,
                                          jnp.zeros((SUBLANES, w), F32), jnp.zeros((1, LANES), F32))
    km, vmt, kaugm = km[0, :N_META], vmt[0, 0, :, :N_META], kaugm[0, :N_META]
    cum_m = cum_m[0, :N_META]
    halo = jnp.zeros((SUBLANES, w), F32).at[SUBLANES - 2:].set(uc_m[0, N_META - 2:N_META])
    cum0 = cum_m[N_META - 1:N_META]

    q, k, vt, conv, qaug, kaug = _inproj(x, False, IN_TILE, consts, halo, cum0)
    attn = _attention(q, qaug, k, kaug, vt, km, kaugm, vmt, attn_out_g.reshape(w, 1))

    wr = _pad_lanes(jnp.concatenate([w_rg, w_re], axis=1)).T
    wr_hi = wr.astype(BF16)
    wr_lo = (wr - wr_hi.astype(F32)).astype(BF16)
    wrt = jnp.concatenate([jnp.concatenate([wr_hi, wr_hi], axis=1),
                           jnp.concatenate([wr_lo, jnp.zeros_like(wr_lo)], axis=1)], axis=0)
    brc = jnp.broadcast_to(_pad_lanes(jnp.concatenate([b_rg, b_re]).reshape(1, -1)).T, (LANES, LANES))
    h, xt, ri, rf, cnt = _outproj(attn.reshape(n, w), conv.reshape(n, w), x.reshape(n, D_MODEL),
                                  w_out.astype(BF16), g_ffn.reshape(1, D_MODEL), wrt, brc)

    bm = EXPERT_ROWS
    nb = (n * TOP_K) // bm + N_EXPERTS
    counts = cnt[:N_EXPERTS, 0].astype(jnp.int32)
    blocks_e = (counts + bm - 1) // bm
    blk_end = jnp.cumsum(blocks_e)
    row_start = (blk_end - blocks_e) * bm
    blk = jnp.arange(nb, dtype=jnp.int32)
    block_expert = jnp.minimum(jnp.sum(blk_end[None, :] <= blk[:, None], axis=1), N_EXPERTS - 1).astype(jnp.int32)
    n_used = blk_end[-1:].astype(jnp.int32)
    tail = n_used[0] + jnp.arange(N_EXPERTS, dtype=jnp.int32)
    fill_blocks = jnp.clip(jnp.concatenate([blk_end - 1, tail]), 0, nb - 1).astype(jnp.int32)
    fill_valid = jnp.concatenate([counts % bm != 0, tail < nb]).astype(jnp.int32)

    dest = _dest_rows(ri, row_start.astype(jnp.int32))
    xs = _dispatch(xt, dest, fill_blocks, fill_valid, nb)
    ys = _experts(xs, block_expert, n_used, w_gate, w_up, w_down)
    out = _combine(ys, dest, h, rf)
    return out.reshape(bsz, slen, D_MODEL)


def kernel(x, meta_tokens, norm_mix_g, w_in, b_forget, q_norm_g, k_norm_g, conv_w, attn_out_g, conv_out_g,
           w_out, norm_ffn_g, w_router_group, b_router_group, w_router_expert, b_router_expert,
           w_gate, w_up, w_down):
    assert norm_mix_g.shape[0] == 1, "single-layer block"
    return _layer(x, meta_tokens, norm_mix_g[0], w_in[0], b_forget[0], q_norm_g[0], k_norm_g[0], conv_w[0],
                  attn_out_g[0], conv_out_g[0], w_out[0], norm_ffn_g[0], w_router_group[0],
                  b_router_group[0], w_router_expert[0], b_router_expert[0], w_gate[0], w_up[0], w_down[0])
```

```python
import functools

import jax
import jax.numpy as jnp
import numpy as np
from jax import lax
from jax.experimental import pallas as pl
from jax.experimental.pallas import tpu as pltpu

D_MODEL = 1024
N_META = 16
HEAD_DIM = 64
ATTN_HEADS = 8
ATTN_WIDTH = ATTN_HEADS * HEAD_DIM
CONV_WIDTH = D_MODEL - ATTN_WIDTH
CONV_K = 3
N_EXPERT_GROUPS = 4
EXPERTS_PER_GROUP = 8
N_EXPERTS = N_EXPERT_GROUPS * EXPERTS_PER_GROUP
TOP_K = 2
D_EXPERT = 512
EPS = 1e-6
MASK_VALUE = -1e30
LOG2E = 1.4426950408889634
AUG = 6

LANES = 128
SUBLANES = 8
MXU_DIM = 256
HEAD_PAIRS = ATTN_HEADS * HEAD_DIM // LANES
PROJ_PAD_COLS = 5 * ATTN_WIDTH + LANES
VMEM_LIMIT = 56 * 1024 * 1024

IN_TILE = 512
META_TILE = 128
ATTN_Q_TILE = 2048
ATTN_K_TILE = 512
OUT_TILE = 1024
EXPERT_ROWS = 256
EXPERT_BLOCKS_PER_STEP = 4
DISPATCH_TILE = 1024
COMBINE_TILE = 256
DMA_GROUP = 8

F32 = jnp.float32
BF16 = jnp.bfloat16
ROW_SUBLANES = D_MODEL // LANES


def _lane_iota(shape):
    return lax.broadcasted_iota(jnp.int32, shape, len(shape) - 1)


def _store_token_rows(ref, x, rows, stage=None):
    dst = ref if stage is None else stage
    for sl in range(ROW_SUBLANES):
        dst[pl.ds(sl, rows, stride=ROW_SUBLANES), :] = x[:, sl * LANES:(sl + 1) * LANES]
    if stage is not None:
        ref[...] = stage[...].astype(ref.dtype)


def _load_token_rows(ref, rows, stage=None):
    src = ref
    if stage is not None:
        stage[...] = ref[...].astype(F32)
        src = stage
    return jnp.concatenate([src[pl.ds(sl, rows, stride=ROW_SUBLANES), :] for sl in range(ROW_SUBLANES)], axis=1)


def _inproj_kernel(is_meta, tm, x_ref, gmix_ref, w1_ref, wvt_ref, bf_ref, gq_ref, gk_ref, cw_ref, gco_ref,
                   gmat_ref, tri_ref, eq_ref, ek_ref, halo_ref, cum0_ref, *rest):
    if is_meta:
        k_ref, vt_ref, kaug_ref, cum_ref, uc_ref, ucbuf, carry = rest
    else:
        q_ref, k_ref, vt_ref, conv_ref, qaug_ref, kaug_ref, ucbuf, carry = rest
    t = pl.program_id(1)

    @pl.when(t == 0)
    def _():
        ucbuf[0:SUBLANES, :] = halo_ref[...]
        carry[0:1, :] = cum0_ref[...]

    x = x_ref[0]
    ms = jnp.mean(x * x, axis=-1, keepdims=True)
    u = (x * lax.rsqrt(ms + EPS)) * gmix_ref[...]
    ub = u.astype(BF16)
    proj = jnp.dot(ub, w1_ref[...], preferred_element_type=F32)
    vt_ref[0, 0] = lax.dot_general(wvt_ref[...], ub, (((1,), (1,)), ((), ())),
                                   preferred_element_type=F32).astype(BF16)

    def head_norm(z, g):
        z2 = (z * z).astype(BF16)
        half = gmat_ref.shape[0]
        ssq = jnp.concatenate([jnp.dot(z2[:, c:c + half], gmat_ref[...], preferred_element_type=F32)
                               for c in range(0, z.shape[1], half)], axis=1)
        return z * lax.rsqrt(ssq * (1.0 / HEAD_DIM) + EPS) * g

    w = ATTN_WIDTH
    kn = head_norm(proj[:, w:2 * w], gk_ref[...])
    k_ref[0] = kn.astype(BF16)

    z = proj[:, 5 * w:5 * w + LANES] + bf_ref[...]
    ls = jnp.minimum(z, 0.0) - jnp.log1p(jnp.exp(-jnp.abs(z)))
    lane_c = _lane_iota(ls.shape)
    is_head = lane_c < ATTN_HEADS
    ls = jnp.where(is_head, ls, 0.0)

    def pieces(val):
        p_hi = val.astype(BF16).astype(F32)
        rem = val - p_hi
        p_mid = rem.astype(BF16).astype(F32)
        p_lo = (rem - p_mid).astype(BF16).astype(F32)
        return jnp.where(is_head, p_hi,
                         jnp.where(lane_c < 2 * ATTN_HEADS, pltpu.roll(p_mid, ATTN_HEADS, axis=1),
                                   pltpu.roll(p_lo, 2 * ATTN_HEADS, axis=1))).astype(BF16)

    cs3 = jnp.dot(tri_ref[...], pieces(ls), preferred_element_type=F32)
    cs = cs3 + pltpu.roll(cs3, LANES - ATTN_HEADS, axis=1) + pltpu.roll(cs3, LANES - 2 * ATTN_HEADS, axis=1)
    cum = jnp.where(is_head, cs, 0.0) + carry[0:1, :]
    carry[0:1, :] = cum[tm - 1:tm, :]

    packed = pieces(cum * LOG2E)
    in_aug = lane_c < AUG * ATTN_HEADS
    ones_k = in_aug & ((lane_c % AUG) < AUG // 2)
    kaug = jnp.where(ones_k, 1.0, -jnp.dot(packed, ek_ref[...], preferred_element_type=F32))
    kaug_ref[0] = kaug.astype(BF16)

    uc = proj[:, 3 * w:4 * w] * proj[:, 4 * w:5 * w]
    ucbuf[SUBLANES:SUBLANES + tm, :] = uc
    uc1 = ucbuf[SUBLANES - 1:SUBLANES - 1 + tm, :]
    uc2 = ucbuf[SUBLANES - 2:SUBLANES - 2 + tm, :]
    ucbuf[0:SUBLANES, :] = uc[tm - SUBLANES:tm, :]

    if is_meta:
        cum_ref[0] = cum
        uc_ref[0] = uc
        return

    qn = head_norm(proj[:, 0:w], gq_ref[...])
    q_ref[0] = qn.astype(BF16)
    y = cw_ref[0:1, :] * uc2 + cw_ref[1:2, :] * uc1 + cw_ref[2:3, :] * uc
    conv = proj[:, 2 * w:3 * w] * y
    conv_ref[0] = head_norm(conv, gco_ref[...]).astype(BF16)
    ones_q = in_aug & ((lane_c % AUG) >= AUG // 2)
    qaug = jnp.where(ones_q, 1.0, jnp.dot(packed, eq_ref[...], preferred_element_type=F32))
    qaug_ref[0] = qaug.astype(BF16)


def _inproj(x, is_meta, tm, consts, halo, cum0):
    bsz, tlen, _ = x.shape
    nt = tlen // tm
    gmix, w1, wvt, bfp, gq, gk, cw, gco, gmat = consts
    tri = jnp.asarray(np.tril(np.ones((tm, tm), np.float32)), BF16)
    w = ATTN_WIDTH
    src = np.arange(LANES)
    dst = np.arange(LANES)
    piece, head = src // ATTN_HEADS, src % ATTN_HEADS
    valid = src < 3 * ATTN_HEADS
    eq = jnp.asarray(valid[:, None] & (dst[None, :] == (AUG * head + piece)[:, None]), BF16)
    ek = jnp.asarray(valid[:, None] & (dst[None, :] == (AUG * head + AUG // 2 + piece)[:, None]), BF16)

    def full(a):
        return pl.BlockSpec(a.shape, lambda b, t: (0,) * a.ndim)

    in_specs = [pl.BlockSpec((1, tm, D_MODEL), lambda b, t: (b, t, 0))] + [
        full(a) for a in (gmix, w1, wvt, bfp, gq, gk, cw, gco, gmat, tri, eq, ek, halo, cum0)]
    tok = lambda width: pl.BlockSpec((1, tm, width), lambda b, t: (b, t, 0))
    vt_spec = pl.BlockSpec((1, 1, w, tm), lambda b, t: (b, t, 0, 0))
    if is_meta:
        out_shape = [jax.ShapeDtypeStruct((bsz, tlen, w), BF16),
                     jax.ShapeDtypeStruct((bsz, nt, w, tm), BF16),
                     jax.ShapeDtypeStruct((bsz, tlen, LANES), BF16),
                     jax.ShapeDtypeStruct((bsz, tlen, LANES), F32),
                     jax.ShapeDtypeStruct((bsz, tlen, w), F32)]
        out_specs = [tok(w), vt_spec, tok(LANES), tok(LANES), tok(w)]
    else:
        out_shape = [jax.ShapeDtypeStruct((bsz, tlen, w), BF16)] * 2 + [
            jax.ShapeDtypeStruct((bsz, nt, w, tm), BF16),
            jax.ShapeDtypeStruct((bsz, tlen, w), BF16),
            jax.ShapeDtypeStruct((bsz, tlen, LANES), BF16),
            jax.ShapeDtypeStruct((bsz, tlen, LANES), BF16)]
        out_specs = [tok(w), tok(w), vt_spec, tok(w), tok(LANES), tok(LANES)]
    return pl.pallas_call(
        functools.partial(_inproj_kernel, is_meta, tm),
        grid=(bsz, nt),
        in_specs=in_specs,
        out_specs=out_specs,
        out_shape=out_shape,
        scratch_shapes=[pltpu.VMEM((tm + SUBLANES, w), F32), pltpu.VMEM((SUBLANES, LANES), F32)],
        compiler_params=pltpu.CompilerParams(
            dimension_semantics=("arbitrary", "arbitrary"), vmem_limit_bytes=VMEM_LIMIT),
        name="inproj_meta" if is_meta else "inproj",
    )(x, gmix, w1, wvt, bfp, gq, gk, cw, gco, gmat, tri, eq, ek, halo, cum0)


def _attn_kernel(tq, tk, q_ref, qaug_ref, k_ref, kaug_ref, vt_ref, km_ref, kaugm_ref, vmt_ref, gao_ref,
                 o_ref, m_sc, l_sc, acc_sc, sa_sc, sb_sc, ma_sc, mb_sc):
    hp = pl.program_id(1)
    qi = pl.program_id(2)
    q = q_ref[0]
    qaug = qaug_ref[0]
    lane = _lane_iota(q.shape)
    first = lane < HEAD_DIM
    zero = jnp.zeros_like(q)
    qcat = []
    for hh in range(2):
        qh = jnp.where(first, q, zero) if hh == 0 else jnp.where(first, zero, q)
        lo = AUG * (2 * hp + hh)
        qa = jnp.where((lane >= lo) & (lane < lo + AUG), qaug, zero)
        qcat.append(jnp.concatenate([qh, qa], axis=1))

    m_sc[...] = jnp.full(m_sc.shape, MASK_VALUE, F32)
    l_sc[...] = jnp.zeros(l_sc.shape, F32)
    acc_sc[...] = jnp.zeros(acc_sc.shape, F32)

    def scores_t(kcat, q_lo=0):
        return tuple(lax.dot_general(kcat, qcat[hh][q_lo:], (((1,), (1,)), ((), ())), preferred_element_type=F32)
                     for hh in range(2))

    def update(hh, st, vtb, q_lo=0, q_hi=None, block_max=None):
        qs = slice(q_lo, q_hi)
        m_old = m_sc[hh, :, qs]
        if block_max is None:
            block_max = jnp.max(st, axis=0, keepdims=True)
        m_new = jnp.maximum(m_old, block_max)
        alpha = jnp.exp2(m_old - m_new)
        p = jnp.exp2(st - m_new)
        l_sc[hh, :, qs] = alpha * l_sc[hh, :, qs] + jnp.sum(p, axis=0, keepdims=True)
        vh = vtb[hh * HEAD_DIM:(hh + 1) * HEAD_DIM, :]
        acc_sc[hh, :, qs] = alpha * acc_sc[hh, :, qs] + jnp.dot(vh, p.astype(BF16), preferred_element_type=F32)
        m_sc[hh, :, qs] = m_new

    def key_block(j):
        start = pl.multiple_of(j * tk, tk)
        return jnp.concatenate([k_ref[0, pl.ds(start, tk), :], kaug_ref[0, pl.ds(start, tk), :]], axis=1)

    def put_scores(j, bufs, q_lo=0):
        buf, mbuf = bufs
        sts = scores_t(key_block(j), q_lo)
        for hh in range(2):
            buf[hh, :, q_lo:] = sts[hh]
            mbuf[hh, :, q_lo:] = jnp.max(sts[hh], axis=0, keepdims=True)

    def consume(bufs, j, diagonal=False, q_lo=0):
        buf, mbuf = bufs
        vtb = vt_ref[0, j]
        for hh in range(2):
            if diagonal:
                st = buf[hh, :, q_lo:q_lo + tk]
                key = lax.broadcasted_iota(jnp.int32, st.shape, 0)
                qry = lax.broadcasted_iota(jnp.int32, st.shape, 1)
                update(hh, jnp.where(key <= qry, st, MASK_VALUE), vtb, q_lo, q_lo + tk)
                if q_lo + tk < tq:
                    update(hh, buf[hh, :, q_lo + tk:], vtb, q_lo + tk, None, mbuf[hh, :, q_lo + tk:])
            else:
                update(hh, buf[hh, :, q_lo:], vtb, q_lo, None, mbuf[hh, :, q_lo:])

    per = tq // tk
    assert per % 2 == 0
    st_m = scores_t(jnp.concatenate([km_ref[...], kaugm_ref[...]], axis=1))
    buf_a, buf_b = (sa_sc, ma_sc), (sb_sc, mb_sc)
    put_scores(0, buf_a)
    vmt = vmt_ref[...]
    for hh in range(2):
        update(hh, st_m[hh], vmt)

    def body(t, carry):
        j = 2 * t
        put_scores(j + 1, buf_b)
        consume(buf_a, j)
        put_scores(j + 2, buf_a)
        consume(buf_b, j + 1)
        return carry

    lax.fori_loop(0, (per // 2) * qi, body, 0)
    bufs = (buf_a, buf_b)
    for d in range(per):
        if d + 1 < per:
            put_scores(per * qi + d + 1, bufs[(d + 1) % 2], (d + 1) * tk)
        consume(bufs[d % 2], per * qi + d, diagonal=True, q_lo=d * tk)

    ot = jnp.concatenate([acc_sc[0] / l_sc[0], acc_sc[1] / l_sc[1]], axis=0)
    o2 = ot * ot
    ms0 = jnp.sum(o2[0:HEAD_DIM], axis=0, keepdims=True) * (1.0 / HEAD_DIM)
    ms1 = jnp.sum(o2[HEAD_DIM:], axis=0, keepdims=True) * (1.0 / HEAD_DIM)
    inv = jnp.concatenate([jnp.broadcast_to(lax.rsqrt(ms0 + EPS), (HEAD_DIM, tq)),
                           jnp.broadcast_to(lax.rsqrt(ms1 + EPS), (HEAD_DIM, tq))], axis=0)
    o_ref[0] = (ot * inv * gao_ref[...]).T.astype(BF16)


def _attention(q, qaug, k, kaug, vt, km, kaugm, vmt, gao):
    bsz, slen, w = q.shape
    tq, tk = ATTN_Q_TILE, ATTN_K_TILE
    nq, nk = slen // tq, slen // tk
    return pl.pallas_call(
        functools.partial(_attn_kernel, tq, tk),
        grid=(bsz, HEAD_PAIRS, nq),
        in_specs=[
            pl.BlockSpec((1, tq, LANES), lambda b, p, i: (b, i, p)),
            pl.BlockSpec((1, tq, LANES), lambda b, p, i: (b, i, 0)),
            pl.BlockSpec((1, slen, LANES), lambda b, p, i: (b, 0, p)),
            pl.BlockSpec((1, slen, LANES), lambda b, p, i: (b, 0, 0)),
            pl.BlockSpec((1, nk, LANES, tk), lambda b, p, i: (b, 0, p, 0)),
            pl.BlockSpec((N_META, LANES), lambda b, p, i: (0, p)),
            pl.BlockSpec((N_META, LANES), lambda b, p, i: (0, 0)),
            pl.BlockSpec((LANES, N_META), lambda b, p, i: (p, 0)),
            pl.BlockSpec((LANES, 1), lambda b, p, i: (p, 0)),
        ],
        out_specs=pl.BlockSpec((1, tq, LANES), lambda b, p, i: (b, i, p)),
        out_shape=jax.ShapeDtypeStruct((bsz, slen, w), BF16),
        scratch_shapes=[pltpu.VMEM((2, 1, tq), F32), pltpu.VMEM((2, 1, tq), F32),
                        pltpu.VMEM((2, HEAD_DIM, tq), F32),
                        pltpu.VMEM((2, tk, tq), F32), pltpu.VMEM((2, tk, tq), F32),
                        pltpu.VMEM((2, 1, tq), F32), pltpu.VMEM((2, 1, tq), F32)],
        compiler_params=pltpu.CompilerParams(
            dimension_semantics=("arbitrary", "arbitrary", "arbitrary"), vmem_limit_bytes=VMEM_LIMIT),
        name="fox_attention",
    )(q, qaug, k, kaug, vt, km, kaugm, vmt, gao)


ROUTE_ROWS = 64


def _outproj_kernel(tm, attn_ref, conv_ref, x_ref, wo_ref, gffn_ref, wrt_ref, brc_ref, tri_ref,
                    h_ref, xt_ref, ri_ref, rf_ref, cnt_ref, carry, stage):
    i = pl.program_id(0)

    @pl.when(i == 0)
    def _():
        carry[...] = jnp.zeros(carry.shape, F32)

    mixed = jnp.concatenate([attn_ref[...], conv_ref[...]], axis=1)
    h = x_ref[...] + jnp.dot(mixed, wo_ref[...], preferred_element_type=F32)
    h_ref[...] = h
    ms = jnp.mean(h * h, axis=-1, keepdims=True)
    xt = (h * lax.rsqrt(ms + EPS)) * gffn_ref[...]
    _store_token_rows(xt_ref, xt, tm, stage)

    x_hi = xt.astype(BF16)
    x_lo = (xt - x_hi.astype(F32)).astype(BF16)
    parts = lax.dot_general(wrt_ref[...], jnp.concatenate([x_hi, x_lo], axis=1), (((1,), (1,)), ((), ())),
                            preferred_element_type=F32)
    logits = (parts[0:ROUTE_ROWS] + parts[LANES:LANES + ROUTE_ROWS]) + brc_ref[0:ROUTE_ROWS, 0:1]
    row = lax.broadcasted_iota(jnp.int32, logits.shape, 0)
    rowf = row.astype(F32)
    big = float(LANES)

    def first_argmax(vals, vmax):
        return jnp.min(jnp.where(vals == vmax, rowf, big), axis=0, keepdims=True)

    is_g = row < N_EXPERT_GROUPS
    gl = jnp.where(is_g, logits, MASK_VALUE)
    gmax = jnp.max(gl, axis=0, keepdims=True)
    gidx = first_argmax(gl, gmax)
    g_p = 1.0 / jnp.sum(jnp.where(is_g, jnp.exp(gl - gmax), 0.0), axis=0, keepdims=True)

    base = N_EXPERT_GROUPS + EXPERTS_PER_GROUP * gidx
    in_grp = (rowf >= base) & (rowf < base + EXPERTS_PER_GROUP)
    el = jnp.where(in_grp, logits, MASK_VALUE)
    l1 = jnp.max(el, axis=0, keepdims=True)
    e1 = first_argmax(el, l1)
    el2 = jnp.where(rowf == e1, MASK_VALUE, el)
    l2 = jnp.max(el2, axis=0, keepdims=True)
    e2 = first_argmax(el2, l2)
    zsum = jnp.sum(jnp.where(in_grp, jnp.exp(el - l1), 0.0), axis=0, keepdims=True)
    p1 = 1.0 / zsum
    p2 = jnp.exp(l2 - l1) / zsum
    den = p1 + p2
    w1 = g_p * p1 / den
    w2 = g_p * p2 / den
    id1 = e1 - N_EXPERT_GROUPS
    id2 = e2 - N_EXPERT_GROUPS

    oh1 = rowf == id1
    oh2 = rowf == id2
    oh = jnp.where(oh1 | oh2, 1.0, 0.0)
    before = jnp.dot(oh.astype(BF16), tri_ref[...], preferred_element_type=F32) + carry[:, 0:1]
    rank1 = jnp.sum(jnp.where(oh1, before, 0.0), axis=0, keepdims=True)
    rank2 = jnp.sum(jnp.where(oh2, before, 0.0), axis=0, keepdims=True)
    carry[...] = carry[...] + jnp.sum(oh, axis=1, keepdims=True)
    cnt_ref[...] = carry[...]

    pad_i = jnp.zeros((SUBLANES - 2 * TOP_K, tm), F32)
    ri_ref[0] = jnp.concatenate([id1, id2, rank1, rank2, pad_i], axis=0).astype(jnp.int32)
    pad_f = jnp.zeros((LANES - TOP_K, tm), F32)
    rf_ref[...] = jnp.concatenate([w1, w2, pad_f], axis=0).T[:, 0:SUBLANES]


def _outproj(attn, conv, x, wo, gffn, wrt, brc):
    n = x.shape[0]
    tm = OUT_TILE
    tri = jnp.asarray(np.triu(np.ones((tm, tm), np.float32), k=1), BF16)

    def full(a):
        return pl.BlockSpec(a.shape, lambda i: (0,) * a.ndim)

    rows = lambda width: pl.BlockSpec((tm, width), lambda i: (i, 0))
    return pl.pallas_call(
        functools.partial(_outproj_kernel, tm),
        grid=(n // tm,),
        in_specs=[rows(ATTN_WIDTH), rows(CONV_WIDTH), rows(D_MODEL)] + [
            full(a) for a in (wo, gffn, wrt, brc, tri)],
        out_specs=[rows(D_MODEL), pl.BlockSpec((tm * ROW_SUBLANES, LANES), lambda i: (i, 0)),
                   pl.BlockSpec((1, SUBLANES, tm), lambda i: (i, 0, 0)), rows(SUBLANES),
                   pl.BlockSpec((ROUTE_ROWS, LANES), lambda i: (0, 0))],
        out_shape=[jax.ShapeDtypeStruct((n, D_MODEL), F32),
                   jax.ShapeDtypeStruct((n * ROW_SUBLANES, LANES), BF16),
                   jax.ShapeDtypeStruct((n // tm, SUBLANES, tm), jnp.int32),
                   jax.ShapeDtypeStruct((n, SUBLANES), F32),
                   jax.ShapeDtypeStruct((ROUTE_ROWS, LANES), F32)],
        scratch_shapes=[pltpu.VMEM((ROUTE_ROWS, LANES), F32), pltpu.VMEM((tm * ROW_SUBLANES, LANES), F32)],
        compiler_params=pltpu.CompilerParams(
            dimension_semantics=("arbitrary",), vmem_limit_bytes=VMEM_LIMIT),
        name="outproj_router",
    )(attn, conv, x, wo, gffn, wrt, brc, tri)


def _dest_kernel(rs_ref, ri_ref, dest_ref):
    ri = ri_ref[...]
    experts = ri[:, 0:TOP_K, :]
    start = jnp.zeros_like(experts)
    for e in range(N_EXPERTS):
        start = jnp.where(experts == e, rs_ref[e], start)
    dest_ref[...] = jnp.concatenate([(start + ri[:, TOP_K:2 * TOP_K, :]) * ROW_SUBLANES,
                                     jnp.zeros((ri.shape[0], SUBLANES - TOP_K, ri.shape[2]), jnp.int32)], axis=1)


def _dest_rows(ri_t, row_start):
    grid_spec = pltpu.PrefetchScalarGridSpec(
        num_scalar_prefetch=1,
        grid=(1,),
        in_specs=[pl.BlockSpec(ri_t.shape, lambda i, rs: (0, 0, 0))],
        out_specs=pl.BlockSpec(ri_t.shape, lambda i, rs: (0, 0, 0)),
    )
    return pl.pallas_call(
        _dest_kernel,
        grid_spec=grid_spec,
        out_shape=jax.ShapeDtypeStruct(ri_t.shape, jnp.int32),
        compiler_params=pltpu.CompilerParams(dimension_semantics=("arbitrary",)),
        name="moe_dest",
    )(row_start, ri_t)


def _tile_copy(src, src_start, dst, dst_start, sem):
    rs = ROW_SUBLANES
    return pltpu.make_async_copy(src.at[pl.ds(pl.multiple_of(src_start, rs), rs), :],
                                 dst.at[pl.ds(pl.multiple_of(dst_start, rs), rs), :], sem)


def _dispatch_kernel(tm, bm, fz_ref, fv_ref, dest_ref, xt_ref, xs_hbm, zeros, sem, zsem):
    i = pl.program_id(0)
    block_rows = bm * ROW_SUBLANES

    def zero_block(z):
        start = pl.multiple_of(fz_ref[z] * block_rows, block_rows)
        return pltpu.make_async_copy(zeros, xs_hbm.at[pl.ds(start, block_rows), :], zsem.at[0])

    @pl.when(i == 0)
    def _():
        zeros[...] = jnp.zeros(zeros.shape, BF16)
        for z in range(fz_ref.shape[0]):
            @pl.when(fv_ref[z] == 1)
            def _():
                zero_block(z).start()
        for z in range(fz_ref.shape[0]):
            @pl.when(fv_ref[z] == 1)
            def _():
                zero_block(z).wait()

    for g in range(0, tm, DMA_GROUP):
        dests = [(r, kk, dest_ref[0, kk, r]) for r in range(g, g + DMA_GROUP) for kk in range(TOP_K)]
        for r, kk, d in dests:
            _tile_copy(xt_ref, r * ROW_SUBLANES, xs_hbm, d, sem.at[0]).start(priority=kk)
    for kk in range(TOP_K):
        pltpu.make_async_copy(xt_ref, xs_hbm.at[pl.ds(0, tm * ROW_SUBLANES), :], sem.at[0]).wait()


def _route_spec(dest, tm, extra_args):
    per = dest.shape[2] // tm
    nt = dest.shape[0] * per
    if extra_args == 0:
        index = lambda i: (jnp.minimum(i, nt - 1) // per, 0, jnp.minimum(i, nt - 1) % per)
    else:
        index = lambda i, *_: (jnp.minimum(i, nt - 1) // per, 0, jnp.minimum(i, nt - 1) % per)
    return nt, pl.BlockSpec((1, SUBLANES, tm), index, memory_space=pltpu.SMEM)


def _dispatch(xt_rows, dest, fill_blocks, fill_valid, nb):
    tm = DISPATCH_TILE
    bm = EXPERT_ROWS
    nt, dest_spec = _route_spec(dest, tm, 2)
    grid_spec = pltpu.PrefetchScalarGridSpec(
        num_scalar_prefetch=2,
        grid=(nt,),
        in_specs=[dest_spec, pl.BlockSpec((tm * ROW_SUBLANES, LANES), lambda i, fz, fv: (i, 0))],
        out_specs=pl.BlockSpec(memory_space=pl.ANY),
        scratch_shapes=[pltpu.VMEM((bm * ROW_SUBLANES, LANES), BF16), pltpu.SemaphoreType.DMA((1,)),
                        pltpu.SemaphoreType.DMA((1,))],
    )
    return pl.pallas_call(
        functools.partial(_dispatch_kernel, tm, bm),
        grid_spec=grid_spec,
        out_shape=jax.ShapeDtypeStruct((nb * bm * ROW_SUBLANES, LANES), BF16),
        compiler_params=pltpu.CompilerParams(
            dimension_semantics=("arbitrary",), vmem_limit_bytes=VMEM_LIMIT),
        name="moe_dispatch",
    )(fill_blocks, fill_valid, dest, xt_rows)


def _expert_kernel(bm, be_ref, nu_ref, first_ref, nxt_ref, slot_ref, xs_ref, wg_hbm, wu_hbm, wd_hbm, ys_ref,
                   wg_buf, wu_buf, wd_buf, stage_in, wsem):
    step = pl.program_id(0)

    def weight_copies(e, slot):
        return [pltpu.make_async_copy(src.at[e], buf.at[slot], wsem.at[m, slot])
                for m, (src, buf) in enumerate(((wg_hbm, wg_buf), (wu_hbm, wu_buf), (wd_hbm, wd_buf)))]

    @pl.when(step == 0)
    def _():
        for c in weight_copies(be_ref[0], 0):
            c.start()

    def one_block(j, xs_blk, ys_blk):
        @pl.when(j < nu_ref[0])
        def _():
            slot = slot_ref[j]

            @pl.when(first_ref[j] == 1)
            def _():
                for c in weight_copies(0, slot):
                    c.wait()

                @pl.when(nxt_ref[j] >= 0)
                def _():
                    for c in weight_copies(nxt_ref[j], 1 - slot):
                        c.start()

            x = _load_token_rows(xs_blk, bm, stage_in)
            hg = jnp.dot(x, wg_buf[slot], preferred_element_type=F32)
            hu = jnp.dot(x, wu_buf[slot], preferred_element_type=F32)
            hdn = hg * (1.0 / (1.0 + jnp.exp(-hg))) * hu
            y = jnp.dot(hdn, wd_buf[slot], preferred_element_type=F32)
            _store_token_rows(ys_blk, y, bm)

        @pl.when(j >= nu_ref[0])
        def _():
            ys_blk[...] = jnp.zeros(ys_blk.shape, F32)

    rows = bm * ROW_SUBLANES
    for sub in range(EXPERT_BLOCKS_PER_STEP):
        view = pl.ds(sub * rows, rows)
        one_block(step * EXPERT_BLOCKS_PER_STEP + sub, xs_ref.at[view, :], ys_ref.at[view, :])


def _experts(xs_rows, block_expert, n_used, w_gate, w_up, w_down):
    bm = EXPERT_ROWS
    nb = block_expert.shape[0]
    blk = jnp.arange(nb, dtype=jnp.int32)
    live = blk < n_used[0]
    first = live & ((blk == 0) | (block_expert != jnp.roll(block_expert, 1)))
    slot = (jnp.cumsum(first.astype(jnp.int32)) - 1) % 2
    later_first = first[None, :] & (blk[None, :] > blk[:, None])
    nxt_start = jnp.min(jnp.where(later_first, blk[None, :], nb), axis=1)
    nxt = jnp.sum(jnp.where(blk[None, :] == nxt_start[:, None], block_expert[None, :], 0), axis=1)
    nxt = jnp.where(nxt_start < nb, nxt, -1)
    per = EXPERT_BLOCKS_PER_STEP
    assert nb % per == 0
    step_rows = per * bm * ROW_SUBLANES
    grid_spec = pltpu.PrefetchScalarGridSpec(
        num_scalar_prefetch=5,
        grid=(nb // per,),
        in_specs=[
            pl.BlockSpec((step_rows, LANES), lambda s, be, nu, *_: (jnp.minimum(s, (nu[0] - 1) // per), 0)),
            pl.BlockSpec(memory_space=pl.ANY),
            pl.BlockSpec(memory_space=pl.ANY),
            pl.BlockSpec(memory_space=pl.ANY),
        ],
        out_specs=pl.BlockSpec((step_rows, LANES), lambda s, *_: (s, 0)),
        scratch_shapes=[pltpu.VMEM((2, D_MODEL, D_EXPERT), F32), pltpu.VMEM((2, D_MODEL, D_EXPERT), F32),
                        pltpu.VMEM((2, D_EXPERT, D_MODEL), F32),
                        pltpu.VMEM((bm * ROW_SUBLANES, LANES), F32), pltpu.SemaphoreType.DMA((3, 2))],
    )
    return pl.pallas_call(
        functools.partial(_expert_kernel, bm),
        grid_spec=grid_spec,
        out_shape=jax.ShapeDtypeStruct((nb * bm * ROW_SUBLANES, LANES), F32),
        compiler_params=pltpu.CompilerParams(
            dimension_semantics=("arbitrary",), vmem_limit_bytes=VMEM_LIMIT),
        name="moe_experts",
    )(block_expert, n_used, first.astype(jnp.int32), nxt.astype(jnp.int32), slot.astype(jnp.int32),
      xs_rows, w_gate, w_up, w_down)


def _combine_kernel(tm, dest_ref, ys_hbm, h_ref, rf_ref, o_ref, ybuf, sem):
    i = pl.program_id(0)
    nt = pl.num_programs(0) - 1

    @pl.when(i < nt)
    def _():
        slot = i % 2
        for g in range(0, tm, DMA_GROUP):
            dests = [(r, kk, dest_ref[0, kk, r]) for r in range(g, g + DMA_GROUP) for kk in range(TOP_K)]
            for r, kk, d in dests:
                _tile_copy(ys_hbm, d, ybuf.at[kk, slot], r * ROW_SUBLANES, sem.at[kk, slot]).start(priority=kk)

    @pl.when(i >= 1)
    def _():
        slot = (i - 1) % 2
        for kk in range(TOP_K):
            pltpu.make_async_copy(ys_hbm.at[pl.ds(0, tm * ROW_SUBLANES), :], ybuf.at[kk, slot],
                                  sem.at[kk, slot]).wait()
        rf = rf_ref[...]
        o_ref[...] = (h_ref[...] + rf[:, 0:1] * _load_token_rows(ybuf.at[0, slot], tm)
                      + rf[:, 1:2] * _load_token_rows(ybuf.at[1, slot], tm))


def _combine(ys_rows, dest, h, rf):
    n = h.shape[0]
    tm = COMBINE_TILE
    nt, dest_spec = _route_spec(dest, tm, 0)
    prev = lambda i: jnp.maximum(i - 1, 0)
    return pl.pallas_call(
        functools.partial(_combine_kernel, tm),
        grid=(nt + 1,),
        in_specs=[
            dest_spec,
            pl.BlockSpec(memory_space=pl.ANY),
            pl.BlockSpec((tm, D_MODEL), lambda i: (prev(i), 0)),
            pl.BlockSpec((tm, SUBLANES), lambda i: (prev(i), 0)),
        ],
        out_specs=pl.BlockSpec((tm, D_MODEL), lambda i: (prev(i), 0)),
        out_shape=jax.ShapeDtypeStruct((n, D_MODEL), F32),
        scratch_shapes=[pltpu.VMEM((TOP_K, 2, tm * ROW_SUBLANES, LANES), F32),
                        pltpu.SemaphoreType.DMA((TOP_K, 2))],
        compiler_params=pltpu.CompilerParams(
            dimension_semantics=("arbitrary",), vmem_limit_bytes=VMEM_LIMIT),
        name="moe_combine",
    )(dest, ys_rows, h, rf)


def _pad_lanes(a, width=LANES):
    return jnp.pad(a, ((0, 0), (0, width - a.shape[-1])))


def _layer(x, meta_tokens, g_mix, w_in, b_forget, q_norm_g, k_norm_g, conv_w, attn_out_g, conv_out_g,
           w_out, g_ffn, w_rg, b_rg, w_re, b_re, w_gate, w_up, w_down):
    bsz, slen, _ = x.shape
    n = bsz * slen
    w = ATTN_WIDTH

    wq, wk, wv, wf, wb, wc, whc = jnp.split(
        w_in, [w, 2 * w, 3 * w, 3 * w + ATTN_HEADS, 4 * w + ATTN_HEADS, 5 * w + ATTN_HEADS], axis=1)
    w1 = jnp.concatenate([wq, wk, wb, wc, whc, _pad_lanes(wf)], axis=1).astype(BF16)
    head_of = np.arange(MXU_DIM) // HEAD_DIM
    gmat = jnp.asarray(head_of[:, None] == head_of[None, :], BF16)
    consts = (
        g_mix.reshape(1, D_MODEL), w1, wv.T.astype(BF16), _pad_lanes(b_forget.reshape(1, ATTN_HEADS)),
        jnp.tile(q_norm_g, ATTN_HEADS).reshape(1, w) * (HEAD_DIM ** -0.5 * LOG2E),
        jnp.tile(k_norm_g, ATTN_HEADS).reshape(1, w),
        jnp.pad(conv_w, ((0, SUBLANES - CONV_K), (0, 0))), conv_out_g.reshape(1, w), gmat)

    xm = jnp.pad(meta_tokens, ((0, META_TILE - N_META), (0, 0)))[None]
    km, vmt, kaugm, cum_m, uc_m = _inproj(xm, True, META_TILE, consts,
                                          jnp.zeros((SUBLANES, w), F32), jnp.zeros((1, LANES), F32))
    km, vmt, kaugm = km[0, :N_META], vmt[0, 0, :, :N_META], kaugm[0, :N_META]
    cum_m = cum_m[0, :N_META]
    halo = jnp.zeros((SUBLANES, w), F32).at[SUBLANES - 2:].set(uc_m[0, N_META - 2:N_META])
    cum0 = cum_m[N_META - 1:N_META]

    q, k, vt, conv, qaug, kaug = _inproj(x, False, IN_TILE, consts, halo, cum0)
    attn = _attention(q, qaug, k, kaug, vt, km, kaugm, vmt, attn_out_g.reshape(w, 1))

    wr = _pad_lanes(jnp.concatenate([w_rg, w_re], axis=1)).T
    wr_hi = wr.astype(BF16)
    wr_lo = (wr - wr_hi.astype(F32)).astype(BF16)
    wrt = jnp.concatenate([jnp.concatenate([wr_hi, wr_hi], axis=1),
                           jnp.concatenate([wr_lo, jnp.zeros_like(wr_lo)], axis=1)], axis=0)
    brc = jnp.broadcast_to(_pad_lanes(jnp.concatenate([b_rg, b_re]).reshape(1, -1)).T, (LANES, LANES))
    h, xt, ri, rf, cnt = _outproj(attn.reshape(n, w), conv.reshape(n, w), x.reshape(n, D_MODEL),
                                  w_out.astype(BF16), g_ffn.reshape(1, D_MODEL), wrt, brc)

    bm = EXPERT_ROWS
    nb = (n * TOP_K) // bm + N_EXPERTS
    counts = cnt[:N_EXPERTS, 0].astype(jnp.int32)
    blocks_e = (counts + bm - 1) // bm
    blk_end = jnp.cumsum(blocks_e)
    row_start = (blk_end - blocks_e) * bm
    blk = jnp.arange(nb, dtype=jnp.int32)
    block_expert = jnp.minimum(jnp.sum(blk_end[None, :] <= blk[:, None], axis=1), N_EXPERTS - 1).astype(jnp.int32)
    n_used = blk_end[-1:].astype(jnp.int32)
    tail = n_used[0] + jnp.arange(N_EXPERTS, dtype=jnp.int32)
    fill_blocks = jnp.clip(jnp.concatenate([blk_end - 1, tail]), 0, nb - 1).astype(jnp.int32)
    fill_valid = jnp.concatenate([counts % bm != 0, tail < nb]).astype(jnp.int32)

    dest = _dest_rows(ri, row_start.astype(jnp.int32))
    xs = _dispatch(xt, dest, fill_blocks, fill_valid, nb)
    ys = _experts(xs, block_expert, n_used, w_gate, w_up, w_down)
    out = _combine(ys, dest, h, rf)
    return out.reshape(bsz, slen, D_MODEL)


def kernel(x, meta_tokens, norm_mix_g, w_in, b_forget, q_norm_g, k_norm_g, conv_w, attn_out_g, conv_out_g,
           w_out, norm_ffn_g, w_router_group, b_router_group, w_router_expert, b_router_expert,
           w_gate, w_up, w_down):
    assert norm_mix_g.shape[0] == 1, "single-layer block"
    return _layer(x, meta_tokens, norm_mix_g[0], w_in[0], b_forget[0], q_norm_g[0], k_norm_g[0], conv_w[0],
                  attn_out_g[0], conv_out_g[0], w_out[0], norm_ffn_g[0], w_router_group[0],
                  b_router_group[0], w_router_expert[0], b_router_expert[0], w_gate[0], w_up[0], w_down[0])
```

```python
import functools

import jax
import jax.numpy as jnp
import numpy as np
from jax import lax
from jax.experimental import pallas as pl
from jax.experimental.pallas import tpu as pltpu

D_MODEL = 1024
N_META = 16
HEAD_DIM = 64
ATTN_HEADS = 8
ATTN_WIDTH = ATTN_HEADS * HEAD_DIM
CONV_WIDTH = D_MODEL - ATTN_WIDTH
CONV_K = 3
N_EXPERT_GROUPS = 4
EXPERTS_PER_GROUP = 8
N_EXPERTS = N_EXPERT_GROUPS * EXPERTS_PER_GROUP
TOP_K = 2
D_EXPERT = 512
EPS = 1e-6
MASK_VALUE = -1e30
LOG2E = 1.4426950408889634
AUG = 6

LANES = 128
SUBLANES = 8
MXU_DIM = 256
HEAD_PAIRS = ATTN_HEADS * HEAD_DIM // LANES
PROJ_PAD_COLS = 5 * ATTN_WIDTH + LANES
VMEM_LIMIT = 56 * 1024 * 1024

IN_TILE = 512
META_TILE = 128
ATTN_Q_TILE = 2048
ATTN_K_TILE = 512
OUT_TILE = 1024
EXPERT_ROWS = 256
EXPERT_BLOCKS_PER_STEP = 4
X_BUFFERS = 3
DISPATCH_TILE = 1024
COMBINE_TILE = 256
DMA_GROUP = 8

F32 = jnp.float32
BF16 = jnp.bfloat16
ROW_SUBLANES = D_MODEL // LANES


def _lane_iota(shape):
    return lax.broadcasted_iota(jnp.int32, shape, len(shape) - 1)


def _store_token_rows(ref, x, rows, stage=None):
    dst = ref if stage is None else stage
    for sl in range(ROW_SUBLANES):
        dst[pl.ds(sl, rows, stride=ROW_SUBLANES), :] = x[:, sl * LANES:(sl + 1) * LANES]
    if stage is not None:
        ref[...] = stage[...].astype(ref.dtype)


def _load_token_rows(ref, rows, stage=None):
    src = ref
    if stage is not None:
        stage[...] = ref[...].astype(F32)
        src = stage
    return jnp.concatenate([src[pl.ds(sl, rows, stride=ROW_SUBLANES), :] for sl in range(ROW_SUBLANES)], axis=1)


def _inproj_kernel(is_meta, tm, x_ref, gmix_ref, w1_ref, wvt_ref, bf_ref, gq_ref, gk_ref, cw_ref, gco_ref,
                   gmat_ref, tri_ref, eq_ref, ek_ref, halo_ref, cum0_ref, *rest):
    if is_meta:
        k_ref, vt_ref, kaug_ref, cum_ref, uc_ref, ucbuf, carry = rest
    else:
        q_ref, k_ref, vt_ref, conv_ref, qaug_ref, kaug_ref, ucbuf, carry = rest
    t = pl.program_id(1)

    @pl.when(t == 0)
    def _():
        ucbuf[0:SUBLANES, :] = halo_ref[...]
        carry[0:1, :] = cum0_ref[...]

    x = x_ref[0]
    ms = jnp.mean(x * x, axis=-1, keepdims=True)
    u = (x * lax.rsqrt(ms + EPS)) * gmix_ref[...]
    ub = u.astype(BF16)
    proj = jnp.dot(ub, w1_ref[...], preferred_element_type=F32)
    vt_ref[0, 0] = lax.dot_general(wvt_ref[...], ub, (((1,), (1,)), ((), ())),
                                   preferred_element_type=F32).astype(BF16)

    def head_norm(z, g):
        z2 = (z * z).astype(BF16)
        half = gmat_ref.shape[0]
        ssq = jnp.concatenate([jnp.dot(z2[:, c:c + half], gmat_ref[...], preferred_element_type=F32)
                               for c in range(0, z.shape[1], half)], axis=1)
        return z * lax.rsqrt(ssq * (1.0 / HEAD_DIM) + EPS) * g

    w = ATTN_WIDTH
    kn = head_norm(proj[:, w:2 * w], gk_ref[...])
    k_ref[0] = kn.astype(BF16)

    z = proj[:, 5 * w:5 * w + LANES] + bf_ref[...]
    ls = jnp.minimum(z, 0.0) - jnp.log1p(jnp.exp(-jnp.abs(z)))
    lane_c = _lane_iota(ls.shape)
    is_head = lane_c < ATTN_HEADS
    ls = jnp.where(is_head, ls, 0.0)

    def pieces(val):
        p_hi = val.astype(BF16).astype(F32)
        rem = val - p_hi
        p_mid = rem.astype(BF16).astype(F32)
        p_lo = (rem - p_mid).astype(BF16).astype(F32)
        return jnp.where(is_head, p_hi,
                         jnp.where(lane_c < 2 * ATTN_HEADS, pltpu.roll(p_mid, ATTN_HEADS, axis=1),
                                   pltpu.roll(p_lo, 2 * ATTN_HEADS, axis=1))).astype(BF16)

    cs3 = jnp.dot(tri_ref[...], pieces(ls), preferred_element_type=F32)
    cs = cs3 + pltpu.roll(cs3, LANES - ATTN_HEADS, axis=1) + pltpu.roll(cs3, LANES - 2 * ATTN_HEADS, axis=1)
    cum = jnp.where(is_head, cs, 0.0) + carry[0:1, :]
    carry[0:1, :] = cum[tm - 1:tm, :]

    packed = pieces(cum * LOG2E)
    in_aug = lane_c < AUG * ATTN_HEADS
    ones_k = in_aug & ((lane_c % AUG) < AUG // 2)
    kaug = jnp.where(ones_k, 1.0, -jnp.dot(packed, ek_ref[...], preferred_element_type=F32))
    kaug_ref[0] = kaug.astype(BF16)

    uc = proj[:, 3 * w:4 * w] * proj[:, 4 * w:5 * w]
    ucbuf[SUBLANES:SUBLANES + tm, :] = uc
    uc1 = ucbuf[SUBLANES - 1:SUBLANES - 1 + tm, :]
    uc2 = ucbuf[SUBLANES - 2:SUBLANES - 2 + tm, :]
    ucbuf[0:SUBLANES, :] = uc[tm - SUBLANES:tm, :]

    if is_meta:
        cum_ref[0] = cum
        uc_ref[0] = uc
        return

    qn = head_norm(proj[:, 0:w], gq_ref[...])
    q_ref[0] = qn.astype(BF16)
    y = cw_ref[0:1, :] * uc2 + cw_ref[1:2, :] * uc1 + cw_ref[2:3, :] * uc
    conv = proj[:, 2 * w:3 * w] * y
    conv_ref[0] = head_norm(conv, gco_ref[...]).astype(BF16)
    ones_q = in_aug & ((lane_c % AUG) >= AUG // 2)
    qaug = jnp.where(ones_q, 1.0, jnp.dot(packed, eq_ref[...], preferred_element_type=F32))
    qaug_ref[0] = qaug.astype(BF16)


def _inproj(x, is_meta, tm, consts, halo, cum0):
    bsz, tlen, _ = x.shape
    nt = tlen // tm
    gmix, w1, wvt, bfp, gq, gk, cw, gco, gmat = consts
    tri = jnp.asarray(np.tril(np.ones((tm, tm), np.float32)), BF16)
    w = ATTN_WIDTH
    src = np.arange(LANES)
    dst = np.arange(LANES)
    piece, head = src // ATTN_HEADS, src % ATTN_HEADS
    valid = src < 3 * ATTN_HEADS
    eq = jnp.asarray(valid[:, None] & (dst[None, :] == (AUG * head + piece)[:, None]), BF16)
    ek = jnp.asarray(valid[:, None] & (dst[None, :] == (AUG * head + AUG // 2 + piece)[:, None]), BF16)

    def full(a):
        return pl.BlockSpec(a.shape, lambda b, t: (0,) * a.ndim)

    in_specs = [pl.BlockSpec((1, tm, D_MODEL), lambda b, t: (b, t, 0))] + [
        full(a) for a in (gmix, w1, wvt, bfp, gq, gk, cw, gco, gmat, tri, eq, ek, halo, cum0)]
    tok = lambda width: pl.BlockSpec((1, tm, width), lambda b, t: (b, t, 0))
    vt_spec = pl.BlockSpec((1, 1, w, tm), lambda b, t: (b, t, 0, 0))
    if is_meta:
        out_shape = [jax.ShapeDtypeStruct((bsz, tlen, w), BF16),
                     jax.ShapeDtypeStruct((bsz, nt, w, tm), BF16),
                     jax.ShapeDtypeStruct((bsz, tlen, LANES), BF16),
                     jax.ShapeDtypeStruct((bsz, tlen, LANES), F32),
                     jax.ShapeDtypeStruct((bsz, tlen, w), F32)]
        out_specs = [tok(w), vt_spec, tok(LANES), tok(LANES), tok(w)]
    else:
        out_shape = [jax.ShapeDtypeStruct((bsz, tlen, w), BF16)] * 2 + [
            jax.ShapeDtypeStruct((bsz, nt, w, tm), BF16),
            jax.ShapeDtypeStruct((bsz, tlen, w), BF16),
            jax.ShapeDtypeStruct((bsz, tlen, LANES), BF16),
            jax.ShapeDtypeStruct((bsz, tlen, LANES), BF16)]
        out_specs = [tok(w), tok(w), vt_spec, tok(w), tok(LANES), tok(LANES)]
    return pl.pallas_call(
        functools.partial(_inproj_kernel, is_meta, tm),
        grid=(bsz, nt),
        in_specs=in_specs,
        out_specs=out_specs,
        out_shape=out_shape,
        scratch_shapes=[pltpu.VMEM((tm + SUBLANES, w), F32), pltpu.VMEM((SUBLANES, LANES), F32)],
        compiler_params=pltpu.CompilerParams(
            dimension_semantics=("arbitrary", "arbitrary"), vmem_limit_bytes=VMEM_LIMIT),
        name="inproj_meta" if is_meta else "inproj",
    )(x, gmix, w1, wvt, bfp, gq, gk, cw, gco, gmat, tri, eq, ek, halo, cum0)


def _attn_kernel(tq, tk, q_ref, qaug_ref, k_ref, kaug_ref, vt_ref, km_ref, kaugm_ref, vmt_ref, gao_ref,
                 o_ref, m_sc, l_sc, acc_sc, sa_sc, sb_sc, ma_sc, mb_sc):
    hp = pl.program_id(1)
    qi = pl.program_id(2)
    q = q_ref[0]
    qaug = qaug_ref[0]
    lane = _lane_iota(q.shape)
    first = lane < HEAD_DIM
    zero = jnp.zeros_like(q)
    qcat = []
    for hh in range(2):
        qh = jnp.where(first, q, zero) if hh == 0 else jnp.where(first, zero, q)
        lo = AUG * (2 * hp + hh)
        qa = jnp.where((lane >= lo) & (lane < lo + AUG), qaug, zero)
        qcat.append(jnp.concatenate([qh, qa], axis=1))

    m_sc[...] = jnp.full(m_sc.shape, MASK_VALUE, F32)
    l_sc[...] = jnp.zeros(l_sc.shape, F32)
    acc_sc[...] = jnp.zeros(acc_sc.shape, F32)

    def scores_t(kcat, q_lo=0):
        return tuple(lax.dot_general(kcat, qcat[hh][q_lo:], (((1,), (1,)), ((), ())), preferred_element_type=F32)
                     for hh in range(2))

    def update(hh, st, vtb, q_lo=0, q_hi=None, block_max=None):
        qs = slice(q_lo, q_hi)
        m_old = m_sc[hh, :, qs]
        if block_max is None:
            block_max = jnp.max(st, axis=0, keepdims=True)
        m_new = jnp.maximum(m_old, block_max)
        alpha = jnp.exp2(m_old - m_new)
        p = jnp.exp2(st - m_new)
        l_sc[hh, :, qs] = alpha * l_sc[hh, :, qs] + jnp.sum(p, axis=0, keepdims=True)
        vh = vtb[hh * HEAD_DIM:(hh + 1) * HEAD_DIM, :]
        acc_sc[hh, :, qs] = alpha * acc_sc[hh, :, qs] + jnp.dot(vh, p.astype(BF16), preferred_element_type=F32)
        m_sc[hh, :, qs] = m_new

    def key_block(j):
        start = pl.multiple_of(j * tk, tk)
        return jnp.concatenate([k_ref[0, pl.ds(start, tk), :], kaug_ref[0, pl.ds(start, tk), :]], axis=1)

    def put_scores(j, bufs, q_lo=0):
        buf, mbuf = bufs
        sts = scores_t(key_block(j), q_lo)
        for hh in range(2):
            buf[hh, :, q_lo:] = sts[hh]
            mbuf[hh, :, q_lo:] = jnp.max(sts[hh], axis=0, keepdims=True)

    def consume(bufs, j, diagonal=False, q_lo=0):
        buf, mbuf = bufs
        vtb = vt_ref[0, j]
        for hh in range(2):
            if diagonal:
                st = buf[hh, :, q_lo:q_lo + tk]
                key = lax.broadcasted_iota(jnp.int32, st.shape, 0)
                qry = lax.broadcasted_iota(jnp.int32, st.shape, 1)
                update(hh, jnp.where(key <= qry, st, MASK_VALUE), vtb, q_lo, q_lo + tk)
                if q_lo + tk < tq:
                    update(hh, buf[hh, :, q_lo + tk:], vtb, q_lo + tk, None, mbuf[hh, :, q_lo + tk:])
            else:
                update(hh, buf[hh, :, q_lo:], vtb, q_lo, None, mbuf[hh, :, q_lo:])

    per = tq // tk
    assert per % 2 == 0
    st_m = scores_t(jnp.concatenate([km_ref[...], kaugm_ref[...]], axis=1))
    buf_a, buf_b = (sa_sc, ma_sc), (sb_sc, mb_sc)
    put_scores(0, buf_a)
    vmt = vmt_ref[...]
    for hh in range(2):
        update(hh, st_m[hh], vmt)

    def body(t, carry):
        j = 2 * t
        put_scores(j + 1, buf_b)
        consume(buf_a, j)
        put_scores(j + 2, buf_a)
        consume(buf_b, j + 1)
        return carry

    lax.fori_loop(0, (per // 2) * qi, body, 0)
    bufs = (buf_a, buf_b)
    for d in range(per):
        if d + 1 < per:
            put_scores(per * qi + d + 1, bufs[(d + 1) % 2], (d + 1) * tk)
        consume(bufs[d % 2], per * qi + d, diagonal=True, q_lo=d * tk)

    ot = jnp.concatenate([acc_sc[0] / l_sc[0], acc_sc[1] / l_sc[1]], axis=0)
    o2 = ot * ot
    ms0 = jnp.sum(o2[0:HEAD_DIM], axis=0, keepdims=True) * (1.0 / HEAD_DIM)
    ms1 = jnp.sum(o2[HEAD_DIM:], axis=0, keepdims=True) * (1.0 / HEAD_DIM)
    inv = jnp.concatenate([jnp.broadcast_to(lax.rsqrt(ms0 + EPS), (HEAD_DIM, tq)),
                           jnp.broadcast_to(lax.rsqrt(ms1 + EPS), (HEAD_DIM, tq))], axis=0)
    o_ref[0] = (ot * inv * gao_ref[...]).T.astype(BF16)


def _attention(q, qaug, k, kaug, vt, km, kaugm, vmt, gao):
    bsz, slen, w = q.shape
    tq, tk = ATTN_Q_TILE, ATTN_K_TILE
    nq, nk = slen // tq, slen // tk
    return pl.pallas_call(
        functools.partial(_attn_kernel, tq, tk),
        grid=(bsz, HEAD_PAIRS, nq),
        in_specs=[
            pl.BlockSpec((1, tq, LANES), lambda b, p, i: (b, i, p)),
            pl.BlockSpec((1, tq, LANES), lambda b, p, i: (b, i, 0)),
            pl.BlockSpec((1, slen, LANES), lambda b, p, i: (b, 0, p)),
            pl.BlockSpec((1, slen, LANES), lambda b, p, i: (b, 0, 0)),
            pl.BlockSpec((1, nk, LANES, tk), lambda b, p, i: (b, 0, p, 0)),
            pl.BlockSpec((N_META, LANES), lambda b, p, i: (0, p)),
            pl.BlockSpec((N_META, LANES), lambda b, p, i: (0, 0)),
            pl.BlockSpec((LANES, N_META), lambda b, p, i: (p, 0)),
            pl.BlockSpec((LANES, 1), lambda b, p, i: (p, 0)),
        ],
        out_specs=pl.BlockSpec((1, tq, LANES), lambda b, p, i: (b, i, p)),
        out_shape=jax.ShapeDtypeStruct((bsz, slen, w), BF16),
        scratch_shapes=[pltpu.VMEM((2, 1, tq), F32), pltpu.VMEM((2, 1, tq), F32),
                        pltpu.VMEM((2, HEAD_DIM, tq), F32),
                        pltpu.VMEM((2, tk, tq), F32), pltpu.VMEM((2, tk, tq), F32),
                        pltpu.VMEM((2, 1, tq), F32), pltpu.VMEM((2, 1, tq), F32)],
        compiler_params=pltpu.CompilerParams(
            dimension_semantics=("arbitrary", "arbitrary", "arbitrary"), vmem_limit_bytes=VMEM_LIMIT),
        name="fox_attention",
    )(q, qaug, k, kaug, vt, km, kaugm, vmt, gao)


ROUTE_ROWS = 64


def _outproj_kernel(tm, attn_ref, conv_ref, x_ref, wo_ref, gffn_ref, wrt_ref, brc_ref, tri_ref,
                    h_ref, xt_ref, ri_ref, rf_ref, cnt_ref, carry, stage):
    i = pl.program_id(0)

    @pl.when(i == 0)
    def _():
        carry[...] = jnp.zeros(carry.shape, F32)

    mixed = jnp.concatenate([attn_ref[...], conv_ref[...]], axis=1)
    h = x_ref[...] + jnp.dot(mixed, wo_ref[...], preferred_element_type=F32)
    h_ref[...] = h
    ms = jnp.mean(h * h, axis=-1, keepdims=True)
    xt = (h * lax.rsqrt(ms + EPS)) * gffn_ref[...]
    _store_token_rows(xt_ref, xt, tm, stage)

    x_hi = xt.astype(BF16)
    x_lo = (xt - x_hi.astype(F32)).astype(BF16)
    parts = lax.dot_general(wrt_ref[...], jnp.concatenate([x_hi, x_lo], axis=1), (((1,), (1,)), ((), ())),
                            preferred_element_type=F32)
    logits = (parts[0:ROUTE_ROWS] + parts[LANES:LANES + ROUTE_ROWS]) + brc_ref[0:ROUTE_ROWS, 0:1]
    row = lax.broadcasted_iota(jnp.int32, logits.shape, 0)
    rowf = row.astype(F32)
    big = float(LANES)

    def first_argmax(vals, vmax):
        return jnp.min(jnp.where(vals == vmax, rowf, big), axis=0, keepdims=True)

    is_g = row < N_EXPERT_GROUPS
    gl = jnp.where(is_g, logits, MASK_VALUE)
    gmax = jnp.max(gl, axis=0, keepdims=True)
    gidx = first_argmax(gl, gmax)
    g_p = 1.0 / jnp.sum(jnp.where(is_g, jnp.exp(gl - gmax), 0.0), axis=0, keepdims=True)

    base = N_EXPERT_GROUPS + EXPERTS_PER_GROUP * gidx
    in_grp = (rowf >= base) & (rowf < base + EXPERTS_PER_GROUP)
    el = jnp.where(in_grp, logits, MASK_VALUE)
    l1 = jnp.max(el, axis=0, keepdims=True)
    e1 = first_argmax(el, l1)
    el2 = jnp.where(rowf == e1, MASK_VALUE, el)
    l2 = jnp.max(el2, axis=0, keepdims=True)
    e2 = first_argmax(el2, l2)
    zsum = jnp.sum(jnp.where(in_grp, jnp.exp(el - l1), 0.0), axis=0, keepdims=True)
    p1 = 1.0 / zsum
    p2 = jnp.exp(l2 - l1) / zsum
    den = p1 + p2
    w1 = g_p * p1 / den
    w2 = g_p * p2 / den
    id1 = e1 - N_EXPERT_GROUPS
    id2 = e2 - N_EXPERT_GROUPS

    oh1 = rowf == id1
    oh2 = rowf == id2
    oh = jnp.where(oh1 | oh2, 1.0, 0.0)
    before = jnp.dot(oh.astype(BF16), tri_ref[...], preferred_element_type=F32) + carry[:, 0:1]
    rank1 = jnp.sum(jnp.where(oh1, before, 0.0), axis=0, keepdims=True)
    rank2 = jnp.sum(jnp.where(oh2, before, 0.0), axis=0, keepdims=True)
    carry[...] = carry[...] + jnp.sum(oh, axis=1, keepdims=True)
    cnt_ref[...] = carry[...]

    pad_i = jnp.zeros((SUBLANES - 2 * TOP_K, tm), F32)
    ri_ref[0] = jnp.concatenate([id1, id2, rank1, rank2, pad_i], axis=0).astype(jnp.int32)
    pad_f = jnp.zeros((LANES - TOP_K, tm), F32)
    rf_ref[...] = jnp.concatenate([w1, w2, pad_f], axis=0).T[:, 0:SUBLANES]


def _outproj(attn, conv, x, wo, gffn, wrt, brc):
    n = x.shape[0]
    tm = OUT_TILE
    tri = jnp.asarray(np.triu(np.ones((tm, tm), np.float32), k=1), BF16)

    def full(a):
        return pl.BlockSpec(a.shape, lambda i: (0,) * a.ndim)

    rows = lambda width: pl.BlockSpec((tm, width), lambda i: (i, 0))
    return pl.pallas_call(
        functools.partial(_outproj_kernel, tm),
        grid=(n // tm,),
        in_specs=[rows(ATTN_WIDTH), rows(CONV_WIDTH), rows(D_MODEL)] + [
            full(a) for a in (wo, gffn, wrt, brc, tri)],
        out_specs=[rows(D_MODEL), pl.BlockSpec((tm * ROW_SUBLANES, LANES), lambda i: (i, 0)),
                   pl.BlockSpec((1, SUBLANES, tm), lambda i: (i, 0, 0)), rows(SUBLANES),
                   pl.BlockSpec((ROUTE_ROWS, LANES), lambda i: (0, 0))],
        out_shape=[jax.ShapeDtypeStruct((n, D_MODEL), F32),
                   jax.ShapeDtypeStruct((n * ROW_SUBLANES, LANES), BF16),
                   jax.ShapeDtypeStruct((n // tm, SUBLANES, tm), jnp.int32),
                   jax.ShapeDtypeStruct((n, SUBLANES), F32),
                   jax.ShapeDtypeStruct((ROUTE_ROWS, LANES), F32)],
        scratch_shapes=[pltpu.VMEM((ROUTE_ROWS, LANES), F32), pltpu.VMEM((tm * ROW_SUBLANES, LANES), F32)],
        compiler_params=pltpu.CompilerParams(
            dimension_semantics=("arbitrary",), vmem_limit_bytes=VMEM_LIMIT),
        name="outproj_router",
    )(attn, conv, x, wo, gffn, wrt, brc, tri)


def _dest_kernel(rs_ref, ri_ref, dest_ref):
    ri = ri_ref[...]
    experts = ri[:, 0:TOP_K, :]
    start = jnp.zeros_like(experts)
    for e in range(N_EXPERTS):
        start = jnp.where(experts == e, rs_ref[e], start)
    dest_ref[...] = jnp.concatenate([(start + ri[:, TOP_K:2 * TOP_K, :]) * ROW_SUBLANES,
                                     jnp.zeros((ri.shape[0], SUBLANES - TOP_K, ri.shape[2]), jnp.int32)], axis=1)


def _dest_rows(ri_t, row_start):
    grid_spec = pltpu.PrefetchScalarGridSpec(
        num_scalar_prefetch=1,
        grid=(1,),
        in_specs=[pl.BlockSpec(ri_t.shape, lambda i, rs: (0, 0, 0))],
        out_specs=pl.BlockSpec(ri_t.shape, lambda i, rs: (0, 0, 0)),
    )
    return pl.pallas_call(
        _dest_kernel,
        grid_spec=grid_spec,
        out_shape=jax.ShapeDtypeStruct(ri_t.shape, jnp.int32),
        compiler_params=pltpu.CompilerParams(dimension_semantics=("arbitrary",)),
        name="moe_dest",
    )(row_start, ri_t)


def _tile_copy(src, src_start, dst, dst_start, sem):
    rs = ROW_SUBLANES
    return pltpu.make_async_copy(src.at[pl.ds(pl.multiple_of(src_start, rs), rs), :],
                                 dst.at[pl.ds(pl.multiple_of(dst_start, rs), rs), :], sem)


def _dispatch_kernel(tm, bm, fz_ref, fv_ref, dest_ref, xt_ref, xs_hbm, zeros, sem, zsem):
    i = pl.program_id(0)
    block_rows = bm * ROW_SUBLANES

    def zero_block(z):
        start = pl.multiple_of(fz_ref[z] * block_rows, block_rows)
        return pltpu.make_async_copy(zeros, xs_hbm.at[pl.ds(start, block_rows), :], zsem.at[0])

    @pl.when(i == 0)
    def _():
        zeros[...] = jnp.zeros(zeros.shape, BF16)
        for z in range(fz_ref.shape[0]):
            @pl.when(fv_ref[z] == 1)
            def _():
                zero_block(z).start()
        for z in range(fz_ref.shape[0]):
            @pl.when(fv_ref[z] == 1)
            def _():
                zero_block(z).wait()

    for g in range(0, tm, DMA_GROUP):
        dests = [(r, kk, dest_ref[0, kk, r]) for r in range(g, g + DMA_GROUP) for kk in range(TOP_K)]
        for r, kk, d in dests:
            _tile_copy(xt_ref, r * ROW_SUBLANES, xs_hbm, d, sem.at[0]).start(priority=kk)
    for kk in range(TOP_K):
        pltpu.make_async_copy(xt_ref, xs_hbm.at[pl.ds(0, tm * ROW_SUBLANES), :], sem.at[0]).wait()


def _route_spec(dest, tm, extra_args):
    per = dest.shape[2] // tm
    nt = dest.shape[0] * per
    if extra_args == 0:
        index = lambda i: (jnp.minimum(i, nt - 1) // per, 0, jnp.minimum(i, nt - 1) % per)
    else:
        index = lambda i, *_: (jnp.minimum(i, nt - 1) // per, 0, jnp.minimum(i, nt - 1) % per)
    return nt, pl.BlockSpec((1, SUBLANES, tm), index, memory_space=pltpu.SMEM)


def _dispatch(xt_rows, dest, fill_blocks, fill_valid, nb):
    tm = DISPATCH_TILE
    bm = EXPERT_ROWS
    nt, dest_spec = _route_spec(dest, tm, 2)
    grid_spec = pltpu.PrefetchScalarGridSpec(
        num_scalar_prefetch=2,
        grid=(nt,),
        in_specs=[dest_spec, pl.BlockSpec((tm * ROW_SUBLANES, LANES), lambda i, fz, fv: (i, 0))],
        out_specs=pl.BlockSpec(memory_space=pl.ANY),
        scratch_shapes=[pltpu.VMEM((bm * ROW_SUBLANES, LANES), BF16), pltpu.SemaphoreType.DMA((1,)),
                        pltpu.SemaphoreType.DMA((1,))],
    )
    return pl.pallas_call(
        functools.partial(_dispatch_kernel, tm, bm),
        grid_spec=grid_spec,
        out_shape=jax.ShapeDtypeStruct((nb * bm * ROW_SUBLANES, LANES), BF16),
        compiler_params=pltpu.CompilerParams(
            dimension_semantics=("arbitrary",), vmem_limit_bytes=VMEM_LIMIT),
        name="moe_dispatch",
    )(fill_blocks, fill_valid, dest, xt_rows)


def _expert_kernel(bm, be_ref, nu_ref, first_ref, nxt_ref, slot_ref, xs_hbm, wg_hbm, wu_hbm, wd_hbm, ys_ref,
                   wg_buf, wu_buf, wd_buf, stage_in, wsem, xbuf, xsem):
    step = pl.program_id(0)
    nsteps = pl.num_programs(0)
    rows = bm * ROW_SUBLANES
    step_rows = EXPERT_BLOCKS_PER_STEP * rows

    def x_copy(s):
        b = s % X_BUFFERS
        return pltpu.make_async_copy(xs_hbm.at[pl.ds(pl.multiple_of(s * step_rows, step_rows), step_rows), :],
                                     xbuf.at[b], xsem.at[b])

    @pl.when(step == 0)
    def _():
        for k in range(X_BUFFERS - 1):
            x_copy(k).start()

    @pl.when(step + (X_BUFFERS - 1) < nsteps)
    def _():
        x_copy(step + (X_BUFFERS - 1)).start()

    x_copy(step).wait()
    xs_ref = xbuf.at[step % X_BUFFERS]

    def weight_copies(e, slot):
        return [pltpu.make_async_copy(src.at[e], buf.at[slot], wsem.at[m, slot])
                for m, (src, buf) in enumerate(((wg_hbm, wg_buf), (wu_hbm, wu_buf), (wd_hbm, wd_buf)))]

    @pl.when(step == 0)
    def _():
        for c in weight_copies(be_ref[0], 0):
            c.start()

    def one_block(j, xs_blk, ys_blk):
        @pl.when(j < nu_ref[0])
        def _():
            slot = slot_ref[j]

            @pl.when(first_ref[j] == 1)
            def _():
                for c in weight_copies(0, slot):
                    c.wait()

                @pl.when(nxt_ref[j] >= 0)
                def _():
                    for c in weight_copies(nxt_ref[j], 1 - slot):
                        c.start()

            x = _load_token_rows(xs_blk, bm, stage_in)
            hg = jnp.dot(x, wg_buf[slot], preferred_element_type=F32)
            hu = jnp.dot(x, wu_buf[slot], preferred_element_type=F32)
            hdn = hg * (1.0 / (1.0 + jnp.exp(-hg))) * hu
            y = jnp.dot(hdn, wd_buf[slot], preferred_element_type=F32)
            _store_token_rows(ys_blk, y, bm)

        @pl.when(j >= nu_ref[0])
        def _():
            ys_blk[...] = jnp.zeros(ys_blk.shape, F32)

    for sub in range(EXPERT_BLOCKS_PER_STEP):
        view = pl.ds(sub * rows, rows)
        one_block(step * EXPERT_BLOCKS_PER_STEP + sub, xs_ref.at[view, :], ys_ref.at[view, :])


def _experts(xs_rows, block_expert, n_used, w_gate, w_up, w_down):
    bm = EXPERT_ROWS
    nb = block_expert.shape[0]
    blk = jnp.arange(nb, dtype=jnp.int32)
    live = blk < n_used[0]
    first = live & ((blk == 0) | (block_expert != jnp.roll(block_expert, 1)))
    slot = (jnp.cumsum(first.astype(jnp.int32)) - 1) % 2
    later_first = first[None, :] & (blk[None, :] > blk[:, None])
    nxt_start = jnp.min(jnp.where(later_first, blk[None, :], nb), axis=1)
    nxt = jnp.sum(jnp.where(blk[None, :] == nxt_start[:, None], block_expert[None, :], 0), axis=1)
    nxt = jnp.where(nxt_start < nb, nxt, -1)
    per = EXPERT_BLOCKS_PER_STEP
    assert nb % per == 0 and nb // per >= X_BUFFERS
    step_rows = per * bm * ROW_SUBLANES
    grid_spec = pltpu.PrefetchScalarGridSpec(
        num_scalar_prefetch=5,
        grid=(nb // per,),
        in_specs=[
            pl.BlockSpec(memory_space=pl.ANY),
            pl.BlockSpec(memory_space=pl.ANY),
            pl.BlockSpec(memory_space=pl.ANY),
            pl.BlockSpec(memory_space=pl.ANY),
        ],
        out_specs=pl.BlockSpec((step_rows, LANES), lambda s, *_: (s, 0)),
        scratch_shapes=[pltpu.VMEM((2, D_MODEL, D_EXPERT), F32), pltpu.VMEM((2, D_MODEL, D_EXPERT), F32),
                        pltpu.VMEM((2, D_EXPERT, D_MODEL), F32),
                        pltpu.VMEM((bm * ROW_SUBLANES, LANES), F32), pltpu.SemaphoreType.DMA((3, 2)),
                        pltpu.VMEM((X_BUFFERS, step_rows, LANES), BF16), pltpu.SemaphoreType.DMA((X_BUFFERS,))],
    )
    return pl.pallas_call(
        functools.partial(_expert_kernel, bm),
        grid_spec=grid_spec,
        out_shape=jax.ShapeDtypeStruct((nb * bm * ROW_SUBLANES, LANES), F32),
        compiler_params=pltpu.CompilerParams(
            dimension_semantics=("arbitrary",), vmem_limit_bytes=VMEM_LIMIT),
        name="moe_experts",
    )(block_expert, n_used, first.astype(jnp.int32), nxt.astype(jnp.int32), slot.astype(jnp.int32),
      xs_rows, w_gate, w_up, w_down)


def _combine_kernel(tm, dest_ref, ys_hbm, h_ref, rf_ref, o_ref, ybuf, sem):
    i = pl.program_id(0)
    nt = pl.num_programs(0) - 1

    @pl.when(i < nt)
    def _():
        slot = i % 2
        for g in range(0, tm, DMA_GROUP):
            dests = [(r, kk, dest_ref[0, kk, r]) for r in range(g, g + DMA_GROUP) for kk in range(TOP_K)]
            for r, kk, d in dests:
                _tile_copy(ys_hbm, d, ybuf.at[kk, slot], r * ROW_SUBLANES, sem.at[kk, slot]).start(priority=kk)

    @pl.when(i >= 1)
    def _():
        slot = (i - 1) % 2
        for kk in range(TOP_K):
            pltpu.make_async_copy(ys_hbm.at[pl.ds(0, tm * ROW_SUBLANES), :], ybuf.at[kk, slot],
                                  sem.at[kk, slot]).wait()
        rf = rf_ref[...]
        o_ref[...] = (h_ref[...] + rf[:, 0:1] * _load_token_rows(ybuf.at[0, slot], tm)
                      + rf[:, 1:2] * _load_token_rows(ybuf.at[1, slot], tm))


def _combine(ys_rows, dest, h, rf):
    n = h.shape[0]
    tm = COMBINE_TILE
    nt, dest_spec = _route_spec(dest, tm, 0)
    prev = lambda i: jnp.maximum(i - 1, 0)
    return pl.pallas_call(
        functools.partial(_combine_kernel, tm),
        grid=(nt + 1,),
        in_specs=[
            dest_spec,
            pl.BlockSpec(memory_space=pl.ANY),
            pl.BlockSpec((tm, D_MODEL), lambda i: (prev(i), 0)),
            pl.BlockSpec((tm, SUBLANES), lambda i: (prev(i), 0)),
        ],
        out_specs=pl.BlockSpec((tm, D_MODEL), lambda i: (prev(i), 0)),
        out_shape=jax.ShapeDtypeStruct((n, D_MODEL), F32),
        scratch_shapes=[pltpu.VMEM((TOP_K, 2, tm * ROW_SUBLANES, LANES), F32),
                        pltpu.SemaphoreType.DMA((TOP_K, 2))],
        compiler_params=pltpu.CompilerParams(
            dimension_semantics=("arbitrary",), vmem_limit_bytes=VMEM_LIMIT),
        name="moe_combine",
    )(dest, ys_rows, h, rf)


def _pad_lanes(a, width=LANES):
    return jnp.pad(a, ((0, 0), (0, width - a.shape[-1])))


def _layer(x, meta_tokens, g_mix, w_in, b_forget, q_norm_g, k_norm_g, conv_w, attn_out_g, conv_out_g,
           w_out, g_ffn, w_rg, b_rg, w_re, b_re, w_gate, w_up, w_down):
    bsz, slen, _ = x.shape
    n = bsz * slen
    w = ATTN_WIDTH

    wq, wk, wv, wf, wb, wc, whc = jnp.split(
        w_in, [w, 2 * w, 3 * w, 3 * w + ATTN_HEADS, 4 * w + ATTN_HEADS, 5 * w + ATTN_HEADS], axis=1)
    w1 = jnp.concatenate([wq, wk, wb, wc, whc, _pad_lanes(wf)], axis=1).astype(BF16)
    head_of = np.arange(MXU_DIM) // HEAD_DIM
    gmat = jnp.asarray(head_of[:, None] == head_of[None, :], BF16)
    consts = (
        g_mix.reshape(1, D_MODEL), w1, wv.T.astype(BF16), _pad_lanes(b_forget.reshape(1, ATTN_HEADS)),
        jnp.tile(q_norm_g, ATTN_HEADS).reshape(1, w) * (HEAD_DIM ** -0.5 * LOG2E),
        jnp.tile(k_norm_g, ATTN_HEADS).reshape(1, w),
        jnp.pad(conv_w, ((0, SUBLANES - CONV_K), (0, 0))), conv_out_g.reshape(1, w), gmat)

    xm = jnp.pad(meta_tokens, ((0, META_TILE - N_META), (0, 0)))[None]
    km, vmt, kaugm, cum_m, uc_m = _inproj(xm, True, META_TILE, consts,
                                          jnp.zeros((SUBLANES, w), F32), jnp.zeros((1, LANES), F32))
    km, vmt, kaugm = km[0, :N_META], vmt[0, 0, :, :N_META], kaugm[0, :N_META]
    cum_m = cum_m[0, :N_META]
    halo = jnp.zeros((SUBLANES, w), F32).at[SUBLANES - 2:].set(uc_m[0, N_META - 2:N_META])
    cum0 = cum_m[N_META - 1:N_META]

    q, k, vt, conv, qaug, kaug = _inproj(x, False, IN_TILE, consts, halo, cum0)
    attn = _attention(q, qaug, k, kaug, vt, km, kaugm, vmt, attn_out_g.reshape(w, 1))

    wr = _pad_lanes(jnp.concatenate([w_rg, w_re], axis=1)).T
    wr_hi = wr.astype(BF16)
    wr_lo = (wr - wr_hi.astype(F32)).astype(BF16)
    wrt = jnp.concatenate([jnp.concatenate([wr_hi, wr_hi], axis=1),
                           jnp.concatenate([wr_lo, jnp.zeros_like(wr_lo)], axis=1)], axis=0)
    brc = jnp.broadcast_to(_pad_lanes(jnp.concatenate([b_rg, b_re]).reshape(1, -1)).T, (LANES, LANES))
    h, xt, ri, rf, cnt = _outproj(attn.reshape(n, w), conv.reshape(n, w), x.reshape(n, D_MODEL),
                                  w_out.astype(BF16), g_ffn.reshape(1, D_MODEL), wrt, brc)

    bm = EXPERT_ROWS
    nb = (n * TOP_K) // bm + N_EXPERTS
    counts = cnt[:N_EXPERTS, 0].astype(jnp.int32)
    blocks_e = (counts + bm - 1) // bm
    blk_end = jnp.cumsum(blocks_e)
    row_start = (blk_end - blocks_e) * bm
    blk = jnp.arange(nb, dtype=jnp.int32)
    block_expert = jnp.minimum(jnp.sum(blk_end[None, :] <= blk[:, None], axis=1), N_EXPERTS - 1).astype(jnp.int32)
    n_used = blk_end[-1:].astype(jnp.int32)
    tail = n_used[0] + jnp.arange(N_EXPERTS, dtype=jnp.int32)
    fill_blocks = jnp.clip(jnp.concatenate([blk_end - 1, tail]), 0, nb - 1).astype(jnp.int32)
    fill_valid = jnp.concatenate([counts % bm != 0, tail < nb]).astype(jnp.int32)

    dest = _dest_rows(ri, row_start.astype(jnp.int32))
    xs = _dispatch(xt, dest, fill_blocks, fill_valid, nb)
    ys = _experts(xs, block_expert, n_used, w_gate, w_up, w_down)
    out = _combine(ys, dest, h, rf)
    return out.reshape(bsz, slen, D_MODEL)


def kernel(x, meta_tokens, norm_mix_g, w_in, b_forget, q_norm_g, k_norm_g, conv_w, attn_out_g, conv_out_g,
           w_out, norm_ffn_g, w_router_group, b_router_group, w_router_expert, b_router_expert,
           w_gate, w_up, w_down):
    assert norm_mix_g.shape[0] == 1, "single-layer block"
    return _layer(x, meta_tokens, norm_mix_g[0], w_in[0], b_forget[0], q_norm_g[0], k_norm_g[0], conv_w[0],
                  attn_out_g[0], conv_out_g[0], w_out[0], norm_ffn_g[0], w_router_group[0],
                  b_router_group[0], w_router_expert[0], b_router_expert[0], w_gate[0], w_up[0], w_down[0])
```
